```python
import math
import jax, jax.numpy as jnp
from jax import lax
import numpy as np

D_MODEL = 1024
BATCH = 4
SEQ = 4096
DEPTH = 4
DEC_BATCH = 128
DEC_SEQ = 4
PAST_LEN = 8192
PAGE_SIZE = 128

N_MIXERS = 2
GDN_QK_HEADS = 4
GDN_V_HEADS = 8
GDN_HEAD_DIM = 128
GDN_KEY_DIM = GDN_QK_HEADS * GDN_HEAD_DIM
GDN_VAL_DIM = GDN_V_HEADS * GDN_HEAD_DIM
GDN_CONV_DIM = 2 * GDN_KEY_DIM + GDN_VAL_DIM
GDN_PROJ_DIM = GDN_CONV_DIM + GDN_VAL_DIM + 2 * GDN_V_HEADS
CONV_WIDTH = 4
GDN_CHUNK = 64
SWA_Q_HEADS = 16
SWA_KV_HEADS = 4
SWA_GROUP = SWA_Q_HEADS // SWA_KV_HEADS
SWA_HEAD_DIM = 64
SWA_Q_DIM = SWA_Q_HEADS * SWA_HEAD_DIM
SWA_KV_DIM = SWA_KV_HEADS * SWA_HEAD_DIM
SWA_PROJ_DIM = SWA_Q_DIM + 2 * SWA_KV_DIM
WINDOW = 128
SWA_BLOCK = WINDOW
REL_BUCKETS = 32
REL_MAX_DIST = 128
D_FF = 2816
N_EXPERTS = 8
TOP_K = 2
D_FF_EXPERT = 2816
N_GDN_LAYERS = (DEPTH + 1) // 2
N_SWA_LAYERS = DEPTH // 2
N_DENSE_LAYERS = (DEPTH + 1) // 2
N_MOE_LAYERS = DEPTH // 2
RMS_EPS = 1e-6

kernel_name = 'hybrid_gdn_swa_sink_moe_step'


def rms_norm(x, g):
    xf = x.astype(jnp.float32)
    y = xf * lax.rsqrt(jnp.mean(xf * xf, axis=-1, keepdims=True) + RMS_EPS)
    return (y * g.astype(jnp.float32)).astype(x.dtype)


def l2_norm(x):
    xf = x.astype(jnp.float32)
    return xf * lax.rsqrt(jnp.sum(xf * xf, axis=-1, keepdims=True) + RMS_EPS)


def modulation(c, w_mod, b_mod):
    m = jax.nn.silu(c) @ w_mod + b_mod
    return jnp.split(m[:, None, :], 6, axis=-1)


def causal_conv(x, buf, w):
    t = x.shape[1]
    xp = jnp.concatenate([buf, x], axis=1)
    y = xp[:, 0:t] * w[0]
    for j in range(1, CONV_WIDTH):
        y = y + xp[:, j:j + t] * w[j]
    return jax.nn.silu(y), xp[:, -(CONV_WIDTH - 1):]


def gated_delta_chunked(q, k, v, g, beta, s0, chunk):
    b, t, h, dk = q.shape
    dv = v.shape[-1]
    n = t // chunk
    def heads_first(a):
        a = a.reshape((b, n, chunk) + a.shape[2:])
        return jnp.swapaxes(a, 2, 3)
    q, k, v, g, beta = map(heads_first, (q, k, v, g, beta))
    gc = jnp.cumsum(g, axis=-1)
    causal = jnp.tril(jnp.ones((chunk, chunk), dtype=bool))
    strict = jnp.tril(jnp.ones((chunk, chunk), dtype=bool), k=-1)
    diff = gc[..., :, None] - gc[..., None, :]
    decay = jnp.exp(jnp.where(causal, diff, -jnp.inf))
    kk = jnp.einsum('bnhid,bnhjd->bnhij', k, k)
    a_mat = jnp.where(strict, beta[..., :, None] * kk * decay, 0.0)
    eye = jnp.eye(chunk, dtype=jnp.float32)
    rhs = jnp.concatenate([v * beta[..., None], k * (beta * jnp.exp(gc))[..., None]], axis=-1)
    sol = lax.linalg.triangular_solve(eye + a_mat, rhs, left_side=True, lower=True, unit_diagonal=True)
    u0, w = sol[..., :dv], sol[..., dv:]
    qk = jnp.where(causal, jnp.einsum('bnhid,bnhjd->bnhij', q, k) * decay, 0.0)
    q_dec = q * jnp.exp(gc)[..., None]
    k_dec = k * jnp.exp(gc[..., -1:] - gc)[..., None]
    g_last = jnp.exp(gc[..., -1])

    def step(s, xs):
        u0_c, w_c, qk_c, qd_c, kd_c, gl_c = xs
        u = u0_c - jnp.einsum('bhik,bhkv->bhiv', w_c, s)
        o = jnp.einsum('bhik,bhkv->bhiv', qd_c, s) + jnp.einsum('bhij,bhjv->bhiv', qk_c, u)
        s = s * gl_c[..., None, None] + jnp.einsum('bhik,bhiv->bhkv', kd_c, u)
        return s, o

    xs = tuple(jnp.moveaxis(a, 1, 0) for a in (u0, w, qk, q_dec, k_dec, g_last))
    s_fin, o = lax.scan(step, s0, xs)
    o = jnp.transpose(o, (1, 0, 3, 2, 4)).reshape(b, t, h, dv)
    return o, s_fin


def gdn_mixer(h, conv_buf, s0, w_in, conv_w, a_log, dt_bias, norm_w, w_out, chunk):
    b, t, _ = h.shape
    p = h @ w_in
    qkv, z, b_raw, a_raw = jnp.split(p, [GDN_CONV_DIM, GDN_CONV_DIM + GDN_VAL_DIM, GDN_CONV_DIM + GDN_VAL_DIM + GDN_V_HEADS], axis=-1)
    qkv, new_buf = causal_conv(qkv, conv_buf, conv_w)
    q, k, v = jnp.split(qkv, [GDN_KEY_DIM, 2 * GDN_KEY_DIM], axis=-1)
    rep = GDN_V_HEADS // GDN_QK_HEADS
    q = jnp.repeat(l2_norm(q.reshape(b, t, GDN_QK_HEADS, GDN_HEAD_DIM)), rep, axis=2) * (GDN_HEAD_DIM ** -0.5)
    k = jnp.repeat(l2_norm(k.reshape(b, t, GDN_QK_HEADS, GDN_HEAD_DIM)), rep, axis=2)
    v = v.reshape(b, t, GDN_V_HEADS, GDN_HEAD_DIM).astype(jnp.float32)
    beta = jax.nn.sigmoid(b_raw.astype(jnp.float32))
    g = -jnp.exp(a_log.astype(jnp.float32)) * jax.nn.softplus(a_raw.astype(jnp.float32) + dt_bias.astype(jnp.float32))
    o, s_new = gated_delta_chunked(q, k, v, g, beta, s0.astype(jnp.float32), chunk)
    o = rms_norm(o, norm_w) * jax.nn.silu(z.reshape(b, t, GDN_V_HEADS, GDN_HEAD_DIM).astype(jnp.float32))
    y = o.reshape(b, t, GDN_VAL_DIM).astype(h.dtype) @ w_out
    return y, new_buf, s_new


def t5_bucket(dist):
    dist = jnp.maximum(dist, 0)
    max_exact = REL_BUCKETS // 2
    log_ratio = jnp.log(jnp.maximum(dist, max_exact).astype(jnp.float32) / max_exact) / math.log(REL_MAX_DIST / max_exact)
    large = max_exact + (log_ratio * (REL_BUCKETS - max_exact)).astype(jnp.int32)
    return jnp.where(dist < max_exact, dist, jnp.minimum(large, REL_BUCKETS - 1))


def sink_attention(q, k, v, qpos, kpos, rel_bias, sinks):
    nb, tq = qpos.shape
    tk = kpos.shape[1]
    s = jnp.einsum('bnqgrd,bnkgd->bngrqk', q, k).astype(jnp.float32) * (SWA_HEAD_DIM ** -0.5)
    dist = qpos[:, :, None] - kpos[:, None, :]
    valid = (dist >= 0) & (dist < WINDOW) & (kpos[:, None, :] >= 0)
    bias = rel_bias.astype(jnp.float32)[t5_bucket(dist)]
    bias = jnp.transpose(bias.reshape(nb, tq, tk, SWA_KV_HEADS, SWA_GROUP), (0, 3, 4, 1, 2))
    s = jnp.where(valid[:, None, None], s + bias, -jnp.inf)
    sink = jnp.broadcast_to(sinks.astype(jnp.float32).reshape(SWA_KV_HEADS, SWA_GROUP)[:, :, None, None], s.shape[:-1] + (1,))
    p = jax.nn.softmax(jnp.concatenate([s, sink], axis=-1), axis=-1)[..., :-1]
    return jnp.einsum('bngrqk,bnkgd->bnqgrd', p.astype(v.dtype), v)


def swa_mixer(h, w_in, b_in, sinks, w_out, b_out, rel_bias, k_buf, v_buf):
    b, t, _ = h.shape
    p = h @ w_in + b_in
    q, k, v = jnp.split(p, [SWA_Q_DIM, SWA_Q_DIM + SWA_KV_DIM], axis=-1)
    q = q.reshape(b, t, SWA_KV_HEADS, SWA_GROUP, SWA_HEAD_DIM)
    k = k.reshape(b, t, SWA_KV_HEADS, SWA_HEAD_DIM)
    v = v.reshape(b, t, SWA_KV_HEADS, SWA_HEAD_DIM)
    if k_buf is None:
        nb = t // SWA_BLOCK
        qb = q.reshape(b, nb, SWA_BLOCK, SWA_KV_HEADS, SWA_GROUP, SWA_HEAD_DIM)
        def band(a):
            prev = jnp.concatenate([jnp.zeros_like(a[:, :SWA_BLOCK]), a[:, :-SWA_BLOCK]], axis=1)
            return jnp.concatenate([prev.reshape(b, nb, SWA_BLOCK, SWA_KV_HEADS, SWA_HEAD_DIM),
                                    a.reshape(b, nb, SWA_BLOCK, SWA_KV_HEADS, SWA_HEAD_DIM)], axis=2)
        base = jnp.arange(nb, dtype=jnp.int32)[:, None] * SWA_BLOCK
        qpos = base + jnp.arange(SWA_BLOCK, dtype=jnp.int32)[None]
        kpos = base - SWA_BLOCK + jnp.arange(2 * SWA_BLOCK, dtype=jnp.int32)[None]
        o = sink_attention(qb, band(k), band(v), qpos, kpos, rel_bias, sinks)
        new_k, new_v = k[:, -WINDOW:], v[:, -WINDOW:]
    else:
        wb = k_buf.shape[1]
        kc = jnp.concatenate([k_buf, k], axis=1)
        vc = jnp.concatenate([v_buf, v], axis=1)
        qpos = (PAST_LEN + jnp.arange(t, dtype=jnp.int32))[None]
        kpos = (PAST_LEN - wb + jnp.arange(wb + t, dtype=jnp.int32))[None]
        o = sink_attention(q[:, None], kc[:, None], vc[:, None], qpos, kpos, rel_bias, sinks)
        new_k, new_v = kc[:, -wb:], vc[:, -wb:]
    y = o.reshape(b, t, SWA_Q_DIM) @ w_out + b_out
    return y, new_k, new_v


def swiglu(h, w_gate, w_up, w_down):
    return (jax.nn.silu(h @ w_gate) * (h @ w_up)) @ w_down


def moe_swiglu(h, w_router, b_router, w_gate, w_up, w_down):
    logits = (h @ w_router).astype(jnp.float32) + b_router.astype(jnp.float32)
    top_val, top_idx = lax.top_k(logits, TOP_K)
    top_w = jax.nn.softmax(top_val, axis=-1)
    gate = jnp.einsum('btk,btke->bte', top_w, jax.nn.one_hot(top_idx, N_EXPERTS, dtype=jnp.float32)).astype(h.dtype)
    y = jnp.zeros_like(h)
    for e in range(N_EXPERTS):
        y = y + gate[..., e:e + 1] * swiglu(h, w_gate[e], w_up[e], w_down[e])
    return y


def trunk(x, c, prm, conv_in, rec_in, win_k_in, win_v_in):
    prompt = conv_in is None
    b, t, _ = x.shape
    new_conv, new_rec, new_k, new_v = [], [], [], []
    for layer in range(DEPTH):
        j = layer // N_MIXERS
        sh1, sc1, gt1, sh2, sc2, gt2 = modulation(c, prm['w_mod'][layer], prm['b_mod'][layer])
        h = rms_norm(x, prm['g_pre_mix'][layer]) * (1 + sc1) + sh1
        if layer % N_MIXERS == 0:
            if prompt:
                cbuf = jnp.zeros((b, CONV_WIDTH - 1, GDN_CONV_DIM), x.dtype)
                s0 = jnp.zeros((b, GDN_V_HEADS, GDN_HEAD_DIM, GDN_HEAD_DIM), jnp.float32)
                chunk = min(GDN_CHUNK, t)
            else:
                cbuf, s0, chunk = conv_in[j], rec_in[j], t
            y, cb, s = gdn_mixer(h, cbuf, s0, prm['gdn_w_in'][j], prm['gdn_conv_w'][j], prm['gdn_a_log'][j],
                                 prm['gdn_dt_bias'][j], prm['gdn_norm_w'][j], prm['gdn_w_out'][j], chunk)
            new_conv.append(cb.astype(x.dtype))
            new_rec.append(s.astype(x.dtype))
        else:
            kb = None if prompt else win_k_in[j]
            vb = None if prompt else win_v_in[j]
            y, nk, nv = swa_mixer(h, prm['swa_w_in'][j], prm['swa_b_in'][j], prm['swa_sinks'][j],
                                  prm['swa_w_out'][j], prm['swa_b_out'][j], prm['rel_bias'], kb, vb)
            new_k.append(nk)
            new_v.append(nv)
        x = x + gt1 * rms_norm(y, prm['g_post_mix'][layer])
        h = rms_norm(x, prm['g_pre_ffn'][layer]) * (1 + sc2) + sh2
        if layer % 2 == 0:
            y = swiglu(h, prm['ffn_w_gate'][j], prm['ffn_w_up'][j], prm['ffn_w_down'][j])
        else:
            y = moe_swiglu(h, prm['moe_w_router'][j], prm['moe_b_router'][j], prm['moe_w_gate'][j],
                           prm['moe_w_up'][j], prm['moe_w_down'][j])
        x = x + gt2 * rms_norm(y, prm['g_post_ffn'][layer])
    return x, jnp.stack(new_conv), jnp.stack(new_rec), jnp.stack(new_k), jnp.stack(new_v)


def setup_inputs(seed: int = 0) -> dict:
    key = jax.random.key(seed)
    keys = iter(jax.random.split(key, 64))
    def nrm(shape, scale):
        return scale * jax.random.normal(next(keys), shape, jnp.float32)
    def gain(shape):
        return 1.0 + nrm(shape, 0.1)
    D = D_MODEL
    w_buf = min(WINDOW, PAST_LEN)
    inp = {}
    inp['x_prompt'] = nrm((BATCH, SEQ, D), 1.0)
    inp['x_sample'] = nrm((DEC_BATCH, DEC_SEQ, D), 1.0)
    inp['c_prompt'] = nrm((BATCH, D), 1.0)
    inp['c_sample'] = nrm((DEC_BATCH, D), 1.0)
    inp['state_conv'] = nrm((N_GDN_LAYERS, DEC_BATCH, CONV_WIDTH - 1, GDN_CONV_DIM), 1.0)
    inp['state_rec'] = nrm((N_GDN_LAYERS, DEC_BATCH, GDN_V_HEADS, GDN_HEAD_DIM, GDN_HEAD_DIM), 0.1)
    inp['cache_win_k'] = nrm((N_SWA_LAYERS, DEC_BATCH, w_buf, SWA_KV_HEADS, SWA_HEAD_DIM), 1.0)
    inp['cache_win_v'] = nrm((N_SWA_LAYERS, DEC_BATCH, w_buf, SWA_KV_HEADS, SWA_HEAD_DIM), 1.0)
    inp['w_mod'] = nrm((DEPTH, D, 6 * D), 0.5 * D ** -0.5)
    inp['b_mod'] = nrm((DEPTH, 6 * D), 0.02)
    inp['g_pre_mix'] = gain((DEPTH, D))
    inp['g_post_mix'] = gain((DEPTH, D))
    inp['g_pre_ffn'] = gain((DEPTH, D))
    inp['g_post_ffn'] = gain((DEPTH, D))
    inp['gdn_w_in'] = nrm((N_GDN_LAYERS, D, GDN_PROJ_DIM), D ** -0.5)
    inp['gdn_conv_w'] = nrm((N_GDN_LAYERS, CONV_WIDTH, GDN_CONV_DIM), CONV_WIDTH ** -0.5)
    inp['gdn_a_log'] = jnp.log(jax.random.uniform(next(keys), (N_GDN_LAYERS, GDN_V_HEADS), jnp.float32, 1.0, 16.0))
    dt = jnp.exp(jax.random.uniform(next(keys), (N_GDN_LAYERS, GDN_V_HEADS), jnp.float32, math.log(1e-3), math.log(1e-1)))
    inp['gdn_dt_bias'] = dt + jnp.log(-jnp.expm1(-dt))
    inp['gdn_norm_w'] = gain((N_GDN_LAYERS, GDN_HEAD_DIM))
    inp['gdn_w_out'] = nrm((N_GDN_LAYERS, GDN_VAL_DIM, D), GDN_VAL_DIM ** -0.5)
    inp['swa_w_in'] = nrm((N_SWA_LAYERS, D, SWA_PROJ_DIM), D ** -0.5)
    inp['swa_b_in'] = nrm((N_SWA_LAYERS, SWA_PROJ_DIM), 0.02)
    inp['swa_sinks'] = nrm((N_SWA_LAYERS, SWA_Q_HEADS), 1.0)
    inp['swa_w_out'] = nrm((N_SWA_LAYERS, SWA_Q_DIM, D), SWA_Q_DIM ** -0.5)
    inp['swa_b_out'] = nrm((N_SWA_LAYERS, D), 0.02)
    inp['rel_bias'] = nrm((REL_BUCKETS, SWA_Q_HEADS), 0.5)
    inp['ffn_w_gate'] = nrm((N_DENSE_LAYERS, D, D_FF), D ** -0.5)
    inp['ffn_w_up'] = nrm((N_DENSE_LAYERS, D, D_FF), D ** -0.5)
    inp['ffn_w_down'] = nrm((N_DENSE_LAYERS, D_FF, D), D_FF ** -0.5)
    inp['moe_w_router'] = nrm((N_MOE_LAYERS, D, N_EXPERTS), D ** -0.5)
    inp['moe_b_router'] = nrm((N_MOE_LAYERS, N_EXPERTS), 0.01)
    inp['moe_w_gate'] = nrm((N_MOE_LAYERS, N_EXPERTS, D, D_FF_EXPERT), D ** -0.5)
    inp['moe_w_up'] = nrm((N_MOE_LAYERS, N_EXPERTS, D, D_FF_EXPERT), D ** -0.5)
    inp['moe_w_down'] = nrm((N_MOE_LAYERS, N_EXPERTS, D_FF_EXPERT, D), D_FF_EXPERT ** -0.5)
    return inp


def reference(x_prompt, x_sample, c_prompt, c_sample, state_conv, state_rec, cache_win_k, cache_win_v,
              w_mod, b_mod, g_pre_mix, g_post_mix, g_pre_ffn, g_post_ffn,
              gdn_w_in, gdn_conv_w, gdn_a_log, gdn_dt_bias, gdn_norm_w, gdn_w_out,
              swa_w_in, swa_b_in, swa_sinks, swa_w_out, swa_b_out, rel_bias,
              ffn_w_gate, ffn_w_up, ffn_w_down,
              moe_w_router, moe_b_router, moe_w_gate, moe_w_up, moe_w_down):
    prm = {
        'w_mod': w_mod, 'b_mod': b_mod, 'g_pre_mix': g_pre_mix, 'g_post_mix': g_post_mix,
        'g_pre_ffn': g_pre_ffn, 'g_post_ffn': g_post_ffn,
        'gdn_w_in': gdn_w_in, 'gdn_conv_w': gdn_conv_w, 'gdn_a_log': gdn_a_log, 'gdn_dt_bias': gdn_dt_bias,
        'gdn_norm_w': gdn_norm_w, 'gdn_w_out': gdn_w_out,
        'swa_w_in': swa_w_in, 'swa_b_in': swa_b_in, 'swa_sinks': swa_sinks, 'swa_w_out': swa_w_out,
        'swa_b_out': swa_b_out, 'rel_bias': rel_bias,
        'ffn_w_gate': ffn_w_gate, 'ffn_w_up': ffn_w_up, 'ffn_w_down': ffn_w_down,
        'moe_w_router': moe_w_router, 'moe_b_router': moe_b_router, 'moe_w_gate': moe_w_gate,
        'moe_w_up': moe_w_up, 'moe_w_down': moe_w_down,
    }
    y_prompt, conv_p, rec_p, win_k_p, win_v_p = trunk(x_prompt, c_prompt, prm, None, None, None, None)
    y_sample, conv_s, rec_s, win_k_s, win_v_s = trunk(x_sample, c_sample, prm, state_conv, state_rec, cache_win_k, cache_win_v)
    return (y_prompt, y_sample, conv_p, rec_p, win_k_p, win_v_p, conv_s, rec_s, win_k_s, win_v_s)
```

```python
import functools
import math

import numpy as np
import jax
import jax.numpy as jnp
from jax import lax
from jax.experimental import pallas as pl
from jax.experimental.pallas import tpu as pltpu

F32 = jnp.float32
BF16 = jnp.bfloat16
I32 = jnp.int32
HIGHEST = lax.Precision.HIGHEST

D = 1024
BATCH = 4
SEQ = 4096
DEPTH = 4
DEC_BATCH = 128
DEC_SEQ = 4
PAST_LEN = 8192
GDN_QK_HEADS = 4
GDN_V_HEADS = 8
GDN_HD = 128
GDN_KEY_DIM = GDN_QK_HEADS * GDN_HD
GDN_VAL_DIM = GDN_V_HEADS * GDN_HD
GDN_CONV_DIM = 2 * GDN_KEY_DIM + GDN_VAL_DIM
GDN_MAIN_DIM = GDN_CONV_DIM + GDN_VAL_DIM
CONV_WIDTH = 4
GDN_CHUNK = 64
SWA_Q_HEADS = 16
SWA_KV_HEADS = 4
SWA_GROUP = SWA_Q_HEADS // SWA_KV_HEADS
SWA_HD = 64
SWA_Q_DIM = SWA_Q_HEADS * SWA_HD
SWA_KV_DIM = SWA_KV_HEADS * SWA_HD
SWA_PROJ_DIM = SWA_Q_DIM + 2 * SWA_KV_DIM
WINDOW = 128
REL_BUCKETS = 32
REL_MAX_DIST = 128
D_FF = 2816
N_EXPERTS = 8
RMS_EPS = 1e-6

LANES = 128
SUBLANES = 8
VMEM_LIMIT = 56 * 1024 * 1024

TM = 512
NP_TOK = BATCH * SEQ
NS_TOK = DEC_BATCH * DEC_SEQ
TT = NP_TOK + NS_TOK
NP_TILES = NP_TOK // TM
NS_TILES = NS_TOK // TM
N_TILES = NP_TILES + NS_TILES
TILES_PER_SEQ = SEQ // TM
TF = 256
NF = D_FF // TF
MOE_ROWS = 2 * TT + N_EXPERTS * TM
MOE_TILES = MOE_ROWS // TM
GDN_NB = 4
GDN_SAMPLE_ROWS = 8
SWA_SB = 8
SWA_KPAD = WINDOW + 8

assert NP_TOK % TM == 0 and NS_TOK % TM == 0 and SEQ % TM == 0
assert D_FF % TF == 0 and DEC_SEQ <= GDN_SAMPLE_ROWS and DEC_SEQ <= 8


def _cparams(sem):
    return pltpu.CompilerParams(dimension_semantics=sem, vmem_limit_bytes=VMEM_LIMIT)


def _bdot(a, b):
    return jnp.dot(a.astype(BF16), b.astype(BF16), preferred_element_type=F32)


def _bdot_nt(a, b):
    return lax.dot_general(a.astype(BF16), b.astype(BF16), (((1,), (1,)), ((), ())),
                           preferred_element_type=F32)


def _silu(x):
    return x * jax.nn.sigmoid(x)


def _rms(x):
    return x * lax.rsqrt(jnp.mean(x * x, axis=-1, keepdims=True) + RMS_EPS)


def _mod_kernel(c_ref, w_ref, b_ref, o_ref):
    c = c_ref[...]
    o_ref[0] = _bdot(_silu(c), w_ref[0]) + b_ref[0]


def _mod_call(c_all, w_mod, b_mod):
    n = c_all.shape[0]
    tn = D
    return pl.pallas_call(
        _mod_kernel,
        grid=(DEPTH, 6 * D // tn),
        in_specs=[
            pl.BlockSpec((n, D), lambda l, j: (0, 0)),
            pl.BlockSpec((1, D, tn), lambda l, j: (l, 0, j)),
            pl.BlockSpec((1, 1, tn), lambda l, j: (l, 0, j)),
        ],
        out_specs=pl.BlockSpec((1, n, tn), lambda l, j: (l, 0, j)),
        out_shape=jax.ShapeDtypeStruct((DEPTH, n, 6 * D), F32),
        compiler_params=_cparams(("arbitrary", "arbitrary")),
        name="modulation",
    )(c_all, w_mod, b_mod.reshape(DEPTH, 1, 6 * D))


def _mod_specs(chunk):
    p = pl.BlockSpec((1, 1, D), lambda i, *_: (jnp.minimum(i // TILES_PER_SEQ, BATCH - 1), 0, chunk))
    s = pl.BlockSpec((TM, D), lambda i, *_: (jnp.maximum(i - NP_TILES, 0), chunk))
    return p, s


def _pick(i, p_ref, s_ref):
    return jnp.where(i < NP_TILES, p_ref[0], s_ref[...])


def _prenorm_kernel(x_ref, g_ref, shp_ref, shs_ref, scp_ref, scs_ref, h_ref):
    i = pl.program_id(0)
    h = _rms(x_ref[...]) * g_ref[...]
    h = h * (1.0 + _pick(i, scp_ref, scs_ref)) + _pick(i, shp_ref, shs_ref)
    h_ref[...] = h.astype(h_ref.dtype)


def _prenorm_call(x, g, mod_p, mod_s, sh_chunk, sc_chunk):
    shp, shs = _mod_specs(sh_chunk)
    scp, scs = _mod_specs(sc_chunk)
    tile = pl.BlockSpec((TM, D), lambda i: (i, 0))
    return pl.pallas_call(
        _prenorm_kernel,
        grid=(N_TILES,),
        in_specs=[tile, pl.BlockSpec((1, D), lambda i: (0, 0)), shp, shs, scp, scs],
        out_specs=tile,
        out_shape=jax.ShapeDtypeStruct((TT, D), BF16),
        compiler_params=_cparams(("arbitrary",)),
        name="prenorm",
    )(x, g.reshape(1, D), mod_p, mod_s, mod_p, mod_s)


def _prenorm_router_kernel(x_ref, g_ref, shp_ref, shs_ref, scp_ref, scs_ref, wr_ref, br_ref,
                           h_ref, idx_ref, wt_ref):
    i = pl.program_id(0)
    h = _rms(x_ref[...]) * g_ref[...]
    h = h * (1.0 + _pick(i, scp_ref, scs_ref)) + _pick(i, shp_ref, shs_ref)
    h_ref[...] = h
    lane = lax.broadcasted_iota(I32, (TM, LANES), 1).astype(F32)
    logits = jnp.dot(h, wr_ref[...], precision=HIGHEST, preferred_element_type=F32) + br_ref[...]
    logits = jnp.where(lane < N_EXPERTS, logits, -jnp.inf)
    m1 = jnp.max(logits, axis=-1, keepdims=True)
    i1 = jnp.min(jnp.where(logits == m1, lane, float(LANES)), axis=-1, keepdims=True)
    rest = jnp.where(lane == i1, -jnp.inf, logits)
    m2 = jnp.max(rest, axis=-1, keepdims=True)
    i2 = jnp.min(jnp.where(rest == m2, lane, float(LANES)), axis=-1, keepdims=True)
    e2 = jnp.exp(m2 - m1)
    w1 = 1.0 / (1.0 + e2)
    w2 = e2 / (1.0 + e2)
    idx_ref[...] = jnp.where(lane == 0, i1, jnp.where(lane == 1, i2, 0.0)).astype(I32)
    wt_ref[...] = jnp.where(lane == 0, w1, jnp.where(lane == 1, w2, 0.0))


def _prenorm_router_call(x, g, mod_p, mod_s, sh_chunk, sc_chunk, w_router, b_router):
    shp, shs = _mod_specs(sh_chunk)
    scp, scs = _mod_specs(sc_chunk)
    tile = pl.BlockSpec((TM, D), lambda i: (i, 0))
    small = pl.BlockSpec((TM, LANES), lambda i: (i, 0))
    wr = jnp.pad(w_router, ((0, 0), (0, LANES - N_EXPERTS)))
    br = jnp.pad(b_router, (0, LANES - N_EXPERTS)).reshape(1, LANES)
    return pl.pallas_call(
        _prenorm_router_kernel,
        grid=(N_TILES,),
        in_specs=[tile, pl.BlockSpec((1, D), lambda i: (0, 0)), shp, shs, scp, scs,
                  pl.BlockSpec((D, LANES), lambda i: (0, 0)),
                  pl.BlockSpec((1, LANES), lambda i: (0, 0))],
        out_specs=[tile, small, small],
        out_shape=[jax.ShapeDtypeStruct((TT, D), F32),
                   jax.ShapeDtypeStruct((TT, LANES), I32),
                   jax.ShapeDtypeStruct((TT, LANES), F32)],
        compiler_params=_cparams(("arbitrary",)),
        name="prenorm_router",
    )(x, g.reshape(1, D), mod_p, mod_s, mod_p, mod_s, wr, br)


def _proj_kernel(h_ref, w_ref, b_ref, o_ref, wbf_ref):
    @pl.when(pl.program_id(1) == 0)
    def _():
        wbf_ref[...] = w_ref[...].astype(BF16)

    o_ref[...] = jnp.dot(h_ref[...], wbf_ref[...], preferred_element_type=F32) + b_ref[...]


def _proj_call(h, w, b, tn):
    k, n = w.shape
    return pl.pallas_call(
        _proj_kernel,
        grid=(n // tn, N_TILES),
        in_specs=[
            pl.BlockSpec((TM, k), lambda j, i: (i, 0)),
            pl.BlockSpec((k, tn), lambda j, i: (0, j)),
            pl.BlockSpec((1, tn), lambda j, i: (0, j)),
        ],
        out_specs=pl.BlockSpec((TM, tn), lambda j, i: (i, j)),
        out_shape=jax.ShapeDtypeStruct((TT, n), F32),
        scratch_shapes=[pltpu.VMEM((k, tn), BF16)],
        compiler_params=_cparams(("arbitrary", "arbitrary")),
        name="proj",
    )(h, w, b.reshape(1, n))


def _post_mm_kernel(x_ref, op_ref, os_ref, w_ref, b_ref, g_ref, gtp_ref, gts_ref, out_ref, wbf_ref):
    i = pl.program_id(0)

    @pl.when(i == 0)
    def _():
        wbf_ref[...] = w_ref[...].astype(BF16)

    o = jnp.where(i < NP_TILES, op_ref[...], os_ref[...]).astype(BF16)
    y = jnp.dot(o, wbf_ref[...], preferred_element_type=F32) + b_ref[...]
    out_ref[...] = x_ref[...] + _pick(i, gtp_ref, gts_ref) * (_rms(y) * g_ref[...])


def _post_mm_call(x, o_p, o_s, w, b, g, mod_p, mod_s, gate_chunk):
    gtp, gts = _mod_specs(gate_chunk)
    tile = pl.BlockSpec((TM, D), lambda i: (i, 0))
    row = pl.BlockSpec((1, D), lambda i: (0, 0))
    return pl.pallas_call(
        _post_mm_kernel,
        grid=(N_TILES,),
        in_specs=[tile,
                  pl.BlockSpec((TM, D), lambda i: (jnp.minimum(i, NP_TILES - 1), 0)),
                  pl.BlockSpec((TM, D), lambda i: (jnp.maximum(i - NP_TILES, 0), 0)),
                  pl.BlockSpec((D, D), lambda i: (0, 0)), row, row, gtp, gts],
        out_specs=tile,
        out_shape=jax.ShapeDtypeStruct((TT, D), F32),
        scratch_shapes=[pltpu.VMEM((D, D), BF16)],
        compiler_params=_cparams(("arbitrary",)),
        name="post_mm",
    )(x, o_p, o_s, w, b.reshape(1, D), g.reshape(1, D), mod_p, mod_s)


def _post_kernel(x_ref, y_ref, g_ref, gtp_ref, gts_ref, out_ref):
    i = pl.program_id(0)
    out_ref[...] = x_ref[...] + _pick(i, gtp_ref, gts_ref) * (_rms(y_ref[...]) * g_ref[...])


def _post_call(x, y, g, mod_p, mod_s, gate_chunk):
    gtp, gts = _mod_specs(gate_chunk)
    tile = pl.BlockSpec((TM, D), lambda i: (i, 0))
    return pl.pallas_call(
        _post_kernel,
        grid=(N_TILES,),
        in_specs=[tile, tile, pl.BlockSpec((1, D), lambda i: (0, 0)), gtp, gts],
        out_specs=tile,
        out_shape=jax.ShapeDtypeStruct((TT, D), F32),
        compiler_params=_cparams(("arbitrary",)),
        name="post",
    )(x, y, g.reshape(1, D), mod_p, mod_s)


def _gdn_kernel(x_ref, ba_ref, cw_ref, alog_ref, dt_ref, nw_ref, cb_ref, s0_ref,
                o_ref, sfin_ref, xbuf, s_scr, *, chunk, n_chunks, t_real):
    c_sz = chunk
    tb = chunk * n_chunks
    j = pl.program_id(1)

    @pl.when(j == 0)
    def _():
        s_scr[...] = s0_ref[0]
        xbuf[0:SUBLANES, :] = cb_ref[0]

    xbuf[SUBLANES:SUBLANES + tb, :] = x_ref[:, 0:GDN_CONV_DIM]

    ii = lax.broadcasted_iota(I32, (c_sz, c_sz), 0)
    jj = lax.broadcasted_iota(I32, (c_sz, c_sz), 1)
    causal = ii >= jj
    strict = ii > jj
    tri = causal.astype(F32)
    eye_c = (ii == jj).astype(F32)
    pair_mask = [jnp.logical_and((ii >> (l + 1)) == (jj >> (l + 1)), (ii >> l) != (jj >> l))
                 for l in range(int(math.log2(c_sz)))]
    eye_t = (lax.broadcasted_iota(I32, (2 * SUBLANES, LANES), 0)
             == lax.broadcasted_iota(I32, (2 * SUBLANES, LANES), 1)).astype(F32)
    lane = lax.broadcasted_iota(I32, (c_sz, LANES), 1)
    row = lax.broadcasted_iota(I32, (c_sz, LANES), 0)
    cw = cw_ref[...]
    neg_a = -jnp.exp(alog_ref[...])
    dt = dt_ref[...]
    nw = nw_ref[...]

    def chunk_body(c, carry):
        r0 = pl.multiple_of(c * c_sz, c_sz)
        win = xbuf[pl.ds(r0, c_sz + SUBLANES), :]
        y = win[SUBLANES:SUBLANES + c_sz] * cw[3:4]
        for tap in range(1, CONV_WIDTH):
            y = y + win[SUBLANES - tap:SUBLANES - tap + c_sz] * cw[3 - tap:4 - tap]
        xc = _silu(y)

        ba = ba_ref[pl.ds(r0, c_sz), :]
        beta = jax.nn.sigmoid(ba)
        g = neg_a * jax.nn.softplus(ba + dt)
        if t_real < tb:
            live = (row + r0) < t_real
            beta = jnp.where(live, beta, 0.0)
            g = jnp.where(live, g, 0.0)
        gc = jnp.dot(tri, g, precision=HIGHEST, preferred_element_type=F32)
        cols = jnp.where(lane < GDN_V_HEADS, beta, gc)
        rows_t = lax.dot_general(eye_t, cols, (((1,), (1,)), ((), ())), precision=HIGHEST,
                                 preferred_element_type=F32)

        qn, kn, kk, qk = [], [], [], []
        for hq in range(GDN_QK_HEADS):
            q = xc[:, hq * GDN_HD:(hq + 1) * GDN_HD]
            k = xc[:, GDN_KEY_DIM + hq * GDN_HD:GDN_KEY_DIM + (hq + 1) * GDN_HD]
            q = q * lax.rsqrt(jnp.sum(q * q, axis=-1, keepdims=True) + RMS_EPS) * (GDN_HD ** -0.5)
            k = k * lax.rsqrt(jnp.sum(k * k, axis=-1, keepdims=True) + RMS_EPS)
            qn.append(q)
            kn.append(k)
            kk.append(_bdot_nt(k, k))
            qk.append(_bdot_nt(q, k))

        for h in range(GDN_V_HEADS):
            hq = h // (GDN_V_HEADS // GDN_QK_HEADS)
            gcc = gc[:, GDN_V_HEADS + h:GDN_V_HEADS + h + 1]
            gcr = rows_t[GDN_V_HEADS + h:GDN_V_HEADS + h + 1, :]
            bcol = beta[:, h:h + 1]
            gcl = gcc[c_sz - 1:c_sz, :]
            decay = jnp.exp(jnp.where(causal, gcc - gcr, -jnp.inf))
            a_mat = jnp.where(strict, bcol * kk[hq] * decay, 0.0)
            inv = eye_c - jnp.where(pair_mask[0], a_mat, 0.0)
            for lvl in range(1, len(pair_mask)):
                off = jnp.where(pair_mask[lvl], a_mat, 0.0)
                inv = inv - _bdot(inv, _bdot(off, inv))
            egc = jnp.exp(gcc)
            v = xc[:, 2 * GDN_KEY_DIM + h * GDN_HD:2 * GDN_KEY_DIM + (h + 1) * GDN_HD]
            rhs = jnp.concatenate([v * bcol, kn[hq] * (bcol * egc)], axis=1)
            sol = _bdot(inv, rhs)
            u0 = sol[:, :GDN_HD]
            w = sol[:, GDN_HD:]
            qkm = jnp.where(causal, qk[hq] * decay, 0.0)
            q_dec = qn[hq] * egc
            k_dec = kn[hq] * jnp.exp(gcl - gcc)
            s = s_scr[h]
            ws = _bdot(jnp.concatenate([w, q_dec], axis=0), s)
            u = u0 - ws[:c_sz]
            o = ws[c_sz:] + _bdot(qkm, u)
            s_scr[h] = s * jnp.exp(gcl) + lax.dot_general(
                k_dec.astype(BF16), u.astype(BF16), (((0,), (0,)), ((), ())),
                preferred_element_type=F32)
            z = x_ref[pl.ds(r0, c_sz), GDN_CONV_DIM + h * GDN_HD:GDN_CONV_DIM + (h + 1) * GDN_HD]
            o_ref[pl.ds(r0, c_sz), h * GDN_HD:(h + 1) * GDN_HD] = _rms(o) * nw * _silu(z)
        return carry

    lax.fori_loop(0, n_chunks, chunk_body, 0)
    xbuf[0:SUBLANES, :] = xbuf[tb:tb + SUBLANES, :]

    @pl.when(j == pl.num_programs(1) - 1)
    def _():
        sfin_ref[0] = s_scr[...]


def _gdn_call(x, ba, conv_w, a_log, dt_bias, norm_w, conv_buf, s0, *, n_seq, n_steps, chunk,
              n_chunks, t_real, batched_rows):
    tb = chunk * n_chunks
    pad = LANES - 2 * GDN_V_HEADS
    alog_row = jnp.pad(a_log, (GDN_V_HEADS, pad)).reshape(1, LANES)
    dt_row = jnp.pad(dt_bias, (GDN_V_HEADS, pad)).reshape(1, LANES)
    if batched_rows:
        x_spec = pl.BlockSpec((None, tb, GDN_MAIN_DIM), lambda s, j: (s, j, 0))
        ba_spec = pl.BlockSpec((None, tb, LANES), lambda s, j: (s, j, 0))
        o_spec = pl.BlockSpec((None, tb, GDN_VAL_DIM), lambda s, j: (s, j, 0))
        o_shape = jax.ShapeDtypeStruct((n_seq, tb * n_steps, GDN_VAL_DIM), F32)
    else:
        x_spec = pl.BlockSpec((tb, GDN_MAIN_DIM), lambda s, j: (s * n_steps + j, 0))
        ba_spec = pl.BlockSpec((tb, LANES), lambda s, j: (s * n_steps + j, 0))
        o_spec = pl.BlockSpec((tb, GDN_VAL_DIM), lambda s, j: (s * n_steps + j, 0))
        o_shape = jax.ShapeDtypeStruct((n_seq * n_steps * tb, GDN_VAL_DIM), F32)
    const = lambda s, j: (0, 0)
    state_spec = pl.BlockSpec((1, GDN_V_HEADS, GDN_HD, GDN_HD), lambda s, j: (s, 0, 0, 0))
    return pl.pallas_call(
        functools.partial(_gdn_kernel, chunk=chunk, n_chunks=n_chunks, t_real=t_real),
        grid=(n_seq, n_steps),
        in_specs=[x_spec, ba_spec,
                  pl.BlockSpec((CONV_WIDTH, GDN_CONV_DIM), const),
                  pl.BlockSpec((1, LANES), const), pl.BlockSpec((1, LANES), const),
                  pl.BlockSpec((1, GDN_HD), const),
                  pl.BlockSpec((1, SUBLANES, GDN_CONV_DIM), lambda s, j: (s, 0, 0)),
                  state_spec],
        out_specs=[o_spec, state_spec],
        out_shape=[o_shape,
                   jax.ShapeDtypeStruct((n_seq, GDN_V_HEADS, GDN_HD, GDN_HD), F32)],
        scratch_shapes=[pltpu.VMEM((tb + SUBLANES, GDN_CONV_DIM), F32),
                        pltpu.VMEM((GDN_V_HEADS, GDN_HD, GDN_HD), F32)],
        compiler_params=_cparams(("arbitrary", "arbitrary")),
        name="gdn",
    )(x, ba, conv_w, alog_row, dt_row, norm_w.reshape(1, GDN_HD), conv_buf, s0)


def _t5_bucket_np(dist):
    dist = np.maximum(dist, 0)
    max_exact = REL_BUCKETS // 2
    ratio = np.log(np.maximum(dist, max_exact).astype(np.float32) / np.float32(max_exact)) \
        / np.float32(math.log(REL_MAX_DIST / max_exact))
    large = max_exact + (ratio.astype(np.float32) * np.float32(REL_BUCKETS - max_exact)).astype(np.int32)
    return np.where(dist < max_exact, dist, np.minimum(large, REL_BUCKETS - 1)).astype(np.int32)


def _bucket_table(qpos, kpos, k_valid):
    dist = qpos[:, None] - kpos[None, :]
    ok = (dist >= 0) & (dist < WINDOW) & (kpos[None, :] >= 0) & k_valid[None, :]
    return np.where(ok, _t5_bucket_np(dist), -1).astype(np.int32)


def _bias_from_buckets(bkt, rb_ref, head):
    def body(b, acc):
        return jnp.where(bkt == b, rb_ref[b, head], acc)
    acc = lax.fori_loop(0, REL_BUCKETS, body, jnp.zeros(bkt.shape, F32))
    return jnp.where(bkt < 0, -jnp.inf, acc)


def _swa_prompt_kernel(rb_ref, sk_ref, bkt_ref, q_ref, kvp_ref, kvc_ref, o_ref, bias_scr):
    first = jnp.logical_and(pl.program_id(0) == 0, pl.program_id(1) == 0)
    n = pl.program_id(1)

    @pl.when(first)
    def _():
        bkt = bkt_ref[...]
        for h in range(SWA_Q_HEADS):
            kv, g = divmod(h, SWA_GROUP)
            bias_scr[kv, g * WINDOW:(g + 1) * WINDOW, :] = _bias_from_buckets(bkt, rb_ref, h)

    q = q_ref[...]
    scale = SWA_HD ** -0.5
    outs = []
    for kv in range(SWA_KV_HEADS):
        qs = jnp.concatenate(
            [q[:, (kv * SWA_GROUP + g) * SWA_HD:(kv * SWA_GROUP + g + 1) * SWA_HD]
             for g in range(SWA_GROUP)], axis=0)
        k_prev = kvp_ref[:, kv * SWA_HD:(kv + 1) * SWA_HD]
        v_prev = kvp_ref[:, SWA_KV_DIM + kv * SWA_HD:SWA_KV_DIM + (kv + 1) * SWA_HD]
        k_cur = kvc_ref[:, kv * SWA_HD:(kv + 1) * SWA_HD]
        v_cur = kvc_ref[:, SWA_KV_DIM + kv * SWA_HD:SWA_KV_DIM + (kv + 1) * SWA_HD]
        s_prev = _bdot_nt(qs, k_prev) * scale + bias_scr[kv, :, 0:WINDOW]
        s_prev = jnp.where(n > 0, s_prev, -jnp.inf)
        s_cur = _bdot_nt(qs, k_cur) * scale + bias_scr[kv, :, WINDOW:2 * WINDOW]
        sink = jnp.concatenate(
            [jnp.full((WINDOW, 1), sk_ref[kv * SWA_GROUP + g], F32) for g in range(SWA_GROUP)], axis=0)
        m = jnp.maximum(jnp.maximum(jnp.max(s_prev, axis=-1, keepdims=True),
                                    jnp.max(s_cur, axis=-1, keepdims=True)), sink)
        p_prev = jnp.exp(s_prev - m)
        p_cur = jnp.exp(s_cur - m)
        den = (jnp.sum(p_prev, axis=-1, keepdims=True) + jnp.sum(p_cur, axis=-1, keepdims=True)
               + jnp.exp(sink - m))
        o = (_bdot(p_prev, v_prev) + _bdot(p_cur, v_cur)) / den
        outs.extend(o[g * WINDOW:(g + 1) * WINDOW] for g in range(SWA_GROUP))
    o_ref[...] = jnp.concatenate(outs, axis=1)


def _swa_prompt_call(qkv, rel_bias, sinks):
    nb = SEQ // WINDOW
    qpos = WINDOW + np.arange(WINDOW)
    kpos = np.arange(2 * WINDOW)
    bkt = jnp.asarray(_bucket_table(qpos, kpos, np.ones(2 * WINDOW, bool)))
    kv_col = SWA_Q_DIM // (2 * SWA_KV_DIM)
    smem = pl.BlockSpec(memory_space=pltpu.SMEM)
    return pl.pallas_call(
        _swa_prompt_kernel,
        grid=(BATCH, nb),
        in_specs=[smem, smem,
                  pl.BlockSpec((WINDOW, 2 * WINDOW), lambda b, n: (0, 0)),
                  pl.BlockSpec((WINDOW, SWA_Q_DIM), lambda b, n: (b * nb + n, 0)),
                  pl.BlockSpec((WINDOW, 2 * SWA_KV_DIM),
                               lambda b, n: (jnp.maximum(b * nb + n - 1, 0), kv_col)),
                  pl.BlockSpec((WINDOW, 2 * SWA_KV_DIM), lambda b, n: (b * nb + n, kv_col))],
        out_specs=pl.BlockSpec((WINDOW, SWA_Q_DIM), lambda b, n: (b * nb + n, 0)),
        out_shape=jax.ShapeDtypeStruct((NP_TOK, SWA_Q_DIM), F32),
        scratch_shapes=[pltpu.VMEM((SWA_KV_HEADS, SWA_GROUP * WINDOW, 2 * WINDOW), F32)],
        compiler_params=_cparams(("arbitrary", "arbitrary")),
        name="swa_prompt",
    )(rel_bias, sinks, bkt, qkv, qkv, qkv)


def _swa_sample_kernel(rb_ref, sk_ref, bkt_ref, q_ref, k_ref, v_ref, o_ref, bias_scr, sink_scr):
    rows = SWA_GROUP * DEC_SEQ

    def group_of_row(shape):
        r = lax.broadcasted_iota(I32, shape, 0)
        return sum((r >= g * DEC_SEQ).astype(I32) for g in range(1, SWA_GROUP))

    @pl.when(pl.program_id(0) == 0)
    def _():
        bkt = bkt_ref[...]
        grp = group_of_row((rows, 1))
        grp_full = group_of_row(bkt.shape)
        for kv in range(SWA_KV_HEADS):
            acc = jnp.zeros(bkt.shape, F32)
            snk = jnp.zeros((rows, 1), F32)
            for g in range(SWA_GROUP):
                head = kv * SWA_GROUP + g
                acc = jnp.where(grp_full == g, _bias_from_buckets(bkt, rb_ref, head), acc)
                snk = jnp.where(grp == g, sk_ref[head], snk)
            bias_scr[kv] = acc
            sink_scr[kv] = snk

    scale = SWA_HD ** -0.5
    for s in range(SWA_SB):
        for kv in range(SWA_KV_HEADS):
            q = q_ref[s, kv]
            k = k_ref[s, :, kv * SWA_HD:(kv + 1) * SWA_HD]
            v = v_ref[s, :, kv * SWA_HD:(kv + 1) * SWA_HD]
            sc = _bdot_nt(q, k) * scale + bias_scr[kv]
            sink = sink_scr[kv]
            m = jnp.maximum(jnp.max(sc, axis=-1, keepdims=True), sink)
            p = jnp.exp(sc - m)
            den = jnp.sum(p, axis=-1, keepdims=True) + jnp.exp(sink - m)
            o_ref[s, kv] = _bdot(p, v) / den


def _swa_sample_call(q_st, kc, vc, rel_bias, sinks):
    rows = SWA_GROUP * DEC_SEQ
    n_keys = kc.shape[1] - (SWA_KPAD - WINDOW) + DEC_SEQ
    qpos = PAST_LEN + np.arange(DEC_SEQ)
    kpos = PAST_LEN - (n_keys - DEC_SEQ) + np.arange(SWA_KPAD)
    bkt4 = _bucket_table(qpos, kpos, np.arange(SWA_KPAD) < n_keys)
    bkt = jnp.asarray(np.tile(bkt4, (SWA_GROUP, 1)))
    smem = pl.BlockSpec(memory_space=pltpu.SMEM)
    q_spec = pl.BlockSpec((SWA_SB, SWA_KV_HEADS, rows, SWA_HD), lambda i: (i, 0, 0, 0))
    kv_spec = pl.BlockSpec((SWA_SB, SWA_KPAD, SWA_KV_DIM), lambda i: (i, 0, 0))
    return pl.pallas_call(
        _swa_sample_kernel,
        grid=(DEC_BATCH // SWA_SB,),
        in_specs=[smem, smem, pl.BlockSpec((rows, SWA_KPAD), lambda i: (0, 0)),
                  q_spec, kv_spec, kv_spec],
        out_specs=q_spec,
        out_shape=jax.ShapeDtypeStruct((DEC_BATCH, SWA_KV_HEADS, rows, SWA_HD), F32),
        scratch_shapes=[pltpu.VMEM((SWA_KV_HEADS, rows, SWA_KPAD), F32),
                        pltpu.VMEM((SWA_KV_HEADS, rows, 1), F32)],
        compiler_params=_cparams(("arbitrary",)),
        name="swa_sample",
    )(rel_bias, sinks, bkt, q_st, kc, vc)


def _ffn_kernel(te_ref, nt_ref, x_ref, wg_ref, wu_ref, wd_ref, rw_ref, o_ref, acc_ref):
    t = pl.program_id(0)

    @pl.when(t < nt_ref[0])
    def _():
        x = x_ref[...].astype(BF16)
        acc_ref[...] = jnp.zeros_like(acc_ref)

        def body(f, carry):
            gate = jnp.dot(x, wg_ref[f], preferred_element_type=F32)
            up = jnp.dot(x, wu_ref[f], preferred_element_type=F32)
            act = (_silu(gate) * up).astype(BF16)
            acc_ref[...] += jnp.dot(act, wd_ref[f], preferred_element_type=F32)
            return carry

        lax.fori_loop(0, NF, body, 0)
        o_ref[...] = acc_ref[...] * rw_ref[...]

    @pl.when(t >= nt_ref[0])
    def _():
        o_ref[...] = jnp.zeros_like(o_ref)


def _ffn_weights(w_gate, w_up, w_down):
    e = w_gate.shape[0]
    wg = w_gate.astype(BF16).reshape(e, D, NF, TF).transpose(0, 2, 1, 3)
    wu = w_up.astype(BF16).reshape(e, D, NF, TF).transpose(0, 2, 1, 3)
    wd = w_down.astype(BF16).reshape(e, NF, TF, D)
    return wg, wu, wd


def _ffn_call(x, tile_expert, n_tiles_used, row_w, wg, wu, wd):
    n_rows = x.shape[0]
    n_tiles = n_rows // TM
    up_spec = pl.BlockSpec((None, NF, D, TF), lambda t, te, nt: (te[t], 0, 0, 0))
    dn_spec = pl.BlockSpec((None, NF, TF, D), lambda t, te, nt: (te[t], 0, 0, 0))
    grid_spec = pltpu.PrefetchScalarGridSpec(
        num_scalar_prefetch=2,
        grid=(n_tiles,),
        in_specs=[pl.BlockSpec((TM, D), lambda t, te, nt: (t, 0)), up_spec, up_spec, dn_spec,
                  pl.BlockSpec((TM, 1), lambda t, te, nt: (t, 0))],
        out_specs=pl.BlockSpec((TM, D), lambda t, te, nt: (t, 0)),
        scratch_shapes=[pltpu.VMEM((TM, D), F32)],
    )
    return pl.pallas_call(
        _ffn_kernel,
        grid_spec=grid_spec,
        out_shape=jax.ShapeDtypeStruct((n_rows, D), F32),
        compiler_params=_cparams(("arbitrary",)),
        name="ffn",
    )(tile_expert, n_tiles_used, x, wg, wu, wd, row_w.reshape(n_rows, 1))


def _row_copy(src_ref, dst_ref, src_row, dst_row, sem):
    return pltpu.make_async_copy(src_ref.at[pl.ds(src_row, 1)], dst_ref.at[pl.ds(dst_row, 1)], sem)


def _gather_kernel(idx_ref, src_ref, out_ref, sem):
    def start(r, carry):
        _row_copy(src_ref, out_ref, idx_ref[0, 0, r], r, sem).start()
        return carry

    def wait(r, carry):
        _row_copy(src_ref, out_ref, 0, r, sem).wait()
        return carry

    lax.fori_loop(0, TM, start, 0)
    lax.fori_loop(0, TM, wait, 0)


def _gather_call(src, idx):
    n_rows = idx.shape[0]
    n_tiles = n_rows // TM
    return pl.pallas_call(
        _gather_kernel,
        grid=(n_tiles,),
        in_specs=[pl.BlockSpec((1, 1, TM), lambda i: (i, 0, 0), memory_space=pltpu.SMEM),
                  pl.BlockSpec(memory_space=pl.ANY)],
        out_specs=pl.BlockSpec((TM, D), lambda i: (i, 0)),
        out_shape=jax.ShapeDtypeStruct((n_rows, D), src.dtype),
        scratch_shapes=[pltpu.SemaphoreType.DMA(())],
        compiler_params=_cparams(("arbitrary",)),
        name="gather_rows",
    )(idx.reshape(n_tiles, 1, TM), src)


def _combine_kernel(d0_ref, d1_ref, x_ref, ys_ref, g_ref, gtp_ref, gts_ref, out_ref, buf, sem):
    i = pl.program_id(0)

    def start(r, carry):
        _row_copy(ys_ref, buf.at[0], d0_ref[0, 0, r], r, sem).start()
        _row_copy(ys_ref, buf.at[1], d1_ref[0, 0, r], r, sem).start()
        return carry

    def wait(r, carry):
        _row_copy(ys_ref, buf.at[0], 0, r, sem).wait()
        _row_copy(ys_ref, buf.at[1], 0, r, sem).wait()
        return carry

    lax.fori_loop(0, TM, start, 0)
    lax.fori_loop(0, TM, wait, 0)
    y = buf[0] + buf[1]
    out_ref[...] = x_ref[...] + _pick(i, gtp_ref, gts_ref) * (_rms(y) * g_ref[...])


def _combine_call(x, ys, dest, g, mod_p, mod_s, gate_chunk):
    gtp, gts = _mod_specs(gate_chunk)
    tile = pl.BlockSpec((TM, D), lambda i: (i, 0))
    idx_spec = pl.BlockSpec((1, 1, TM), lambda i: (i, 0, 0), memory_space=pltpu.SMEM)
    return pl.pallas_call(
        _combine_kernel,
        grid=(N_TILES,),
        in_specs=[idx_spec, idx_spec, tile, pl.BlockSpec(memory_space=pl.ANY),
                  pl.BlockSpec((1, D), lambda i: (0, 0)), gtp, gts],
        out_specs=tile,
        out_shape=jax.ShapeDtypeStruct((TT, D), F32),
        scratch_shapes=[pltpu.VMEM((2, TM, D), F32), pltpu.SemaphoreType.DMA(())],
        compiler_params=_cparams(("arbitrary",)),
        name="moe_combine",
    )(dest[0].reshape(N_TILES, 1, TM), dest[1].reshape(N_TILES, 1, TM), x, ys,
      g.reshape(1, D), mod_p, mod_s)


def _route(top_idx, top_w):
    flat_e = top_idx.T.reshape(-1)
    flat_w = top_w.T.reshape(-1)
    onehot = (flat_e[:, None] == jnp.arange(N_EXPERTS, dtype=I32)[None, :]).astype(I32)
    csum = jnp.cumsum(onehot, axis=0)
    rank = jnp.sum((csum - onehot) * onehot, axis=1)
    counts = csum[-1]
    padded = ((counts + TM - 1) // TM) * TM
    ends = jnp.cumsum(padded)
    starts = ends - padded
    dest = (starts[flat_e] + rank).astype(I32)
    token = jnp.arange(2 * TT, dtype=I32) % TT
    src_tok = jnp.zeros((MOE_ROWS,), I32).at[dest].set(token, unique_indices=True)
    row_w = jnp.zeros((MOE_ROWS,), F32).at[dest].set(flat_w, unique_indices=True)
    tile_start = jnp.arange(MOE_TILES, dtype=I32) * TM
    tile_expert = jnp.minimum(jnp.sum(tile_start[:, None] >= ends[None, :], axis=1),
                              N_EXPERTS - 1).astype(I32)
    n_used = (ends[-1] // TM).astype(I32).reshape(1)
    return dest.reshape(2, TT), src_tok, row_w, tile_expert, n_used


def _gdn_layer(x, h, w_in, conv_w, a_log, dt_bias, norm_w, w_out, state_conv, state_rec,
               g_post, mod_p, mod_s):
    w_main = w_in[:, :GDN_MAIN_DIM]
    w_ba = jnp.pad(w_in[:, GDN_MAIN_DIM:], ((0, 0), (0, LANES - 2 * GDN_V_HEADS)))
    qkvz = _proj_call(h, w_main, jnp.zeros((GDN_MAIN_DIM,), F32), 1024)
    ba = _proj_call(h, w_ba, jnp.zeros((LANES,), F32), LANES)

    n_steps = SEQ // (GDN_CHUNK * GDN_NB)
    o_p, rec_p = _gdn_call(
        qkvz, ba, conv_w, a_log, dt_bias, norm_w,
        jnp.zeros((BATCH, SUBLANES, GDN_CONV_DIM), F32),
        jnp.zeros((BATCH, GDN_V_HEADS, GDN_HD, GDN_HD), F32),
        n_seq=BATCH, n_steps=n_steps, chunk=GDN_CHUNK, n_chunks=GDN_NB,
        t_real=GDN_CHUNK * GDN_NB, batched_rows=False)

    row_pad = ((0, 0), (0, GDN_SAMPLE_ROWS - DEC_SEQ), (0, 0))
    qkvz_s = qkvz[NP_TOK:].reshape(DEC_BATCH, DEC_SEQ, GDN_MAIN_DIM)
    ba_s = ba[NP_TOK:].reshape(DEC_BATCH, DEC_SEQ, LANES)
    cbuf_s = jnp.pad(state_conv, ((0, 0), (SUBLANES - (CONV_WIDTH - 1), 0), (0, 0)))
    o_s, rec_s = _gdn_call(
        jnp.pad(qkvz_s, row_pad), jnp.pad(ba_s, row_pad), conv_w, a_log, dt_bias, norm_w,
        cbuf_s, state_rec,
        n_seq=DEC_BATCH, n_steps=1, chunk=GDN_SAMPLE_ROWS, n_chunks=1,
        t_real=DEC_SEQ, batched_rows=True)
    o_s = o_s[:, :DEC_SEQ].reshape(NS_TOK, GDN_VAL_DIM)

    pre_p = qkvz[:NP_TOK].reshape(BATCH, SEQ, GDN_MAIN_DIM)[:, SEQ - (CONV_WIDTH - 1):, :GDN_CONV_DIM]
    pre_s = jnp.concatenate([state_conv, qkvz_s[:, :, :GDN_CONV_DIM]], axis=1)[:, -(CONV_WIDTH - 1):]
    x = _post_mm_call(x, o_p, o_s, w_out, jnp.zeros((D,), F32), g_post, mod_p, mod_s, 2)
    return x, pre_p, rec_p, pre_s, rec_s


def _swa_layer(x, h, w_in, b_in, sinks, w_out, b_out, rel_bias, cache_k, cache_v,
               g_post, mod_p, mod_s):
    qkv = _proj_call(h, w_in, b_in, SWA_PROJ_DIM)
    o_p = _swa_prompt_call(qkv, rel_bias, sinks)
    qkv_p = qkv[:NP_TOK].reshape(BATCH, SEQ, SWA_PROJ_DIM)
    new_k_p = qkv_p[:, SEQ - WINDOW:, SWA_Q_DIM:SWA_Q_DIM + SWA_KV_DIM]
    new_v_p = qkv_p[:, SEQ - WINDOW:, SWA_Q_DIM + SWA_KV_DIM:]

    qkv_s = qkv[NP_TOK:].reshape(DEC_BATCH, DEC_SEQ, SWA_PROJ_DIM)
    q_st = qkv_s[:, :, :SWA_Q_DIM].reshape(DEC_BATCH, DEC_SEQ, SWA_KV_HEADS, SWA_GROUP, SWA_HD)
    q_st = q_st.transpose(0, 2, 3, 1, 4).reshape(DEC_BATCH, SWA_KV_HEADS, SWA_GROUP * DEC_SEQ, SWA_HD)
    wb = cache_k.shape[1]
    zpad = jnp.zeros((DEC_BATCH, SWA_KPAD - WINDOW - DEC_SEQ + (WINDOW - wb), SWA_KV_DIM), F32)
    kc = jnp.concatenate([cache_k.reshape(DEC_BATCH, wb, SWA_KV_DIM),
                          qkv_s[:, :, SWA_Q_DIM:SWA_Q_DIM + SWA_KV_DIM], zpad], axis=1)
    vc = jnp.concatenate([cache_v.reshape(DEC_BATCH, wb, SWA_KV_DIM),
                          qkv_s[:, :, SWA_Q_DIM + SWA_KV_DIM:], zpad], axis=1)
    o_st = _swa_sample_call(q_st, kc, vc, rel_bias, sinks)
    o_s = o_st.reshape(DEC_BATCH, SWA_KV_HEADS, SWA_GROUP, DEC_SEQ, SWA_HD)
    o_s = o_s.transpose(0, 3, 1, 2, 4).reshape(NS_TOK, SWA_Q_DIM)
    new_k_s = kc[:, DEC_SEQ:DEC_SEQ + wb]
    new_v_s = vc[:, DEC_SEQ:DEC_SEQ + wb]

    x = _post_mm_call(x, o_p, o_s, w_out, b_out, g_post, mod_p, mod_s, 2)
    shape_p = (BATCH, WINDOW, SWA_KV_HEADS, SWA_HD)
    shape_s = (DEC_BATCH, wb, SWA_KV_HEADS, SWA_HD)
    return (x, new_k_p.reshape(shape_p), new_v_p.reshape(shape_p),
            new_k_s.reshape(shape_s), new_v_s.reshape(shape_s))


def kernel(x_prompt, x_sample, c_prompt, c_sample, state_conv, state_rec, cache_win_k, cache_win_v, w_mod, b_mod, g_pre_mix, g_post_mix, g_pre_ffn, g_post_ffn, gdn_w_in, gdn_conv_w, gdn_a_log, gdn_dt_bias, gdn_norm_w, gdn_w_out, swa_w_in, swa_b_in, swa_sinks, swa_w_out, swa_b_out, rel_bias, ffn_w_gate, ffn_w_up, ffn_w_down, moe_w_router, moe_b_router, moe_w_gate, moe_w_up, moe_w_down):
    x = jnp.concatenate([x_prompt.reshape(NP_TOK, D), x_sample.reshape(NS_TOK, D)], axis=0)
    n_c = BATCH + DEC_BATCH
    c_all = jnp.concatenate([c_prompt, c_sample, jnp.zeros((-n_c % SUBLANES, D), F32)], axis=0)
    m_all = _mod_call(c_all, w_mod, b_mod)

    conv_p, rec_p, conv_s, rec_s = [], [], [], []
    k_p, v_p, k_s, v_s = [], [], [], []
    for layer in range(DEPTH):
        j = layer // 2
        mod_p = m_all[layer, :BATCH].reshape(BATCH, 1, 6 * D)
        mod_s = jnp.repeat(m_all[layer, BATCH:n_c], DEC_SEQ, axis=0)
        h = _prenorm_call(x, g_pre_mix[layer], mod_p, mod_s, 0, 1)
        if layer % 2 == 0:
            x, cp, rp, cs, rs = _gdn_layer(
                x, h, gdn_w_in[j], gdn_conv_w[j], gdn_a_log[j], gdn_dt_bias[j], gdn_norm_w[j],
                gdn_w_out[j], state_conv[j], state_rec[j], g_post_mix[layer], mod_p, mod_s)
            conv_p.append(cp); rec_p.append(rp); conv_s.append(cs); rec_s.append(rs)
        else:
            x, kp, vp, ks, vs = _swa_layer(
                x, h, swa_w_in[j], swa_b_in[j], swa_sinks[j], swa_w_out[j], swa_b_out[j], rel_bias,
                cache_win_k[j], cache_win_v[j], g_post_mix[layer], mod_p, mod_s)
            k_p.append(kp); v_p.append(vp); k_s.append(ks); v_s.append(vs)

        if layer % 2 == 0:
            h = _prenorm_call(x, g_pre_ffn[layer], mod_p, mod_s, 3, 4)
            wg, wu, wd = _ffn_weights(ffn_w_gate[j][None], ffn_w_up[j][None], ffn_w_down[j][None])
            y = _ffn_call(h, jnp.zeros((N_TILES,), I32), jnp.full((1,), N_TILES, I32),
                          jnp.ones((TT,), F32), wg, wu, wd)
            x = _post_call(x, y, g_post_ffn[layer], mod_p, mod_s, 5)
        else:
            h, idx, wts = _prenorm_router_call(x, g_pre_ffn[layer], mod_p, mod_s, 3, 4,
                                               moe_w_router[j], moe_b_router[j])
            dest, src_tok, row_w, tile_expert, n_used = _route(idx[:, :2], wts[:, :2])
            xs = _gather_call(h, src_tok)
            wg, wu, wd = _ffn_weights(moe_w_gate[j], moe_w_up[j], moe_w_down[j])
            ys = _ffn_call(xs, tile_expert, n_used, row_w, wg, wu, wd)
            x = _combine_call(x, ys, dest, g_post_ffn[layer], mod_p, mod_s, 5)

    y_prompt = x[:NP_TOK].reshape(BATCH, SEQ, D)
    y_sample = x[NP_TOK:].reshape(DEC_BATCH, DEC_SEQ, D)
    return (y_prompt, y_sample, jnp.stack(conv_p), jnp.stack(rec_p), jnp.stack(k_p), jnp.stack(v_p),
            jnp.stack(conv_s), jnp.stack(rec_s), jnp.stack(k_s), jnp.stack(v_s))
```

```python
import functools
import math

import numpy as np
import jax
import jax.numpy as jnp
from jax import lax
from jax.experimental import pallas as pl
from jax.experimental.pallas import tpu as pltpu

F32 = jnp.float32
BF16 = jnp.bfloat16
I32 = jnp.int32
HIGHEST = lax.Precision.HIGHEST

D = 1024
BATCH = 4
SEQ = 4096
DEPTH = 4
DEC_BATCH = 128
DEC_SEQ = 4
PAST_LEN = 8192
GDN_QK_HEADS = 4
GDN_V_HEADS = 8
GDN_HD = 128
GDN_KEY_DIM = GDN_QK_HEADS * GDN_HD
GDN_VAL_DIM = GDN_V_HEADS * GDN_HD
GDN_CONV_DIM = 2 * GDN_KEY_DIM + GDN_VAL_DIM
GDN_MAIN_DIM = GDN_CONV_DIM + GDN_VAL_DIM
CONV_WIDTH = 4
GDN_CHUNK = 64
SWA_Q_HEADS = 16
SWA_KV_HEADS = 4
SWA_GROUP = SWA_Q_HEADS // SWA_KV_HEADS
SWA_HD = 64
SWA_Q_DIM = SWA_Q_HEADS * SWA_HD
SWA_KV_DIM = SWA_KV_HEADS * SWA_HD
SWA_PROJ_DIM = SWA_Q_DIM + 2 * SWA_KV_DIM
WINDOW = 128
REL_BUCKETS = 32
REL_MAX_DIST = 128
D_FF = 2816
N_EXPERTS = 8
RMS_EPS = 1e-6

LANES = 128
SUBLANES = 8
VMEM_LIMIT = 56 * 1024 * 1024

TM = 512
NP_TOK = BATCH * SEQ
NS_TOK = DEC_BATCH * DEC_SEQ
TT = NP_TOK + NS_TOK
NP_TILES = NP_TOK // TM
NS_TILES = NS_TOK // TM
N_TILES = NP_TILES + NS_TILES
TILES_PER_SEQ = SEQ // TM
TF = 256
NF = D_FF // TF
MOE_ROWS = 2 * TT + N_EXPERTS * TM
MOE_TILES = MOE_ROWS // TM
GDN_NB = 4
GDN_PROMPT_PAR = 2
GDN_SAMPLE_PAR = 4
GDN_SAMPLE_ROWS = 8
SWA_SB = 8
SWA_KPAD = WINDOW + 8

assert NP_TOK % TM == 0 and NS_TOK % TM == 0 and SEQ % TM == 0
assert D_FF % TF == 0 and DEC_SEQ <= GDN_SAMPLE_ROWS and DEC_SEQ <= 8


def _cparams(sem):
    return pltpu.CompilerParams(dimension_semantics=sem, vmem_limit_bytes=VMEM_LIMIT)


def _bdot(a, b):
    return jnp.dot(a.astype(BF16), b.astype(BF16), preferred_element_type=F32)


def _bdot_nt(a, b):
    return lax.dot_general(a.astype(BF16), b.astype(BF16), (((1,), (1,)), ((), ())),
                           preferred_element_type=F32)


def _silu(x):
    return x * jax.nn.sigmoid(x)


def _rms(x):
    return x * lax.rsqrt(jnp.mean(x * x, axis=-1, keepdims=True) + RMS_EPS)


def _mod_kernel(c_ref, w_ref, b_ref, o_ref):
    c = c_ref[...]
    o_ref[0] = _bdot(_silu(c), w_ref[0]) + b_ref[0]


def _mod_call(c_all, w_mod, b_mod):
    n = c_all.shape[0]
    tn = D
    return pl.pallas_call(
        _mod_kernel,
        grid=(DEPTH, 6 * D // tn),
        in_specs=[
            pl.BlockSpec((n, D), lambda l, j: (0, 0)),
            pl.BlockSpec((1, D, tn), lambda l, j: (l, 0, j)),
            pl.BlockSpec((1, 1, tn), lambda l, j: (l, 0, j)),
        ],
        out_specs=pl.BlockSpec((1, n, tn), lambda l, j: (l, 0, j)),
        out_shape=jax.ShapeDtypeStruct((DEPTH, n, 6 * D), F32),
        compiler_params=_cparams(("arbitrary", "arbitrary")),
        name="modulation",
    )(c_all, w_mod, b_mod.reshape(DEPTH, 1, 6 * D))


def _mod_specs(chunk):
    p = pl.BlockSpec((1, 1, D), lambda i, *_: (jnp.minimum(i // TILES_PER_SEQ, BATCH - 1), 0, chunk))
    s = pl.BlockSpec((TM, D), lambda i, *_: (jnp.maximum(i - NP_TILES, 0), chunk))
    return p, s


def _pick(i, p_ref, s_ref):
    return jnp.where(i < NP_TILES, p_ref[0], s_ref[...])


def _prenorm_kernel(x_ref, g_ref, shp_ref, shs_ref, scp_ref, scs_ref, h_ref):
    i = pl.program_id(0)
    h = _rms(x_ref[...]) * g_ref[...]
    h = h * (1.0 + _pick(i, scp_ref, scs_ref)) + _pick(i, shp_ref, shs_ref)
    h_ref[...] = h.astype(h_ref.dtype)


def _prenorm_call(x, g, mod_p, mod_s, sh_chunk, sc_chunk):
    shp, shs = _mod_specs(sh_chunk)
    scp, scs = _mod_specs(sc_chunk)
    tile = pl.BlockSpec((TM, D), lambda i: (i, 0))
    return pl.pallas_call(
        _prenorm_kernel,
        grid=(N_TILES,),
        in_specs=[tile, pl.BlockSpec((1, D), lambda i: (0, 0)), shp, shs, scp, scs],
        out_specs=tile,
        out_shape=jax.ShapeDtypeStruct((TT, D), BF16),
        compiler_params=_cparams(("arbitrary",)),
        name="prenorm",
    )(x, g.reshape(1, D), mod_p, mod_s, mod_p, mod_s)


def _prenorm_router_kernel(x_ref, g_ref, shp_ref, shs_ref, scp_ref, scs_ref, wr_ref, br_ref,
                           h_ref, idx_ref, wt_ref):
    i = pl.program_id(0)
    h = _rms(x_ref[...]) * g_ref[...]
    h = h * (1.0 + _pick(i, scp_ref, scs_ref)) + _pick(i, shp_ref, shs_ref)
    h_ref[...] = h
    lane = lax.broadcasted_iota(I32, (TM, LANES), 1).astype(F32)
    logits = jnp.dot(h, wr_ref[...], precision=HIGHEST, preferred_element_type=F32) + br_ref[...]
    logits = jnp.where(lane < N_EXPERTS, logits, -jnp.inf)
    m1 = jnp.max(logits, axis=-1, keepdims=True)
    i1 = jnp.min(jnp.where(logits == m1, lane, float(LANES)), axis=-1, keepdims=True)
    rest = jnp.where(lane == i1, -jnp.inf, logits)
    m2 = jnp.max(rest, axis=-1, keepdims=True)
    i2 = jnp.min(jnp.where(rest == m2, lane, float(LANES)), axis=-1, keepdims=True)
    e2 = jnp.exp(m2 - m1)
    w1 = 1.0 / (1.0 + e2)
    w2 = e2 / (1.0 + e2)
    idx_ref[...] = jnp.where(lane == 0, i1, jnp.where(lane == 1, i2, 0.0)).astype(I32)
    wt_ref[...] = jnp.where(lane == 0, w1, jnp.where(lane == 1, w2, 0.0))


def _prenorm_router_call(x, g, mod_p, mod_s, sh_chunk, sc_chunk, w_router, b_router):
    shp, shs = _mod_specs(sh_chunk)
    scp, scs = _mod_specs(sc_chunk)
    tile = pl.BlockSpec((TM, D), lambda i: (i, 0))
    small = pl.BlockSpec((TM, LANES), lambda i: (i, 0))
    wr = jnp.pad(w_router, ((0, 0), (0, LANES - N_EXPERTS)))
    br = jnp.pad(b_router, (0, LANES - N_EXPERTS)).reshape(1, LANES)
    return pl.pallas_call(
        _prenorm_router_kernel,
        grid=(N_TILES,),
        in_specs=[tile, pl.BlockSpec((1, D), lambda i: (0, 0)), shp, shs, scp, scs,
                  pl.BlockSpec((D, LANES), lambda i: (0, 0)),
                  pl.BlockSpec((1, LANES), lambda i: (0, 0))],
        out_specs=[tile, small, small],
        out_shape=[jax.ShapeDtypeStruct((TT, D), F32),
                   jax.ShapeDtypeStruct((TT, LANES), I32),
                   jax.ShapeDtypeStruct((TT, LANES), F32)],
        compiler_params=_cparams(("arbitrary",)),
        name="prenorm_router",
    )(x, g.reshape(1, D), mod_p, mod_s, mod_p, mod_s, wr, br)


def _proj_kernel(h_ref, w_ref, b_ref, o_ref, wbf_ref):
    @pl.when(pl.program_id(1) == 0)
    def _():
        wbf_ref[...] = w_ref[...].astype(BF16)

    o_ref[...] = jnp.dot(h_ref[...], wbf_ref[...], preferred_element_type=F32) + b_ref[...]


def _proj_call(h, w, b, tn):
    k, n = w.shape
    return pl.pallas_call(
        _proj_kernel,
        grid=(n // tn, N_TILES),
        in_specs=[
            pl.BlockSpec((TM, k), lambda j, i: (i, 0)),
            pl.BlockSpec((k, tn), lambda j, i: (0, j)),
            pl.BlockSpec((1, tn), lambda j, i: (0, j)),
        ],
        out_specs=pl.BlockSpec((TM, tn), lambda j, i: (i, j)),
        out_shape=jax.ShapeDtypeStruct((TT, n), F32),
        scratch_shapes=[pltpu.VMEM((k, tn), BF16)],
        compiler_params=_cparams(("arbitrary", "arbitrary")),
        name="proj",
    )(h, w, b.reshape(1, n))


def _post_mm_kernel(x_ref, op_ref, os_ref, w_ref, b_ref, g_ref, gtp_ref, gts_ref, out_ref, wbf_ref):
    i = pl.program_id(0)

    @pl.when(i == 0)
    def _():
        wbf_ref[...] = w_ref[...].astype(BF16)

    o = jnp.where(i < NP_TILES, op_ref[...], os_ref[...]).astype(BF16)
    y = jnp.dot(o, wbf_ref[...], preferred_element_type=F32) + b_ref[...]
    out_ref[...] = x_ref[...] + _pick(i, gtp_ref, gts_ref) * (_rms(y) * g_ref[...])


def _post_mm_call(x, o_p, o_s, w, b, g, mod_p, mod_s, gate_chunk):
    gtp, gts = _mod_specs(gate_chunk)
    tile = pl.BlockSpec((TM, D), lambda i: (i, 0))
    row = pl.BlockSpec((1, D), lambda i: (0, 0))
    return pl.pallas_call(
        _post_mm_kernel,
        grid=(N_TILES,),
        in_specs=[tile,
                  pl.BlockSpec((TM, D), lambda i: (jnp.minimum(i, NP_TILES - 1), 0)),
                  pl.BlockSpec((TM, D), lambda i: (jnp.maximum(i - NP_TILES, 0), 0)),
                  pl.BlockSpec((D, D), lambda i: (0, 0)), row, row, gtp, gts],
        out_specs=tile,
        out_shape=jax.ShapeDtypeStruct((TT, D), F32),
        scratch_shapes=[pltpu.VMEM((D, D), BF16)],
        compiler_params=_cparams(("arbitrary",)),
        name="post_mm",
    )(x, o_p, o_s, w, b.reshape(1, D), g.reshape(1, D), mod_p, mod_s)


def _post_kernel(x_ref, y_ref, g_ref, gtp_ref, gts_ref, out_ref):
    i = pl.program_id(0)
    out_ref[...] = x_ref[...] + _pick(i, gtp_ref, gts_ref) * (_rms(y_ref[...]) * g_ref[...])


def _post_call(x, y, g, mod_p, mod_s, gate_chunk):
    gtp, gts = _mod_specs(gate_chunk)
    tile = pl.BlockSpec((TM, D), lambda i: (i, 0))
    return pl.pallas_call(
        _post_kernel,
        grid=(N_TILES,),
        in_specs=[tile, tile, pl.BlockSpec((1, D), lambda i: (0, 0)), gtp, gts],
        out_specs=tile,
        out_shape=jax.ShapeDtypeStruct((TT, D), F32),
        compiler_params=_cparams(("arbitrary",)),
        name="post",
    )(x, y, g.reshape(1, D), mod_p, mod_s)


def _gdn_kernel(*refs, chunk, n_chunks, t_real, n_par):
    x_refs = refs[:n_par]
    ba_refs = refs[n_par:2 * n_par]
    (cw_ref, alog_ref, dt_ref, nw_ref, cb_ref, s0_ref, o_ref, sfin_ref, xbuf, s_scr) = refs[2 * n_par:]
    c_sz = chunk
    tb = chunk * n_chunks
    j = pl.program_id(1)
    rep = GDN_V_HEADS // GDN_QK_HEADS
    units = [(p, h) for p in range(n_par) for h in range(GDN_V_HEADS)]

    @pl.when(j == 0)
    def _():
        s_scr[...] = s0_ref[...]
        xbuf[:, 0:SUBLANES, :] = cb_ref[...]

    for p in range(n_par):
        xbuf[p, SUBLANES:SUBLANES + tb, :] = x_refs[p][:, 0:GDN_CONV_DIM]

    ii = lax.broadcasted_iota(I32, (c_sz, c_sz), 0)
    jj = lax.broadcasted_iota(I32, (c_sz, c_sz), 1)
    causal = ii >= jj
    strict = ii > jj
    tri = causal.astype(F32)
    eye_c = (ii == jj).astype(F32)
    pair_mask = [jnp.logical_and((ii >> (l + 1)) == (jj >> (l + 1)), (ii >> l) != (jj >> l))
                 for l in range(int(math.log2(c_sz)))]
    eye_t = (lax.broadcasted_iota(I32, (2 * SUBLANES, LANES), 0)
             == lax.broadcasted_iota(I32, (2 * SUBLANES, LANES), 1)).astype(F32)
    lane = lax.broadcasted_iota(I32, (c_sz, LANES), 1)
    row = lax.broadcasted_iota(I32, (c_sz, LANES), 0)
    cw = cw_ref[...]
    neg_a = -jnp.exp(alog_ref[...])
    dt = dt_ref[...]
    nw = nw_ref[...]

    def chunk_body(c, carry):
        r0 = pl.multiple_of(c * c_sz, c_sz)
        xc, beta, gc, rows_t = [], [], [], []
        for p in range(n_par):
            win = xbuf[p, pl.ds(r0, c_sz + SUBLANES), :]
            y = win[SUBLANES:SUBLANES + c_sz] * cw[3:4]
            for tap in range(1, CONV_WIDTH):
                y = y + win[SUBLANES - tap:SUBLANES - tap + c_sz] * cw[3 - tap:4 - tap]
            xc.append(_silu(y))
            ba = ba_refs[p][pl.ds(r0, c_sz), :]
            b_p = jax.nn.sigmoid(ba)
            g_p = neg_a * jax.nn.softplus(ba + dt)
            if t_real < tb:
                live = (row + r0) < t_real
                b_p = jnp.where(live, b_p, 0.0)
                g_p = jnp.where(live, g_p, 0.0)
            gc_p = jnp.dot(tri, g_p, precision=HIGHEST, preferred_element_type=F32)
            cols = jnp.where(lane < GDN_V_HEADS, b_p, gc_p)
            rows_t.append(lax.dot_general(eye_t, cols, (((1,), (1,)), ((), ())), precision=HIGHEST,
                                          preferred_element_type=F32))
            beta.append(b_p)
            gc.append(gc_p)

        qn, kn, kk, qk = {}, {}, {}, {}
        for p in range(n_par):
            for hq in range(GDN_QK_HEADS):
                q = xc[p][:, hq * GDN_HD:(hq + 1) * GDN_HD]
                k = xc[p][:, GDN_KEY_DIM + hq * GDN_HD:GDN_KEY_DIM + (hq + 1) * GDN_HD]
                q = q * lax.rsqrt(jnp.sum(q * q, axis=-1, keepdims=True) + RMS_EPS) * (GDN_HD ** -0.5)
                k = k * lax.rsqrt(jnp.sum(k * k, axis=-1, keepdims=True) + RMS_EPS)
                qn[p, hq], kn[p, hq] = q, k
        for key in qn:
            kk[key] = _bdot_nt(kn[key], kn[key])
            qk[key] = _bdot_nt(qn[key], kn[key])

        gcc, gcl, bcol, decay, a_mat, inv = {}, {}, {}, {}, {}, {}
        for (p, h) in units:
            gcc[p, h] = gc[p][:, GDN_V_HEADS + h:GDN_V_HEADS + h + 1]
            gcr = rows_t[p][GDN_V_HEADS + h:GDN_V_HEADS + h + 1, :]
            bcol[p, h] = beta[p][:, h:h + 1]
            gcl[p, h] = gcc[p, h][c_sz - 1:c_sz, :]
            decay[p, h] = jnp.exp(jnp.where(causal, gcc[p, h] - gcr, -jnp.inf))
            a_mat[p, h] = jnp.where(strict, bcol[p, h] * kk[p, h // rep] * decay[p, h], 0.0)
            inv[p, h] = eye_c - jnp.where(pair_mask[0], a_mat[p, h], 0.0)
        for lvl in range(1, len(pair_mask)):
            t1 = {u: _bdot(jnp.where(pair_mask[lvl], a_mat[u], 0.0), inv[u]) for u in units}
            t2 = {u: _bdot(inv[u], t1[u]) for u in units}
            inv = {u: inv[u] - t2[u] for u in units}

        egc = {u: jnp.exp(gcc[u]) for u in units}
        sol = {}
        for (p, h) in units:
            v = xc[p][:, 2 * GDN_KEY_DIM + h * GDN_HD:2 * GDN_KEY_DIM + (h + 1) * GDN_HD]
            rhs = jnp.concatenate([v * bcol[p, h], kn[p, h // rep] * (bcol[p, h] * egc[p, h])], axis=1)
            sol[p, h] = _bdot(inv[p, h], rhs)
        ws = {}
        for (p, h) in units:
            q_dec = qn[p, h // rep] * egc[p, h]
            ws[p, h] = _bdot(jnp.concatenate([sol[p, h][:, GDN_HD:], q_dec], axis=0), s_scr[p, h])
        u_new = {u: sol[u][:, :GDN_HD] - ws[u][:c_sz] for u in units}
        o_part, s_part = {}, {}
        for (p, h) in units:
            qkm = jnp.where(causal, qk[p, h // rep] * decay[p, h], 0.0)
            k_dec = kn[p, h // rep] * jnp.exp(gcl[p, h] - gcc[p, h])
            o_part[p, h] = _bdot(qkm, u_new[p, h])
            s_part[p, h] = lax.dot_general(k_dec.astype(BF16), u_new[p, h].astype(BF16),
                                           (((0,), (0,)), ((), ())), preferred_element_type=F32)
        for (p, h) in units:
            s_scr[p, h] = s_scr[p, h] * jnp.exp(gcl[p, h]) + s_part[p, h]
            o = ws[p, h][c_sz:] + o_part[p, h]
            z = x_refs[p][pl.ds(r0, c_sz), GDN_CONV_DIM + h * GDN_HD:GDN_CONV_DIM + (h + 1) * GDN_HD]
            o_ref[p, pl.ds(r0, c_sz), h * GDN_HD:(h + 1) * GDN_HD] = _rms(o) * nw * _silu(z)
        return carry

    lax.fori_loop(0, n_chunks, chunk_body, 0)
    xbuf[:, 0:SUBLANES, :] = xbuf[:, tb:tb + SUBLANES, :]

    @pl.when(j == pl.num_programs(1) - 1)
    def _():
        sfin_ref[...] = s_scr[...]


def _gdn_call(x, ba, conv_w, a_log, dt_bias, norm_w, conv_buf, s0, *, n_seq, n_steps, chunk,
              n_chunks, t_real, n_par):
    tb = chunk * n_chunks
    pad = LANES - 2 * GDN_V_HEADS
    alog_row = jnp.pad(a_log, (GDN_V_HEADS, pad)).reshape(1, LANES)
    dt_row = jnp.pad(dt_bias, (GDN_V_HEADS, pad)).reshape(1, LANES)

    def seq_block(width, p):
        return pl.BlockSpec((None, tb, width), lambda s, j: ((s * n_par + p) * n_steps + j, 0, 0))

    const = lambda s, j: (0, 0)
    state_spec = pl.BlockSpec((n_par, GDN_V_HEADS, GDN_HD, GDN_HD), lambda s, j: (s, 0, 0, 0))
    return pl.pallas_call(
        functools.partial(_gdn_kernel, chunk=chunk, n_chunks=n_chunks, t_real=t_real, n_par=n_par),
        grid=(n_seq // n_par, n_steps),
        in_specs=[seq_block(GDN_MAIN_DIM, p) for p in range(n_par)]
                 + [seq_block(LANES, p) for p in range(n_par)]
                 + [pl.BlockSpec((CONV_WIDTH, GDN_CONV_DIM), const),
                    pl.BlockSpec((1, LANES), const), pl.BlockSpec((1, LANES), const),
                    pl.BlockSpec((1, GDN_HD), const),
                    pl.BlockSpec((n_par, SUBLANES, GDN_CONV_DIM), lambda s, j: (s, 0, 0)),
                    state_spec],
        out_specs=[pl.BlockSpec((n_par, None, tb, GDN_VAL_DIM), lambda s, j: (s, j, 0, 0)),
                   state_spec],
        out_shape=[jax.ShapeDtypeStruct((n_seq, n_steps, tb, GDN_VAL_DIM), F32),
                   jax.ShapeDtypeStruct((n_seq, GDN_V_HEADS, GDN_HD, GDN_HD), F32)],
        scratch_shapes=[pltpu.VMEM((n_par, tb + SUBLANES, GDN_CONV_DIM), F32),
                        pltpu.VMEM((n_par, GDN_V_HEADS, GDN_HD, GDN_HD), F32)],
        compiler_params=_cparams(("arbitrary", "arbitrary")),
        name="gdn",
    )(*([x] * n_par), *([ba] * n_par), conv_w, alog_row, dt_row, norm_w.reshape(1, GDN_HD),
      conv_buf, s0)


def _t5_bucket_np(dist):
    dist = np.maximum(dist, 0)
    max_exact = REL_BUCKETS // 2
    ratio = np.log(np.maximum(dist, max_exact).astype(np.float32) / np.float32(max_exact)) \
        / np.float32(math.log(REL_MAX_DIST / max_exact))
    large = max_exact + (ratio.astype(np.float32) * np.float32(REL_BUCKETS - max_exact)).astype(np.int32)
    return np.where(dist < max_exact, dist, np.minimum(large, REL_BUCKETS - 1)).astype(np.int32)


def _bucket_table(qpos, kpos, k_valid):
    dist = qpos[:, None] - kpos[None, :]
    ok = (dist >= 0) & (dist < WINDOW) & (kpos[None, :] >= 0) & k_valid[None, :]
    return np.where(ok, _t5_bucket_np(dist), -1).astype(np.int32)


def _bias_from_buckets(bkt, rb_ref, head):
    def body(b, acc):
        return jnp.where(bkt == b, rb_ref[b, head], acc)
    acc = lax.fori_loop(0, REL_BUCKETS, body, jnp.zeros(bkt.shape, F32))
    return jnp.where(bkt < 0, -jnp.inf, acc)


def _swa_prompt_kernel(rb_ref, sk_ref, bkt_ref, q_ref, kvp_ref, kvc_ref, o_ref, bias_scr):
    first = jnp.logical_and(pl.program_id(0) == 0, pl.program_id(1) == 0)
    n = pl.program_id(1)

    @pl.when(first)
    def _():
        bkt = bkt_ref[...]
        for h in range(SWA_Q_HEADS):
            kv, g = divmod(h, SWA_GROUP)
            bias_scr[kv, g * WINDOW:(g + 1) * WINDOW, :] = _bias_from_buckets(bkt, rb_ref, h)

    q = q_ref[...]
    scale = SWA_HD ** -0.5
    outs = []
    for kv in range(SWA_KV_HEADS):
        qs = jnp.concatenate(
            [q[:, (kv * SWA_GROUP + g) * SWA_HD:(kv * SWA_GROUP + g + 1) * SWA_HD]
             for g in range(SWA_GROUP)], axis=0)
        k_prev = kvp_ref[:, kv * SWA_HD:(kv + 1) * SWA_HD]
        v_prev = kvp_ref[:, SWA_KV_DIM + kv * SWA_HD:SWA_KV_DIM + (kv + 1) * SWA_HD]
        k_cur = kvc_ref[:, kv * SWA_HD:(kv + 1) * SWA_HD]
        v_cur = kvc_ref[:, SWA_KV_DIM + kv * SWA_HD:SWA_KV_DIM + (kv + 1) * SWA_HD]
        s_prev = _bdot_nt(qs, k_prev) * scale + bias_scr[kv, :, 0:WINDOW]
        s_prev = jnp.where(n > 0, s_prev, -jnp.inf)
        s_cur = _bdot_nt(qs, k_cur) * scale + bias_scr[kv, :, WINDOW:2 * WINDOW]
        sink = jnp.concatenate(
            [jnp.full((WINDOW, 1), sk_ref[kv * SWA_GROUP + g], F32) for g in range(SWA_GROUP)], axis=0)
        m = jnp.maximum(jnp.maximum(jnp.max(s_prev, axis=-1, keepdims=True),
                                    jnp.max(s_cur, axis=-1, keepdims=True)), sink)
        p_prev = jnp.exp(s_prev - m)
        p_cur = jnp.exp(s_cur - m)
        den = (jnp.sum(p_prev, axis=-1, keepdims=True) + jnp.sum(p_cur, axis=-1, keepdims=True)
               + jnp.exp(sink - m))
        o = (_bdot(p_prev, v_prev) + _bdot(p_cur, v_cur)) / den
        outs.extend(o[g * WINDOW:(g + 1) * WINDOW] for g in range(SWA_GROUP))
    o_ref[...] = jnp.concatenate(outs, axis=1)


def _swa_prompt_call(qkv, rel_bias, sinks):
    nb = SEQ // WINDOW
    qpos = WINDOW + np.arange(WINDOW)
    kpos = np.arange(2 * WINDOW)
    bkt = jnp.asarray(_bucket_table(qpos, kpos, np.ones(2 * WINDOW, bool)))
    kv_col = SWA_Q_DIM // (2 * SWA_KV_DIM)
    smem = pl.BlockSpec(memory_space=pltpu.SMEM)
    return pl.pallas_call(
        _swa_prompt_kernel,
        grid=(BATCH, nb),
        in_specs=[smem, smem,
                  pl.BlockSpec((WINDOW, 2 * WINDOW), lambda b, n: (0, 0)),
                  pl.BlockSpec((WINDOW, SWA_Q_DIM), lambda b, n: (b * nb + n, 0)),
                  pl.BlockSpec((WINDOW, 2 * SWA_KV_DIM),
                               lambda b, n: (jnp.maximum(b * nb + n - 1, 0), kv_col)),
                  pl.BlockSpec((WINDOW, 2 * SWA_KV_DIM), lambda b, n: (b * nb + n, kv_col))],
        out_specs=pl.BlockSpec((WINDOW, SWA_Q_DIM), lambda b, n: (b * nb + n, 0)),
        out_shape=jax.ShapeDtypeStruct((NP_TOK, SWA_Q_DIM), F32),
        scratch_shapes=[pltpu.VMEM((SWA_KV_HEADS, SWA_GROUP * WINDOW, 2 * WINDOW), F32)],
        compiler_params=_cparams(("arbitrary", "arbitrary")),
        name="swa_prompt",
    )(rel_bias, sinks, bkt, qkv, qkv, qkv)


def _swa_sample_kernel(rb_ref, sk_ref, bkt_ref, q_ref, k_ref, v_ref, o_ref, bias_scr, sink_scr):
    rows = SWA_GROUP * DEC_SEQ

    def group_of_row(shape):
        r = lax.broadcasted_iota(I32, shape, 0)
        return sum((r >= g * DEC_SEQ).astype(I32) for g in range(1, SWA_GROUP))

    @pl.when(pl.program_id(0) == 0)
    def _():
        bkt = bkt_ref[...]
        grp = group_of_row((rows, 1))
        grp_full = group_of_row(bkt.shape)
        for kv in range(SWA_KV_HEADS):
            acc = jnp.zeros(bkt.shape, F32)
            snk = jnp.zeros((rows, 1), F32)
            for g in range(SWA_GROUP):
                head = kv * SWA_GROUP + g
                acc = jnp.where(grp_full == g, _bias_from_buckets(bkt, rb_ref, head), acc)
                snk = jnp.where(grp == g, sk_ref[head], snk)
            bias_scr[kv] = acc
            sink_scr[kv] = snk

    scale = SWA_HD ** -0.5
    for s in range(SWA_SB):
        for kv in range(SWA_KV_HEADS):
            q = q_ref[s, kv]
            k = k_ref[s, :, kv * SWA_HD:(kv + 1) * SWA_HD]
            v = v_ref[s, :, kv * SWA_HD:(kv + 1) * SWA_HD]
            sc = _bdot_nt(q, k) * scale + bias_scr[kv]
            sink = sink_scr[kv]
            m = jnp.maximum(jnp.max(sc, axis=-1, keepdims=True), sink)
            p = jnp.exp(sc - m)
            den = jnp.sum(p, axis=-1, keepdims=True) + jnp.exp(sink - m)
            o_ref[s, kv] = _bdot(p, v) / den


def _swa_sample_call(q_st, kc, vc, rel_bias, sinks):
    rows = SWA_GROUP * DEC_SEQ
    n_keys = kc.shape[1] - (SWA_KPAD - WINDOW) + DEC_SEQ
    qpos = PAST_LEN + np.arange(DEC_SEQ)
    kpos = PAST_LEN - (n_keys - DEC_SEQ) + np.arange(SWA_KPAD)
    bkt4 = _bucket_table(qpos, kpos, np.arange(SWA_KPAD) < n_keys)
    bkt = jnp.asarray(np.tile(bkt4, (SWA_GROUP, 1)))
    smem = pl.BlockSpec(memory_space=pltpu.SMEM)
    q_spec = pl.BlockSpec((SWA_SB, SWA_KV_HEADS, rows, SWA_HD), lambda i: (i, 0, 0, 0))
    kv_spec = pl.BlockSpec((SWA_SB, SWA_KPAD, SWA_KV_DIM), lambda i: (i, 0, 0))
    return pl.pallas_call(
        _swa_sample_kernel,
        grid=(DEC_BATCH // SWA_SB,),
        in_specs=[smem, smem, pl.BlockSpec((rows, SWA_KPAD), lambda i: (0, 0)),
                  q_spec, kv_spec, kv_spec],
        out_specs=q_spec,
        out_shape=jax.ShapeDtypeStruct((DEC_BATCH, SWA_KV_HEADS, rows, SWA_HD), F32),
        scratch_shapes=[pltpu.VMEM((SWA_KV_HEADS, rows, SWA_KPAD), F32),
                        pltpu.VMEM((SWA_KV_HEADS, rows, 1), F32)],
        compiler_params=_cparams(("arbitrary",)),
        name="swa_sample",
    )(rel_bias, sinks, bkt, q_st, kc, vc)


def _ffn_kernel(te_ref, nt_ref, x_ref, wg_ref, wu_ref, wd_ref, o_ref, acc_ref):
    t = pl.program_id(0)

    @pl.when(t < nt_ref[0])
    def _():
        x = x_ref[...].astype(BF16)
        acc_ref[...] = jnp.zeros_like(acc_ref)

        def body(f, carry):
            gate = jnp.dot(x, wg_ref[f], preferred_element_type=F32)
            up = jnp.dot(x, wu_ref[f], preferred_element_type=F32)
            act = (_silu(gate) * up).astype(BF16)
            acc_ref[...] += jnp.dot(act, wd_ref[f], preferred_element_type=F32)
            return carry

        lax.fori_loop(0, NF, body, 0)
        o_ref[...] = acc_ref[...]

    @pl.when(t >= nt_ref[0])
    def _():
        o_ref[...] = jnp.zeros_like(o_ref)


def _ffn_weights(w_gate, w_up, w_down):
    wg = w_gate.astype(BF16).reshape(-1, D, NF, TF).transpose(0, 2, 1, 3)
    wu = w_up.astype(BF16).reshape(-1, D, NF, TF).transpose(0, 2, 1, 3)
    wd = w_down.astype(BF16).reshape(-1, NF, TF, D)
    return wg, wu, wd


def _ffn_call(x, tile_expert, n_tiles_used, wg, wu, wd):
    n_rows = x.shape[0]
    n_tiles = n_rows // TM
    up_spec = pl.BlockSpec((None, NF, D, TF), lambda t, te, nt: (te[t], 0, 0, 0))
    dn_spec = pl.BlockSpec((None, NF, TF, D), lambda t, te, nt: (te[t], 0, 0, 0))
    grid_spec = pltpu.PrefetchScalarGridSpec(
        num_scalar_prefetch=2,
        grid=(n_tiles,),
        in_specs=[pl.BlockSpec((TM, D), lambda t, te, nt: (t, 0)), up_spec, up_spec, dn_spec],
        out_specs=pl.BlockSpec((TM, D), lambda t, te, nt: (t, 0)),
        scratch_shapes=[pltpu.VMEM((TM, D), F32)],
    )
    return pl.pallas_call(
        _ffn_kernel,
        grid_spec=grid_spec,
        out_shape=jax.ShapeDtypeStruct((n_rows, D), F32),
        compiler_params=_cparams(("arbitrary",)),
        name="ffn",
    )(tile_expert, n_tiles_used, x, wg, wu, wd)


DMA_UNROLL = 8


def _row_copy(src_ref, dst_ref, src_row, dst_row, sem):
    return pltpu.make_async_copy(src_ref.at[pl.ds(src_row, 1)], dst_ref.at[pl.ds(dst_row, 1)], sem)


def _dispatch_kernel(d0_ref, d1_ref, h_ref, init_ref, out_ref, sem):
    del init_ref

    def start(r, carry):
        _row_copy(h_ref, out_ref, r, d0_ref[0, 0, r], sem).start()
        _row_copy(h_ref, out_ref, r, d1_ref[0, 0, r], sem).start()
        return carry

    def wait(r, carry):
        _row_copy(h_ref, out_ref, 0, 0, sem).wait()
        _row_copy(h_ref, out_ref, 0, 0, sem).wait()
        return carry

    lax.fori_loop(0, TM, start, 0, unroll=DMA_UNROLL)
    lax.fori_loop(0, TM, wait, 0, unroll=DMA_UNROLL)


def _dispatch_call(h, dest):
    idx_spec = pl.BlockSpec((1, 1, TM), lambda i: (i, 0, 0), memory_space=pltpu.SMEM)
    return pl.pallas_call(
        _dispatch_kernel,
        grid=(N_TILES,),
        in_specs=[idx_spec, idx_spec, pl.BlockSpec((TM, D), lambda i: (i, 0)),
                  pl.BlockSpec(memory_space=pl.ANY)],
        out_specs=pl.BlockSpec(memory_space=pl.ANY),
        out_shape=jax.ShapeDtypeStruct((MOE_ROWS, D), h.dtype),
        scratch_shapes=[pltpu.SemaphoreType.DMA(())],
        input_output_aliases={3: 0},
        compiler_params=_cparams(("arbitrary",)),
        name="moe_dispatch",
    )(dest[0].reshape(N_TILES, 1, TM), dest[1].reshape(N_TILES, 1, TM), h,
      jnp.zeros((MOE_ROWS, D), h.dtype))


def _combine_kernel(d0_ref, d1_ref, x_ref, wt_ref, ys_ref, g_ref, gtp_ref, gts_ref, out_ref, buf, sem):
    i = pl.program_id(0)

    def start(r, carry):
        _row_copy(ys_ref, buf.at[0], d0_ref[0, 0, r], r, sem).start()
        _row_copy(ys_ref, buf.at[1], d1_ref[0, 0, r], r, sem).start()
        return carry

    def wait(r, carry):
        _row_copy(ys_ref, buf.at[0], 0, 0, sem).wait()
        _row_copy(ys_ref, buf.at[1], 0, 0, sem).wait()
        return carry

    lax.fori_loop(0, TM, start, 0, unroll=DMA_UNROLL)
    lax.fori_loop(0, TM, wait, 0, unroll=DMA_UNROLL)
    wt = wt_ref[...]
    y = wt[:, 0:1] * buf[0] + wt[:, 1:2] * buf[1]
    out_ref[...] = x_ref[...] + _pick(i, gtp_ref, gts_ref) * (_rms(y) * g_ref[...])


def _combine_call(x, ys, dest, wts, g, mod_p, mod_s, gate_chunk):
    gtp, gts = _mod_specs(gate_chunk)
    tile = pl.BlockSpec((TM, D), lambda i: (i, 0))
    idx_spec = pl.BlockSpec((1, 1, TM), lambda i: (i, 0, 0), memory_space=pltpu.SMEM)
    return pl.pallas_call(
        _combine_kernel,
        grid=(N_TILES,),
        in_specs=[idx_spec, idx_spec, tile, pl.BlockSpec((TM, LANES), lambda i: (i, 0)),
                  pl.BlockSpec(memory_space=pl.ANY),
                  pl.BlockSpec((1, D), lambda i: (0, 0)), gtp, gts],
        out_specs=tile,
        out_shape=jax.ShapeDtypeStruct((TT, D), F32),
        scratch_shapes=[pltpu.VMEM((2, TM, D), F32), pltpu.SemaphoreType.DMA(())],
        compiler_params=_cparams(("arbitrary",)),
        name="moe_combine",
    )(dest[0].reshape(N_TILES, 1, TM), dest[1].reshape(N_TILES, 1, TM), x, wts, ys,
      g.reshape(1, D), mod_p, mod_s)


def _route(top_idx):
    flat_e = top_idx.T.reshape(-1)
    onehot = (flat_e[:, None] == jnp.arange(N_EXPERTS, dtype=I32)[None, :]).astype(I32)
    csum = jnp.cumsum(onehot, axis=0)
    rank = jnp.sum((csum - onehot) * onehot, axis=1)
    counts = csum[-1]
    padded = ((counts + TM - 1) // TM) * TM
    ends = jnp.cumsum(padded)
    starts = ends - padded
    dest = (jnp.sum(onehot * starts[None, :], axis=1) + rank).astype(I32)
    tile_start = jnp.arange(MOE_TILES, dtype=I32) * TM
    tile_expert = jnp.minimum(jnp.sum(tile_start[:, None] >= ends[None, :], axis=1),
                              N_EXPERTS - 1).astype(I32)
    n_used = (ends[-1] // TM).astype(I32).reshape(1)
    return dest.reshape(2, TT), tile_expert, n_used


def _gdn_layer(x, h, w_in, conv_w, a_log, dt_bias, norm_w, w_out, state_conv, state_rec,
               g_post, mod_p, mod_s):
    w_main = w_in[:, :GDN_MAIN_DIM]
    w_ba = jnp.pad(w_in[:, GDN_MAIN_DIM:], ((0, 0), (0, LANES - 2 * GDN_V_HEADS)))
    qkvz = _proj_call(h, w_main, jnp.zeros((GDN_MAIN_DIM,), F32), 1024)
    ba = _proj_call(h, w_ba, jnp.zeros((LANES,), F32), LANES)

    tb = GDN_CHUNK * GDN_NB
    n_steps = SEQ // tb
    o_p, rec_p = _gdn_call(
        qkvz.reshape(TT // tb, tb, GDN_MAIN_DIM), ba.reshape(TT // tb, tb, LANES),
        conv_w, a_log, dt_bias, norm_w,
        jnp.zeros((BATCH, SUBLANES, GDN_CONV_DIM), F32),
        jnp.zeros((BATCH, GDN_V_HEADS, GDN_HD, GDN_HD), F32),
        n_seq=BATCH, n_steps=n_steps, chunk=GDN_CHUNK, n_chunks=GDN_NB, t_real=tb,
        n_par=GDN_PROMPT_PAR)
    o_p = o_p.reshape(NP_TOK, GDN_VAL_DIM)

    row_pad = ((0, 0), (0, GDN_SAMPLE_ROWS - DEC_SEQ), (0, 0))
    qkvz_s = qkvz[NP_TOK:].reshape(DEC_BATCH, DEC_SEQ, GDN_MAIN_DIM)
    ba_s = ba[NP_TOK:].reshape(DEC_BATCH, DEC_SEQ, LANES)
    cbuf_s = jnp.pad(state_conv, ((0, 0), (SUBLANES - (CONV_WIDTH - 1), 0), (0, 0)))
    o_s, rec_s = _gdn_call(
        jnp.pad(qkvz_s, row_pad), jnp.pad(ba_s, row_pad), conv_w, a_log, dt_bias, norm_w,
        cbuf_s, state_rec,
        n_seq=DEC_BATCH, n_steps=1, chunk=GDN_SAMPLE_ROWS, n_chunks=1, t_real=DEC_SEQ,
        n_par=GDN_SAMPLE_PAR)
    o_s = o_s[:, 0, :DEC_SEQ].reshape(NS_TOK, GDN_VAL_DIM)

    keep = CONV_WIDTH - 1
    pre_p = jnp.stack([qkvz[(b + 1) * SEQ - keep:(b + 1) * SEQ, :GDN_CONV_DIM] for b in range(BATCH)])
    pre_s = jnp.concatenate([state_conv, qkvz_s[:, :, :GDN_CONV_DIM]], axis=1)[:, -keep:]
    x = _post_mm_call(x, o_p, o_s, w_out, jnp.zeros((D,), F32), g_post, mod_p, mod_s, 2)
    return x, pre_p, rec_p, pre_s, rec_s


def _swa_layer(x, h, w_in, b_in, sinks, w_out, b_out, rel_bias, cache_k, cache_v,
               g_post, mod_p, mod_s):
    qkv = _proj_call(h, w_in, b_in, SWA_PROJ_DIM)
    o_p = _swa_prompt_call(qkv, rel_bias, sinks)
    last = jnp.stack([qkv[(b + 1) * SEQ - WINDOW:(b + 1) * SEQ, SWA_Q_DIM:] for b in range(BATCH)])
    new_k_p = last[:, :, :SWA_KV_DIM]
    new_v_p = last[:, :, SWA_KV_DIM:]

    qkv_s = qkv[NP_TOK:].reshape(DEC_BATCH, DEC_SEQ, SWA_PROJ_DIM)
    q_st = qkv_s[:, :, :SWA_Q_DIM].reshape(DEC_BATCH, DEC_SEQ, SWA_KV_HEADS, SWA_GROUP, SWA_HD)
    q_st = q_st.transpose(0, 2, 3, 1, 4).reshape(DEC_BATCH, SWA_KV_HEADS, SWA_GROUP * DEC_SEQ, SWA_HD)
    wb = cache_k.shape[1]
    zpad = jnp.zeros((DEC_BATCH, SWA_KPAD - WINDOW - DEC_SEQ + (WINDOW - wb), SWA_KV_DIM), F32)
    kc = jnp.concatenate([cache_k.reshape(DEC_BATCH, wb, SWA_KV_DIM),
                          qkv_s[:, :, SWA_Q_DIM:SWA_Q_DIM + SWA_KV_DIM], zpad], axis=1)
    vc = jnp.concatenate([cache_v.reshape(DEC_BATCH, wb, SWA_KV_DIM),
                          qkv_s[:, :, SWA_Q_DIM + SWA_KV_DIM:], zpad], axis=1)
    o_st = _swa_sample_call(q_st, kc, vc, rel_bias, sinks)
    o_s = o_st.reshape(DEC_BATCH, SWA_KV_HEADS, SWA_GROUP, DEC_SEQ, SWA_HD)
    o_s = o_s.transpose(0, 3, 1, 2, 4).reshape(NS_TOK, SWA_Q_DIM)
    new_k_s = kc[:, DEC_SEQ:DEC_SEQ + wb]
    new_v_s = vc[:, DEC_SEQ:DEC_SEQ + wb]

    x = _post_mm_call(x, o_p, o_s, w_out, b_out, g_post, mod_p, mod_s, 2)
    shape_p = (BATCH, WINDOW, SWA_KV_HEADS, SWA_HD)
    shape_s = (DEC_BATCH, wb, SWA_KV_HEADS, SWA_HD)
    return (x, new_k_p.reshape(shape_p), new_v_p.reshape(shape_p),
            new_k_s.reshape(shape_s), new_v_s.reshape(shape_s))


def kernel(x_prompt, x_sample, c_prompt, c_sample, state_conv, state_rec, cache_win_k, cache_win_v, w_mod, b_mod, g_pre_mix, g_post_mix, g_pre_ffn, g_post_ffn, gdn_w_in, gdn_conv_w, gdn_a_log, gdn_dt_bias, gdn_norm_w, gdn_w_out, swa_w_in, swa_b_in, swa_sinks, swa_w_out, swa_b_out, rel_bias, ffn_w_gate, ffn_w_up, ffn_w_down, moe_w_router, moe_b_router, moe_w_gate, moe_w_up, moe_w_down):
    x = jnp.concatenate([x_prompt.reshape(NP_TOK, D), x_sample.reshape(NS_TOK, D)], axis=0)
    n_c = BATCH + DEC_BATCH
    c_all = jnp.concatenate([c_prompt, c_sample, jnp.zeros((-n_c % SUBLANES, D), F32)], axis=0)
    m_all = _mod_call(c_all, w_mod, b_mod)
    ffn_w = _ffn_weights(ffn_w_gate, ffn_w_up, ffn_w_down)
    moe_w = _ffn_weights(moe_w_gate, moe_w_up, moe_w_down)

    conv_p, rec_p, conv_s, rec_s = [], [], [], []
    k_p, v_p, k_s, v_s = [], [], [], []
    for layer in range(DEPTH):
        j = layer // 2
        mod_p = m_all[layer, :BATCH].reshape(BATCH, 1, 6 * D)
        mod_s = jnp.repeat(m_all[layer, BATCH:n_c], DEC_SEQ, axis=0)
        h = _prenorm_call(x, g_pre_mix[layer], mod_p, mod_s, 0, 1)
        if layer % 2 == 0:
            x, cp, rp, cs, rs = _gdn_layer(
                x, h, gdn_w_in[j], gdn_conv_w[j], gdn_a_log[j], gdn_dt_bias[j], gdn_norm_w[j],
                gdn_w_out[j], state_conv[j], state_rec[j], g_post_mix[layer], mod_p, mod_s)
            conv_p.append(cp); rec_p.append(rp); conv_s.append(cs); rec_s.append(rs)
        else:
            x, kp, vp, ks, vs = _swa_layer(
                x, h, swa_w_in[j], swa_b_in[j], swa_sinks[j], swa_w_out[j], swa_b_out[j], rel_bias,
                cache_win_k[j], cache_win_v[j], g_post_mix[layer], mod_p, mod_s)
            k_p.append(kp); v_p.append(vp); k_s.append(ks); v_s.append(vs)

        if layer % 2 == 0:
            h = _prenorm_call(x, g_pre_ffn[layer], mod_p, mod_s, 3, 4)
            y = _ffn_call(h, jnp.full((N_TILES,), j, I32), jnp.full((1,), N_TILES, I32), *ffn_w)
            x = _post_call(x, y, g_post_ffn[layer], mod_p, mod_s, 5)
        else:
            h, idx, wts = _prenorm_router_call(x, g_pre_ffn[layer], mod_p, mod_s, 3, 4,
                                               moe_w_router[j], moe_b_router[j])
            dest, tile_expert, n_used = _route(idx[:, :2])
            xs = _dispatch_call(h, dest)
            ys = _ffn_call(xs, tile_expert + j * N_EXPERTS, n_used, *moe_w)
            x = _combine_call(x, ys, dest, wts, g_post_ffn[layer], mod_p, mod_s, 5)

    y_prompt = x[:NP_TOK].reshape(BATCH, SEQ, D)
    y_sample = x[NP_TOK:].reshape(DEC_BATCH, DEC_SEQ, D)
    return (y_prompt, y_sample, jnp.stack(conv_p), jnp.stack(rec_p), jnp.stack(k_p), jnp.stack(v_p),
            jnp.stack(conv_s), jnp.stack(rec_s), jnp.stack(k_s), jnp.stack(v_s))
```

```python
import functools
import math

import numpy as np
import jax
import jax.numpy as jnp
from jax import lax
from jax.experimental import pallas as pl
from jax.experimental.pallas import tpu as pltpu

F32 = jnp.float32
BF16 = jnp.bfloat16
I32 = jnp.int32
HIGHEST = lax.Precision.HIGHEST

D = 1024
BATCH = 4
SEQ = 4096
DEPTH = 4
DEC_BATCH = 128
DEC_SEQ = 4
PAST_LEN = 8192
GDN_QK_HEADS = 4
GDN_V_HEADS = 8
GDN_HD = 128
GDN_KEY_DIM = GDN_QK_HEADS * GDN_HD
GDN_VAL_DIM = GDN_V_HEADS * GDN_HD
GDN_CONV_DIM = 2 * GDN_KEY_DIM + GDN_VAL_DIM
GDN_MAIN_DIM = GDN_CONV_DIM + GDN_VAL_DIM
CONV_WIDTH = 4
GDN_CHUNK = 64
SWA_Q_HEADS = 16
SWA_KV_HEADS = 4
SWA_GROUP = SWA_Q_HEADS // SWA_KV_HEADS
SWA_HD = 64
SWA_Q_DIM = SWA_Q_HEADS * SWA_HD
SWA_KV_DIM = SWA_KV_HEADS * SWA_HD
SWA_PROJ_DIM = SWA_Q_DIM + 2 * SWA_KV_DIM
WINDOW = 128
REL_BUCKETS = 32
REL_MAX_DIST = 128
D_FF = 2816
N_EXPERTS = 8
RMS_EPS = 1e-6

LANES = 128
SUBLANES = 8
VMEM_LIMIT = 56 * 1024 * 1024

TM = 512
NP_TOK = BATCH * SEQ
NS_TOK = DEC_BATCH * DEC_SEQ
TT = NP_TOK + NS_TOK
NP_TILES = NP_TOK // TM
NS_TILES = NS_TOK // TM
N_TILES = NP_TILES + NS_TILES
TILES_PER_SEQ = SEQ // TM
TF = 256
NF = D_FF // TF
MOE_ROWS = 2 * TT + N_EXPERTS * TM
MOE_TILES = MOE_ROWS // TM
GDN_NB = 4
GDN_PROMPT_PAR = 2
GDN_SAMPLE_PAR = 4
GDN_SAMPLE_ROWS = 8
SWA_SB = 8
SWA_KPAD = WINDOW + 8

assert NP_TOK % TM == 0 and NS_TOK % TM == 0 and SEQ % TM == 0
assert D_FF % TF == 0 and DEC_SEQ <= GDN_SAMPLE_ROWS and DEC_SEQ <= 8


def _cparams(sem):
    return pltpu.CompilerParams(dimension_semantics=sem, vmem_limit_bytes=VMEM_LIMIT)


def _bdot(a, b):
    return jnp.dot(a.astype(BF16), b.astype(BF16), preferred_element_type=F32)


def _bdot_nt(a, b):
    return lax.dot_general(a.astype(BF16), b.astype(BF16), (((1,), (1,)), ((), ())),
                           preferred_element_type=F32)


def _silu(x):
    return x * jax.nn.sigmoid(x)


def _rms(x):
    return x * lax.rsqrt(jnp.mean(x * x, axis=-1, keepdims=True) + RMS_EPS)


def _mod_kernel(c_ref, w_ref, b_ref, o_ref):
    c = c_ref[...]
    o_ref[0] = _bdot(_silu(c), w_ref[0]) + b_ref[0]


def _mod_call(c_all, w_mod, b_mod):
    n = c_all.shape[0]
    tn = D
    return pl.pallas_call(
        _mod_kernel,
        grid=(DEPTH, 6 * D // tn),
        in_specs=[
            pl.BlockSpec((n, D), lambda l, j: (0, 0)),
            pl.BlockSpec((1, D, tn), lambda l, j: (l, 0, j)),
            pl.BlockSpec((1, 1, tn), lambda l, j: (l, 0, j)),
        ],
        out_specs=pl.BlockSpec((1, n, tn), lambda l, j: (l, 0, j)),
        out_shape=jax.ShapeDtypeStruct((DEPTH, n, 6 * D), F32),
        compiler_params=_cparams(("arbitrary", "arbitrary")),
        name="modulation",
    )(c_all, w_mod, b_mod.reshape(DEPTH, 1, 6 * D))


def _mod_specs(chunk, token_axis=0):
    p = pl.BlockSpec((1, 1, D), lambda *ids: (jnp.minimum(ids[token_axis] // TILES_PER_SEQ, BATCH - 1),
                                              0, chunk))
    s = pl.BlockSpec((TM, D), lambda *ids: (jnp.maximum(ids[token_axis] - NP_TILES, 0), chunk))
    return p, s


def _pick(i, p_ref, s_ref):
    return jnp.where(i < NP_TILES, p_ref[0], s_ref[...])


def _prenorm(i, x, g_ref, shp_ref, shs_ref, scp_ref, scs_ref):
    h = _rms(x) * g_ref[...]
    return h * (1.0 + _pick(i, scp_ref, scs_ref)) + _pick(i, shp_ref, shs_ref)


def _prenorm_router_kernel(x_ref, g_ref, shp_ref, shs_ref, scp_ref, scs_ref, wr_ref, br_ref,
                           h_ref, idx_ref, wt_ref):
    h = _prenorm(pl.program_id(0), x_ref[...], g_ref, shp_ref, shs_ref, scp_ref, scs_ref)
    h_ref[...] = h
    lane = lax.broadcasted_iota(I32, (TM, LANES), 1).astype(F32)
    logits = jnp.dot(h, wr_ref[...], precision=HIGHEST, preferred_element_type=F32) + br_ref[...]
    logits = jnp.where(lane < N_EXPERTS, logits, -jnp.inf)
    m1 = jnp.max(logits, axis=-1, keepdims=True)
    i1 = jnp.min(jnp.where(logits == m1, lane, float(LANES)), axis=-1, keepdims=True)
    rest = jnp.where(lane == i1, -jnp.inf, logits)
    m2 = jnp.max(rest, axis=-1, keepdims=True)
    i2 = jnp.min(jnp.where(rest == m2, lane, float(LANES)), axis=-1, keepdims=True)
    e2 = jnp.exp(m2 - m1)
    w1 = 1.0 / (1.0 + e2)
    w2 = e2 / (1.0 + e2)
    idx_ref[...] = jnp.where(lane == 0, i1, jnp.where(lane == 1, i2, 0.0)).astype(I32)
    wt_ref[...] = jnp.where(lane == 0, w1, jnp.where(lane == 1, w2, 0.0))


def _prenorm_router_call(x, g, mod_p, mod_s, sh_chunk, sc_chunk, w_router, b_router):
    shp, shs = _mod_specs(sh_chunk)
    scp, scs = _mod_specs(sc_chunk)
    tile = pl.BlockSpec((TM, D), lambda i: (i, 0))
    small = pl.BlockSpec((TM, LANES), lambda i: (i, 0))
    wr = jnp.pad(w_router, ((0, 0), (0, LANES - N_EXPERTS)))
    br = jnp.pad(b_router, (0, LANES - N_EXPERTS)).reshape(1, LANES)
    return pl.pallas_call(
        _prenorm_router_kernel,
        grid=(N_TILES,),
        in_specs=[tile, pl.BlockSpec((1, D), lambda i: (0, 0)), shp, shs, scp, scs,
                  pl.BlockSpec((D, LANES), lambda i: (0, 0)),
                  pl.BlockSpec((1, LANES), lambda i: (0, 0))],
        out_specs=[tile, small, small],
        out_shape=[jax.ShapeDtypeStruct((TT, D), F32),
                   jax.ShapeDtypeStruct((TT, LANES), I32),
                   jax.ShapeDtypeStruct((TT, LANES), F32)],
        compiler_params=_cparams(("arbitrary",)),
        name="prenorm_router",
    )(x, g.reshape(1, D), mod_p, mod_s, mod_p, mod_s, wr, br)


def _proj_kernel(x_ref, g_ref, shp_ref, shs_ref, scp_ref, scs_ref, w_ref, b_ref, o_ref, wbf_ref):
    i = pl.program_id(1)

    @pl.when(i == 0)
    def _():
        wbf_ref[...] = w_ref[...].astype(BF16)

    h = _prenorm(i, x_ref[...], g_ref, shp_ref, shs_ref, scp_ref, scs_ref).astype(BF16)
    o_ref[...] = jnp.dot(h, wbf_ref[...], preferred_element_type=F32) + b_ref[...]


def _proj_call(x, g, mod_p, mod_s, w, b, tn):
    k, n = w.shape
    shp, shs = _mod_specs(0, token_axis=1)
    scp, scs = _mod_specs(1, token_axis=1)
    return pl.pallas_call(
        _proj_kernel,
        grid=(n // tn, N_TILES),
        in_specs=[
            pl.BlockSpec((TM, k), lambda j, i: (i, 0)),
            pl.BlockSpec((1, k), lambda j, i: (0, 0)), shp, shs, scp, scs,
            pl.BlockSpec((k, tn), lambda j, i: (0, j)),
            pl.BlockSpec((1, tn), lambda j, i: (0, j)),
        ],
        out_specs=pl.BlockSpec((TM, tn), lambda j, i: (i, j)),
        out_shape=jax.ShapeDtypeStruct((TT, n), F32),
        scratch_shapes=[pltpu.VMEM((k, tn), BF16)],
        compiler_params=_cparams(("arbitrary", "arbitrary")),
        name="proj",
    )(x, g.reshape(1, k), mod_p, mod_s, mod_p, mod_s, w, b.reshape(1, n))


def _post_mm_kernel(x_ref, op_ref, os_ref, w_ref, b_ref, g_ref, gtp_ref, gts_ref, out_ref, wbf_ref):
    i = pl.program_id(0)

    @pl.when(i == 0)
    def _():
        wbf_ref[...] = w_ref[...].astype(BF16)

    o = jnp.where(i < NP_TILES, op_ref[...], os_ref[...]).astype(BF16)
    y = jnp.dot(o, wbf_ref[...], preferred_element_type=F32) + b_ref[...]
    out_ref[...] = x_ref[...] + _pick(i, gtp_ref, gts_ref) * (_rms(y) * g_ref[...])


def _post_mm_call(x, o_p, o_s, w, b, g, mod_p, mod_s, gate_chunk):
    gtp, gts = _mod_specs(gate_chunk)
    tile = pl.BlockSpec((TM, D), lambda i: (i, 0))
    row = pl.BlockSpec((1, D), lambda i: (0, 0))
    return pl.pallas_call(
        _post_mm_kernel,
        grid=(N_TILES,),
        in_specs=[tile,
                  pl.BlockSpec((TM, D), lambda i: (jnp.minimum(i, NP_TILES - 1), 0)),
                  pl.BlockSpec((TM, D), lambda i: (jnp.maximum(i - NP_TILES, 0), 0)),
                  pl.BlockSpec((D, D), lambda i: (0, 0)), row, row, gtp, gts],
        out_specs=tile,
        out_shape=jax.ShapeDtypeStruct((TT, D), F32),
        scratch_shapes=[pltpu.VMEM((D, D), BF16)],
        compiler_params=_cparams(("arbitrary",)),
        name="post_mm",
    )(x, o_p, o_s, w, b.reshape(1, D), g.reshape(1, D), mod_p, mod_s)


def _gdn_kernel(*refs, chunk, n_chunks, t_real, n_par):
    x_refs = refs[:n_par]
    ba_refs = refs[n_par:2 * n_par]
    (cw_ref, alog_ref, dt_ref, nw_ref, cb_ref, s0_ref, o_ref, sfin_ref, xbuf, s_scr) = refs[2 * n_par:]
    c_sz = chunk
    tb = chunk * n_chunks
    j = pl.program_id(1)
    rep = GDN_V_HEADS // GDN_QK_HEADS
    units = [(p, h) for p in range(n_par) for h in range(GDN_V_HEADS)]

    @pl.when(j == 0)
    def _():
        s_scr[...] = s0_ref[...]
        xbuf[:, 0:SUBLANES, :] = cb_ref[...]

    for p in range(n_par):
        xbuf[p, SUBLANES:SUBLANES + tb, :] = x_refs[p][:, 0:GDN_CONV_DIM]

    ii = lax.broadcasted_iota(I32, (c_sz, c_sz), 0)
    jj = lax.broadcasted_iota(I32, (c_sz, c_sz), 1)
    causal = ii >= jj
    strict = ii > jj
    tri = causal.astype(F32)
    eye_c = (ii == jj).astype(F32)
    pair_mask = [jnp.logical_and((ii >> (l + 1)) == (jj >> (l + 1)), (ii >> l) != (jj >> l))
                 for l in range(int(math.log2(c_sz)))]
    eye_t = (lax.broadcasted_iota(I32, (2 * SUBLANES, LANES), 0)
             == lax.broadcasted_iota(I32, (2 * SUBLANES, LANES), 1)).astype(F32)
    lane = lax.broadcasted_iota(I32, (c_sz, LANES), 1)
    row = lax.broadcasted_iota(I32, (c_sz, LANES), 0)
    cw = cw_ref[...]
    neg_a = -jnp.exp(alog_ref[...])
    dt = dt_ref[...]
    nw = nw_ref[...]

    def chunk_body(c, carry):
        r0 = pl.multiple_of(c * c_sz, c_sz)
        xc, beta, gc, rows_t = [], [], [], []
        for p in range(n_par):
            win = xbuf[p, pl.ds(r0, c_sz + SUBLANES), :]
            y = win[SUBLANES:SUBLANES + c_sz] * cw[3:4]
            for tap in range(1, CONV_WIDTH):
                y = y + win[SUBLANES - tap:SUBLANES - tap + c_sz] * cw[3 - tap:4 - tap]
            xc.append(_silu(y))
            ba = ba_refs[p][pl.ds(r0, c_sz), :]
            b_p = jax.nn.sigmoid(ba)
            g_p = neg_a * jax.nn.softplus(ba + dt)
            if t_real < tb:
                live = (row + r0) < t_real
                b_p = jnp.where(live, b_p, 0.0)
                g_p = jnp.where(live, g_p, 0.0)
            gc_p = jnp.dot(tri, g_p, precision=HIGHEST, preferred_element_type=F32)
            cols = jnp.where(lane < GDN_V_HEADS, b_p, gc_p)
            rows_t.append(lax.dot_general(eye_t, cols, (((1,), (1,)), ((), ())), precision=HIGHEST,
                                          preferred_element_type=F32))
            beta.append(b_p)
            gc.append(gc_p)

        qn, kn, kk, qk = {}, {}, {}, {}
        for p in range(n_par):
            for hq in range(GDN_QK_HEADS):
                q = xc[p][:, hq * GDN_HD:(hq + 1) * GDN_HD]
                k = xc[p][:, GDN_KEY_DIM + hq * GDN_HD:GDN_KEY_DIM + (hq + 1) * GDN_HD]
                q = q * lax.rsqrt(jnp.sum(q * q, axis=-1, keepdims=True) + RMS_EPS) * (GDN_HD ** -0.5)
                k = k * lax.rsqrt(jnp.sum(k * k, axis=-1, keepdims=True) + RMS_EPS)
                qn[p, hq], kn[p, hq] = q, k
        for key in qn:
            kk[key] = _bdot_nt(kn[key], kn[key])
            qk[key] = _bdot_nt(qn[key], kn[key])

        gcc, gcl, bcol, decay, a_mat, inv = {}, {}, {}, {}, {}, {}
        for (p, h) in units:
            gcc[p, h] = gc[p][:, GDN_V_HEADS + h:GDN_V_HEADS + h + 1]
            gcr = rows_t[p][GDN_V_HEADS + h:GDN_V_HEADS + h + 1, :]
            bcol[p, h] = beta[p][:, h:h + 1]
            gcl[p, h] = gcc[p, h][c_sz - 1:c_sz, :]
            decay[p, h] = jnp.exp(jnp.where(causal, gcc[p, h] - gcr, -jnp.inf))
            a_mat[p, h] = jnp.where(strict, bcol[p, h] * kk[p, h // rep] * decay[p, h], 0.0)
            inv[p, h] = eye_c - jnp.where(pair_mask[0], a_mat[p, h], 0.0)
        for lvl in range(1, len(pair_mask)):
            t1 = {u: _bdot(jnp.where(pair_mask[lvl], a_mat[u], 0.0), inv[u]) for u in units}
            t2 = {u: _bdot(inv[u], t1[u]) for u in units}
            inv = {u: inv[u] - t2[u] for u in units}

        egc = {u: jnp.exp(gcc[u]) for u in units}
        sol = {}
        for (p, h) in units:
            v = xc[p][:, 2 * GDN_KEY_DIM + h * GDN_HD:2 * GDN_KEY_DIM + (h + 1) * GDN_HD]
            rhs = jnp.concatenate([v * bcol[p, h], kn[p, h // rep] * (bcol[p, h] * egc[p, h])], axis=1)
            sol[p, h] = _bdot(inv[p, h], rhs)
        ws = {}
        for (p, h) in units:
            q_dec = qn[p, h // rep] * egc[p, h]
            ws[p, h] = _bdot(jnp.concatenate([sol[p, h][:, GDN_HD:], q_dec], axis=0), s_scr[p, h])
        u_new = {u: sol[u][:, :GDN_HD] - ws[u][:c_sz] for u in units}
        o_part, s_part = {}, {}
        for (p, h) in units:
            qkm = jnp.where(causal, qk[p, h // rep] * decay[p, h], 0.0)
            k_dec = kn[p, h // rep] * jnp.exp(gcl[p, h] - gcc[p, h])
            o_part[p, h] = _bdot(qkm, u_new[p, h])
            s_part[p, h] = lax.dot_general(k_dec.astype(BF16), u_new[p, h].astype(BF16),
                                           (((0,), (0,)), ((), ())), preferred_element_type=F32)
        for (p, h) in units:
            s_scr[p, h] = s_scr[p, h] * jnp.exp(gcl[p, h]) + s_part[p, h]
            o = ws[p, h][c_sz:] + o_part[p, h]
            z = x_refs[p][pl.ds(r0, c_sz), GDN_CONV_DIM + h * GDN_HD:GDN_CONV_DIM + (h + 1) * GDN_HD]
            o_ref[p, pl.ds(r0, c_sz), h * GDN_HD:(h + 1) * GDN_HD] = _rms(o) * nw * _silu(z)
        return carry

    lax.fori_loop(0, n_chunks, chunk_body, 0)
    xbuf[:, 0:SUBLANES, :] = xbuf[:, tb:tb + SUBLANES, :]

    @pl.when(j == pl.num_programs(1) - 1)
    def _():
        sfin_ref[...] = s_scr[...]


def _gdn_call(x, ba, conv_w, a_log, dt_bias, norm_w, conv_buf, s0, *, n_seq, n_steps, chunk,
              n_chunks, t_real, n_par, s0_first_seq=0):
    s0_block = s0_first_seq // n_par
    tb = chunk * n_chunks
    pad = LANES - 2 * GDN_V_HEADS
    alog_row = jnp.pad(a_log, (GDN_V_HEADS, pad)).reshape(1, LANES)
    dt_row = jnp.pad(dt_bias, (GDN_V_HEADS, pad)).reshape(1, LANES)

    def seq_block(width, p):
        return pl.BlockSpec((None, tb, width), lambda s, j: ((s * n_par + p) * n_steps + j, 0, 0))

    const = lambda s, j: (0, 0)
    state_spec = pl.BlockSpec((n_par, GDN_V_HEADS, GDN_HD, GDN_HD), lambda s, j: (s, 0, 0, 0))
    return pl.pallas_call(
        functools.partial(_gdn_kernel, chunk=chunk, n_chunks=n_chunks, t_real=t_real, n_par=n_par),
        grid=(n_seq // n_par, n_steps),
        in_specs=[seq_block(GDN_MAIN_DIM, p) for p in range(n_par)]
                 + [seq_block(LANES, p) for p in range(n_par)]
                 + [pl.BlockSpec((CONV_WIDTH, GDN_CONV_DIM), const),
                    pl.BlockSpec((1, LANES), const), pl.BlockSpec((1, LANES), const),
                    pl.BlockSpec((1, GDN_HD), const),
                    pl.BlockSpec((n_par, SUBLANES, GDN_CONV_DIM), lambda s, j: (s, 0, 0)),
                    pl.BlockSpec((n_par, GDN_V_HEADS, GDN_HD, GDN_HD),
                                 lambda s, j: (s + s0_block, 0, 0, 0))],
        out_specs=[pl.BlockSpec((n_par, None, tb, GDN_VAL_DIM), lambda s, j: (s, j, 0, 0)),
                   state_spec],
        out_shape=[jax.ShapeDtypeStruct((n_seq, n_steps, tb, GDN_VAL_DIM), F32),
                   jax.ShapeDtypeStruct((n_seq, GDN_V_HEADS, GDN_HD, GDN_HD), F32)],
        scratch_shapes=[pltpu.VMEM((n_par, tb + SUBLANES, GDN_CONV_DIM), F32),
                        pltpu.VMEM((n_par, GDN_V_HEADS, GDN_HD, GDN_HD), F32)],
        compiler_params=_cparams(("arbitrary", "arbitrary")),
        name="gdn",
    )(*([x] * n_par), *([ba] * n_par), conv_w, alog_row, dt_row, norm_w.reshape(1, GDN_HD),
      conv_buf, s0)


def _t5_bucket_np(dist):
    dist = np.maximum(dist, 0)
    max_exact = REL_BUCKETS // 2
    ratio = np.log(np.maximum(dist, max_exact).astype(np.float32) / np.float32(max_exact)) \
        / np.float32(math.log(REL_MAX_DIST / max_exact))
    large = max_exact + (ratio.astype(np.float32) * np.float32(REL_BUCKETS - max_exact)).astype(np.int32)
    return np.where(dist < max_exact, dist, np.minimum(large, REL_BUCKETS - 1)).astype(np.int32)


def _bucket_table(qpos, kpos, k_valid):
    dist = qpos[:, None] - kpos[None, :]
    ok = (dist >= 0) & (dist < WINDOW) & (kpos[None, :] >= 0) & k_valid[None, :]
    return np.where(ok, _t5_bucket_np(dist), -1).astype(np.int32)


def _bias_from_buckets(bkt, rb_ref, head):
    def body(b, acc):
        return jnp.where(bkt == b, rb_ref[b, head], acc)
    acc = lax.fori_loop(0, REL_BUCKETS, body, jnp.zeros(bkt.shape, F32))
    return jnp.where(bkt < 0, -jnp.inf, acc)


def _swa_prompt_kernel(rb_ref, sk_ref, bkt_ref, q_ref, kvp_ref, kvc_ref, o_ref, bias_scr):
    first = jnp.logical_and(pl.program_id(0) == 0, pl.program_id(1) == 0)
    n = pl.program_id(1)

    @pl.when(first)
    def _():
        bkt = bkt_ref[...]
        for h in range(SWA_Q_HEADS):
            kv, g = divmod(h, SWA_GROUP)
            bias_scr[kv, g * WINDOW:(g + 1) * WINDOW, :] = _bias_from_buckets(bkt, rb_ref, h)

    q = q_ref[...]
    scale = SWA_HD ** -0.5
    heads = range(SWA_KV_HEADS)
    s_prev, s_cur = [], []
    for kv in heads:
        qs = jnp.concatenate(
            [q[:, (kv * SWA_GROUP + g) * SWA_HD:(kv * SWA_GROUP + g + 1) * SWA_HD]
             for g in range(SWA_GROUP)], axis=0).astype(BF16)
        s_prev.append(_bdot_nt(qs, kvp_ref[:, kv * SWA_HD:(kv + 1) * SWA_HD]))
        s_cur.append(_bdot_nt(qs, kvc_ref[:, kv * SWA_HD:(kv + 1) * SWA_HD]))
    p_prev, p_cur, den = [], [], []
    for kv in heads:
        sp = s_prev[kv] * scale + bias_scr[kv, :, 0:WINDOW]
        sp = jnp.where(n > 0, sp, -jnp.inf)
        sc = s_cur[kv] * scale + bias_scr[kv, :, WINDOW:2 * WINDOW]
        sink = jnp.concatenate(
            [jnp.full((WINDOW, 1), sk_ref[kv * SWA_GROUP + g], F32) for g in range(SWA_GROUP)], axis=0)
        m = jnp.maximum(jnp.maximum(jnp.max(sp, axis=-1, keepdims=True),
                                    jnp.max(sc, axis=-1, keepdims=True)), sink)
        pp = jnp.exp(sp - m)
        pc = jnp.exp(sc - m)
        den.append(jnp.sum(pp, axis=-1, keepdims=True) + jnp.sum(pc, axis=-1, keepdims=True)
                   + jnp.exp(sink - m))
        p_prev.append(pp.astype(BF16))
        p_cur.append(pc.astype(BF16))
    outs = []
    for kv in heads:
        v_prev = kvp_ref[:, SWA_KV_DIM + kv * SWA_HD:SWA_KV_DIM + (kv + 1) * SWA_HD]
        v_cur = kvc_ref[:, SWA_KV_DIM + kv * SWA_HD:SWA_KV_DIM + (kv + 1) * SWA_HD]
        o = (_bdot(p_prev[kv], v_prev) + _bdot(p_cur[kv], v_cur)) / den[kv]
        outs.extend(o[g * WINDOW:(g + 1) * WINDOW] for g in range(SWA_GROUP))
    o_ref[...] = jnp.concatenate(outs, axis=1)


def _swa_prompt_call(qkv, rel_bias, sinks):
    nb = SEQ // WINDOW
    qpos = WINDOW + np.arange(WINDOW)
    kpos = np.arange(2 * WINDOW)
    bkt = jnp.asarray(_bucket_table(qpos, kpos, np.ones(2 * WINDOW, bool)))
    kv_col = SWA_Q_DIM // (2 * SWA_KV_DIM)
    smem = pl.BlockSpec(memory_space=pltpu.SMEM)
    return pl.pallas_call(
        _swa_prompt_kernel,
        grid=(BATCH, nb),
        in_specs=[smem, smem,
                  pl.BlockSpec((WINDOW, 2 * WINDOW), lambda b, n: (0, 0)),
                  pl.BlockSpec((WINDOW, SWA_Q_DIM), lambda b, n: (b * nb + n, 0)),
                  pl.BlockSpec((WINDOW, 2 * SWA_KV_DIM),
                               lambda b, n: (jnp.maximum(b * nb + n - 1, 0), kv_col)),
                  pl.BlockSpec((WINDOW, 2 * SWA_KV_DIM), lambda b, n: (b * nb + n, kv_col))],
        out_specs=pl.BlockSpec((WINDOW, SWA_Q_DIM), lambda b, n: (b * nb + n, 0)),
        out_shape=jax.ShapeDtypeStruct((NP_TOK, SWA_Q_DIM), F32),
        scratch_shapes=[pltpu.VMEM((SWA_KV_HEADS, SWA_GROUP * WINDOW, 2 * WINDOW), F32)],
        compiler_params=_cparams(("arbitrary", "arbitrary")),
        name="swa_prompt",
    )(rel_bias, sinks, bkt, qkv, qkv, qkv)


def _swa_sample_kernel(rb_ref, sk_ref, bkt_ref, q_ref, k_ref, v_ref, o_ref, bias_scr, sink_scr):
    rows = SWA_GROUP * DEC_SEQ

    def group_of_row(shape):
        r = lax.broadcasted_iota(I32, shape, 0)
        return sum((r >= g * DEC_SEQ).astype(I32) for g in range(1, SWA_GROUP))

    @pl.when(pl.program_id(0) == 0)
    def _():
        bkt = bkt_ref[...]
        grp = group_of_row((rows, 1))
        grp_full = group_of_row(bkt.shape)
        for kv in range(SWA_KV_HEADS):
            acc = jnp.zeros(bkt.shape, F32)
            snk = jnp.zeros((rows, 1), F32)
            for g in range(SWA_GROUP):
                head = kv * SWA_GROUP + g
                acc = jnp.where(grp_full == g, _bias_from_buckets(bkt, rb_ref, head), acc)
                snk = jnp.where(grp == g, sk_ref[head], snk)
            bias_scr[kv] = acc
            sink_scr[kv] = snk

    scale = SWA_HD ** -0.5
    units = [(s, kv) for s in range(SWA_SB) for kv in range(SWA_KV_HEADS)]
    scores = {(s, kv): _bdot_nt(q_ref[s, kv], k_ref[s, :, kv * SWA_HD:(kv + 1) * SWA_HD])
              for (s, kv) in units}
    probs, den = {}, {}
    for (s, kv) in units:
        sc = scores[s, kv] * scale + bias_scr[kv]
        sink = sink_scr[kv]
        m = jnp.maximum(jnp.max(sc, axis=-1, keepdims=True), sink)
        p = jnp.exp(sc - m)
        den[s, kv] = jnp.sum(p, axis=-1, keepdims=True) + jnp.exp(sink - m)
        probs[s, kv] = p.astype(BF16)
    for (s, kv) in units:
        o_ref[s, kv] = _bdot(probs[s, kv], v_ref[s, :, kv * SWA_HD:(kv + 1) * SWA_HD]) / den[s, kv]


def _swa_sample_call(q_st, kc, vc, rel_bias, sinks):
    rows = SWA_GROUP * DEC_SEQ
    n_keys = kc.shape[1] - (SWA_KPAD - WINDOW) + DEC_SEQ
    qpos = PAST_LEN + np.arange(DEC_SEQ)
    kpos = PAST_LEN - (n_keys - DEC_SEQ) + np.arange(SWA_KPAD)
    bkt4 = _bucket_table(qpos, kpos, np.arange(SWA_KPAD) < n_keys)
    bkt = jnp.asarray(np.tile(bkt4, (SWA_GROUP, 1)))
    smem = pl.BlockSpec(memory_space=pltpu.SMEM)
    q_spec = pl.BlockSpec((SWA_SB, SWA_KV_HEADS, rows, SWA_HD), lambda i: (i, 0, 0, 0))
    kv_spec = pl.BlockSpec((SWA_SB, SWA_KPAD, SWA_KV_DIM), lambda i: (i, 0, 0))
    return pl.pallas_call(
        _swa_sample_kernel,
        grid=(DEC_BATCH // SWA_SB,),
        in_specs=[smem, smem, pl.BlockSpec((rows, SWA_KPAD), lambda i: (0, 0)),
                  q_spec, kv_spec, kv_spec],
        out_specs=q_spec,
        out_shape=jax.ShapeDtypeStruct((DEC_BATCH, SWA_KV_HEADS, rows, SWA_HD), F32),
        scratch_shapes=[pltpu.VMEM((SWA_KV_HEADS, rows, SWA_KPAD), F32),
                        pltpu.VMEM((SWA_KV_HEADS, rows, 1), F32)],
        compiler_params=_cparams(("arbitrary",)),
        name="swa_sample",
    )(rel_bias, sinks, bkt, q_st, kc, vc)


def _swiglu_into(acc_ref, x, wg_ref, wu_ref, wd_ref):
    acc_ref[...] = jnp.zeros_like(acc_ref)

    def body(f, carry):
        gate = jnp.dot(x, wg_ref[f], preferred_element_type=F32)
        up = jnp.dot(x, wu_ref[f], preferred_element_type=F32)
        act = (_silu(gate) * up).astype(BF16)
        acc_ref[...] += jnp.dot(act, wd_ref[f], preferred_element_type=F32)
        return carry

    lax.fori_loop(0, NF, body, 0)


def _ffn_kernel(te_ref, nt_ref, x_ref, wg_ref, wu_ref, wd_ref, o_ref, acc_ref):
    t = pl.program_id(0)

    @pl.when(t < nt_ref[0])
    def _():
        _swiglu_into(acc_ref, x_ref[...].astype(BF16), wg_ref, wu_ref, wd_ref)
        o_ref[...] = acc_ref[...]

    @pl.when(t >= nt_ref[0])
    def _():
        o_ref[...] = jnp.zeros_like(o_ref)


def _dense_ffn_kernel(x_ref, gpre_ref, shp_ref, shs_ref, scp_ref, scs_ref, wg_ref, wu_ref, wd_ref,
                      gpost_ref, gtp_ref, gts_ref, out_ref, acc_ref):
    i = pl.program_id(0)
    x = x_ref[...]
    h = _prenorm(i, x, gpre_ref, shp_ref, shs_ref, scp_ref, scs_ref).astype(BF16)
    _swiglu_into(acc_ref, h, wg_ref, wu_ref, wd_ref)
    out_ref[...] = x + _pick(i, gtp_ref, gts_ref) * (_rms(acc_ref[...]) * gpost_ref[...])


def _dense_ffn_call(x, g_pre, g_post, mod_p, mod_s, wg, wu, wd, layer_pair):
    shp, shs = _mod_specs(3)
    scp, scs = _mod_specs(4)
    gtp, gts = _mod_specs(5)
    tile = pl.BlockSpec((TM, D), lambda i: (i, 0))
    row = pl.BlockSpec((1, D), lambda i: (0, 0))
    up_spec = pl.BlockSpec((None, NF, D, TF), lambda i: (layer_pair, 0, 0, 0))
    dn_spec = pl.BlockSpec((None, NF, TF, D), lambda i: (layer_pair, 0, 0, 0))
    return pl.pallas_call(
        _dense_ffn_kernel,
        grid=(N_TILES,),
        in_specs=[tile, row, shp, shs, scp, scs, up_spec, up_spec, dn_spec, row, gtp, gts],
        out_specs=tile,
        out_shape=jax.ShapeDtypeStruct((TT, D), F32),
        scratch_shapes=[pltpu.VMEM((TM, D), F32)],
        compiler_params=_cparams(("arbitrary",)),
        name="dense_ffn",
    )(x, g_pre.reshape(1, D), mod_p, mod_s, mod_p, mod_s, wg, wu, wd, g_post.reshape(1, D),
      mod_p, mod_s)


def _ffn_weights(w_gate, w_up, w_down):
    wg = w_gate.astype(BF16).reshape(-1, D, NF, TF).transpose(0, 2, 1, 3)
    wu = w_up.astype(BF16).reshape(-1, D, NF, TF).transpose(0, 2, 1, 3)
    wd = w_down.astype(BF16).reshape(-1, NF, TF, D)
    return wg, wu, wd


def _ffn_call(x, tile_expert, n_tiles_used, wg, wu, wd):
    n_rows = x.shape[0]
    n_tiles = n_rows // TM
    up_spec = pl.BlockSpec((None, NF, D, TF), lambda t, te, nt: (te[t], 0, 0, 0))
    dn_spec = pl.BlockSpec((None, NF, TF, D), lambda t, te, nt: (te[t], 0, 0, 0))
    grid_spec = pltpu.PrefetchScalarGridSpec(
        num_scalar_prefetch=2,
        grid=(n_tiles,),
        in_specs=[pl.BlockSpec((TM, D), lambda t, te, nt: (t, 0)), up_spec, up_spec, dn_spec],
        out_specs=pl.BlockSpec((TM, D), lambda t, te, nt: (t, 0)),
        scratch_shapes=[pltpu.VMEM((TM, D), F32)],
    )
    return pl.pallas_call(
        _ffn_kernel,
        grid_spec=grid_spec,
        out_shape=jax.ShapeDtypeStruct((n_rows, D), F32),
        compiler_params=_cparams(("arbitrary",)),
        name="ffn",
    )(tile_expert, n_tiles_used, x, wg, wu, wd)


DMA_UNROLL = 8


def _row_copy(src_ref, dst_ref, src_row, dst_row, sem):
    return pltpu.make_async_copy(src_ref.at[pl.ds(src_row, 1)], dst_ref.at[pl.ds(dst_row, 1)], sem)


def _dispatch_kernel(d0_ref, d1_ref, h_ref, init_ref, out_ref, sem):
    del init_ref

    def start(r, carry):
        _row_copy(h_ref, out_ref, r, d0_ref[0, 0, r], sem).start()
        _row_copy(h_ref, out_ref, r, d1_ref[0, 0, r], sem).start()
        return carry

    def wait(r, carry):
        _row_copy(h_ref, out_ref, 0, 0, sem).wait()
        _row_copy(h_ref, out_ref, 0, 0, sem).wait()
        return carry

    lax.fori_loop(0, TM, start, 0, unroll=DMA_UNROLL)
    lax.fori_loop(0, TM, wait, 0, unroll=DMA_UNROLL)


def _dispatch_call(h, dest):
    idx_spec = pl.BlockSpec((1, 1, TM), lambda i: (i, 0, 0), memory_space=pltpu.SMEM)
    return pl.pallas_call(
        _dispatch_kernel,
        grid=(N_TILES,),
        in_specs=[idx_spec, idx_spec, pl.BlockSpec((TM, D), lambda i: (i, 0)),
                  pl.BlockSpec(memory_space=pl.ANY)],
        out_specs=pl.BlockSpec(memory_space=pl.ANY),
        out_shape=jax.ShapeDtypeStruct((MOE_ROWS, D), h.dtype),
        scratch_shapes=[pltpu.SemaphoreType.DMA(())],
        input_output_aliases={3: 0},
        compiler_params=_cparams(("arbitrary",)),
        name="moe_dispatch",
    )(dest[0].reshape(N_TILES, 1, TM), dest[1].reshape(N_TILES, 1, TM), h,
      jnp.zeros((MOE_ROWS, D), h.dtype))


def _combine_kernel(d0_ref, d1_ref, x_ref, wt_ref, ys_ref, g_ref, gtp_ref, gts_ref, out_ref, buf, sem):
    i = pl.program_id(0)

    def start(r, carry):
        _row_copy(ys_ref, buf.at[0], d0_ref[0, 0, r], r, sem).start()
        _row_copy(ys_ref, buf.at[1], d1_ref[0, 0, r], r, sem).start()
        return carry

    def wait(r, carry):
        _row_copy(ys_ref, buf.at[0], 0, 0, sem).wait()
        _row_copy(ys_ref, buf.at[1], 0, 0, sem).wait()
        return carry

    lax.fori_loop(0, TM, start, 0, unroll=DMA_UNROLL)
    lax.fori_loop(0, TM, wait, 0, unroll=DMA_UNROLL)
    wt = wt_ref[...]
    y = wt[:, 0:1] * buf[0] + wt[:, 1:2] * buf[1]
    out_ref[...] = x_ref[...] + _pick(i, gtp_ref, gts_ref) * (_rms(y) * g_ref[...])


def _combine_call(x, ys, dest, wts, g, mod_p, mod_s, gate_chunk):
    gtp, gts = _mod_specs(gate_chunk)
    tile = pl.BlockSpec((TM, D), lambda i: (i, 0))
    idx_spec = pl.BlockSpec((1, 1, TM), lambda i: (i, 0, 0), memory_space=pltpu.SMEM)
    return pl.pallas_call(
        _combine_kernel,
        grid=(N_TILES,),
        in_specs=[idx_spec, idx_spec, tile, pl.BlockSpec((TM, LANES), lambda i: (i, 0)),
                  pl.BlockSpec(memory_space=pl.ANY),
                  pl.BlockSpec((1, D), lambda i: (0, 0)), gtp, gts],
        out_specs=tile,
        out_shape=jax.ShapeDtypeStruct((TT, D), F32),
        scratch_shapes=[pltpu.VMEM((2, TM, D), F32), pltpu.SemaphoreType.DMA(())],
        compiler_params=_cparams(("arbitrary",)),
        name="moe_combine",
    )(dest[0].reshape(N_TILES, 1, TM), dest[1].reshape(N_TILES, 1, TM), x, wts, ys,
      g.reshape(1, D), mod_p, mod_s)


def _route(top_idx):
    flat_e = top_idx.T.reshape(-1)
    onehot = (flat_e[:, None] == jnp.arange(N_EXPERTS, dtype=I32)[None, :]).astype(I32)
    csum = jnp.cumsum(onehot, axis=0)
    rank = jnp.sum((csum - onehot) * onehot, axis=1)
    counts = csum[-1]
    padded = ((counts + TM - 1) // TM) * TM
    ends = jnp.cumsum(padded)
    starts = ends - padded
    dest = (jnp.sum(onehot * starts[None, :], axis=1) + rank).astype(I32)
    tile_start = jnp.arange(MOE_TILES, dtype=I32) * TM
    tile_expert = jnp.minimum(jnp.sum(tile_start[:, None] >= ends[None, :], axis=1),
                              N_EXPERTS - 1).astype(I32)
    n_used = (ends[-1] // TM).astype(I32).reshape(1)
    return dest.reshape(2, TT), tile_expert, n_used


def _gdn_layer(x, g_pre, w_in, conv_w, a_log, dt_bias, norm_w, w_out, state_conv, state_rec_all,
               layer_pair, g_post, mod_p, mod_s):
    w_main = w_in[:, :GDN_MAIN_DIM]
    w_ba = jnp.pad(w_in[:, GDN_MAIN_DIM:], ((0, 0), (0, LANES - 2 * GDN_V_HEADS)))
    qkvz = _proj_call(x, g_pre, mod_p, mod_s, w_main, jnp.zeros((GDN_MAIN_DIM,), F32), 1024)
    ba = _proj_call(x, g_pre, mod_p, mod_s, w_ba, jnp.zeros((LANES,), F32), LANES)

    tb = GDN_CHUNK * GDN_NB
    n_steps = SEQ // tb
    o_p, rec_p = _gdn_call(
        qkvz.reshape(TT // tb, tb, GDN_MAIN_DIM), ba.reshape(TT // tb, tb, LANES),
        conv_w, a_log, dt_bias, norm_w,
        jnp.zeros((BATCH, SUBLANES, GDN_CONV_DIM), F32),
        jnp.zeros((BATCH, GDN_V_HEADS, GDN_HD, GDN_HD), F32),
        n_seq=BATCH, n_steps=n_steps, chunk=GDN_CHUNK, n_chunks=GDN_NB, t_real=tb,
        n_par=GDN_PROMPT_PAR)
    o_p = o_p.reshape(NP_TOK, GDN_VAL_DIM)

    row_pad = ((0, 0), (0, GDN_SAMPLE_ROWS - DEC_SEQ), (0, 0))
    qkvz_s = qkvz[NP_TOK:].reshape(DEC_BATCH, DEC_SEQ, GDN_MAIN_DIM)
    ba_s = ba[NP_TOK:].reshape(DEC_BATCH, DEC_SEQ, LANES)
    cbuf_s = jnp.pad(state_conv, ((0, 0), (SUBLANES - (CONV_WIDTH - 1), 0), (0, 0)))
    o_s, rec_s = _gdn_call(
        jnp.pad(qkvz_s, row_pad), jnp.pad(ba_s, row_pad), conv_w, a_log, dt_bias, norm_w,
        cbuf_s, state_rec_all,
        n_seq=DEC_BATCH, n_steps=1, chunk=GDN_SAMPLE_ROWS, n_chunks=1, t_real=DEC_SEQ,
        n_par=GDN_SAMPLE_PAR, s0_first_seq=layer_pair * DEC_BATCH)
    o_s = o_s[:, 0, :DEC_SEQ].reshape(NS_TOK, GDN_VAL_DIM)

    keep = CONV_WIDTH - 1
    pre_p = jnp.stack([qkvz[(b + 1) * SEQ - keep:(b + 1) * SEQ, :GDN_CONV_DIM] for b in range(BATCH)])
    pre_s = jnp.concatenate([state_conv, qkvz_s[:, :, :GDN_CONV_DIM]], axis=1)[:, -keep:]
    x = _post_mm_call(x, o_p, o_s, w_out, jnp.zeros((D,), F32), g_post, mod_p, mod_s, 2)
    return x, pre_p, rec_p, pre_s, rec_s


def _swa_layer(x, g_pre, w_in, b_in, sinks, w_out, b_out, rel_bias, cache_k, cache_v,
               g_post, mod_p, mod_s):
    qkv = _proj_call(x, g_pre, mod_p, mod_s, w_in, b_in, SWA_PROJ_DIM)
    o_p = _swa_prompt_call(qkv, rel_bias, sinks)
    last = jnp.stack([qkv[(b + 1) * SEQ - WINDOW:(b + 1) * SEQ, SWA_Q_DIM:] for b in range(BATCH)])
    new_k_p = last[:, :, :SWA_KV_DIM]
    new_v_p = last[:, :, SWA_KV_DIM:]

    qkv_s = qkv[NP_TOK:].reshape(DEC_BATCH, DEC_SEQ, SWA_PROJ_DIM)
    q_st = qkv_s[:, :, :SWA_Q_DIM].reshape(DEC_BATCH, DEC_SEQ, SWA_KV_HEADS, SWA_GROUP, SWA_HD)
    q_st = q_st.transpose(0, 2, 3, 1, 4).reshape(DEC_BATCH, SWA_KV_HEADS, SWA_GROUP * DEC_SEQ, SWA_HD)
    wb = cache_k.shape[1]
    zpad = jnp.zeros((DEC_BATCH, SWA_KPAD - WINDOW - DEC_SEQ + (WINDOW - wb), SWA_KV_DIM), F32)
    kc = jnp.concatenate([cache_k.reshape(DEC_BATCH, wb, SWA_KV_DIM),
                          qkv_s[:, :, SWA_Q_DIM:SWA_Q_DIM + SWA_KV_DIM], zpad], axis=1)
    vc = jnp.concatenate([cache_v.reshape(DEC_BATCH, wb, SWA_KV_DIM),
                          qkv_s[:, :, SWA_Q_DIM + SWA_KV_DIM:], zpad], axis=1)
    o_st = _swa_sample_call(q_st, kc, vc, rel_bias, sinks)
    o_s = o_st.reshape(DEC_BATCH, SWA_KV_HEADS, SWA_GROUP, DEC_SEQ, SWA_HD)
    o_s = o_s.transpose(0, 3, 1, 2, 4).reshape(NS_TOK, SWA_Q_DIM)
    new_k_s = kc[:, DEC_SEQ:DEC_SEQ + wb]
    new_v_s = vc[:, DEC_SEQ:DEC_SEQ + wb]

    x = _post_mm_call(x, o_p, o_s, w_out, b_out, g_post, mod_p, mod_s, 2)
    shape_p = (BATCH, WINDOW, SWA_KV_HEADS, SWA_HD)
    shape_s = (DEC_BATCH, wb, SWA_KV_HEADS, SWA_HD)
    return (x, new_k_p.reshape(shape_p), new_v_p.reshape(shape_p),
            new_k_s.reshape(shape_s), new_v_s.reshape(shape_s))


def kernel(x_prompt, x_sample, c_prompt, c_sample, state_conv, state_rec, cache_win_k, cache_win_v, w_mod, b_mod, g_pre_mix, g_post_mix, g_pre_ffn, g_post_ffn, gdn_w_in, gdn_conv_w, gdn_a_log, gdn_dt_bias, gdn_norm_w, gdn_w_out, swa_w_in, swa_b_in, swa_sinks, swa_w_out, swa_b_out, rel_bias, ffn_w_gate, ffn_w_up, ffn_w_down, moe_w_router, moe_b_router, moe_w_gate, moe_w_up, moe_w_down):
    x = jnp.concatenate([x_prompt.reshape(NP_TOK, D), x_sample.reshape(NS_TOK, D)], axis=0)
    n_c = BATCH + DEC_BATCH
    c_all = jnp.concatenate([c_prompt, c_sample, jnp.zeros((-n_c % SUBLANES, D), F32)], axis=0)
    m_all = _mod_call(c_all, w_mod, b_mod)
    ffn_w = _ffn_weights(ffn_w_gate, ffn_w_up, ffn_w_down)
    moe_w = _ffn_weights(moe_w_gate, moe_w_up, moe_w_down)
    state_rec_all = state_rec.reshape((-1,) + state_rec.shape[2:])

    conv_p, rec_p, conv_s, rec_s = [], [], [], []
    k_p, v_p, k_s, v_s = [], [], [], []
    for layer in range(DEPTH):
        j = layer // 2
        mod_p = m_all[layer, :BATCH].reshape(BATCH, 1, 6 * D)
        mod_s = jnp.repeat(m_all[layer, BATCH:n_c], DEC_SEQ, axis=0)
        if layer % 2 == 0:
            x, cp, rp, cs, rs = _gdn_layer(
                x, g_pre_mix[layer], gdn_w_in[j], gdn_conv_w[j], gdn_a_log[j], gdn_dt_bias[j],
                gdn_norm_w[j], gdn_w_out[j], state_conv[j], state_rec_all, j, g_post_mix[layer],
                mod_p, mod_s)
            conv_p.append(cp); rec_p.append(rp); conv_s.append(cs); rec_s.append(rs)
        else:
            x, kp, vp, ks, vs = _swa_layer(
                x, g_pre_mix[layer], swa_w_in[j], swa_b_in[j], swa_sinks[j], swa_w_out[j],
                swa_b_out[j], rel_bias, cache_win_k[j], cache_win_v[j], g_post_mix[layer],
                mod_p, mod_s)
            k_p.append(kp); v_p.append(vp); k_s.append(ks); v_s.append(vs)

        if layer % 2 == 0:
            x = _dense_ffn_call(x, g_pre_ffn[layer], g_post_ffn[layer], mod_p, mod_s, *ffn_w, j)
        else:
            h, idx, wts = _prenorm_router_call(x, g_pre_ffn[layer], mod_p, mod_s, 3, 4,
                                               moe_w_router[j], moe_b_router[j])
            dest, tile_expert, n_used = _route(idx[:, :2])
            xs = _dispatch_call(h, dest)
            ys = _ffn_call(xs, tile_expert + j * N_EXPERTS, n_used, *moe_w)
            x = _combine_call(x, ys, dest, wts, g_post_ffn[layer], mod_p, mod_s, 5)

    y_prompt = x[:NP_TOK].reshape(BATCH, SEQ, D)
    y_sample = x[NP_TOK:].reshape(DEC_BATCH, DEC_SEQ, D)
    return (y_prompt, y_sample, jnp.stack(conv_p), jnp.stack(rec_p), jnp.stack(k_p), jnp.stack(v_p),
            jnp.stack(conv_s), jnp.stack(rec_s), jnp.stack(k_s), jnp.stack(v_s))
```

```python
import functools
import math

import numpy as np
import jax
import jax.numpy as jnp
from jax import lax
from jax.experimental import pallas as pl
from jax.experimental.pallas import tpu as pltpu

F32 = jnp.float32
BF16 = jnp.bfloat16
I32 = jnp.int32
HIGHEST = lax.Precision.HIGHEST

D = 1024
BATCH = 4
SEQ = 4096
DEPTH = 4
DEC_BATCH = 128
DEC_SEQ = 4
PAST_LEN = 8192
GDN_QK_HEADS = 4
GDN_V_HEADS = 8
GDN_HD = 128
GDN_KEY_DIM = GDN_QK_HEADS * GDN_HD
GDN_VAL_DIM = GDN_V_HEADS * GDN_HD
GDN_CONV_DIM = 2 * GDN_KEY_DIM + GDN_VAL_DIM
GDN_MAIN_DIM = GDN_CONV_DIM + GDN_VAL_DIM
CONV_WIDTH = 4
GDN_CHUNK = 64
SWA_Q_HEADS = 16
SWA_KV_HEADS = 4
SWA_GROUP = SWA_Q_HEADS // SWA_KV_HEADS
SWA_HD = 64
SWA_Q_DIM = SWA_Q_HEADS * SWA_HD
SWA_KV_DIM = SWA_KV_HEADS * SWA_HD
SWA_PROJ_DIM = SWA_Q_DIM + 2 * SWA_KV_DIM
WINDOW = 128
REL_BUCKETS = 32
REL_MAX_DIST = 128
D_FF = 2816
N_EXPERTS = 8
RMS_EPS = 1e-6

LANES = 128
SUBLANES = 8
VMEM_LIMIT = 56 * 1024 * 1024

TM = 512
NP_TOK = BATCH * SEQ
NS_TOK = DEC_BATCH * DEC_SEQ
TT = NP_TOK + NS_TOK
NP_TILES = NP_TOK // TM
NS_TILES = NS_TOK // TM
N_TILES = NP_TILES + NS_TILES
TILES_PER_SEQ = SEQ // TM
TF = 256
NF = D_FF // TF
MOE_ROWS = 2 * TT + N_EXPERTS * TM
MOE_TILES = MOE_ROWS // TM
GDN_NB = 2
GDN_PROMPT_PAR = 4
GDN_SAMPLE_PAR = 4
GDN_SAMPLE_ROWS = 8
SWA_SB = 8
SWA_KPAD = WINDOW + 8

assert NP_TOK % TM == 0 and NS_TOK % TM == 0 and SEQ % TM == 0
assert D_FF % TF == 0 and DEC_SEQ <= GDN_SAMPLE_ROWS and DEC_SEQ <= 8


def _cparams(sem):
    return pltpu.CompilerParams(dimension_semantics=sem, vmem_limit_bytes=VMEM_LIMIT)


def _bdot(a, b):
    return jnp.dot(a.astype(BF16), b.astype(BF16), preferred_element_type=F32)


def _bdot_nt(a, b):
    return lax.dot_general(a.astype(BF16), b.astype(BF16), (((1,), (1,)), ((), ())),
                           preferred_element_type=F32)


def _silu(x):
    return x * jax.nn.sigmoid(x)


def _rms(x):
    return x * lax.rsqrt(jnp.mean(x * x, axis=-1, keepdims=True) + RMS_EPS)


def _mod_kernel(c_ref, w_ref, b_ref, o_ref):
    c = c_ref[...]
    o_ref[0] = _bdot(_silu(c), w_ref[0]) + b_ref[0]


def _mod_call(c_all, w_mod, b_mod):
    n = c_all.shape[0]
    tn = D
    return pl.pallas_call(
        _mod_kernel,
        grid=(DEPTH, 6 * D // tn),
        in_specs=[
            pl.BlockSpec((n, D), lambda l, j: (0, 0)),
            pl.BlockSpec((1, D, tn), lambda l, j: (l, 0, j)),
            pl.BlockSpec((1, 1, tn), lambda l, j: (l, 0, j)),
        ],
        out_specs=pl.BlockSpec((1, n, tn), lambda l, j: (l, 0, j)),
        out_shape=jax.ShapeDtypeStruct((DEPTH, n, 6 * D), F32),
        compiler_params=_cparams(("arbitrary", "arbitrary")),
        name="modulation",
    )(c_all, w_mod, b_mod.reshape(DEPTH, 1, 6 * D))


def _mod_specs(chunk, token_axis=0):
    p = pl.BlockSpec((1, 1, D), lambda *ids: (jnp.minimum(ids[token_axis] // TILES_PER_SEQ, BATCH - 1),
                                              0, chunk))
    s = pl.BlockSpec((TM, D), lambda *ids: (jnp.maximum(ids[token_axis] - NP_TILES, 0), chunk))
    return p, s


def _pick(i, p_ref, s_ref):
    return jnp.where(i < NP_TILES, p_ref[0], s_ref[...])


def _prenorm(i, x, g_ref, shp_ref, shs_ref, scp_ref, scs_ref):
    h = _rms(x) * g_ref[...]
    return h * (1.0 + _pick(i, scp_ref, scs_ref)) + _pick(i, shp_ref, shs_ref)


def _prenorm_router_kernel(x_ref, g_ref, shp_ref, shs_ref, scp_ref, scs_ref, wr_ref, br_ref,
                           h_ref, idx_ref, wt_ref):
    h = _prenorm(pl.program_id(0), x_ref[...], g_ref, shp_ref, shs_ref, scp_ref, scs_ref)
    h_ref[...] = h
    lane = lax.broadcasted_iota(I32, (TM, LANES), 1).astype(F32)
    logits = jnp.dot(h, wr_ref[...], precision=HIGHEST, preferred_element_type=F32) + br_ref[...]
    logits = jnp.where(lane < N_EXPERTS, logits, -jnp.inf)
    m1 = jnp.max(logits, axis=-1, keepdims=True)
    i1 = jnp.min(jnp.where(logits == m1, lane, float(LANES)), axis=-1, keepdims=True)
    rest = jnp.where(lane == i1, -jnp.inf, logits)
    m2 = jnp.max(rest, axis=-1, keepdims=True)
    i2 = jnp.min(jnp.where(rest == m2, lane, float(LANES)), axis=-1, keepdims=True)
    e2 = jnp.exp(m2 - m1)
    w1 = 1.0 / (1.0 + e2)
    w2 = e2 / (1.0 + e2)
    idx_ref[...] = jnp.where(lane == 0, i1, jnp.where(lane == 1, i2, 0.0)).astype(I32)
    wt_ref[...] = jnp.where(lane == 0, w1, jnp.where(lane == 1, w2, 0.0))


def _prenorm_router_call(x, g, mod_p, mod_s, sh_chunk, sc_chunk, w_router, b_router):
    shp, shs = _mod_specs(sh_chunk)
    scp, scs = _mod_specs(sc_chunk)
    tile = pl.BlockSpec((TM, D), lambda i: (i, 0))
    small = pl.BlockSpec((TM, LANES), lambda i: (i, 0))
    wr = jnp.pad(w_router, ((0, 0), (0, LANES - N_EXPERTS)))
    br = jnp.pad(b_router, (0, LANES - N_EXPERTS)).reshape(1, LANES)
    return pl.pallas_call(
        _prenorm_router_kernel,
        grid=(N_TILES,),
        in_specs=[tile, pl.BlockSpec((1, D), lambda i: (0, 0)), shp, shs, scp, scs,
                  pl.BlockSpec((D, LANES), lambda i: (0, 0)),
                  pl.BlockSpec((1, LANES), lambda i: (0, 0))],
        out_specs=[tile, small, small],
        out_shape=[jax.ShapeDtypeStruct((TT, D), F32),
                   jax.ShapeDtypeStruct((TT, LANES), I32),
                   jax.ShapeDtypeStruct((TT, LANES), F32)],
        compiler_params=_cparams(("arbitrary",)),
        name="prenorm_router",
    )(x, g.reshape(1, D), mod_p, mod_s, mod_p, mod_s, wr, br)


def _proj_kernel(x_ref, g_ref, shp_ref, shs_ref, scp_ref, scs_ref, w_ref, b_ref, o_ref, wbf_ref):
    i = pl.program_id(1)

    @pl.when(i == 0)
    def _():
        wbf_ref[...] = w_ref[...].astype(BF16)

    h = _prenorm(i, x_ref[...], g_ref, shp_ref, shs_ref, scp_ref, scs_ref).astype(BF16)
    o_ref[...] = jnp.dot(h, wbf_ref[...], preferred_element_type=F32) + b_ref[...]


def _proj_call(x, g, mod_p, mod_s, w, b, tn):
    k, n = w.shape
    shp, shs = _mod_specs(0, token_axis=1)
    scp, scs = _mod_specs(1, token_axis=1)
    return pl.pallas_call(
        _proj_kernel,
        grid=(n // tn, N_TILES),
        in_specs=[
            pl.BlockSpec((TM, k), lambda j, i: (i, 0)),
            pl.BlockSpec((1, k), lambda j, i: (0, 0)), shp, shs, scp, scs,
            pl.BlockSpec((k, tn), lambda j, i: (0, j)),
            pl.BlockSpec((1, tn), lambda j, i: (0, j)),
        ],
        out_specs=pl.BlockSpec((TM, tn), lambda j, i: (i, j)),
        out_shape=jax.ShapeDtypeStruct((TT, n), F32),
        scratch_shapes=[pltpu.VMEM((k, tn), BF16)],
        compiler_params=_cparams(("arbitrary", "arbitrary")),
        name="proj",
    )(x, g.reshape(1, k), mod_p, mod_s, mod_p, mod_s, w, b.reshape(1, n))


def _post_mm_kernel(x_ref, op_ref, os_ref, w_ref, b_ref, g_ref, gtp_ref, gts_ref, out_ref, wbf_ref):
    i = pl.program_id(0)

    @pl.when(i == 0)
    def _():
        wbf_ref[...] = w_ref[...].astype(BF16)

    o = jnp.where(i < NP_TILES, op_ref[...], os_ref[...]).astype(BF16)
    y = jnp.dot(o, wbf_ref[...], preferred_element_type=F32) + b_ref[...]
    out_ref[...] = x_ref[...] + _pick(i, gtp_ref, gts_ref) * (_rms(y) * g_ref[...])


def _post_mm_call(x, o_p, o_s, w, b, g, mod_p, mod_s, gate_chunk):
    gtp, gts = _mod_specs(gate_chunk)
    tile = pl.BlockSpec((TM, D), lambda i: (i, 0))
    row = pl.BlockSpec((1, D), lambda i: (0, 0))
    return pl.pallas_call(
        _post_mm_kernel,
        grid=(N_TILES,),
        in_specs=[tile,
                  pl.BlockSpec((TM, D), lambda i: (jnp.minimum(i, NP_TILES - 1), 0)),
                  pl.BlockSpec((TM, D), lambda i: (jnp.maximum(i - NP_TILES, 0), 0)),
                  pl.BlockSpec((D, D), lambda i: (0, 0)), row, row, gtp, gts],
        out_specs=tile,
        out_shape=jax.ShapeDtypeStruct((TT, D), F32),
        scratch_shapes=[pltpu.VMEM((D, D), BF16)],
        compiler_params=_cparams(("arbitrary",)),
        name="post_mm",
    )(x, o_p, o_s, w, b.reshape(1, D), g.reshape(1, D), mod_p, mod_s)


def _gdn_kernel(*refs, chunk, n_chunks, t_real, n_par):
    x_refs = refs[:n_par]
    ba_refs = refs[n_par:2 * n_par]
    (cw_ref, alog_ref, dt_ref, nw_ref, cb_ref, s0_ref, o_ref, sfin_ref, xbuf, s_scr) = refs[2 * n_par:]
    c_sz = chunk
    tb = chunk * n_chunks
    j = pl.program_id(1)
    rep = GDN_V_HEADS // GDN_QK_HEADS
    units = [(p, h) for p in range(n_par) for h in range(GDN_V_HEADS)]

    @pl.when(j == 0)
    def _():
        s_scr[...] = s0_ref[...]
        xbuf[:, 0:SUBLANES, :] = cb_ref[...]

    for p in range(n_par):
        xbuf[p, SUBLANES:SUBLANES + tb, :] = x_refs[p][:, 0:GDN_CONV_DIM]

    ii = lax.broadcasted_iota(I32, (c_sz, c_sz), 0)
    jj = lax.broadcasted_iota(I32, (c_sz, c_sz), 1)
    causal = ii >= jj
    strict = ii > jj
    tri = causal.astype(F32)
    eye_c = (ii == jj).astype(F32)
    pair_mask = [jnp.logical_and((ii >> (l + 1)) == (jj >> (l + 1)), (ii >> l) != (jj >> l))
                 for l in range(int(math.log2(c_sz)))]
    eye_t = (lax.broadcasted_iota(I32, (2 * SUBLANES, LANES), 0)
             == lax.broadcasted_iota(I32, (2 * SUBLANES, LANES), 1)).astype(F32)
    lane = lax.broadcasted_iota(I32, (c_sz, LANES), 1)
    row = lax.broadcasted_iota(I32, (c_sz, LANES), 0)
    cw = cw_ref[...]
    neg_a = -jnp.exp(alog_ref[...])
    dt = dt_ref[...]
    nw = nw_ref[...]

    def chunk_body(c, carry):
        r0 = pl.multiple_of(c * c_sz, c_sz)
        xc, beta, gc, rows_t = [], [], [], []
        for p in range(n_par):
            win = xbuf[p, pl.ds(r0, c_sz + SUBLANES), :]
            y = win[SUBLANES:SUBLANES + c_sz] * cw[3:4]
            for tap in range(1, CONV_WIDTH):
                y = y + win[SUBLANES - tap:SUBLANES - tap + c_sz] * cw[3 - tap:4 - tap]
            xc.append(_silu(y))
            ba = ba_refs[p][pl.ds(r0, c_sz), :]
            b_p = jax.nn.sigmoid(ba)
            g_p = neg_a * jax.nn.softplus(ba + dt)
            if t_real < tb:
                live = (row + r0) < t_real
                b_p = jnp.where(live, b_p, 0.0)
                g_p = jnp.where(live, g_p, 0.0)
            gc_p = jnp.dot(tri, g_p, precision=HIGHEST, preferred_element_type=F32)
            cols = jnp.where(lane < GDN_V_HEADS, b_p, gc_p)
            rows_t.append(lax.dot_general(eye_t, cols, (((1,), (1,)), ((), ())), precision=HIGHEST,
                                          preferred_element_type=F32))
            beta.append(b_p)
            gc.append(gc_p)

        qn, kn, kk, qk = {}, {}, {}, {}
        for p in range(n_par):
            for hq in range(GDN_QK_HEADS):
                q = xc[p][:, hq * GDN_HD:(hq + 1) * GDN_HD]
                k = xc[p][:, GDN_KEY_DIM + hq * GDN_HD:GDN_KEY_DIM + (hq + 1) * GDN_HD]
                q = q * lax.rsqrt(jnp.sum(q * q, axis=-1, keepdims=True) + RMS_EPS) * (GDN_HD ** -0.5)
                k = k * lax.rsqrt(jnp.sum(k * k, axis=-1, keepdims=True) + RMS_EPS)
                qn[p, hq], kn[p, hq] = q, k
        for key in qn:
            kk[key] = _bdot_nt(kn[key], kn[key])
            qk[key] = _bdot_nt(qn[key], kn[key])

        gcc, gcl, bcol, decay, a_mat, inv = {}, {}, {}, {}, {}, {}
        for (p, h) in units:
            gcc[p, h] = gc[p][:, GDN_V_HEADS + h:GDN_V_HEADS + h + 1]
            gcr = rows_t[p][GDN_V_HEADS + h:GDN_V_HEADS + h + 1, :]
            bcol[p, h] = beta[p][:, h:h + 1]
            gcl[p, h] = gcc[p, h][c_sz - 1:c_sz, :]
            decay[p, h] = jnp.exp(jnp.where(causal, gcc[p, h] - gcr, -jnp.inf))
            a_mat[p, h] = jnp.where(strict, bcol[p, h] * kk[p, h // rep] * decay[p, h], 0.0)
            inv[p, h] = eye_c - jnp.where(pair_mask[0], a_mat[p, h], 0.0)
        for lvl in range(1, len(pair_mask)):
            t1 = {u: _bdot(jnp.where(pair_mask[lvl], a_mat[u], 0.0), inv[u]) for u in units}
            t2 = {u: _bdot(inv[u], t1[u]) for u in units}
            inv = {u: inv[u] - t2[u] for u in units}

        egc = {u: jnp.exp(gcc[u]) for u in units}
        sol = {}
        for (p, h) in units:
            v = xc[p][:, 2 * GDN_KEY_DIM + h * GDN_HD:2 * GDN_KEY_DIM + (h + 1) * GDN_HD]
            rhs = jnp.concatenate([v * bcol[p, h], kn[p, h // rep] * (bcol[p, h] * egc[p, h])], axis=1)
            sol[p, h] = _bdot(inv[p, h], rhs)
        ws = {}
        for (p, h) in units:
            q_dec = qn[p, h // rep] * egc[p, h]
            ws[p, h] = _bdot(jnp.concatenate([sol[p, h][:, GDN_HD:], q_dec], axis=0), s_scr[p, h])
        u_new = {u: sol[u][:, :GDN_HD] - ws[u][:c_sz] for u in units}
        o_part, s_part = {}, {}
        for (p, h) in units:
            qkm = jnp.where(causal, qk[p, h // rep] * decay[p, h], 0.0)
            k_dec = kn[p, h // rep] * jnp.exp(gcl[p, h] - gcc[p, h])
            o_part[p, h] = _bdot(qkm, u_new[p, h])
            s_part[p, h] = lax.dot_general(k_dec.astype(BF16), u_new[p, h].astype(BF16),
                                           (((0,), (0,)), ((), ())), preferred_element_type=F32)
        for (p, h) in units:
            s_scr[p, h] = s_scr[p, h] * jnp.exp(gcl[p, h]) + s_part[p, h]
            o = ws[p, h][c_sz:] + o_part[p, h]
            z = x_refs[p][pl.ds(r0, c_sz), GDN_CONV_DIM + h * GDN_HD:GDN_CONV_DIM + (h + 1) * GDN_HD]
            o_ref[p, pl.ds(r0, c_sz), h * GDN_HD:(h + 1) * GDN_HD] = _rms(o) * nw * _silu(z)
        return carry

    lax.fori_loop(0, n_chunks, chunk_body, 0)
    xbuf[:, 0:SUBLANES, :] = xbuf[:, tb:tb + SUBLANES, :]

    @pl.when(j == pl.num_programs(1) - 1)
    def _():
        sfin_ref[...] = s_scr[...]


def _gdn_call(x, ba, conv_w, a_log, dt_bias, norm_w, conv_buf, s0, *, n_seq, n_steps, chunk,
              n_chunks, t_real, n_par, s0_first_seq=0):
    s0_block = s0_first_seq // n_par
    tb = chunk * n_chunks
    pad = LANES - 2 * GDN_V_HEADS
    alog_row = jnp.pad(a_log, (GDN_V_HEADS, pad)).reshape(1, LANES)
    dt_row = jnp.pad(dt_bias, (GDN_V_HEADS, pad)).reshape(1, LANES)

    def seq_block(width, p):
        return pl.BlockSpec((None, tb, width), lambda s, j: ((s * n_par + p) * n_steps + j, 0, 0))

    const = lambda s, j: (0, 0)
    state_spec = pl.BlockSpec((n_par, GDN_V_HEADS, GDN_HD, GDN_HD), lambda s, j: (s, 0, 0, 0))
    return pl.pallas_call(
        functools.partial(_gdn_kernel, chunk=chunk, n_chunks=n_chunks, t_real=t_real, n_par=n_par),
        grid=(n_seq // n_par, n_steps),
        in_specs=[seq_block(GDN_MAIN_DIM, p) for p in range(n_par)]
                 + [seq_block(LANES, p) for p in range(n_par)]
                 + [pl.BlockSpec((CONV_WIDTH, GDN_CONV_DIM), const),
                    pl.BlockSpec((1, LANES), const), pl.BlockSpec((1, LANES), const),
                    pl.BlockSpec((1, GDN_HD), const),
                    pl.BlockSpec((n_par, SUBLANES, GDN_CONV_DIM), lambda s, j: (s, 0, 0)),
                    pl.BlockSpec((n_par, GDN_V_HEADS, GDN_HD, GDN_HD),
                                 lambda s, j: (s + s0_block, 0, 0, 0))],
        out_specs=[pl.BlockSpec((n_par, None, tb, GDN_VAL_DIM), lambda s, j: (s, j, 0, 0)),
                   state_spec],
        out_shape=[jax.ShapeDtypeStruct((n_seq, n_steps, tb, GDN_VAL_DIM), F32),
                   jax.ShapeDtypeStruct((n_seq, GDN_V_HEADS, GDN_HD, GDN_HD), F32)],
        scratch_shapes=[pltpu.VMEM((n_par, tb + SUBLANES, GDN_CONV_DIM), F32),
                        pltpu.VMEM((n_par, GDN_V_HEADS, GDN_HD, GDN_HD), F32)],
        compiler_params=_cparams(("arbitrary", "arbitrary")),
        name="gdn",
    )(*([x] * n_par), *([ba] * n_par), conv_w, alog_row, dt_row, norm_w.reshape(1, GDN_HD),
      conv_buf, s0)


def _t5_bucket_np(dist):
    dist = np.maximum(dist, 0)
    max_exact = REL_BUCKETS // 2
    ratio = np.log(np.maximum(dist, max_exact).astype(np.float32) / np.float32(max_exact)) \
        / np.float32(math.log(REL_MAX_DIST / max_exact))
    large = max_exact + (ratio.astype(np.float32) * np.float32(REL_BUCKETS - max_exact)).astype(np.int32)
    return np.where(dist < max_exact, dist, np.minimum(large, REL_BUCKETS - 1)).astype(np.int32)


def _bucket_table(qpos, kpos, k_valid):
    dist = qpos[:, None] - kpos[None, :]
    ok = (dist >= 0) & (dist < WINDOW) & (kpos[None, :] >= 0) & k_valid[None, :]
    return np.where(ok, _t5_bucket_np(dist), -1).astype(np.int32)


def _bias_from_buckets(bkt, rb_ref, head):
    def body(b, acc):
        return jnp.where(bkt == b, rb_ref[b, head], acc)
    acc = lax.fori_loop(0, REL_BUCKETS, body, jnp.zeros(bkt.shape, F32))
    return jnp.where(bkt < 0, -jnp.inf, acc)


def _swa_prompt_kernel(rb_ref, sk_ref, bkt_ref, q_ref, kvp_ref, kvc_ref, o_ref, bias_scr):
    first = jnp.logical_and(pl.program_id(0) == 0, pl.program_id(1) == 0)
    n = pl.program_id(1)

    @pl.when(first)
    def _():
        bkt = bkt_ref[...]
        for h in range(SWA_Q_HEADS):
            kv, g = divmod(h, SWA_GROUP)
            bias_scr[kv, g * WINDOW:(g + 1) * WINDOW, :] = _bias_from_buckets(bkt, rb_ref, h)

    q = q_ref[...]
    scale = SWA_HD ** -0.5
    heads = range(SWA_KV_HEADS)
    s_prev, s_cur = [], []
    for kv in heads:
        qs = jnp.concatenate(
            [q[:, (kv * SWA_GROUP + g) * SWA_HD:(kv * SWA_GROUP + g + 1) * SWA_HD]
             for g in range(SWA_GROUP)], axis=0).astype(BF16)
        s_prev.append(_bdot_nt(qs, kvp_ref[:, kv * SWA_HD:(kv + 1) * SWA_HD]))
        s_cur.append(_bdot_nt(qs, kvc_ref[:, kv * SWA_HD:(kv + 1) * SWA_HD]))
    p_prev, p_cur, den = [], [], []
    for kv in heads:
        sp = s_prev[kv] * scale + bias_scr[kv, :, 0:WINDOW]
        sp = jnp.where(n > 0, sp, -jnp.inf)
        sc = s_cur[kv] * scale + bias_scr[kv, :, WINDOW:2 * WINDOW]
        sink = jnp.concatenate(
            [jnp.full((WINDOW, 1), sk_ref[kv * SWA_GROUP + g], F32) for g in range(SWA_GROUP)], axis=0)
        m = jnp.maximum(jnp.maximum(jnp.max(sp, axis=-1, keepdims=True),
                                    jnp.max(sc, axis=-1, keepdims=True)), sink)
        pp = jnp.exp(sp - m)
        pc = jnp.exp(sc - m)
        den.append(jnp.sum(pp, axis=-1, keepdims=True) + jnp.sum(pc, axis=-1, keepdims=True)
                   + jnp.exp(sink - m))
        p_prev.append(pp.astype(BF16))
        p_cur.append(pc.astype(BF16))
    outs = []
    for kv in heads:
        v_prev = kvp_ref[:, SWA_KV_DIM + kv * SWA_HD:SWA_KV_DIM + (kv + 1) * SWA_HD]
        v_cur = kvc_ref[:, SWA_KV_DIM + kv * SWA_HD:SWA_KV_DIM + (kv + 1) * SWA_HD]
        o = (_bdot(p_prev[kv], v_prev) + _bdot(p_cur[kv], v_cur)) / den[kv]
        outs.extend(o[g * WINDOW:(g + 1) * WINDOW] for g in range(SWA_GROUP))
    o_ref[...] = jnp.concatenate(outs, axis=1)


def _swa_prompt_call(qkv, rel_bias, sinks):
    nb = SEQ // WINDOW
    qpos = WINDOW + np.arange(WINDOW)
    kpos = np.arange(2 * WINDOW)
    bkt = jnp.asarray(_bucket_table(qpos, kpos, np.ones(2 * WINDOW, bool)))
    kv_col = SWA_Q_DIM // (2 * SWA_KV_DIM)
    smem = pl.BlockSpec(memory_space=pltpu.SMEM)
    return pl.pallas_call(
        _swa_prompt_kernel,
        grid=(BATCH, nb),
        in_specs=[smem, smem,
                  pl.BlockSpec((WINDOW, 2 * WINDOW), lambda b, n: (0, 0)),
                  pl.BlockSpec((WINDOW, SWA_Q_DIM), lambda b, n: (b * nb + n, 0)),
                  pl.BlockSpec((WINDOW, 2 * SWA_KV_DIM),
                               lambda b, n: (jnp.maximum(b * nb + n - 1, 0), kv_col)),
                  pl.BlockSpec((WINDOW, 2 * SWA_KV_DIM), lambda b, n: (b * nb + n, kv_col))],
        out_specs=pl.BlockSpec((WINDOW, SWA_Q_DIM), lambda b, n: (b * nb + n, 0)),
        out_shape=jax.ShapeDtypeStruct((NP_TOK, SWA_Q_DIM), F32),
        scratch_shapes=[pltpu.VMEM((SWA_KV_HEADS, SWA_GROUP * WINDOW, 2 * WINDOW), F32)],
        compiler_params=_cparams(("arbitrary", "arbitrary")),
        name="swa_prompt",
    )(rel_bias, sinks, bkt, qkv, qkv, qkv)


def _swa_sample_kernel(rb_ref, sk_ref, bkt_ref, q_ref, k_ref, v_ref, o_ref, bias_scr, sink_scr):
    rows = SWA_GROUP * DEC_SEQ

    def group_of_row(shape):
        r = lax.broadcasted_iota(I32, shape, 0)
        return sum((r >= g * DEC_SEQ).astype(I32) for g in range(1, SWA_GROUP))

    @pl.when(pl.program_id(0) == 0)
    def _():
        bkt = bkt_ref[...]
        grp = group_of_row((rows, 1))
        grp_full = group_of_row(bkt.shape)
        for kv in range(SWA_KV_HEADS):
            acc = jnp.zeros(bkt.shape, F32)
            snk = jnp.zeros((rows, 1), F32)
            for g in range(SWA_GROUP):
                head = kv * SWA_GROUP + g
                acc = jnp.where(grp_full == g, _bias_from_buckets(bkt, rb_ref, head), acc)
                snk = jnp.where(grp == g, sk_ref[head], snk)
            bias_scr[kv] = acc
            sink_scr[kv] = snk

    scale = SWA_HD ** -0.5
    units = [(s, kv) for s in range(SWA_SB) for kv in range(SWA_KV_HEADS)]
    scores = {(s, kv): _bdot_nt(q_ref[s, kv], k_ref[s, :, kv * SWA_HD:(kv + 1) * SWA_HD])
              for (s, kv) in units}
    probs, den = {}, {}
    for (s, kv) in units:
        sc = scores[s, kv] * scale + bias_scr[kv]
        sink = sink_scr[kv]
        m = jnp.maximum(jnp.max(sc, axis=-1, keepdims=True), sink)
        p = jnp.exp(sc - m)
        den[s, kv] = jnp.sum(p, axis=-1, keepdims=True) + jnp.exp(sink - m)
        probs[s, kv] = p.astype(BF16)
    for (s, kv) in units:
        o_ref[s, kv] = _bdot(probs[s, kv], v_ref[s, :, kv * SWA_HD:(kv + 1) * SWA_HD]) / den[s, kv]


def _swa_sample_call(q_st, kc, vc, rel_bias, sinks):
    rows = SWA_GROUP * DEC_SEQ
    n_keys = kc.shape[1] - (SWA_KPAD - WINDOW) + DEC_SEQ
    qpos = PAST_LEN + np.arange(DEC_SEQ)
    kpos = PAST_LEN - (n_keys - DEC_SEQ) + np.arange(SWA_KPAD)
    bkt4 = _bucket_table(qpos, kpos, np.arange(SWA_KPAD) < n_keys)
    bkt = jnp.asarray(np.tile(bkt4, (SWA_GROUP, 1)))
    smem = pl.BlockSpec(memory_space=pltpu.SMEM)
    q_spec = pl.BlockSpec((SWA_SB, SWA_KV_HEADS, rows, SWA_HD), lambda i: (i, 0, 0, 0))
    kv_spec = pl.BlockSpec((SWA_SB, SWA_KPAD, SWA_KV_DIM), lambda i: (i, 0, 0))
    return pl.pallas_call(
        _swa_sample_kernel,
        grid=(DEC_BATCH // SWA_SB,),
        in_specs=[smem, smem, pl.BlockSpec((rows, SWA_KPAD), lambda i: (0, 0)),
                  q_spec, kv_spec, kv_spec],
        out_specs=q_spec,
        out_shape=jax.ShapeDtypeStruct((DEC_BATCH, SWA_KV_HEADS, rows, SWA_HD), F32),
        scratch_shapes=[pltpu.VMEM((SWA_KV_HEADS, rows, SWA_KPAD), F32),
                        pltpu.VMEM((SWA_KV_HEADS, rows, 1), F32)],
        compiler_params=_cparams(("arbitrary",)),
        name="swa_sample",
    )(rel_bias, sinks, bkt, q_st, kc, vc)


def _ffn_scratch():
    return [pltpu.VMEM((TM, D), F32),
            pltpu.VMEM((NF, D, TF), BF16),
            pltpu.VMEM((NF, D, TF), BF16),
            pltpu.VMEM((NF, TF, D), BF16),
            pltpu.VMEM((2, D, TF), F32),
            pltpu.VMEM((2, D, TF), F32),
            pltpu.VMEM((2, TF, D), F32),
            pltpu.SemaphoreType.DMA((3, 2))]


def _swiglu_into(acc_ref, x, expert, load, w_hbm, res, stg, sem):
    wg_hbm, wu_hbm, wd_hbm = w_hbm
    res_g, res_u, res_d = res
    stg_g, stg_u, stg_d = stg

    def copies(f, slot):
        col = pl.multiple_of(f * TF, TF)
        return (pltpu.make_async_copy(wg_hbm.at[expert, :, pl.ds(col, TF)], stg_g.at[slot], sem.at[0, slot]),
                pltpu.make_async_copy(wu_hbm.at[expert, :, pl.ds(col, TF)], stg_u.at[slot], sem.at[1, slot]),
                pltpu.make_async_copy(wd_hbm.at[expert, pl.ds(col, TF), :], stg_d.at[slot], sem.at[2, slot]))

    def block(f):
        gate = jnp.dot(x, res_g[f], preferred_element_type=F32)
        up = jnp.dot(x, res_u[f], preferred_element_type=F32)
        act = (_silu(gate) * up).astype(BF16)
        acc_ref[...] += jnp.dot(act, res_d[f], preferred_element_type=F32)

    acc_ref[...] = jnp.zeros_like(acc_ref)

    @pl.when(load)
    def _():
        for c in copies(0, 0):
            c.start()

        def body(f, carry):
            slot = lax.rem(f, 2)

            @pl.when(f + 1 < NF)
            def _():
                for c in copies(f + 1, 1 - slot):
                    c.start()

            for c in copies(f, slot):
                c.wait()
            res_g[f] = stg_g[slot].astype(BF16)
            res_u[f] = stg_u[slot].astype(BF16)
            res_d[f] = stg_d[slot].astype(BF16)
            block(f)
            return carry

        lax.fori_loop(0, NF, body, 0)

    @pl.when(jnp.logical_not(load))
    def _():
        def body(f, carry):
            block(f)
            return carry

        lax.fori_loop(0, NF, body, 0)


def _ffn_kernel(te_ref, ld_ref, nt_ref, x_ref, wg_hbm, wu_hbm, wd_hbm, o_ref, acc_ref, *scratch):
    t = pl.program_id(0)

    @pl.when(t < nt_ref[0])
    def _():
        _swiglu_into(acc_ref, x_ref[...].astype(BF16), te_ref[t], ld_ref[t] != 0,
                     (wg_hbm, wu_hbm, wd_hbm), scratch[0:3], scratch[3:6], scratch[6])
        o_ref[...] = acc_ref[...]

    @pl.when(t >= nt_ref[0])
    def _():
        o_ref[...] = jnp.zeros_like(o_ref)


def _ffn_call(x, tile_expert, n_tiles_used, w_gate, w_up, w_down):
    n_rows = x.shape[0]
    n_tiles = n_rows // TM
    changed = jnp.concatenate([jnp.ones((1,), I32),
                               (tile_expert[1:] != tile_expert[:-1]).astype(I32)])
    hbm = pl.BlockSpec(memory_space=pl.ANY)
    grid_spec = pltpu.PrefetchScalarGridSpec(
        num_scalar_prefetch=3,
        grid=(n_tiles,),
        in_specs=[pl.BlockSpec((TM, D), lambda t, *_: (t, 0)), hbm, hbm, hbm],
        out_specs=pl.BlockSpec((TM, D), lambda t, *_: (t, 0)),
        scratch_shapes=_ffn_scratch(),
    )
    return pl.pallas_call(
        _ffn_kernel,
        grid_spec=grid_spec,
        out_shape=jax.ShapeDtypeStruct((n_rows, D), F32),
        compiler_params=_cparams(("arbitrary",)),
        name="ffn",
    )(tile_expert, changed, n_tiles_used, x, w_gate, w_up, w_down)


def _dense_ffn_kernel(x_ref, gpre_ref, shp_ref, shs_ref, scp_ref, scs_ref, wg_hbm, wu_hbm, wd_hbm,
                      gpost_ref, gtp_ref, gts_ref, out_ref, acc_ref, *scratch, layer_pair):
    i = pl.program_id(0)
    x = x_ref[...]
    h = _prenorm(i, x, gpre_ref, shp_ref, shs_ref, scp_ref, scs_ref).astype(BF16)
    _swiglu_into(acc_ref, h, layer_pair, i == 0, (wg_hbm, wu_hbm, wd_hbm),
                 scratch[0:3], scratch[3:6], scratch[6])
    out_ref[...] = x + _pick(i, gtp_ref, gts_ref) * (_rms(acc_ref[...]) * gpost_ref[...])


def _dense_ffn_call(x, g_pre, g_post, mod_p, mod_s, w_gate, w_up, w_down, layer_pair):
    shp, shs = _mod_specs(3)
    scp, scs = _mod_specs(4)
    gtp, gts = _mod_specs(5)
    tile = pl.BlockSpec((TM, D), lambda i: (i, 0))
    row = pl.BlockSpec((1, D), lambda i: (0, 0))
    hbm = pl.BlockSpec(memory_space=pl.ANY)
    return pl.pallas_call(
        functools.partial(_dense_ffn_kernel, layer_pair=layer_pair),
        grid=(N_TILES,),
        in_specs=[tile, row, shp, shs, scp, scs, hbm, hbm, hbm, row, gtp, gts],
        out_specs=tile,
        out_shape=jax.ShapeDtypeStruct((TT, D), F32),
        scratch_shapes=_ffn_scratch(),
        compiler_params=_cparams(("arbitrary",)),
        name="dense_ffn",
    )(x, g_pre.reshape(1, D), mod_p, mod_s, mod_p, mod_s, w_gate, w_up, w_down,
      g_post.reshape(1, D), mod_p, mod_s)


DMA_UNROLL = 8


def _row_copy(src_ref, dst_ref, src_row, dst_row, sem):
    return pltpu.make_async_copy(src_ref.at[pl.ds(src_row, 1)], dst_ref.at[pl.ds(dst_row, 1)], sem)


def _dispatch_kernel(d0_ref, d1_ref, h_ref, init_ref, out_ref, sem):
    del init_ref

    def start(r, carry):
        _row_copy(h_ref, out_ref, r, d0_ref[0, 0, r], sem).start()
        _row_copy(h_ref, out_ref, r, d1_ref[0, 0, r], sem).start()
        return carry

    def wait(r, carry):
        _row_copy(h_ref, out_ref, 0, 0, sem).wait()
        _row_copy(h_ref, out_ref, 0, 0, sem).wait()
        return carry

    lax.fori_loop(0, TM, start, 0, unroll=DMA_UNROLL)
    lax.fori_loop(0, TM, wait, 0, unroll=DMA_UNROLL)


def _dispatch_call(h, dest):
    idx_spec = pl.BlockSpec((1, 1, TM), lambda i: (i, 0, 0), memory_space=pltpu.SMEM)
    return pl.pallas_call(
        _dispatch_kernel,
        grid=(N_TILES,),
        in_specs=[idx_spec, idx_spec, pl.BlockSpec((TM, D), lambda i: (i, 0)),
                  pl.BlockSpec(memory_space=pl.ANY)],
        out_specs=pl.BlockSpec(memory_space=pl.ANY),
        out_shape=jax.ShapeDtypeStruct((MOE_ROWS, D), h.dtype),
        scratch_shapes=[pltpu.SemaphoreType.DMA(())],
        input_output_aliases={3: 0},
        compiler_params=_cparams(("arbitrary",)),
        name="moe_dispatch",
    )(dest[0].reshape(N_TILES, 1, TM), dest[1].reshape(N_TILES, 1, TM), h,
      jnp.zeros((MOE_ROWS, D), h.dtype))


def _combine_kernel(d0_ref, d1_ref, x_ref, wt_ref, ys_ref, g_ref, gtp_ref, gts_ref, out_ref, buf, sem):
    i = pl.program_id(0)

    def start(r, carry):
        _row_copy(ys_ref, buf.at[0], d0_ref[0, 0, r], r, sem).start()
        _row_copy(ys_ref, buf.at[1], d1_ref[0, 0, r], r, sem).start()
        return carry

    def wait(r, carry):
        _row_copy(ys_ref, buf.at[0], 0, 0, sem).wait()
        _row_copy(ys_ref, buf.at[1], 0, 0, sem).wait()
        return carry

    lax.fori_loop(0, TM, start, 0, unroll=DMA_UNROLL)
    lax.fori_loop(0, TM, wait, 0, unroll=DMA_UNROLL)
    wt = wt_ref[...]
    y = wt[:, 0:1] * buf[0] + wt[:, 1:2] * buf[1]
    out_ref[...] = x_ref[...] + _pick(i, gtp_ref, gts_ref) * (_rms(y) * g_ref[...])


def _combine_call(x, ys, dest, wts, g, mod_p, mod_s, gate_chunk):
    gtp, gts = _mod_specs(gate_chunk)
    tile = pl.BlockSpec((TM, D), lambda i: (i, 0))
    idx_spec = pl.BlockSpec((1, 1, TM), lambda i: (i, 0, 0), memory_space=pltpu.SMEM)
    return pl.pallas_call(
        _combine_kernel,
        grid=(N_TILES,),
        in_specs=[idx_spec, idx_spec, tile, pl.BlockSpec((TM, LANES), lambda i: (i, 0)),
                  pl.BlockSpec(memory_space=pl.ANY),
                  pl.BlockSpec((1, D), lambda i: (0, 0)), gtp, gts],
        out_specs=tile,
        out_shape=jax.ShapeDtypeStruct((TT, D), F32),
        scratch_shapes=[pltpu.VMEM((2, TM, D), F32), pltpu.SemaphoreType.DMA(())],
        compiler_params=_cparams(("arbitrary",)),
        name="moe_combine",
    )(dest[0].reshape(N_TILES, 1, TM), dest[1].reshape(N_TILES, 1, TM), x, wts, ys,
      g.reshape(1, D), mod_p, mod_s)


def _route(top_idx):
    flat_e = top_idx.T.reshape(-1)
    onehot = (flat_e[:, None] == jnp.arange(N_EXPERTS, dtype=I32)[None, :]).astype(I32)
    csum = jnp.cumsum(onehot, axis=0)
    rank = jnp.sum((csum - onehot) * onehot, axis=1)
    counts = csum[-1]
    padded = ((counts + TM - 1) // TM) * TM
    ends = jnp.cumsum(padded)
    starts = ends - padded
    dest = (jnp.sum(onehot * starts[None, :], axis=1) + rank).astype(I32)
    tile_start = jnp.arange(MOE_TILES, dtype=I32) * TM
    tile_expert = jnp.minimum(jnp.sum(tile_start[:, None] >= ends[None, :], axis=1),
                              N_EXPERTS - 1).astype(I32)
    n_used = (ends[-1] // TM).astype(I32).reshape(1)
    return dest.reshape(2, TT), tile_expert, n_used


def _gdn_layer(x, g_pre, w_in, conv_w, a_log, dt_bias, norm_w, w_out, state_conv, state_rec_all,
               layer_pair, g_post, mod_p, mod_s):
    w_main = w_in[:, :GDN_MAIN_DIM]
    w_ba = jnp.pad(w_in[:, GDN_MAIN_DIM:], ((0, 0), (0, LANES - 2 * GDN_V_HEADS)))
    qkvz = _proj_call(x, g_pre, mod_p, mod_s, w_main, jnp.zeros((GDN_MAIN_DIM,), F32), 1024)
    ba = _proj_call(x, g_pre, mod_p, mod_s, w_ba, jnp.zeros((LANES,), F32), LANES)

    tb = GDN_CHUNK * GDN_NB
    n_steps = SEQ // tb
    o_p, rec_p = _gdn_call(
        qkvz.reshape(TT // tb, tb, GDN_MAIN_DIM), ba.reshape(TT // tb, tb, LANES),
        conv_w, a_log, dt_bias, norm_w,
        jnp.zeros((BATCH, SUBLANES, GDN_CONV_DIM), F32),
        jnp.zeros((BATCH, GDN_V_HEADS, GDN_HD, GDN_HD), F32),
        n_seq=BATCH, n_steps=n_steps, chunk=GDN_CHUNK, n_chunks=GDN_NB, t_real=tb,
        n_par=GDN_PROMPT_PAR)
    o_p = o_p.reshape(NP_TOK, GDN_VAL_DIM)

    row_pad = ((0, 0), (0, GDN_SAMPLE_ROWS - DEC_SEQ), (0, 0))
    qkvz_s = qkvz[NP_TOK:].reshape(DEC_BATCH, DEC_SEQ, GDN_MAIN_DIM)
    ba_s = ba[NP_TOK:].reshape(DEC_BATCH, DEC_SEQ, LANES)
    cbuf_s = jnp.pad(state_conv, ((0, 0), (SUBLANES - (CONV_WIDTH - 1), 0), (0, 0)))
    o_s, rec_s = _gdn_call(
        jnp.pad(qkvz_s, row_pad), jnp.pad(ba_s, row_pad), conv_w, a_log, dt_bias, norm_w,
        cbuf_s, state_rec_all,
        n_seq=DEC_BATCH, n_steps=1, chunk=GDN_SAMPLE_ROWS, n_chunks=1, t_real=DEC_SEQ,
        n_par=GDN_SAMPLE_PAR, s0_first_seq=layer_pair * DEC_BATCH)
    o_s = o_s[:, 0, :DEC_SEQ].reshape(NS_TOK, GDN_VAL_DIM)

    keep = CONV_WIDTH - 1
    pre_p = jnp.stack([qkvz[(b + 1) * SEQ - keep:(b + 1) * SEQ, :GDN_CONV_DIM] for b in range(BATCH)])
    pre_s = jnp.concatenate([state_conv, qkvz_s[:, :, :GDN_CONV_DIM]], axis=1)[:, -keep:]
    x = _post_mm_call(x, o_p, o_s, w_out, jnp.zeros((D,), F32), g_post, mod_p, mod_s, 2)
    return x, pre_p, rec_p, pre_s, rec_s


def _swa_layer(x, g_pre, w_in, b_in, sinks, w_out, b_out, rel_bias, cache_k, cache_v,
               g_post, mod_p, mod_s):
    qkv = _proj_call(x, g_pre, mod_p, mod_s, w_in, b_in, SWA_PROJ_DIM)
    o_p = _swa_prompt_call(qkv, rel_bias, sinks)
    last = jnp.stack([qkv[(b + 1) * SEQ - WINDOW:(b + 1) * SEQ, SWA_Q_DIM:] for b in range(BATCH)])
    new_k_p = last[:, :, :SWA_KV_DIM]
    new_v_p = last[:, :, SWA_KV_DIM:]

    qkv_s = qkv[NP_TOK:].reshape(DEC_BATCH, DEC_SEQ, SWA_PROJ_DIM)
    q_st = qkv_s[:, :, :SWA_Q_DIM].reshape(DEC_BATCH, DEC_SEQ, SWA_KV_HEADS, SWA_GROUP, SWA_HD)
    q_st = q_st.transpose(0, 2, 3, 1, 4).reshape(DEC_BATCH, SWA_KV_HEADS, SWA_GROUP * DEC_SEQ, SWA_HD)
    wb = cache_k.shape[1]
    zpad = jnp.zeros((DEC_BATCH, SWA_KPAD - WINDOW - DEC_SEQ + (WINDOW - wb), SWA_KV_DIM), F32)
    kc = jnp.concatenate([cache_k.reshape(DEC_BATCH, wb, SWA_KV_DIM),
                          qkv_s[:, :, SWA_Q_DIM:SWA_Q_DIM + SWA_KV_DIM], zpad], axis=1)
    vc = jnp.concatenate([cache_v.reshape(DEC_BATCH, wb, SWA_KV_DIM),
                          qkv_s[:, :, SWA_Q_DIM + SWA_KV_DIM:], zpad], axis=1)
    o_st = _swa_sample_call(q_st, kc, vc, rel_bias, sinks)
    o_s = o_st.reshape(DEC_BATCH, SWA_KV_HEADS, SWA_GROUP, DEC_SEQ, SWA_HD)
    o_s = o_s.transpose(0, 3, 1, 2, 4).reshape(NS_TOK, SWA_Q_DIM)
    new_k_s = kc[:, DEC_SEQ:DEC_SEQ + wb]
    new_v_s = vc[:, DEC_SEQ:DEC_SEQ + wb]

    x = _post_mm_call(x, o_p, o_s, w_out, b_out, g_post, mod_p, mod_s, 2)
    shape_p = (BATCH, WINDOW, SWA_KV_HEADS, SWA_HD)
    shape_s = (DEC_BATCH, wb, SWA_KV_HEADS, SWA_HD)
    return (x, new_k_p.reshape(shape_p), new_v_p.reshape(shape_p),
            new_k_s.reshape(shape_s), new_v_s.reshape(shape_s))


def kernel(x_prompt, x_sample, c_prompt, c_sample, state_conv, state_rec, cache_win_k, cache_win_v, w_mod, b_mod, g_pre_mix, g_post_mix, g_pre_ffn, g_post_ffn, gdn_w_in, gdn_conv_w, gdn_a_log, gdn_dt_bias, gdn_norm_w, gdn_w_out, swa_w_in, swa_b_in, swa_sinks, swa_w_out, swa_b_out, rel_bias, ffn_w_gate, ffn_w_up, ffn_w_down, moe_w_router, moe_b_router, moe_w_gate, moe_w_up, moe_w_down):
    x = jnp.concatenate([x_prompt.reshape(NP_TOK, D), x_sample.reshape(NS_TOK, D)], axis=0)
    n_c = BATCH + DEC_BATCH
    c_all = jnp.concatenate([c_prompt, c_sample, jnp.zeros((-n_c % SUBLANES, D), F32)], axis=0)
    m_all = _mod_call(c_all, w_mod, b_mod)
    ffn_w = (ffn_w_gate, ffn_w_up, ffn_w_down)
    moe_w = (moe_w_gate.reshape(-1, D, D_FF), moe_w_up.reshape(-1, D, D_FF),
             moe_w_down.reshape(-1, D_FF, D))
    state_rec_all = state_rec.reshape((-1,) + state_rec.shape[2:])

    conv_p, rec_p, conv_s, rec_s = [], [], [], []
    k_p, v_p, k_s, v_s = [], [], [], []
    for layer in range(DEPTH):
        j = layer // 2
        mod_p = m_all[layer, :BATCH].reshape(BATCH, 1, 6 * D)
        mod_s = jnp.repeat(m_all[layer, BATCH:n_c], DEC_SEQ, axis=0)
        if layer % 2 == 0:
            x, cp, rp, cs, rs = _gdn_layer(
                x, g_pre_mix[layer], gdn_w_in[j], gdn_conv_w[j], gdn_a_log[j], gdn_dt_bias[j],
                gdn_norm_w[j], gdn_w_out[j], state_conv[j], state_rec_all, j, g_post_mix[layer],
                mod_p, mod_s)
            conv_p.append(cp); rec_p.append(rp); conv_s.append(cs); rec_s.append(rs)
        else:
            x, kp, vp, ks, vs = _swa_layer(
                x, g_pre_mix[layer], swa_w_in[j], swa_b_in[j], swa_sinks[j], swa_w_out[j],
                swa_b_out[j], rel_bias, cache_win_k[j], cache_win_v[j], g_post_mix[layer],
                mod_p, mod_s)
            k_p.append(kp); v_p.append(vp); k_s.append(ks); v_s.append(vs)

        if layer % 2 == 0:
            x = _dense_ffn_call(x, g_pre_ffn[layer], g_post_ffn[layer], mod_p, mod_s, *ffn_w, j)
        else:
            h, idx, wts = _prenorm_router_call(x, g_pre_ffn[layer], mod_p, mod_s, 3, 4,
                                               moe_w_router[j], moe_b_router[j])
            dest, tile_expert, n_used = _route(idx[:, :2])
            xs = _dispatch_call(h, dest)
            ys = _ffn_call(xs, tile_expert + j * N_EXPERTS, n_used, *moe_w)
            x = _combine_call(x, ys, dest, wts, g_post_ffn[layer], mod_p, mod_s, 5)

    y_prompt = x[:NP_TOK].reshape(BATCH, SEQ, D)
    y_sample = x[NP_TOK:].reshape(DEC_BATCH, DEC_SEQ, D)
    return (y_prompt, y_sample, jnp.stack(conv_p), jnp.stack(rec_p), jnp.stack(k_p), jnp.stack(v_p),
            jnp.stack(conv_s), jnp.stack(rec_s), jnp.stack(k_s), jnp.stack(v_s))
```

```python
import functools
import math

import numpy as np
import jax
import jax.numpy as jnp
from jax import lax
from jax.experimental import pallas as pl
from jax.experimental.pallas import tpu as pltpu

F32 = jnp.float32
BF16 = jnp.bfloat16
I32 = jnp.int32
HIGHEST = lax.Precision.HIGHEST

D = 1024
BATCH = 4
SEQ = 4096
DEPTH = 4
DEC_BATCH = 128
DEC_SEQ = 4
PAST_LEN = 8192
GDN_QK_HEADS = 4
GDN_V_HEADS = 8
GDN_HD = 128
GDN_KEY_DIM = GDN_QK_HEADS * GDN_HD
GDN_VAL_DIM = GDN_V_HEADS * GDN_HD
GDN_CONV_DIM = 2 * GDN_KEY_DIM + GDN_VAL_DIM
GDN_MAIN_DIM = GDN_CONV_DIM + GDN_VAL_DIM
CONV_WIDTH = 4
GDN_CHUNK = 64
SWA_Q_HEADS = 16
SWA_KV_HEADS = 4
SWA_GROUP = SWA_Q_HEADS // SWA_KV_HEADS
SWA_HD = 64
SWA_Q_DIM = SWA_Q_HEADS * SWA_HD
SWA_KV_DIM = SWA_KV_HEADS * SWA_HD
SWA_PROJ_DIM = SWA_Q_DIM + 2 * SWA_KV_DIM
WINDOW = 128
REL_BUCKETS = 32
REL_MAX_DIST = 128
D_FF = 2816
N_EXPERTS = 8
RMS_EPS = 1e-6

LANES = 128
SUBLANES = 8
VMEM_LIMIT = 56 * 1024 * 1024

TM = 512
NP_TOK = BATCH * SEQ
NS_TOK = DEC_BATCH * DEC_SEQ
TT = NP_TOK + NS_TOK
NP_TILES = NP_TOK // TM
NS_TILES = NS_TOK // TM
N_TILES = NP_TILES + NS_TILES
TILES_PER_SEQ = SEQ // TM
TF = 256
NF = D_FF // TF
MOE_ROWS = 2 * TT + N_EXPERTS * TM
MOE_TILES = MOE_ROWS // TM
GDN_NB = 2
GDN_PROMPT_PAR = 4
GDN_SAMPLE_PAR = 4
GDN_SAMPLE_ROWS = 8
SWA_SB = 8
SWA_KPAD = WINDOW + 8

assert NP_TOK % TM == 0 and NS_TOK % TM == 0 and SEQ % TM == 0
assert D_FF % TF == 0 and DEC_SEQ <= GDN_SAMPLE_ROWS and DEC_SEQ <= 8


def _cparams(sem):
    return pltpu.CompilerParams(dimension_semantics=sem, vmem_limit_bytes=VMEM_LIMIT)


def _bdot(a, b):
    return jnp.dot(a.astype(BF16), b.astype(BF16), preferred_element_type=F32)


def _bdot_nt(a, b):
    return lax.dot_general(a.astype(BF16), b.astype(BF16), (((1,), (1,)), ((), ())),
                           preferred_element_type=F32)


def _silu(x):
    return x * jax.nn.sigmoid(x)


def _rms(x):
    return x * lax.rsqrt(jnp.mean(x * x, axis=-1, keepdims=True) + RMS_EPS)


def _mod_kernel(c_ref, w_ref, b_ref, o_ref):
    c = c_ref[...]
    o_ref[0] = _bdot(_silu(c), w_ref[0]) + b_ref[0]


def _mod_call(c_all, w_mod, b_mod):
    n = c_all.shape[0]
    tn = D
    return pl.pallas_call(
        _mod_kernel,
        grid=(DEPTH, 6 * D // tn),
        in_specs=[
            pl.BlockSpec((n, D), lambda l, j: (0, 0)),
            pl.BlockSpec((1, D, tn), lambda l, j: (l, 0, j)),
            pl.BlockSpec((1, 1, tn), lambda l, j: (l, 0, j)),
        ],
        out_specs=pl.BlockSpec((1, n, tn), lambda l, j: (l, 0, j)),
        out_shape=jax.ShapeDtypeStruct((DEPTH, n, 6 * D), F32),
        compiler_params=_cparams(("arbitrary", "arbitrary")),
        name="modulation",
    )(c_all, w_mod, b_mod.reshape(DEPTH, 1, 6 * D))


def _mod_specs(chunk, token_axis=0):
    p = pl.BlockSpec((1, 1, D), lambda *ids: (jnp.minimum(ids[token_axis] // TILES_PER_SEQ, BATCH - 1),
                                              0, chunk))
    s = pl.BlockSpec((TM, D), lambda *ids: (jnp.maximum(ids[token_axis] - NP_TILES, 0), chunk))
    return p, s


def _pick(i, p_ref, s_ref):
    return jnp.where(i < NP_TILES, p_ref[0], s_ref[...])


def _prenorm(i, x, g_ref, shp_ref, shs_ref, scp_ref, scs_ref):
    h = _rms(x) * g_ref[...]
    return h * (1.0 + _pick(i, scp_ref, scs_ref)) + _pick(i, shp_ref, shs_ref)


def _prenorm_router_kernel(x_ref, g_ref, shp_ref, shs_ref, scp_ref, scs_ref, wr_ref, br_ref,
                           h_ref, idx_ref, wt_ref):
    h = _prenorm(pl.program_id(0), x_ref[...], g_ref, shp_ref, shs_ref, scp_ref, scs_ref)
    h_ref[...] = h
    lane = lax.broadcasted_iota(I32, (TM, LANES), 1).astype(F32)
    logits = jnp.dot(h, wr_ref[...], precision=HIGHEST, preferred_element_type=F32) + br_ref[...]
    logits = jnp.where(lane < N_EXPERTS, logits, -jnp.inf)
    m1 = jnp.max(logits, axis=-1, keepdims=True)
    i1 = jnp.min(jnp.where(logits == m1, lane, float(LANES)), axis=-1, keepdims=True)
    rest = jnp.where(lane == i1, -jnp.inf, logits)
    m2 = jnp.max(rest, axis=-1, keepdims=True)
    i2 = jnp.min(jnp.where(rest == m2, lane, float(LANES)), axis=-1, keepdims=True)
    e2 = jnp.exp(m2 - m1)
    w1 = 1.0 / (1.0 + e2)
    w2 = e2 / (1.0 + e2)
    idx_ref[...] = jnp.where(lane == 0, i1, jnp.where(lane == 1, i2, 0.0)).astype(I32)
    wt_ref[...] = jnp.where(lane == 0, w1, jnp.where(lane == 1, w2, 0.0))


def _prenorm_router_call(x, g, mod_p, mod_s, sh_chunk, sc_chunk, w_router, b_router):
    shp, shs = _mod_specs(sh_chunk)
    scp, scs = _mod_specs(sc_chunk)
    tile = pl.BlockSpec((TM, D), lambda i: (i, 0))
    small = pl.BlockSpec((TM, LANES), lambda i: (i, 0))
    wr = jnp.pad(w_router, ((0, 0), (0, LANES - N_EXPERTS)))
    br = jnp.pad(b_router, (0, LANES - N_EXPERTS)).reshape(1, LANES)
    return pl.pallas_call(
        _prenorm_router_kernel,
        grid=(N_TILES,),
        in_specs=[tile, pl.BlockSpec((1, D), lambda i: (0, 0)), shp, shs, scp, scs,
                  pl.BlockSpec((D, LANES), lambda i: (0, 0)),
                  pl.BlockSpec((1, LANES), lambda i: (0, 0))],
        out_specs=[tile, small, small],
        out_shape=[jax.ShapeDtypeStruct((TT, D), F32),
                   jax.ShapeDtypeStruct((TT, LANES), I32),
                   jax.ShapeDtypeStruct((TT, LANES), F32)],
        compiler_params=_cparams(("arbitrary",)),
        name="prenorm_router",
    )(x, g.reshape(1, D), mod_p, mod_s, mod_p, mod_s, wr, br)


def _proj_kernel(x_ref, g_ref, shp_ref, shs_ref, scp_ref, scs_ref, w_ref, b_ref, o_ref, wbf_ref):
    i = pl.program_id(1)

    @pl.when(i == 0)
    def _():
        wbf_ref[...] = w_ref[...].astype(BF16)

    h = _prenorm(i, x_ref[...], g_ref, shp_ref, shs_ref, scp_ref, scs_ref).astype(BF16)
    o_ref[...] = jnp.dot(h, wbf_ref[...], preferred_element_type=F32) + b_ref[...]


def _proj_call(x, g, mod_p, mod_s, w, b, tn):
    k, n = w.shape
    shp, shs = _mod_specs(0, token_axis=1)
    scp, scs = _mod_specs(1, token_axis=1)
    return pl.pallas_call(
        _proj_kernel,
        grid=(n // tn, N_TILES),
        in_specs=[
            pl.BlockSpec((TM, k), lambda j, i: (i, 0)),
            pl.BlockSpec((1, k), lambda j, i: (0, 0)), shp, shs, scp, scs,
            pl.BlockSpec((k, tn), lambda j, i: (0, j)),
            pl.BlockSpec((1, tn), lambda j, i: (0, j)),
        ],
        out_specs=pl.BlockSpec((TM, tn), lambda j, i: (i, j)),
        out_shape=jax.ShapeDtypeStruct((TT, n), F32),
        scratch_shapes=[pltpu.VMEM((k, tn), BF16)],
        compiler_params=_cparams(("arbitrary", "arbitrary")),
        name="proj",
    )(x, g.reshape(1, k), mod_p, mod_s, mod_p, mod_s, w, b.reshape(1, n))


def _post_mm_kernel(x_ref, op_ref, os_ref, w_ref, b_ref, g_ref, gtp_ref, gts_ref, out_ref, wbf_ref):
    i = pl.program_id(0)

    @pl.when(i == 0)
    def _():
        wbf_ref[...] = w_ref[...].astype(BF16)

    o = jnp.where(i < NP_TILES, op_ref[...], os_ref[...]).astype(BF16)
    y = jnp.dot(o, wbf_ref[...], preferred_element_type=F32) + b_ref[...]
    out_ref[...] = x_ref[...] + _pick(i, gtp_ref, gts_ref) * (_rms(y) * g_ref[...])


def _post_mm_call(x, o_p, o_s, w, b, g, mod_p, mod_s, gate_chunk):
    gtp, gts = _mod_specs(gate_chunk)
    tile = pl.BlockSpec((TM, D), lambda i: (i, 0))
    row = pl.BlockSpec((1, D), lambda i: (0, 0))
    return pl.pallas_call(
        _post_mm_kernel,
        grid=(N_TILES,),
        in_specs=[tile,
                  pl.BlockSpec((TM, D), lambda i: (jnp.minimum(i, NP_TILES - 1), 0)),
                  pl.BlockSpec((TM, D), lambda i: (jnp.maximum(i - NP_TILES, 0), 0)),
                  pl.BlockSpec((D, D), lambda i: (0, 0)), row, row, gtp, gts],
        out_specs=tile,
        out_shape=jax.ShapeDtypeStruct((TT, D), F32),
        scratch_shapes=[pltpu.VMEM((D, D), BF16)],
        compiler_params=_cparams(("arbitrary",)),
        name="post_mm",
    )(x, o_p, o_s, w, b.reshape(1, D), g.reshape(1, D), mod_p, mod_s)


def _gdn_kernel(*refs, chunk, n_chunks, t_real, n_par):
    x_refs = refs[:n_par]
    ba_refs = refs[n_par:2 * n_par]
    (cw_ref, alog_ref, dt_ref, nw_ref, cb_ref, s0_ref, o_ref, sfin_ref, xbuf, s_scr) = refs[2 * n_par:]
    c_sz = chunk
    tb = chunk * n_chunks
    j = pl.program_id(1)
    rep = GDN_V_HEADS // GDN_QK_HEADS
    units = [(p, h) for p in range(n_par) for h in range(GDN_V_HEADS)]

    @pl.when(j == 0)
    def _():
        s_scr[...] = s0_ref[...]
        xbuf[:, 0:SUBLANES, :] = cb_ref[...]

    for p in range(n_par):
        xbuf[p, SUBLANES:SUBLANES + tb, :] = x_refs[p][:, 0:GDN_CONV_DIM]

    ii = lax.broadcasted_iota(I32, (c_sz, c_sz), 0)
    jj = lax.broadcasted_iota(I32, (c_sz, c_sz), 1)
    causal = ii >= jj
    strict = ii > jj
    tri = causal.astype(F32)
    eye_c = (ii == jj).astype(F32)
    pair_mask = [jnp.logical_and((ii >> (l + 1)) == (jj >> (l + 1)), (ii >> l) != (jj >> l))
                 for l in range(int(math.log2(c_sz)))]
    eye_t = (lax.broadcasted_iota(I32, (2 * SUBLANES, LANES), 0)
             == lax.broadcasted_iota(I32, (2 * SUBLANES, LANES), 1)).astype(F32)
    lane = lax.broadcasted_iota(I32, (c_sz, LANES), 1)
    row = lax.broadcasted_iota(I32, (c_sz, LANES), 0)
    cw = cw_ref[...]
    neg_a = -jnp.exp(alog_ref[...])
    dt = dt_ref[...]
    nw = nw_ref[...]

    def chunk_body(c, carry):
        r0 = pl.multiple_of(c * c_sz, c_sz)
        xc, beta, gc, rows_t = [], [], [], []
        for p in range(n_par):
            win = xbuf[p, pl.ds(r0, c_sz + SUBLANES), :]
            y = win[SUBLANES:SUBLANES + c_sz] * cw[3:4]
            for tap in range(1, CONV_WIDTH):
                y = y + win[SUBLANES - tap:SUBLANES - tap + c_sz] * cw[3 - tap:4 - tap]
            xc.append(_silu(y))
            ba = ba_refs[p][pl.ds(r0, c_sz), :]
            b_p = jax.nn.sigmoid(ba)
            g_p = neg_a * jax.nn.softplus(ba + dt)
            if t_real < tb:
                live = (row + r0) < t_real
                b_p = jnp.where(live, b_p, 0.0)
                g_p = jnp.where(live, g_p, 0.0)
            gc_p = jnp.dot(tri, g_p, precision=HIGHEST, preferred_element_type=F32)
            cols = jnp.where(lane < GDN_V_HEADS, b_p, gc_p)
            rows_t.append(lax.dot_general(eye_t, cols, (((1,), (1,)), ((), ())), precision=HIGHEST,
                                          preferred_element_type=F32))
            beta.append(b_p)
            gc.append(gc_p)

        qn, kn, kk, qk = {}, {}, {}, {}
        for p in range(n_par):
            for hq in range(GDN_QK_HEADS):
                q = xc[p][:, hq * GDN_HD:(hq + 1) * GDN_HD]
                k = xc[p][:, GDN_KEY_DIM + hq * GDN_HD:GDN_KEY_DIM + (hq + 1) * GDN_HD]
                q = q * lax.rsqrt(jnp.sum(q * q, axis=-1, keepdims=True) + RMS_EPS) * (GDN_HD ** -0.5)
                k = k * lax.rsqrt(jnp.sum(k * k, axis=-1, keepdims=True) + RMS_EPS)
                qn[p, hq], kn[p, hq] = q, k
        for key in qn:
            kk[key] = _bdot_nt(kn[key], kn[key])
            qk[key] = _bdot_nt(qn[key], kn[key])

        gcc, gcl, bcol, decay, a_mat, inv = {}, {}, {}, {}, {}, {}
        for (p, h) in units:
            gcc[p, h] = gc[p][:, GDN_V_HEADS + h:GDN_V_HEADS + h + 1]
            gcr = rows_t[p][GDN_V_HEADS + h:GDN_V_HEADS + h + 1, :]
            bcol[p, h] = beta[p][:, h:h + 1]
            gcl[p, h] = gcc[p, h][c_sz - 1:c_sz, :]
            decay[p, h] = jnp.exp(jnp.where(causal, gcc[p, h] - gcr, -jnp.inf))
            a_mat[p, h] = jnp.where(strict, bcol[p, h] * kk[p, h // rep] * decay[p, h], 0.0)
            inv[p, h] = eye_c - jnp.where(pair_mask[0], a_mat[p, h], 0.0)
        for lvl in range(1, len(pair_mask)):
            t1 = {u: _bdot(jnp.where(pair_mask[lvl], a_mat[u], 0.0), inv[u]) for u in units}
            t2 = {u: _bdot(inv[u], t1[u]) for u in units}
            inv = {u: inv[u] - t2[u] for u in units}

        egc = {u: jnp.exp(gcc[u]) for u in units}
        sol = {}
        for (p, h) in units:
            v = xc[p][:, 2 * GDN_KEY_DIM + h * GDN_HD:2 * GDN_KEY_DIM + (h + 1) * GDN_HD]
            rhs = jnp.concatenate([v * bcol[p, h], kn[p, h // rep] * (bcol[p, h] * egc[p, h])], axis=1)
            sol[p, h] = _bdot(inv[p, h], rhs)
        ws = {}
        for (p, h) in units:
            q_dec = qn[p, h // rep] * egc[p, h]
            ws[p, h] = _bdot(jnp.concatenate([sol[p, h][:, GDN_HD:], q_dec], axis=0), s_scr[p, h])
        u_new = {u: sol[u][:, :GDN_HD] - ws[u][:c_sz] for u in units}
        o_part, s_part = {}, {}
        for (p, h) in units:
            qkm = jnp.where(causal, qk[p, h // rep] * decay[p, h], 0.0)
            k_dec = kn[p, h // rep] * jnp.exp(gcl[p, h] - gcc[p, h])
            o_part[p, h] = _bdot(qkm, u_new[p, h])
            s_part[p, h] = lax.dot_general(k_dec.astype(BF16), u_new[p, h].astype(BF16),
                                           (((0,), (0,)), ((), ())), preferred_element_type=F32)
        for (p, h) in units:
            s_scr[p, h] = s_scr[p, h] * jnp.exp(gcl[p, h]) + s_part[p, h]
            o = ws[p, h][c_sz:] + o_part[p, h]
            z = x_refs[p][pl.ds(r0, c_sz), GDN_CONV_DIM + h * GDN_HD:GDN_CONV_DIM + (h + 1) * GDN_HD]
            o_ref[p, pl.ds(r0, c_sz), h * GDN_HD:(h + 1) * GDN_HD] = _rms(o) * nw * _silu(z)
        return carry

    lax.fori_loop(0, n_chunks, chunk_body, 0)
    xbuf[:, 0:SUBLANES, :] = xbuf[:, tb:tb + SUBLANES, :]

    @pl.when(j == pl.num_programs(1) - 1)
    def _():
        sfin_ref[...] = s_scr[...]


def _gdn_call(x, ba, conv_w, a_log, dt_bias, norm_w, conv_buf, s0, *, n_seq, n_steps, chunk,
              n_chunks, t_real, n_par, s0_first_seq=0):
    s0_block = s0_first_seq // n_par
    tb = chunk * n_chunks
    pad = LANES - 2 * GDN_V_HEADS
    alog_row = jnp.pad(a_log, (GDN_V_HEADS, pad)).reshape(1, LANES)
    dt_row = jnp.pad(dt_bias, (GDN_V_HEADS, pad)).reshape(1, LANES)

    def seq_block(width, p):
        return pl.BlockSpec((None, tb, width), lambda s, j: ((s * n_par + p) * n_steps + j, 0, 0))

    const = lambda s, j: (0, 0)
    state_spec = pl.BlockSpec((n_par, GDN_V_HEADS, GDN_HD, GDN_HD), lambda s, j: (s, 0, 0, 0))
    return pl.pallas_call(
        functools.partial(_gdn_kernel, chunk=chunk, n_chunks=n_chunks, t_real=t_real, n_par=n_par),
        grid=(n_seq // n_par, n_steps),
        in_specs=[seq_block(GDN_MAIN_DIM, p) for p in range(n_par)]
                 + [seq_block(LANES, p) for p in range(n_par)]
                 + [pl.BlockSpec((CONV_WIDTH, GDN_CONV_DIM), const),
                    pl.BlockSpec((1, LANES), const), pl.BlockSpec((1, LANES), const),
                    pl.BlockSpec((1, GDN_HD), const),
                    pl.BlockSpec((n_par, SUBLANES, GDN_CONV_DIM), lambda s, j: (s, 0, 0)),
                    pl.BlockSpec((n_par, GDN_V_HEADS, GDN_HD, GDN_HD),
                                 lambda s, j: (s + s0_block, 0, 0, 0))],
        out_specs=[pl.BlockSpec((n_par, None, tb, GDN_VAL_DIM), lambda s, j: (s, j, 0, 0)),
                   state_spec],
        out_shape=[jax.ShapeDtypeStruct((n_seq, n_steps, tb, GDN_VAL_DIM), F32),
                   jax.ShapeDtypeStruct((n_seq, GDN_V_HEADS, GDN_HD, GDN_HD), F32)],
        scratch_shapes=[pltpu.VMEM((n_par, tb + SUBLANES, GDN_CONV_DIM), F32),
                        pltpu.VMEM((n_par, GDN_V_HEADS, GDN_HD, GDN_HD), F32)],
        compiler_params=_cparams(("arbitrary", "arbitrary")),
        name="gdn",
    )(*([x] * n_par), *([ba] * n_par), conv_w, alog_row, dt_row, norm_w.reshape(1, GDN_HD),
      conv_buf, s0)


def _t5_bucket_np(dist):
    dist = np.maximum(dist, 0)
    max_exact = REL_BUCKETS // 2
    ratio = np.log(np.maximum(dist, max_exact).astype(np.float32) / np.float32(max_exact)) \
        / np.float32(math.log(REL_MAX_DIST / max_exact))
    large = max_exact + (ratio.astype(np.float32) * np.float32(REL_BUCKETS - max_exact)).astype(np.int32)
    return np.where(dist < max_exact, dist, np.minimum(large, REL_BUCKETS - 1)).astype(np.int32)


def _bucket_table(qpos, kpos, k_valid):
    dist = qpos[:, None] - kpos[None, :]
    ok = (dist >= 0) & (dist < WINDOW) & (kpos[None, :] >= 0) & k_valid[None, :]
    return np.where(ok, _t5_bucket_np(dist), -1).astype(np.int32)


def _bias_from_buckets(bkt, rb_ref, head):
    def body(b, acc):
        return jnp.where(bkt == b, rb_ref[b, head], acc)
    acc = lax.fori_loop(0, REL_BUCKETS, body, jnp.zeros(bkt.shape, F32))
    return jnp.where(bkt < 0, -jnp.inf, acc)


def _swa_prompt_kernel(rb_ref, sk_ref, bkt_ref, q_ref, kvp_ref, kvc_ref, o_ref, bias_scr):
    first = jnp.logical_and(pl.program_id(0) == 0, pl.program_id(1) == 0)
    n = pl.program_id(1)

    @pl.when(first)
    def _():
        bkt = bkt_ref[...]
        for h in range(SWA_Q_HEADS):
            kv, g = divmod(h, SWA_GROUP)
            bias_scr[kv, g * WINDOW:(g + 1) * WINDOW, :] = _bias_from_buckets(bkt, rb_ref, h)

    q = q_ref[...]
    scale = SWA_HD ** -0.5
    heads = range(SWA_KV_HEADS)
    s_prev, s_cur = [], []
    for kv in heads:
        qs = jnp.concatenate(
            [q[:, (kv * SWA_GROUP + g) * SWA_HD:(kv * SWA_GROUP + g + 1) * SWA_HD]
             for g in range(SWA_GROUP)], axis=0).astype(BF16)
        s_prev.append(_bdot_nt(qs, kvp_ref[:, kv * SWA_HD:(kv + 1) * SWA_HD]))
        s_cur.append(_bdot_nt(qs, kvc_ref[:, kv * SWA_HD:(kv + 1) * SWA_HD]))
    p_prev, p_cur, den = [], [], []
    for kv in heads:
        sp = s_prev[kv] * scale + bias_scr[kv, :, 0:WINDOW]
        sp = jnp.where(n > 0, sp, -jnp.inf)
        sc = s_cur[kv] * scale + bias_scr[kv, :, WINDOW:2 * WINDOW]
        sink = jnp.concatenate(
            [jnp.full((WINDOW, 1), sk_ref[kv * SWA_GROUP + g], F32) for g in range(SWA_GROUP)], axis=0)
        m = jnp.maximum(jnp.maximum(jnp.max(sp, axis=-1, keepdims=True),
                                    jnp.max(sc, axis=-1, keepdims=True)), sink)
        pp = jnp.exp(sp - m)
        pc = jnp.exp(sc - m)
        den.append(jnp.sum(pp, axis=-1, keepdims=True) + jnp.sum(pc, axis=-1, keepdims=True)
                   + jnp.exp(sink - m))
        p_prev.append(pp.astype(BF16))
        p_cur.append(pc.astype(BF16))
    outs = []
    for kv in heads:
        v_prev = kvp_ref[:, SWA_KV_DIM + kv * SWA_HD:SWA_KV_DIM + (kv + 1) * SWA_HD]
        v_cur = kvc_ref[:, SWA_KV_DIM + kv * SWA_HD:SWA_KV_DIM + (kv + 1) * SWA_HD]
        o = (_bdot(p_prev[kv], v_prev) + _bdot(p_cur[kv], v_cur)) / den[kv]
        outs.extend(o[g * WINDOW:(g + 1) * WINDOW] for g in range(SWA_GROUP))
    o_ref[...] = jnp.concatenate(outs, axis=1)


def _swa_prompt_call(qkv, rel_bias, sinks):
    nb = SEQ // WINDOW
    qpos = WINDOW + np.arange(WINDOW)
    kpos = np.arange(2 * WINDOW)
    bkt = jnp.asarray(_bucket_table(qpos, kpos, np.ones(2 * WINDOW, bool)))
    kv_col = SWA_Q_DIM // (2 * SWA_KV_DIM)
    smem = pl.BlockSpec(memory_space=pltpu.SMEM)
    return pl.pallas_call(
        _swa_prompt_kernel,
        grid=(BATCH, nb),
        in_specs=[smem, smem,
                  pl.BlockSpec((WINDOW, 2 * WINDOW), lambda b, n: (0, 0)),
                  pl.BlockSpec((WINDOW, SWA_Q_DIM), lambda b, n: (b * nb + n, 0)),
                  pl.BlockSpec((WINDOW, 2 * SWA_KV_DIM),
                               lambda b, n: (jnp.maximum(b * nb + n - 1, 0), kv_col)),
                  pl.BlockSpec((WINDOW, 2 * SWA_KV_DIM), lambda b, n: (b * nb + n, kv_col))],
        out_specs=pl.BlockSpec((WINDOW, SWA_Q_DIM), lambda b, n: (b * nb + n, 0)),
        out_shape=jax.ShapeDtypeStruct((NP_TOK, SWA_Q_DIM), F32),
        scratch_shapes=[pltpu.VMEM((SWA_KV_HEADS, SWA_GROUP * WINDOW, 2 * WINDOW), F32)],
        compiler_params=_cparams(("arbitrary", "arbitrary")),
        name="swa_prompt",
    )(rel_bias, sinks, bkt, qkv, qkv, qkv)


def _swa_sample_kernel(rb_ref, sk_ref, bkt_ref, q_ref, k_ref, v_ref, o_ref, bias_scr, sink_scr):
    rows = SWA_GROUP * DEC_SEQ

    def group_of_row(shape):
        r = lax.broadcasted_iota(I32, shape, 0)
        return sum((r >= g * DEC_SEQ).astype(I32) for g in range(1, SWA_GROUP))

    @pl.when(pl.program_id(0) == 0)
    def _():
        bkt = bkt_ref[...]
        grp = group_of_row((rows, 1))
        grp_full = group_of_row(bkt.shape)
        for kv in range(SWA_KV_HEADS):
            acc = jnp.zeros(bkt.shape, F32)
            snk = jnp.zeros((rows, 1), F32)
            for g in range(SWA_GROUP):
                head = kv * SWA_GROUP + g
                acc = jnp.where(grp_full == g, _bias_from_buckets(bkt, rb_ref, head), acc)
                snk = jnp.where(grp == g, sk_ref[head], snk)
            bias_scr[kv] = acc
            sink_scr[kv] = snk

    scale = SWA_HD ** -0.5
    units = [(s, kv) for s in range(SWA_SB) for kv in range(SWA_KV_HEADS)]
    scores = {(s, kv): _bdot_nt(q_ref[s, kv], k_ref[s, :, kv * SWA_HD:(kv + 1) * SWA_HD])
              for (s, kv) in units}
    probs, den = {}, {}
    for (s, kv) in units:
        sc = scores[s, kv] * scale + bias_scr[kv]
        sink = sink_scr[kv]
        m = jnp.maximum(jnp.max(sc, axis=-1, keepdims=True), sink)
        p = jnp.exp(sc - m)
        den[s, kv] = jnp.sum(p, axis=-1, keepdims=True) + jnp.exp(sink - m)
        probs[s, kv] = p.astype(BF16)
    for (s, kv) in units:
        o_ref[s, kv] = _bdot(probs[s, kv], v_ref[s, :, kv * SWA_HD:(kv + 1) * SWA_HD]) / den[s, kv]


def _swa_sample_call(q_st, kc, vc, rel_bias, sinks):
    rows = SWA_GROUP * DEC_SEQ
    n_keys = kc.shape[1] - (SWA_KPAD - WINDOW) + DEC_SEQ
    qpos = PAST_LEN + np.arange(DEC_SEQ)
    kpos = PAST_LEN - (n_keys - DEC_SEQ) + np.arange(SWA_KPAD)
    bkt4 = _bucket_table(qpos, kpos, np.arange(SWA_KPAD) < n_keys)
    bkt = jnp.asarray(np.tile(bkt4, (SWA_GROUP, 1)))
    smem = pl.BlockSpec(memory_space=pltpu.SMEM)
    q_spec = pl.BlockSpec((SWA_SB, SWA_KV_HEADS, rows, SWA_HD), lambda i: (i, 0, 0, 0))
    kv_spec = pl.BlockSpec((SWA_SB, SWA_KPAD, SWA_KV_DIM), lambda i: (i, 0, 0))
    return pl.pallas_call(
        _swa_sample_kernel,
        grid=(DEC_BATCH // SWA_SB,),
        in_specs=[smem, smem, pl.BlockSpec((rows, SWA_KPAD), lambda i: (0, 0)),
                  q_spec, kv_spec, kv_spec],
        out_specs=q_spec,
        out_shape=jax.ShapeDtypeStruct((DEC_BATCH, SWA_KV_HEADS, rows, SWA_HD), F32),
        scratch_shapes=[pltpu.VMEM((SWA_KV_HEADS, rows, SWA_KPAD), F32),
                        pltpu.VMEM((SWA_KV_HEADS, rows, 1), F32)],
        compiler_params=_cparams(("arbitrary",)),
        name="swa_sample",
    )(rel_bias, sinks, bkt, q_st, kc, vc)


def _ffn_scratch():
    return [pltpu.VMEM((TM, D), F32),
            pltpu.VMEM((NF, D, TF), BF16),
            pltpu.VMEM((NF, D, TF), BF16),
            pltpu.VMEM((NF, TF, D), BF16),
            pltpu.VMEM((2, D, TF), F32),
            pltpu.VMEM((2, D, TF), F32),
            pltpu.VMEM((2, TF, D), F32),
            pltpu.SemaphoreType.DMA((3, 2))]


def _swiglu_into(acc_ref, x, expert, load, w_hbm, res, stg, sem):
    wg_hbm, wu_hbm, wd_hbm = w_hbm
    res_g, res_u, res_d = res
    stg_g, stg_u, stg_d = stg

    def copies(f, slot):
        col = pl.multiple_of(f * TF, TF)
        return (pltpu.make_async_copy(wg_hbm.at[expert, :, pl.ds(col, TF)], stg_g.at[slot], sem.at[0, slot]),
                pltpu.make_async_copy(wu_hbm.at[expert, :, pl.ds(col, TF)], stg_u.at[slot], sem.at[1, slot]),
                pltpu.make_async_copy(wd_hbm.at[expert, pl.ds(col, TF), :], stg_d.at[slot], sem.at[2, slot]))

    def block(f):
        gate = jnp.dot(x, res_g[f], preferred_element_type=F32)
        up = jnp.dot(x, res_u[f], preferred_element_type=F32)
        act = (_silu(gate) * up).astype(BF16)
        acc_ref[...] += jnp.dot(act, res_d[f], preferred_element_type=F32)

    acc_ref[...] = jnp.zeros_like(acc_ref)

    @pl.when(load)
    def _():
        for c in copies(0, 0):
            c.start()

        def body(f, carry):
            slot = lax.rem(f, 2)

            @pl.when(f + 1 < NF)
            def _():
                for c in copies(f + 1, 1 - slot):
                    c.start()

            for c in copies(f, slot):
                c.wait()
            res_g[f] = stg_g[slot].astype(BF16)
            res_u[f] = stg_u[slot].astype(BF16)
            res_d[f] = stg_d[slot].astype(BF16)
            block(f)
            return carry

        lax.fori_loop(0, NF, body, 0)

    @pl.when(jnp.logical_not(load))
    def _():
        def body(f, carry):
            block(f)
            return carry

        lax.fori_loop(0, NF, body, 0)


def _ffn_kernel(te_ref, ld_ref, nt_ref, x_ref, wg_hbm, wu_hbm, wd_hbm, o_ref, acc_ref, *scratch):
    t = pl.program_id(0)

    @pl.when(t < nt_ref[0])
    def _():
        _swiglu_into(acc_ref, x_ref[...].astype(BF16), te_ref[t], ld_ref[t] != 0,
                     (wg_hbm, wu_hbm, wd_hbm), scratch[0:3], scratch[3:6], scratch[6])
        o_ref[...] = acc_ref[...]

    @pl.when(t >= nt_ref[0])
    def _():
        o_ref[...] = jnp.zeros_like(o_ref)


def _ffn_call(x, tile_expert, n_tiles_used, w_gate, w_up, w_down):
    n_rows = x.shape[0]
    n_tiles = n_rows // TM
    changed = jnp.concatenate([jnp.ones((1,), I32),
                               (tile_expert[1:] != tile_expert[:-1]).astype(I32)])
    hbm = pl.BlockSpec(memory_space=pl.ANY)
    grid_spec = pltpu.PrefetchScalarGridSpec(
        num_scalar_prefetch=3,
        grid=(n_tiles,),
        in_specs=[pl.BlockSpec((TM, D), lambda t, te, ld, nt: (jnp.minimum(t, nt[0] - 1), 0)),
                  hbm, hbm, hbm],
        out_specs=pl.BlockSpec((TM, D), lambda t, *_: (t, 0)),
        scratch_shapes=_ffn_scratch(),
    )
    return pl.pallas_call(
        _ffn_kernel,
        grid_spec=grid_spec,
        out_shape=jax.ShapeDtypeStruct((n_rows, D), F32),
        compiler_params=_cparams(("arbitrary",)),
        name="ffn",
    )(tile_expert, changed, n_tiles_used, x, w_gate, w_up, w_down)


def _dense_ffn_kernel(x_ref, gpre_ref, shp_ref, shs_ref, scp_ref, scs_ref, wg_hbm, wu_hbm, wd_hbm,
                      gpost_ref, gtp_ref, gts_ref, out_ref, acc_ref, *scratch, layer_pair):
    i = pl.program_id(0)
    x = x_ref[...]
    h = _prenorm(i, x, gpre_ref, shp_ref, shs_ref, scp_ref, scs_ref).astype(BF16)
    _swiglu_into(acc_ref, h, layer_pair, i == 0, (wg_hbm, wu_hbm, wd_hbm),
                 scratch[0:3], scratch[3:6], scratch[6])
    out_ref[...] = x + _pick(i, gtp_ref, gts_ref) * (_rms(acc_ref[...]) * gpost_ref[...])


def _dense_ffn_call(x, g_pre, g_post, mod_p, mod_s, w_gate, w_up, w_down, layer_pair):
    shp, shs = _mod_specs(3)
    scp, scs = _mod_specs(4)
    gtp, gts = _mod_specs(5)
    tile = pl.BlockSpec((TM, D), lambda i: (i, 0))
    row = pl.BlockSpec((1, D), lambda i: (0, 0))
    hbm = pl.BlockSpec(memory_space=pl.ANY)
    return pl.pallas_call(
        functools.partial(_dense_ffn_kernel, layer_pair=layer_pair),
        grid=(N_TILES,),
        in_specs=[tile, row, shp, shs, scp, scs, hbm, hbm, hbm, row, gtp, gts],
        out_specs=tile,
        out_shape=jax.ShapeDtypeStruct((TT, D), F32),
        scratch_shapes=_ffn_scratch(),
        compiler_params=_cparams(("arbitrary",)),
        name="dense_ffn",
    )(x, g_pre.reshape(1, D), mod_p, mod_s, mod_p, mod_s, w_gate, w_up, w_down,
      g_post.reshape(1, D), mod_p, mod_s)


DMA_UNROLL = 8


def _row_copy(src_ref, dst_ref, src_row, dst_row, sem):
    return pltpu.make_async_copy(src_ref.at[pl.ds(src_row, 1)], dst_ref.at[pl.ds(dst_row, 1)], sem)


def _dispatch_kernel(tail_ref, nt_ref, d0_ref, d1_ref, h_ref, out_ref, zero_ref, sem, zsem):
    @pl.when(pl.program_id(0) == 0)
    def _():
        zero_ref[...] = jnp.zeros_like(zero_ref)

        def fill_tile(row0):
            fill = pltpu.make_async_copy(zero_ref, out_ref.at[pl.ds(pl.multiple_of(row0, TM), TM)], zsem)
            fill.start()
            fill.wait()

        for e in range(N_EXPERTS):
            @pl.when(tail_ref[e] >= 0)
            def _():
                fill_tile(tail_ref[e])

            @pl.when(nt_ref[0] + e < MOE_TILES)
            def _():
                fill_tile((nt_ref[0] + e) * TM)

    def start(r, carry):
        _row_copy(h_ref, out_ref, r, d0_ref[0, 0, r], sem).start(priority=0)
        _row_copy(h_ref, out_ref, r, d1_ref[0, 0, r], sem).start(priority=1)
        return carry

    def wait(r, carry):
        _row_copy(h_ref, out_ref, 0, 0, sem).wait()
        _row_copy(h_ref, out_ref, 0, 0, sem).wait()
        return carry

    lax.fori_loop(0, TM, start, 0, unroll=DMA_UNROLL)
    lax.fori_loop(0, TM, wait, 0, unroll=DMA_UNROLL)


def _dispatch_call(h, dest, tail_rows, n_tiles_used):
    idx_spec = pl.BlockSpec((1, 1, TM), lambda i, *_: (i, 0, 0), memory_space=pltpu.SMEM)
    grid_spec = pltpu.PrefetchScalarGridSpec(
        num_scalar_prefetch=2,
        grid=(N_TILES,),
        in_specs=[idx_spec, idx_spec, pl.BlockSpec((TM, D), lambda i, *_: (i, 0))],
        out_specs=pl.BlockSpec(memory_space=pl.ANY),
        scratch_shapes=[pltpu.VMEM((TM, D), h.dtype), pltpu.SemaphoreType.DMA(()),
                        pltpu.SemaphoreType.DMA(())],
    )
    return pl.pallas_call(
        _dispatch_kernel,
        grid_spec=grid_spec,
        out_shape=jax.ShapeDtypeStruct((MOE_ROWS, D), h.dtype),
        compiler_params=_cparams(("arbitrary",)),
        name="moe_dispatch",
    )(tail_rows, n_tiles_used, dest[0].reshape(N_TILES, 1, TM), dest[1].reshape(N_TILES, 1, TM), h)


def _combine_kernel(d0_ref, d1_ref, x_ref, wt_ref, ys_ref, g_ref, gtp_ref, gts_ref, *rest, split):
    i = pl.program_id(0)
    buf, sem = rest[-2:]

    def start(r, carry):
        _row_copy(ys_ref, buf.at[0], d0_ref[0, 0, r], r, sem).start(priority=0)
        _row_copy(ys_ref, buf.at[1], d1_ref[0, 0, r], r, sem).start(priority=1)
        return carry

    def wait(r, carry):
        _row_copy(ys_ref, buf.at[0], 0, 0, sem).wait()
        _row_copy(ys_ref, buf.at[1], 0, 0, sem).wait()
        return carry

    lax.fori_loop(0, TM, start, 0, unroll=DMA_UNROLL)
    lax.fori_loop(0, TM, wait, 0, unroll=DMA_UNROLL)
    wt = wt_ref[...]
    y = wt[:, 0:1] * buf[0] + wt[:, 1:2] * buf[1]
    out = x_ref[...] + _pick(i, gtp_ref, gts_ref) * (_rms(y) * g_ref[...])
    if split:
        prompt_ref, sample_ref = rest[:2]

        @pl.when(i < NP_TILES)
        def _():
            prompt_ref[...] = out

        @pl.when(i >= NP_TILES)
        def _():
            sample_ref[...] = out
    else:
        rest[0][...] = out


def _combine_call(x, ys, dest, wts, g, mod_p, mod_s, gate_chunk, split):
    gtp, gts = _mod_specs(gate_chunk)
    tile = pl.BlockSpec((TM, D), lambda i: (i, 0))
    idx_spec = pl.BlockSpec((1, 1, TM), lambda i: (i, 0, 0), memory_space=pltpu.SMEM)
    if split:
        out_specs = [pl.BlockSpec((TM, D), lambda i: (jnp.minimum(i, NP_TILES - 1), 0)),
                     pl.BlockSpec((TM, D), lambda i: (jnp.maximum(i - NP_TILES, 0), 0))]
        out_shape = [jax.ShapeDtypeStruct((NP_TOK, D), F32), jax.ShapeDtypeStruct((NS_TOK, D), F32)]
    else:
        out_specs = tile
        out_shape = jax.ShapeDtypeStruct((TT, D), F32)
    return pl.pallas_call(
        functools.partial(_combine_kernel, split=split),
        grid=(N_TILES,),
        in_specs=[idx_spec, idx_spec, tile, pl.BlockSpec((TM, LANES), lambda i: (i, 0)),
                  pl.BlockSpec(memory_space=pl.ANY),
                  pl.BlockSpec((1, D), lambda i: (0, 0)), gtp, gts],
        out_specs=out_specs,
        out_shape=out_shape,
        scratch_shapes=[pltpu.VMEM((2, TM, D), F32), pltpu.SemaphoreType.DMA(())],
        compiler_params=_cparams(("arbitrary",)),
        name="moe_combine",
    )(dest[0].reshape(N_TILES, 1, TM), dest[1].reshape(N_TILES, 1, TM), x, wts, ys,
      g.reshape(1, D), mod_p, mod_s)


def _route(top_idx):
    flat_e = top_idx.T.reshape(-1)
    onehot = (flat_e[:, None] == jnp.arange(N_EXPERTS, dtype=I32)[None, :]).astype(I32)
    csum = jnp.cumsum(onehot, axis=0)
    rank = jnp.sum((csum - onehot) * onehot, axis=1)
    counts = csum[-1]
    padded = ((counts + TM - 1) // TM) * TM
    ends = jnp.cumsum(padded)
    starts = ends - padded
    dest = (jnp.sum(onehot * starts[None, :], axis=1) + rank).astype(I32)
    tile_start = jnp.arange(MOE_TILES, dtype=I32) * TM
    tile_expert = jnp.minimum(jnp.sum(tile_start[:, None] >= ends[None, :], axis=1),
                              N_EXPERTS - 1).astype(I32)
    n_used = (ends[-1] // TM).astype(I32).reshape(1)
    tail_rows = jnp.where(padded > 0, ends - TM, -1).astype(I32)
    return dest.reshape(2, TT), tile_expert, n_used, tail_rows


def _gdn_layer(x, g_pre, w_in, conv_w, a_log, dt_bias, norm_w, w_out, state_conv, state_rec_all,
               layer_pair, g_post, mod_p, mod_s):
    w_main = w_in[:, :GDN_MAIN_DIM]
    w_ba = jnp.pad(w_in[:, GDN_MAIN_DIM:], ((0, 0), (0, LANES - 2 * GDN_V_HEADS)))
    qkvz = _proj_call(x, g_pre, mod_p, mod_s, w_main, jnp.zeros((GDN_MAIN_DIM,), F32), 1024)
    ba = _proj_call(x, g_pre, mod_p, mod_s, w_ba, jnp.zeros((LANES,), F32), LANES)

    tb = GDN_CHUNK * GDN_NB
    n_steps = SEQ // tb
    o_p, rec_p = _gdn_call(
        qkvz.reshape(TT // tb, tb, GDN_MAIN_DIM), ba.reshape(TT // tb, tb, LANES),
        conv_w, a_log, dt_bias, norm_w,
        jnp.zeros((BATCH, SUBLANES, GDN_CONV_DIM), F32),
        jnp.zeros((BATCH, GDN_V_HEADS, GDN_HD, GDN_HD), F32),
        n_seq=BATCH, n_steps=n_steps, chunk=GDN_CHUNK, n_chunks=GDN_NB, t_real=tb,
        n_par=GDN_PROMPT_PAR)
    o_p = o_p.reshape(NP_TOK, GDN_VAL_DIM)

    row_pad = ((0, 0), (0, GDN_SAMPLE_ROWS - DEC_SEQ), (0, 0))
    qkvz_s = qkvz[NP_TOK:].reshape(DEC_BATCH, DEC_SEQ, GDN_MAIN_DIM)
    ba_s = ba[NP_TOK:].reshape(DEC_BATCH, DEC_SEQ, LANES)
    cbuf_s = jnp.pad(state_conv, ((0, 0), (SUBLANES - (CONV_WIDTH - 1), 0), (0, 0)))
    o_s, rec_s = _gdn_call(
        jnp.pad(qkvz_s, row_pad), jnp.pad(ba_s, row_pad), conv_w, a_log, dt_bias, norm_w,
        cbuf_s, state_rec_all,
        n_seq=DEC_BATCH, n_steps=1, chunk=GDN_SAMPLE_ROWS, n_chunks=1, t_real=DEC_SEQ,
        n_par=GDN_SAMPLE_PAR, s0_first_seq=layer_pair * DEC_BATCH)
    o_s = o_s[:, 0, :DEC_SEQ].reshape(NS_TOK, GDN_VAL_DIM)

    keep = CONV_WIDTH - 1
    pre_p = jnp.stack([qkvz[(b + 1) * SEQ - keep:(b + 1) * SEQ, :GDN_CONV_DIM] for b in range(BATCH)])
    pre_s = jnp.concatenate([state_conv, qkvz_s[:, :, :GDN_CONV_DIM]], axis=1)[:, -keep:]
    x = _post_mm_call(x, o_p, o_s, w_out, jnp.zeros((D,), F32), g_post, mod_p, mod_s, 2)
    return x, pre_p, rec_p, pre_s, rec_s


def _swa_layer(x, g_pre, w_in, b_in, sinks, w_out, b_out, rel_bias, cache_k, cache_v,
               g_post, mod_p, mod_s):
    qkv = _proj_call(x, g_pre, mod_p, mod_s, w_in, b_in, SWA_PROJ_DIM)
    o_p = _swa_prompt_call(qkv, rel_bias, sinks)
    last = jnp.stack([qkv[(b + 1) * SEQ - WINDOW:(b + 1) * SEQ, SWA_Q_DIM:] for b in range(BATCH)])
    new_k_p = last[:, :, :SWA_KV_DIM]
    new_v_p = last[:, :, SWA_KV_DIM:]

    qkv_s = qkv[NP_TOK:].reshape(DEC_BATCH, DEC_SEQ, SWA_PROJ_DIM)
    q_st = qkv_s[:, :, :SWA_Q_DIM].reshape(DEC_BATCH, DEC_SEQ, SWA_KV_HEADS, SWA_GROUP, SWA_HD)
    q_st = q_st.transpose(0, 2, 3, 1, 4).reshape(DEC_BATCH, SWA_KV_HEADS, SWA_GROUP * DEC_SEQ, SWA_HD)
    wb = cache_k.shape[1]
    zpad = jnp.zeros((DEC_BATCH, SWA_KPAD - WINDOW - DEC_SEQ + (WINDOW - wb), SWA_KV_DIM), F32)
    kc = jnp.concatenate([cache_k.reshape(DEC_BATCH, wb, SWA_KV_DIM),
                          qkv_s[:, :, SWA_Q_DIM:SWA_Q_DIM + SWA_KV_DIM], zpad], axis=1)
    vc = jnp.concatenate([cache_v.reshape(DEC_BATCH, wb, SWA_KV_DIM),
                          qkv_s[:, :, SWA_Q_DIM + SWA_KV_DIM:], zpad], axis=1)
    o_st = _swa_sample_call(q_st, kc, vc, rel_bias, sinks)
    o_s = o_st.reshape(DEC_BATCH, SWA_KV_HEADS, SWA_GROUP, DEC_SEQ, SWA_HD)
    o_s = o_s.transpose(0, 3, 1, 2, 4).reshape(NS_TOK, SWA_Q_DIM)
    new_k_s = kc[:, DEC_SEQ:DEC_SEQ + wb]
    new_v_s = vc[:, DEC_SEQ:DEC_SEQ + wb]

    x = _post_mm_call(x, o_p, o_s, w_out, b_out, g_post, mod_p, mod_s, 2)
    shape_p = (BATCH, WINDOW, SWA_KV_HEADS, SWA_HD)
    shape_s = (DEC_BATCH, wb, SWA_KV_HEADS, SWA_HD)
    return (x, new_k_p.reshape(shape_p), new_v_p.reshape(shape_p),
            new_k_s.reshape(shape_s), new_v_s.reshape(shape_s))


def kernel(x_prompt, x_sample, c_prompt, c_sample, state_conv, state_rec, cache_win_k, cache_win_v, w_mod, b_mod, g_pre_mix, g_post_mix, g_pre_ffn, g_post_ffn, gdn_w_in, gdn_conv_w, gdn_a_log, gdn_dt_bias, gdn_norm_w, gdn_w_out, swa_w_in, swa_b_in, swa_sinks, swa_w_out, swa_b_out, rel_bias, ffn_w_gate, ffn_w_up, ffn_w_down, moe_w_router, moe_b_router, moe_w_gate, moe_w_up, moe_w_down):
    x = jnp.concatenate([x_prompt.reshape(NP_TOK, D), x_sample.reshape(NS_TOK, D)], axis=0)
    n_c = BATCH + DEC_BATCH
    c_all = jnp.concatenate([c_prompt, c_sample, jnp.zeros((-n_c % SUBLANES, D), F32)], axis=0)
    m_all = _mod_call(c_all, w_mod, b_mod)
    ffn_w = (ffn_w_gate, ffn_w_up, ffn_w_down)
    moe_w = (moe_w_gate.reshape(-1, D, D_FF), moe_w_up.reshape(-1, D, D_FF),
             moe_w_down.reshape(-1, D_FF, D))
    state_rec_all = state_rec.reshape((-1,) + state_rec.shape[2:])

    conv_p, rec_p, conv_s, rec_s = [], [], [], []
    k_p, v_p, k_s, v_s = [], [], [], []
    for layer in range(DEPTH):
        j = layer // 2
        mod_p = m_all[layer, :BATCH].reshape(BATCH, 1, 6 * D)
        mod_s = jnp.repeat(m_all[layer, BATCH:n_c], DEC_SEQ, axis=0)
        if layer % 2 == 0:
            x, cp, rp, cs, rs = _gdn_layer(
                x, g_pre_mix[layer], gdn_w_in[j], gdn_conv_w[j], gdn_a_log[j], gdn_dt_bias[j],
                gdn_norm_w[j], gdn_w_out[j], state_conv[j], state_rec_all, j, g_post_mix[layer],
                mod_p, mod_s)
            conv_p.append(cp); rec_p.append(rp); conv_s.append(cs); rec_s.append(rs)
        else:
            x, kp, vp, ks, vs = _swa_layer(
                x, g_pre_mix[layer], swa_w_in[j], swa_b_in[j], swa_sinks[j], swa_w_out[j],
                swa_b_out[j], rel_bias, cache_win_k[j], cache_win_v[j], g_post_mix[layer],
                mod_p, mod_s)
            k_p.append(kp); v_p.append(vp); k_s.append(ks); v_s.append(vs)

        if layer % 2 == 0:
            x = _dense_ffn_call(x, g_pre_ffn[layer], g_post_ffn[layer], mod_p, mod_s, *ffn_w, j)
        else:
            h, idx, wts = _prenorm_router_call(x, g_pre_ffn[layer], mod_p, mod_s, 3, 4,
                                               moe_w_router[j], moe_b_router[j])
            dest, tile_expert, n_used, tail_rows = _route(idx[:, :2])
            xs = _dispatch_call(h, dest, tail_rows, n_used)
            ys = _ffn_call(xs, tile_expert + j * N_EXPERTS, n_used, *moe_w)
            x = _combine_call(x, ys, dest, wts, g_post_ffn[layer], mod_p, mod_s, 5,
                              split=layer == DEPTH - 1)

    x_p, x_s = x if isinstance(x, (list, tuple)) else (x[:NP_TOK], x[NP_TOK:])
    y_prompt = x_p.reshape(BATCH, SEQ, D)
    y_sample = x_s.reshape(DEC_BATCH, DEC_SEQ, D)
    return (y_prompt, y_sample, jnp.stack(conv_p), jnp.stack(rec_p), jnp.stack(k_p), jnp.stack(v_p),
            jnp.stack(conv_s), jnp.stack(rec_s), jnp.stack(k_s), jnp.stack(v_s))
```

```python
import functools
import math

import numpy as np
import jax
import jax.numpy as jnp
from jax import lax
from jax.experimental import pallas as pl
from jax.experimental.pallas import tpu as pltpu

F32 = jnp.float32
BF16 = jnp.bfloat16
I32 = jnp.int32
HIGHEST = lax.Precision.HIGHEST

D = 1024
BATCH = 4
SEQ = 4096
DEPTH = 4
DEC_BATCH = 128
DEC_SEQ = 4
PAST_LEN = 8192
GDN_QK_HEADS = 4
GDN_V_HEADS = 8
GDN_HD = 128
GDN_KEY_DIM = GDN_QK_HEADS * GDN_HD
GDN_VAL_DIM = GDN_V_HEADS * GDN_HD
GDN_CONV_DIM = 2 * GDN_KEY_DIM + GDN_VAL_DIM
GDN_MAIN_DIM = GDN_CONV_DIM + GDN_VAL_DIM
CONV_WIDTH = 4
GDN_CHUNK = 64
SWA_Q_HEADS = 16
SWA_KV_HEADS = 4
SWA_GROUP = SWA_Q_HEADS // SWA_KV_HEADS
SWA_HD = 64
SWA_Q_DIM = SWA_Q_HEADS * SWA_HD
SWA_KV_DIM = SWA_KV_HEADS * SWA_HD
SWA_PROJ_DIM = SWA_Q_DIM + 2 * SWA_KV_DIM
WINDOW = 128
REL_BUCKETS = 32
REL_MAX_DIST = 128
D_FF = 2816
N_EXPERTS = 8
RMS_EPS = 1e-6

LANES = 128
SUBLANES = 8
VMEM_LIMIT = 56 * 1024 * 1024

TM = 512
NP_TOK = BATCH * SEQ
NS_TOK = DEC_BATCH * DEC_SEQ
TT = NP_TOK + NS_TOK
NP_TILES = NP_TOK // TM
NS_TILES = NS_TOK // TM
N_TILES = NP_TILES + NS_TILES
TILES_PER_SEQ = SEQ // TM
TF = 256
NF = D_FF // TF
TME = 1024
MOE_ROWS = 2 * TT + N_EXPERTS * TME
MOE_TILES = MOE_ROWS // TME
GDN_NB = 2
GDN_PROMPT_PAR = 4
GDN_SAMPLE_PAR = 4
GDN_SAMPLE_ROWS = 8
SWA_SB = 8
SWA_KPAD = WINDOW + 8

assert NP_TOK % TM == 0 and NS_TOK % TM == 0 and SEQ % TM == 0
assert D_FF % TF == 0 and DEC_SEQ <= GDN_SAMPLE_ROWS and DEC_SEQ <= 8


def _cparams(sem):
    return pltpu.CompilerParams(dimension_semantics=sem, vmem_limit_bytes=VMEM_LIMIT)


def _bdot(a, b):
    return jnp.dot(a.astype(BF16), b.astype(BF16), preferred_element_type=F32)


def _bdot_nt(a, b):
    return lax.dot_general(a.astype(BF16), b.astype(BF16), (((1,), (1,)), ((), ())),
                           preferred_element_type=F32)


def _silu(x):
    return x * jax.nn.sigmoid(x)


def _rms(x):
    return x * lax.rsqrt(jnp.mean(x * x, axis=-1, keepdims=True) + RMS_EPS)


def _mod_kernel(c_ref, w_ref, b_ref, o_ref):
    c = c_ref[...]
    o_ref[0] = _bdot(_silu(c), w_ref[0]) + b_ref[0]


def _mod_call(c_all, w_mod, b_mod):
    n = c_all.shape[0]
    tn = D
    return pl.pallas_call(
        _mod_kernel,
        grid=(DEPTH, 6 * D // tn),
        in_specs=[
            pl.BlockSpec((n, D), lambda l, j: (0, 0)),
            pl.BlockSpec((1, D, tn), lambda l, j: (l, 0, j)),
            pl.BlockSpec((1, 1, tn), lambda l, j: (l, 0, j)),
        ],
        out_specs=pl.BlockSpec((1, n, tn), lambda l, j: (l, 0, j)),
        out_shape=jax.ShapeDtypeStruct((DEPTH, n, 6 * D), F32),
        compiler_params=_cparams(("arbitrary", "arbitrary")),
        name="modulation",
    )(c_all, w_mod, b_mod.reshape(DEPTH, 1, 6 * D))


def _mod_specs(chunk, token_axis=0):
    p = pl.BlockSpec((1, 1, D), lambda *ids: (jnp.minimum(ids[token_axis] // TILES_PER_SEQ, BATCH - 1),
                                              0, chunk))
    s = pl.BlockSpec((TM, D), lambda *ids: (jnp.maximum(ids[token_axis] - NP_TILES, 0), chunk))
    return p, s


def _pick(i, p_ref, s_ref):
    return jnp.where(i < NP_TILES, p_ref[0], s_ref[...])


def _prenorm(i, x, g_ref, shp_ref, shs_ref, scp_ref, scs_ref):
    h = _rms(x) * g_ref[...]
    return h * (1.0 + _pick(i, scp_ref, scs_ref)) + _pick(i, shp_ref, shs_ref)


def _prenorm_router_kernel(x_ref, g_ref, shp_ref, shs_ref, scp_ref, scs_ref, wr_ref, br_ref,
                           h_ref, idx_ref, wt_ref):
    h = _prenorm(pl.program_id(0), x_ref[...], g_ref, shp_ref, shs_ref, scp_ref, scs_ref)
    h_ref[...] = h
    lane = lax.broadcasted_iota(I32, (TM, LANES), 1).astype(F32)
    logits = jnp.dot(h, wr_ref[...], precision=HIGHEST, preferred_element_type=F32) + br_ref[...]
    logits = jnp.where(lane < N_EXPERTS, logits, -jnp.inf)
    m1 = jnp.max(logits, axis=-1, keepdims=True)
    i1 = jnp.min(jnp.where(logits == m1, lane, float(LANES)), axis=-1, keepdims=True)
    rest = jnp.where(lane == i1, -jnp.inf, logits)
    m2 = jnp.max(rest, axis=-1, keepdims=True)
    i2 = jnp.min(jnp.where(rest == m2, lane, float(LANES)), axis=-1, keepdims=True)
    e2 = jnp.exp(m2 - m1)
    w1 = 1.0 / (1.0 + e2)
    w2 = e2 / (1.0 + e2)
    idx_ref[...] = jnp.where(lane == 0, i1, jnp.where(lane == 1, i2, 0.0)).astype(I32)
    wt_ref[...] = jnp.where(lane == 0, w1, jnp.where(lane == 1, w2, 0.0))


def _prenorm_router_call(x, g, mod_p, mod_s, sh_chunk, sc_chunk, w_router, b_router):
    shp, shs = _mod_specs(sh_chunk)
    scp, scs = _mod_specs(sc_chunk)
    tile = pl.BlockSpec((TM, D), lambda i: (i, 0))
    small = pl.BlockSpec((TM, LANES), lambda i: (i, 0))
    wr = jnp.pad(w_router, ((0, 0), (0, LANES - N_EXPERTS)))
    br = jnp.pad(b_router, (0, LANES - N_EXPERTS)).reshape(1, LANES)
    return pl.pallas_call(
        _prenorm_router_kernel,
        grid=(N_TILES,),
        in_specs=[tile, pl.BlockSpec((1, D), lambda i: (0, 0)), shp, shs, scp, scs,
                  pl.BlockSpec((D, LANES), lambda i: (0, 0)),
                  pl.BlockSpec((1, LANES), lambda i: (0, 0))],
        out_specs=[tile, small, small],
        out_shape=[jax.ShapeDtypeStruct((TT, D), F32),
                   jax.ShapeDtypeStruct((TT, LANES), I32),
                   jax.ShapeDtypeStruct((TT, LANES), F32)],
        compiler_params=_cparams(("arbitrary",)),
        name="prenorm_router",
    )(x, g.reshape(1, D), mod_p, mod_s, mod_p, mod_s, wr, br)


def _proj_kernel(x_ref, g_ref, shp_ref, shs_ref, scp_ref, scs_ref, w_ref, b_ref, o_ref, wbf_ref):
    i = pl.program_id(1)

    @pl.when(i == 0)
    def _():
        wbf_ref[...] = w_ref[...].astype(BF16)

    h = _prenorm(i, x_ref[...], g_ref, shp_ref, shs_ref, scp_ref, scs_ref).astype(BF16)
    o_ref[...] = jnp.dot(h, wbf_ref[...], preferred_element_type=F32) + b_ref[...]


def _proj_call(x, g, mod_p, mod_s, w, b, tn):
    k, n = w.shape
    shp, shs = _mod_specs(0, token_axis=1)
    scp, scs = _mod_specs(1, token_axis=1)
    return pl.pallas_call(
        _proj_kernel,
        grid=(n // tn, N_TILES),
        in_specs=[
            pl.BlockSpec((TM, k), lambda j, i: (i, 0)),
            pl.BlockSpec((1, k), lambda j, i: (0, 0)), shp, shs, scp, scs,
            pl.BlockSpec((k, tn), lambda j, i: (0, j)),
            pl.BlockSpec((1, tn), lambda j, i: (0, j)),
        ],
        out_specs=pl.BlockSpec((TM, tn), lambda j, i: (i, j)),
        out_shape=jax.ShapeDtypeStruct((TT, n), F32),
        scratch_shapes=[pltpu.VMEM((k, tn), BF16)],
        compiler_params=_cparams(("arbitrary", "arbitrary")),
        name="proj",
    )(x, g.reshape(1, k), mod_p, mod_s, mod_p, mod_s, w, b.reshape(1, n))


def _post_mm_kernel(x_ref, op_ref, os_ref, w_ref, b_ref, g_ref, gtp_ref, gts_ref, out_ref, wbf_ref):
    i = pl.program_id(0)

    @pl.when(i == 0)
    def _():
        wbf_ref[...] = w_ref[...].astype(BF16)

    o = jnp.where(i < NP_TILES, op_ref[...], os_ref[...]).astype(BF16)
    y = jnp.dot(o, wbf_ref[...], preferred_element_type=F32) + b_ref[...]
    out_ref[...] = x_ref[...] + _pick(i, gtp_ref, gts_ref) * (_rms(y) * g_ref[...])


def _post_mm_call(x, o_p, o_s, w, b, g, mod_p, mod_s, gate_chunk):
    gtp, gts = _mod_specs(gate_chunk)
    tile = pl.BlockSpec((TM, D), lambda i: (i, 0))
    row = pl.BlockSpec((1, D), lambda i: (0, 0))
    return pl.pallas_call(
        _post_mm_kernel,
        grid=(N_TILES,),
        in_specs=[tile,
                  pl.BlockSpec((TM, D), lambda i: (jnp.minimum(i, NP_TILES - 1), 0)),
                  pl.BlockSpec((TM, D), lambda i: (jnp.maximum(i - NP_TILES, 0), 0)),
                  pl.BlockSpec((D, D), lambda i: (0, 0)), row, row, gtp, gts],
        out_specs=tile,
        out_shape=jax.ShapeDtypeStruct((TT, D), F32),
        scratch_shapes=[pltpu.VMEM((D, D), BF16)],
        compiler_params=_cparams(("arbitrary",)),
        name="post_mm",
    )(x, o_p, o_s, w, b.reshape(1, D), g.reshape(1, D), mod_p, mod_s)


def _gdn_kernel(*refs, chunk, n_chunks, t_real, n_par):
    x_refs = refs[:n_par]
    ba_refs = refs[n_par:2 * n_par]
    (cw_ref, alog_ref, dt_ref, nw_ref, cb_ref, s0_ref, o_ref, sfin_ref, xbuf, s_scr) = refs[2 * n_par:]
    c_sz = chunk
    tb = chunk * n_chunks
    j = pl.program_id(1)
    rep = GDN_V_HEADS // GDN_QK_HEADS
    units = [(p, h) for p in range(n_par) for h in range(GDN_V_HEADS)]

    @pl.when(j == 0)
    def _():
        s_scr[...] = s0_ref[...]
        xbuf[:, 0:SUBLANES, :] = cb_ref[...]

    for p in range(n_par):
        xbuf[p, SUBLANES:SUBLANES + tb, :] = x_refs[p][:, 0:GDN_CONV_DIM]

    ii = lax.broadcasted_iota(I32, (c_sz, c_sz), 0)
    jj = lax.broadcasted_iota(I32, (c_sz, c_sz), 1)
    causal = ii >= jj
    strict = ii > jj
    tri = causal.astype(F32)
    eye_c = (ii == jj).astype(F32)
    pair_mask = [jnp.logical_and((ii >> (l + 1)) == (jj >> (l + 1)), (ii >> l) != (jj >> l))
                 for l in range(int(math.log2(c_sz)))]
    eye_t = (lax.broadcasted_iota(I32, (2 * SUBLANES, LANES), 0)
             == lax.broadcasted_iota(I32, (2 * SUBLANES, LANES), 1)).astype(F32)
    lane = lax.broadcasted_iota(I32, (c_sz, LANES), 1)
    row = lax.broadcasted_iota(I32, (c_sz, LANES), 0)
    cw = cw_ref[...]
    neg_a = -jnp.exp(alog_ref[...])
    dt = dt_ref[...]
    nw = nw_ref[...]

    def chunk_body(c, carry):
        r0 = pl.multiple_of(c * c_sz, c_sz)
        xc, beta, gc, rows_t = [], [], [], []
        for p in range(n_par):
            win = xbuf[p, pl.ds(r0, c_sz + SUBLANES), :]
            y = win[SUBLANES:SUBLANES + c_sz] * cw[3:4]
            for tap in range(1, CONV_WIDTH):
                y = y + win[SUBLANES - tap:SUBLANES - tap + c_sz] * cw[3 - tap:4 - tap]
            xc.append(_silu(y))
            ba = ba_refs[p][pl.ds(r0, c_sz), :]
            b_p = jax.nn.sigmoid(ba)
            g_p = neg_a * jax.nn.softplus(ba + dt)
            if t_real < tb:
                live = (row + r0) < t_real
                b_p = jnp.where(live, b_p, 0.0)
                g_p = jnp.where(live, g_p, 0.0)
            gc_p = jnp.dot(tri, g_p, precision=HIGHEST, preferred_element_type=F32)
            cols = jnp.where(lane < GDN_V_HEADS, b_p, gc_p)
            rows_t.append(lax.dot_general(eye_t, cols, (((1,), (1,)), ((), ())), precision=HIGHEST,
                                          preferred_element_type=F32))
            beta.append(b_p)
            gc.append(gc_p)

        qn, kn, kk, qk = {}, {}, {}, {}
        for p in range(n_par):
            for hq in range(GDN_QK_HEADS):
                q = xc[p][:, hq * GDN_HD:(hq + 1) * GDN_HD]
                k = xc[p][:, GDN_KEY_DIM + hq * GDN_HD:GDN_KEY_DIM + (hq + 1) * GDN_HD]
                q = q * lax.rsqrt(jnp.sum(q * q, axis=-1, keepdims=True) + RMS_EPS) * (GDN_HD ** -0.5)
                k = k * lax.rsqrt(jnp.sum(k * k, axis=-1, keepdims=True) + RMS_EPS)
                qn[p, hq], kn[p, hq] = q, k
        for key in qn:
            kk[key] = _bdot_nt(kn[key], kn[key])
            qk[key] = _bdot_nt(qn[key], kn[key])

        gcc, gcl, bcol, decay, a_mat, inv = {}, {}, {}, {}, {}, {}
        for (p, h) in units:
            gcc[p, h] = gc[p][:, GDN_V_HEADS + h:GDN_V_HEADS + h + 1]
            gcr = rows_t[p][GDN_V_HEADS + h:GDN_V_HEADS + h + 1, :]
            bcol[p, h] = beta[p][:, h:h + 1]
            gcl[p, h] = gcc[p, h][c_sz - 1:c_sz, :]
            decay[p, h] = jnp.exp(jnp.where(causal, gcc[p, h] - gcr, -jnp.inf))
            a_mat[p, h] = jnp.where(strict, bcol[p, h] * kk[p, h // rep] * decay[p, h], 0.0)
            inv[p, h] = eye_c - jnp.where(pair_mask[0], a_mat[p, h], 0.0)
        for lvl in range(1, len(pair_mask)):
            t1 = {u: _bdot(jnp.where(pair_mask[lvl], a_mat[u], 0.0), inv[u]) for u in units}
            t2 = {u: _bdot(inv[u], t1[u]) for u in units}
            inv = {u: inv[u] - t2[u] for u in units}

        egc = {u: jnp.exp(gcc[u]) for u in units}
        sol = {}
        for (p, h) in units:
            v = xc[p][:, 2 * GDN_KEY_DIM + h * GDN_HD:2 * GDN_KEY_DIM + (h + 1) * GDN_HD]
            rhs = jnp.concatenate([v * bcol[p, h], kn[p, h // rep] * (bcol[p, h] * egc[p, h])], axis=1)
            sol[p, h] = _bdot(inv[p, h], rhs)
        ws = {}
        for (p, h) in units:
            q_dec = qn[p, h // rep] * egc[p, h]
            ws[p, h] = _bdot(jnp.concatenate([sol[p, h][:, GDN_HD:], q_dec], axis=0), s_scr[p, h])
        u_new = {u: sol[u][:, :GDN_HD] - ws[u][:c_sz] for u in units}
        o_part, s_part = {}, {}
        for (p, h) in units:
            qkm = jnp.where(causal, qk[p, h // rep] * decay[p, h], 0.0)
            k_dec = kn[p, h // rep] * jnp.exp(gcl[p, h] - gcc[p, h])
            o_part[p, h] = _bdot(qkm, u_new[p, h])
            s_part[p, h] = lax.dot_general(k_dec.astype(BF16), u_new[p, h].astype(BF16),
                                           (((0,), (0,)), ((), ())), preferred_element_type=F32)
        for (p, h) in units:
            s_scr[p, h] = s_scr[p, h] * jnp.exp(gcl[p, h]) + s_part[p, h]
            o = ws[p, h][c_sz:] + o_part[p, h]
            z = x_refs[p][pl.ds(r0, c_sz), GDN_CONV_DIM + h * GDN_HD:GDN_CONV_DIM + (h + 1) * GDN_HD]
            o_ref[p, pl.ds(r0, c_sz), h * GDN_HD:(h + 1) * GDN_HD] = _rms(o) * nw * _silu(z)
        return carry

    lax.fori_loop(0, n_chunks, chunk_body, 0)
    xbuf[:, 0:SUBLANES, :] = xbuf[:, tb:tb + SUBLANES, :]

    @pl.when(j == pl.num_programs(1) - 1)
    def _():
        sfin_ref[...] = s_scr[...]


def _gdn_call(x, ba, conv_w, a_log, dt_bias, norm_w, conv_buf, s0, *, n_seq, n_steps, chunk,
              n_chunks, t_real, n_par, s0_first_seq=0):
    s0_block = s0_first_seq // n_par
    tb = chunk * n_chunks
    pad = LANES - 2 * GDN_V_HEADS
    alog_row = jnp.pad(a_log, (GDN_V_HEADS, pad)).reshape(1, LANES)
    dt_row = jnp.pad(dt_bias, (GDN_V_HEADS, pad)).reshape(1, LANES)

    def seq_block(width, p):
        return pl.BlockSpec((None, tb, width), lambda s, j: ((s * n_par + p) * n_steps + j, 0, 0))

    const = lambda s, j: (0, 0)
    state_spec = pl.BlockSpec((n_par, GDN_V_HEADS, GDN_HD, GDN_HD), lambda s, j: (s, 0, 0, 0))
    return pl.pallas_call(
        functools.partial(_gdn_kernel, chunk=chunk, n_chunks=n_chunks, t_real=t_real, n_par=n_par),
        grid=(n_seq // n_par, n_steps),
        in_specs=[seq_block(GDN_MAIN_DIM, p) for p in range(n_par)]
                 + [seq_block(LANES, p) for p in range(n_par)]
                 + [pl.BlockSpec((CONV_WIDTH, GDN_CONV_DIM), const),
                    pl.BlockSpec((1, LANES), const), pl.BlockSpec((1, LANES), const),
                    pl.BlockSpec((1, GDN_HD), const),
                    pl.BlockSpec((n_par, SUBLANES, GDN_CONV_DIM), lambda s, j: (s, 0, 0)),
                    pl.BlockSpec((n_par, GDN_V_HEADS, GDN_HD, GDN_HD),
                                 lambda s, j: (s + s0_block, 0, 0, 0))],
        out_specs=[pl.BlockSpec((n_par, None, tb, GDN_VAL_DIM), lambda s, j: (s, j, 0, 0)),
                   state_spec],
        out_shape=[jax.ShapeDtypeStruct((n_seq, n_steps, tb, GDN_VAL_DIM), F32),
                   jax.ShapeDtypeStruct((n_seq, GDN_V_HEADS, GDN_HD, GDN_HD), F32)],
        scratch_shapes=[pltpu.VMEM((n_par, tb + SUBLANES, GDN_CONV_DIM), F32),
                        pltpu.VMEM((n_par, GDN_V_HEADS, GDN_HD, GDN_HD), F32)],
        compiler_params=_cparams(("arbitrary", "arbitrary")),
        name="gdn",
    )(*([x] * n_par), *([ba] * n_par), conv_w, alog_row, dt_row, norm_w.reshape(1, GDN_HD),
      conv_buf, s0)


def _t5_bucket_np(dist):
    dist = np.maximum(dist, 0)
    max_exact = REL_BUCKETS // 2
    ratio = np.log(np.maximum(dist, max_exact).astype(np.float32) / np.float32(max_exact)) \
        / np.float32(math.log(REL_MAX_DIST / max_exact))
    large = max_exact + (ratio.astype(np.float32) * np.float32(REL_BUCKETS - max_exact)).astype(np.int32)
    return np.where(dist < max_exact, dist, np.minimum(large, REL_BUCKETS - 1)).astype(np.int32)


def _bucket_table(qpos, kpos, k_valid):
    dist = qpos[:, None] - kpos[None, :]
    ok = (dist >= 0) & (dist < WINDOW) & (kpos[None, :] >= 0) & k_valid[None, :]
    return np.where(ok, _t5_bucket_np(dist), -1).astype(np.int32)


def _bias_from_buckets(bkt, rb_ref, head):
    def body(b, acc):
        return jnp.where(bkt == b, rb_ref[b, head], acc)
    acc = lax.fori_loop(0, REL_BUCKETS, body, jnp.zeros(bkt.shape, F32))
    return jnp.where(bkt < 0, -jnp.inf, acc)


def _swa_prompt_kernel(rb_ref, sk_ref, bkt_ref, q_ref, kvp_ref, kvc_ref, o_ref, bias_scr):
    first = jnp.logical_and(pl.program_id(0) == 0, pl.program_id(1) == 0)
    n = pl.program_id(1)

    @pl.when(first)
    def _():
        bkt = bkt_ref[...]
        for h in range(SWA_Q_HEADS):
            kv, g = divmod(h, SWA_GROUP)
            bias_scr[kv, g * WINDOW:(g + 1) * WINDOW, :] = _bias_from_buckets(bkt, rb_ref, h)

    q = q_ref[...]
    scale = SWA_HD ** -0.5
    heads = range(SWA_KV_HEADS)
    s_prev, s_cur = [], []
    for kv in heads:
        qs = jnp.concatenate(
            [q[:, (kv * SWA_GROUP + g) * SWA_HD:(kv * SWA_GROUP + g + 1) * SWA_HD]
             for g in range(SWA_GROUP)], axis=0).astype(BF16)
        s_prev.append(_bdot_nt(qs, kvp_ref[:, kv * SWA_HD:(kv + 1) * SWA_HD]))
        s_cur.append(_bdot_nt(qs, kvc_ref[:, kv * SWA_HD:(kv + 1) * SWA_HD]))
    p_prev, p_cur, den = [], [], []
    for kv in heads:
        sp = s_prev[kv] * scale + bias_scr[kv, :, 0:WINDOW]
        sp = jnp.where(n > 0, sp, -jnp.inf)
        sc = s_cur[kv] * scale + bias_scr[kv, :, WINDOW:2 * WINDOW]
        sink = jnp.concatenate(
            [jnp.full((WINDOW, 1), sk_ref[kv * SWA_GROUP + g], F32) for g in range(SWA_GROUP)], axis=0)
        m = jnp.maximum(jnp.maximum(jnp.max(sp, axis=-1, keepdims=True),
                                    jnp.max(sc, axis=-1, keepdims=True)), sink)
        pp = jnp.exp(sp - m)
        pc = jnp.exp(sc - m)
        den.append(jnp.sum(pp, axis=-1, keepdims=True) + jnp.sum(pc, axis=-1, keepdims=True)
                   + jnp.exp(sink - m))
        p_prev.append(pp.astype(BF16))
        p_cur.append(pc.astype(BF16))
    outs = []
    for kv in heads:
        v_prev = kvp_ref[:, SWA_KV_DIM + kv * SWA_HD:SWA_KV_DIM + (kv + 1) * SWA_HD]
        v_cur = kvc_ref[:, SWA_KV_DIM + kv * SWA_HD:SWA_KV_DIM + (kv + 1) * SWA_HD]
        o = (_bdot(p_prev[kv], v_prev) + _bdot(p_cur[kv], v_cur)) / den[kv]
        outs.extend(o[g * WINDOW:(g + 1) * WINDOW] for g in range(SWA_GROUP))
    o_ref[...] = jnp.concatenate(outs, axis=1)


def _swa_prompt_call(qkv, rel_bias, sinks):
    nb = SEQ // WINDOW
    qpos = WINDOW + np.arange(WINDOW)
    kpos = np.arange(2 * WINDOW)
    bkt = jnp.asarray(_bucket_table(qpos, kpos, np.ones(2 * WINDOW, bool)))
    kv_col = SWA_Q_DIM // (2 * SWA_KV_DIM)
    smem = pl.BlockSpec(memory_space=pltpu.SMEM)
    return pl.pallas_call(
        _swa_prompt_kernel,
        grid=(BATCH, nb),
        in_specs=[smem, smem,
                  pl.BlockSpec((WINDOW, 2 * WINDOW), lambda b, n: (0, 0)),
                  pl.BlockSpec((WINDOW, SWA_Q_DIM), lambda b, n: (b * nb + n, 0)),
                  pl.BlockSpec((WINDOW, 2 * SWA_KV_DIM),
                               lambda b, n: (jnp.maximum(b * nb + n - 1, 0), kv_col)),
                  pl.BlockSpec((WINDOW, 2 * SWA_KV_DIM), lambda b, n: (b * nb + n, kv_col))],
        out_specs=pl.BlockSpec((WINDOW, SWA_Q_DIM), lambda b, n: (b * nb + n, 0)),
        out_shape=jax.ShapeDtypeStruct((NP_TOK, SWA_Q_DIM), F32),
        scratch_shapes=[pltpu.VMEM((SWA_KV_HEADS, SWA_GROUP * WINDOW, 2 * WINDOW), F32)],
        compiler_params=_cparams(("arbitrary", "arbitrary")),
        name="swa_prompt",
    )(rel_bias, sinks, bkt, qkv, qkv, qkv)


def _swa_sample_kernel(rb_ref, sk_ref, bkt_ref, q_ref, k_ref, v_ref, o_ref, bias_scr, sink_scr):
    rows = SWA_GROUP * DEC_SEQ

    def group_of_row(shape):
        r = lax.broadcasted_iota(I32, shape, 0)
        return sum((r >= g * DEC_SEQ).astype(I32) for g in range(1, SWA_GROUP))

    @pl.when(pl.program_id(0) == 0)
    def _():
        bkt = bkt_ref[...]
        grp = group_of_row((rows, 1))
        grp_full = group_of_row(bkt.shape)
        for kv in range(SWA_KV_HEADS):
            acc = jnp.zeros(bkt.shape, F32)
            snk = jnp.zeros((rows, 1), F32)
            for g in range(SWA_GROUP):
                head = kv * SWA_GROUP + g
                acc = jnp.where(grp_full == g, _bias_from_buckets(bkt, rb_ref, head), acc)
                snk = jnp.where(grp == g, sk_ref[head], snk)
            bias_scr[kv] = acc
            sink_scr[kv] = snk

    scale = SWA_HD ** -0.5
    units = [(s, kv) for s in range(SWA_SB) for kv in range(SWA_KV_HEADS)]
    scores = {(s, kv): _bdot_nt(q_ref[s, kv], k_ref[s, :, kv * SWA_HD:(kv + 1) * SWA_HD])
              for (s, kv) in units}
    probs, den = {}, {}
    for (s, kv) in units:
        sc = scores[s, kv] * scale + bias_scr[kv]
        sink = sink_scr[kv]
        m = jnp.maximum(jnp.max(sc, axis=-1, keepdims=True), sink)
        p = jnp.exp(sc - m)
        den[s, kv] = jnp.sum(p, axis=-1, keepdims=True) + jnp.exp(sink - m)
        probs[s, kv] = p.astype(BF16)
    for (s, kv) in units:
        o_ref[s, kv] = _bdot(probs[s, kv], v_ref[s, :, kv * SWA_HD:(kv + 1) * SWA_HD]) / den[s, kv]


def _swa_sample_call(q_st, kc, vc, rel_bias, sinks):
    rows = SWA_GROUP * DEC_SEQ
    n_keys = kc.shape[1] - (SWA_KPAD - WINDOW) + DEC_SEQ
    qpos = PAST_LEN + np.arange(DEC_SEQ)
    kpos = PAST_LEN - (n_keys - DEC_SEQ) + np.arange(SWA_KPAD)
    bkt4 = _bucket_table(qpos, kpos, np.arange(SWA_KPAD) < n_keys)
    bkt = jnp.asarray(np.tile(bkt4, (SWA_GROUP, 1)))
    smem = pl.BlockSpec(memory_space=pltpu.SMEM)
    q_spec = pl.BlockSpec((SWA_SB, SWA_KV_HEADS, rows, SWA_HD), lambda i: (i, 0, 0, 0))
    kv_spec = pl.BlockSpec((SWA_SB, SWA_KPAD, SWA_KV_DIM), lambda i: (i, 0, 0))
    return pl.pallas_call(
        _swa_sample_kernel,
        grid=(DEC_BATCH // SWA_SB,),
        in_specs=[smem, smem, pl.BlockSpec((rows, SWA_KPAD), lambda i: (0, 0)),
                  q_spec, kv_spec, kv_spec],
        out_specs=q_spec,
        out_shape=jax.ShapeDtypeStruct((DEC_BATCH, SWA_KV_HEADS, rows, SWA_HD), F32),
        scratch_shapes=[pltpu.VMEM((SWA_KV_HEADS, rows, SWA_KPAD), F32),
                        pltpu.VMEM((SWA_KV_HEADS, rows, 1), F32)],
        compiler_params=_cparams(("arbitrary",)),
        name="swa_sample",
    )(rel_bias, sinks, bkt, q_st, kc, vc)


def _ffn_scratch(rows):
    return [pltpu.VMEM((rows, D), F32),
            pltpu.VMEM((NF, D, TF), BF16),
            pltpu.VMEM((NF, D, TF), BF16),
            pltpu.VMEM((NF, TF, D), BF16),
            pltpu.VMEM((2, D, TF), F32),
            pltpu.VMEM((2, D, TF), F32),
            pltpu.VMEM((2, TF, D), F32),
            pltpu.SemaphoreType.DMA((3, 2))]


def _swiglu_into(acc_ref, x, expert, load, w_hbm, res, stg, sem):
    wg_hbm, wu_hbm, wd_hbm = w_hbm
    res_g, res_u, res_d = res
    stg_g, stg_u, stg_d = stg

    def copies(f, slot):
        col = pl.multiple_of(f * TF, TF)
        return (pltpu.make_async_copy(wg_hbm.at[expert, :, pl.ds(col, TF)], stg_g.at[slot], sem.at[0, slot]),
                pltpu.make_async_copy(wu_hbm.at[expert, :, pl.ds(col, TF)], stg_u.at[slot], sem.at[1, slot]),
                pltpu.make_async_copy(wd_hbm.at[expert, pl.ds(col, TF), :], stg_d.at[slot], sem.at[2, slot]))

    def block(f):
        gate = jnp.dot(x, res_g[f], preferred_element_type=F32)
        up = jnp.dot(x, res_u[f], preferred_element_type=F32)
        act = (_silu(gate) * up).astype(BF16)
        acc_ref[...] += jnp.dot(act, res_d[f], preferred_element_type=F32)

    acc_ref[...] = jnp.zeros_like(acc_ref)

    @pl.when(load)
    def _():
        for c in copies(0, 0):
            c.start()

        def body(f, carry):
            slot = lax.rem(f, 2)

            @pl.when(f + 1 < NF)
            def _():
                for c in copies(f + 1, 1 - slot):
                    c.start()

            for c in copies(f, slot):
                c.wait()
            res_g[f] = stg_g[slot].astype(BF16)
            res_u[f] = stg_u[slot].astype(BF16)
            res_d[f] = stg_d[slot].astype(BF16)
            block(f)
            return carry

        lax.fori_loop(0, NF, body, 0)

    @pl.when(jnp.logical_not(load))
    def _():
        def body(f, carry):
            block(f)
            return carry

        lax.fori_loop(0, NF, body, 0)


def _ffn_kernel(te_ref, ld_ref, nt_ref, x_ref, wg_hbm, wu_hbm, wd_hbm, o_ref, acc_ref, *scratch):
    t = pl.program_id(0)

    @pl.when(t < nt_ref[0])
    def _():
        _swiglu_into(acc_ref, x_ref[...].astype(BF16), te_ref[t], ld_ref[t] != 0,
                     (wg_hbm, wu_hbm, wd_hbm), scratch[0:3], scratch[3:6], scratch[6])
        o_ref[...] = acc_ref[...]

    @pl.when(t >= nt_ref[0])
    def _():
        o_ref[...] = jnp.zeros_like(o_ref)


def _ffn_call(x, tile_expert, n_tiles_used, w_gate, w_up, w_down):
    n_rows = x.shape[0]
    n_tiles = n_rows // TME
    changed = jnp.concatenate([jnp.ones((1,), I32),
                               (tile_expert[1:] != tile_expert[:-1]).astype(I32)])
    hbm = pl.BlockSpec(memory_space=pl.ANY)
    grid_spec = pltpu.PrefetchScalarGridSpec(
        num_scalar_prefetch=3,
        grid=(n_tiles,),
        in_specs=[pl.BlockSpec((TME, D), lambda t, te, ld, nt: (jnp.minimum(t, nt[0] - 1), 0)),
                  hbm, hbm, hbm],
        out_specs=pl.BlockSpec((TME, D), lambda t, *_: (t, 0)),
        scratch_shapes=_ffn_scratch(TME),
    )
    return pl.pallas_call(
        _ffn_kernel,
        grid_spec=grid_spec,
        out_shape=jax.ShapeDtypeStruct((n_rows, D), F32),
        compiler_params=_cparams(("arbitrary",)),
        name="ffn",
    )(tile_expert, changed, n_tiles_used, x, w_gate, w_up, w_down)


def _dense_ffn_kernel(x_ref, gpre_ref, shp_ref, shs_ref, scp_ref, scs_ref, wg_hbm, wu_hbm, wd_hbm,
                      gpost_ref, gtp_ref, gts_ref, out_ref, acc_ref, *scratch, layer_pair):
    i = pl.program_id(0)
    x = x_ref[...]
    h = _prenorm(i, x, gpre_ref, shp_ref, shs_ref, scp_ref, scs_ref).astype(BF16)
    _swiglu_into(acc_ref, h, layer_pair, i == 0, (wg_hbm, wu_hbm, wd_hbm),
                 scratch[0:3], scratch[3:6], scratch[6])
    out_ref[...] = x + _pick(i, gtp_ref, gts_ref) * (_rms(acc_ref[...]) * gpost_ref[...])


def _dense_ffn_call(x, g_pre, g_post, mod_p, mod_s, w_gate, w_up, w_down, layer_pair):
    shp, shs = _mod_specs(3)
    scp, scs = _mod_specs(4)
    gtp, gts = _mod_specs(5)
    tile = pl.BlockSpec((TM, D), lambda i: (i, 0))
    row = pl.BlockSpec((1, D), lambda i: (0, 0))
    hbm = pl.BlockSpec(memory_space=pl.ANY)
    return pl.pallas_call(
        functools.partial(_dense_ffn_kernel, layer_pair=layer_pair),
        grid=(N_TILES,),
        in_specs=[tile, row, shp, shs, scp, scs, hbm, hbm, hbm, row, gtp, gts],
        out_specs=tile,
        out_shape=jax.ShapeDtypeStruct((TT, D), F32),
        scratch_shapes=_ffn_scratch(TM),
        compiler_params=_cparams(("arbitrary",)),
        name="dense_ffn",
    )(x, g_pre.reshape(1, D), mod_p, mod_s, mod_p, mod_s, w_gate, w_up, w_down,
      g_post.reshape(1, D), mod_p, mod_s)


DMA_UNROLL = 8


def _row_copy(src_ref, dst_ref, src_row, dst_row, sem):
    return pltpu.make_async_copy(src_ref.at[pl.ds(src_row, 1)], dst_ref.at[pl.ds(dst_row, 1)], sem)


def _dispatch_kernel(tail_ref, nt_ref, d0_ref, d1_ref, h_ref, out_ref, zero_ref, sem, zsem):
    @pl.when(pl.program_id(0) == 0)
    def _():
        zero_ref[...] = jnp.zeros_like(zero_ref)

        def fill_tile(row0):
            fill = pltpu.make_async_copy(zero_ref, out_ref.at[pl.ds(pl.multiple_of(row0, TME), TME)], zsem)
            fill.start()
            fill.wait()

        for e in range(N_EXPERTS):
            @pl.when(tail_ref[e] >= 0)
            def _():
                fill_tile(tail_ref[e])

            @pl.when(nt_ref[0] + e < MOE_TILES)
            def _():
                fill_tile((nt_ref[0] + e) * TME)

    def start(r, carry):
        _row_copy(h_ref, out_ref, r, d0_ref[0, 0, r], sem).start(priority=0)
        _row_copy(h_ref, out_ref, r, d1_ref[0, 0, r], sem).start(priority=1)
        return carry

    def wait(r, carry):
        _row_copy(h_ref, out_ref, 0, 0, sem).wait()
        _row_copy(h_ref, out_ref, 0, 0, sem).wait()
        return carry

    lax.fori_loop(0, TM, start, 0, unroll=DMA_UNROLL)
    lax.fori_loop(0, TM, wait, 0, unroll=DMA_UNROLL)


def _dispatch_call(h, dest, tail_rows, n_tiles_used):
    idx_spec = pl.BlockSpec((1, 1, TM), lambda i, *_: (i, 0, 0), memory_space=pltpu.SMEM)
    grid_spec = pltpu.PrefetchScalarGridSpec(
        num_scalar_prefetch=2,
        grid=(N_TILES,),
        in_specs=[idx_spec, idx_spec, pl.BlockSpec((TM, D), lambda i, *_: (i, 0))],
        out_specs=pl.BlockSpec(memory_space=pl.ANY),
        scratch_shapes=[pltpu.VMEM((TME, D), h.dtype), pltpu.SemaphoreType.DMA(()),
                        pltpu.SemaphoreType.DMA(())],
    )
    return pl.pallas_call(
        _dispatch_kernel,
        grid_spec=grid_spec,
        out_shape=jax.ShapeDtypeStruct((MOE_ROWS, D), h.dtype),
        compiler_params=_cparams(("arbitrary",)),
        name="moe_dispatch",
    )(tail_rows, n_tiles_used, dest[0].reshape(N_TILES, 1, TM), dest[1].reshape(N_TILES, 1, TM), h)


def _combine_kernel(d0_ref, d1_ref, x_ref, wt_ref, ys_ref, g_ref, gtp_ref, gts_ref, *rest, split):
    i = pl.program_id(0)
    buf, sem = rest[-2:]

    def start(r, carry):
        _row_copy(ys_ref, buf.at[0], d0_ref[0, 0, r], r, sem).start(priority=0)
        _row_copy(ys_ref, buf.at[1], d1_ref[0, 0, r], r, sem).start(priority=1)
        return carry

    def wait(r, carry):
        _row_copy(ys_ref, buf.at[0], 0, 0, sem).wait()
        _row_copy(ys_ref, buf.at[1], 0, 0, sem).wait()
        return carry

    lax.fori_loop(0, TM, start, 0, unroll=DMA_UNROLL)
    lax.fori_loop(0, TM, wait, 0, unroll=DMA_UNROLL)
    wt = wt_ref[...]
    y = wt[:, 0:1] * buf[0] + wt[:, 1:2] * buf[1]
    out = x_ref[...] + _pick(i, gtp_ref, gts_ref) * (_rms(y) * g_ref[...])
    if split:
        prompt_ref, sample_ref = rest[:2]

        @pl.when(i < NP_TILES)
        def _():
            prompt_ref[...] = out

        @pl.when(i >= NP_TILES)
        def _():
            sample_ref[...] = out
    else:
        rest[0][...] = out


def _combine_call(x, ys, dest, wts, g, mod_p, mod_s, gate_chunk, split):
    gtp, gts = _mod_specs(gate_chunk)
    tile = pl.BlockSpec((TM, D), lambda i: (i, 0))
    idx_spec = pl.BlockSpec((1, 1, TM), lambda i: (i, 0, 0), memory_space=pltpu.SMEM)
    if split:
        out_specs = [pl.BlockSpec((TM, D), lambda i: (jnp.minimum(i, NP_TILES - 1), 0)),
                     pl.BlockSpec((TM, D), lambda i: (jnp.maximum(i - NP_TILES, 0), 0))]
        out_shape = [jax.ShapeDtypeStruct((NP_TOK, D), F32), jax.ShapeDtypeStruct((NS_TOK, D), F32)]
    else:
        out_specs = tile
        out_shape = jax.ShapeDtypeStruct((TT, D), F32)
    return pl.pallas_call(
        functools.partial(_combine_kernel, split=split),
        grid=(N_TILES,),
        in_specs=[idx_spec, idx_spec, tile, pl.BlockSpec((TM, LANES), lambda i: (i, 0)),
                  pl.BlockSpec(memory_space=pl.ANY),
                  pl.BlockSpec((1, D), lambda i: (0, 0)), gtp, gts],
        out_specs=out_specs,
        out_shape=out_shape,
        scratch_shapes=[pltpu.VMEM((2, TM, D), F32), pltpu.SemaphoreType.DMA(())],
        compiler_params=_cparams(("arbitrary",)),
        name="moe_combine",
    )(dest[0].reshape(N_TILES, 1, TM), dest[1].reshape(N_TILES, 1, TM), x, wts, ys,
      g.reshape(1, D), mod_p, mod_s)


def _route(top_idx):
    flat_e = top_idx.T.reshape(-1)
    onehot = (flat_e[:, None] == jnp.arange(N_EXPERTS, dtype=I32)[None, :]).astype(I32)
    csum = jnp.cumsum(onehot, axis=0)
    rank = jnp.sum((csum - onehot) * onehot, axis=1)
    counts = csum[-1]
    padded = ((counts + TME - 1) // TME) * TME
    ends = jnp.cumsum(padded)
    starts = ends - padded
    dest = (jnp.sum(onehot * starts[None, :], axis=1) + rank).astype(I32)
    tile_start = jnp.arange(MOE_TILES, dtype=I32) * TME
    tile_expert = jnp.minimum(jnp.sum(tile_start[:, None] >= ends[None, :], axis=1),
                              N_EXPERTS - 1).astype(I32)
    n_used = (ends[-1] // TME).astype(I32).reshape(1)
    tail_rows = jnp.where(padded > 0, ends - TME, -1).astype(I32)
    return dest.reshape(2, TT), tile_expert, n_used, tail_rows


def _gdn_layer(x, g_pre, w_in, conv_w, a_log, dt_bias, norm_w, w_out, state_conv, state_rec_all,
               layer_pair, g_post, mod_p, mod_s):
    w_main = w_in[:, :GDN_MAIN_DIM]
    w_ba = jnp.pad(w_in[:, GDN_MAIN_DIM:], ((0, 0), (0, LANES - 2 * GDN_V_HEADS)))
    qkvz = _proj_call(x, g_pre, mod_p, mod_s, w_main, jnp.zeros((GDN_MAIN_DIM,), F32), 1024)
    ba = _proj_call(x, g_pre, mod_p, mod_s, w_ba, jnp.zeros((LANES,), F32), LANES)

    tb = GDN_CHUNK * GDN_NB
    n_steps = SEQ // tb
    o_p, rec_p = _gdn_call(
        qkvz.reshape(TT // tb, tb, GDN_MAIN_DIM), ba.reshape(TT // tb, tb, LANES),
        conv_w, a_log, dt_bias, norm_w,
        jnp.zeros((BATCH, SUBLANES, GDN_CONV_DIM), F32),
        jnp.zeros((BATCH, GDN_V_HEADS, GDN_HD, GDN_HD), F32),
        n_seq=BATCH, n_steps=n_steps, chunk=GDN_CHUNK, n_chunks=GDN_NB, t_real=tb,
        n_par=GDN_PROMPT_PAR)
    o_p = o_p.reshape(NP_TOK, GDN_VAL_DIM)

    row_pad = ((0, 0), (0, GDN_SAMPLE_ROWS - DEC_SEQ), (0, 0))
    qkvz_s = qkvz[NP_TOK:].reshape(DEC_BATCH, DEC_SEQ, GDN_MAIN_DIM)
    ba_s = ba[NP_TOK:].reshape(DEC_BATCH, DEC_SEQ, LANES)
    cbuf_s = jnp.pad(state_conv, ((0, 0), (SUBLANES - (CONV_WIDTH - 1), 0), (0, 0)))
    o_s, rec_s = _gdn_call(
        jnp.pad(qkvz_s, row_pad), jnp.pad(ba_s, row_pad), conv_w, a_log, dt_bias, norm_w,
        cbuf_s, state_rec_all,
        n_seq=DEC_BATCH, n_steps=1, chunk=GDN_SAMPLE_ROWS, n_chunks=1, t_real=DEC_SEQ,
        n_par=GDN_SAMPLE_PAR, s0_first_seq=layer_pair * DEC_BATCH)
    o_s = o_s[:, 0, :DEC_SEQ].reshape(NS_TOK, GDN_VAL_DIM)

    keep = CONV_WIDTH - 1
    pre_p = jnp.stack([qkvz[(b + 1) * SEQ - keep:(b + 1) * SEQ, :GDN_CONV_DIM] for b in range(BATCH)])
    pre_s = jnp.concatenate([state_conv, qkvz_s[:, :, :GDN_CONV_DIM]], axis=1)[:, -keep:]
    x = _post_mm_call(x, o_p, o_s, w_out, jnp.zeros((D,), F32), g_post, mod_p, mod_s, 2)
    return x, pre_p, rec_p, pre_s, rec_s


def _swa_layer(x, g_pre, w_in, b_in, sinks, w_out, b_out, rel_bias, cache_k, cache_v,
               g_post, mod_p, mod_s):
    qkv = _proj_call(x, g_pre, mod_p, mod_s, w_in, b_in, SWA_PROJ_DIM)
    o_p = _swa_prompt_call(qkv, rel_bias, sinks)
    last = jnp.stack([qkv[(b + 1) * SEQ - WINDOW:(b + 1) * SEQ, SWA_Q_DIM:] for b in range(BATCH)])
    new_k_p = last[:, :, :SWA_KV_DIM]
    new_v_p = last[:, :, SWA_KV_DIM:]

    qkv_s = qkv[NP_TOK:].reshape(DEC_BATCH, DEC_SEQ, SWA_PROJ_DIM)
    q_st = qkv_s[:, :, :SWA_Q_DIM].reshape(DEC_BATCH, DEC_SEQ, SWA_KV_HEADS, SWA_GROUP, SWA_HD)
    q_st = q_st.transpose(0, 2, 3, 1, 4).reshape(DEC_BATCH, SWA_KV_HEADS, SWA_GROUP * DEC_SEQ, SWA_HD)
    wb = cache_k.shape[1]
    zpad = jnp.zeros((DEC_BATCH, SWA_KPAD - WINDOW - DEC_SEQ + (WINDOW - wb), SWA_KV_DIM), F32)
    kc = jnp.concatenate([cache_k.reshape(DEC_BATCH, wb, SWA_KV_DIM),
                          qkv_s[:, :, SWA_Q_DIM:SWA_Q_DIM + SWA_KV_DIM], zpad], axis=1)
    vc = jnp.concatenate([cache_v.reshape(DEC_BATCH, wb, SWA_KV_DIM),
                          qkv_s[:, :, SWA_Q_DIM + SWA_KV_DIM:], zpad], axis=1)
    o_st = _swa_sample_call(q_st, kc, vc, rel_bias, sinks)
    o_s = o_st.reshape(DEC_BATCH, SWA_KV_HEADS, SWA_GROUP, DEC_SEQ, SWA_HD)
    o_s = o_s.transpose(0, 3, 1, 2, 4).reshape(NS_TOK, SWA_Q_DIM)
    new_k_s = kc[:, DEC_SEQ:DEC_SEQ + wb]
    new_v_s = vc[:, DEC_SEQ:DEC_SEQ + wb]

    x = _post_mm_call(x, o_p, o_s, w_out, b_out, g_post, mod_p, mod_s, 2)
    shape_p = (BATCH, WINDOW, SWA_KV_HEADS, SWA_HD)
    shape_s = (DEC_BATCH, wb, SWA_KV_HEADS, SWA_HD)
    return (x, new_k_p.reshape(shape_p), new_v_p.reshape(shape_p),
            new_k_s.reshape(shape_s), new_v_s.reshape(shape_s))


def kernel(x_prompt, x_sample, c_prompt, c_sample, state_conv, state_rec, cache_win_k, cache_win_v, w_mod, b_mod, g_pre_mix, g_post_mix, g_pre_ffn, g_post_ffn, gdn_w_in, gdn_conv_w, gdn_a_log, gdn_dt_bias, gdn_norm_w, gdn_w_out, swa_w_in, swa_b_in, swa_sinks, swa_w_out, swa_b_out, rel_bias, ffn_w_gate, ffn_w_up, ffn_w_down, moe_w_router, moe_b_router, moe_w_gate, moe_w_up, moe_w_down):
    x = jnp.concatenate([x_prompt.reshape(NP_TOK, D), x_sample.reshape(NS_TOK, D)], axis=0)
    n_c = BATCH + DEC_BATCH
    c_all = jnp.concatenate([c_prompt, c_sample, jnp.zeros((-n_c % SUBLANES, D), F32)], axis=0)
    m_all = _mod_call(c_all, w_mod, b_mod)
    ffn_w = (ffn_w_gate, ffn_w_up, ffn_w_down)
    moe_w = (moe_w_gate.reshape(-1, D, D_FF), moe_w_up.reshape(-1, D, D_FF),
             moe_w_down.reshape(-1, D_FF, D))
    state_rec_all = state_rec.reshape((-1,) + state_rec.shape[2:])

    conv_p, rec_p, conv_s, rec_s = [], [], [], []
    k_p, v_p, k_s, v_s = [], [], [], []
    for layer in range(DEPTH):
        j = layer // 2
        mod_p = m_all[layer, :BATCH].reshape(BATCH, 1, 6 * D)
        mod_s = jnp.repeat(m_all[layer, BATCH:n_c], DEC_SEQ, axis=0)
        if layer % 2 == 0:
            x, cp, rp, cs, rs = _gdn_layer(
                x, g_pre_mix[layer], gdn_w_in[j], gdn_conv_w[j], gdn_a_log[j], gdn_dt_bias[j],
                gdn_norm_w[j], gdn_w_out[j], state_conv[j], state_rec_all, j, g_post_mix[layer],
                mod_p, mod_s)
            conv_p.append(cp); rec_p.append(rp); conv_s.append(cs); rec_s.append(rs)
        else:
            x, kp, vp, ks, vs = _swa_layer(
                x, g_pre_mix[layer], swa_w_in[j], swa_b_in[j], swa_sinks[j], swa_w_out[j],
                swa_b_out[j], rel_bias, cache_win_k[j], cache_win_v[j], g_post_mix[layer],
                mod_p, mod_s)
            k_p.append(kp); v_p.append(vp); k_s.append(ks); v_s.append(vs)

        if layer % 2 == 0:
            x = _dense_ffn_call(x, g_pre_ffn[layer], g_post_ffn[layer], mod_p, mod_s, *ffn_w, j)
        else:
            h, idx, wts = _prenorm_router_call(x, g_pre_ffn[layer], mod_p, mod_s, 3, 4,
                                               moe_w_router[j], moe_b_router[j])
            dest, tile_expert, n_used, tail_rows = _route(idx[:, :2])
            xs = _dispatch_call(h, dest, tail_rows, n_used)
            ys = _ffn_call(xs, tile_expert + j * N_EXPERTS, n_used, *moe_w)
            x = _combine_call(x, ys, dest, wts, g_post_ffn[layer], mod_p, mod_s, 5,
                              split=layer == DEPTH - 1)

    x_p, x_s = x if isinstance(x, (list, tuple)) else (x[:NP_TOK], x[NP_TOK:])
    y_prompt = x_p.reshape(BATCH, SEQ, D)
    y_sample = x_s.reshape(DEC_BATCH, DEC_SEQ, D)
    return (y_prompt, y_sample, jnp.stack(conv_p), jnp.stack(rec_p), jnp.stack(k_p), jnp.stack(v_p),
            jnp.stack(conv_s), jnp.stack(rec_s), jnp.stack(k_s), jnp.stack(v_s))
```

```python
import functools
import math

import numpy as np
import jax
import jax.numpy as jnp
from jax import lax
from jax.experimental import pallas as pl
from jax.experimental.pallas import tpu as pltpu

F32 = jnp.float32
BF16 = jnp.bfloat16
I32 = jnp.int32
HIGHEST = lax.Precision.HIGHEST

D = 1024
BATCH = 4
SEQ = 4096
DEPTH = 4
DEC_BATCH = 128
DEC_SEQ = 4
PAST_LEN = 8192
GDN_QK_HEADS = 4
GDN_V_HEADS = 8
GDN_HD = 128
GDN_KEY_DIM = GDN_QK_HEADS * GDN_HD
GDN_VAL_DIM = GDN_V_HEADS * GDN_HD
GDN_CONV_DIM = 2 * GDN_KEY_DIM + GDN_VAL_DIM
GDN_MAIN_DIM = GDN_CONV_DIM + GDN_VAL_DIM
CONV_WIDTH = 4
GDN_CHUNK = 64
SWA_Q_HEADS = 16
SWA_KV_HEADS = 4
SWA_GROUP = SWA_Q_HEADS // SWA_KV_HEADS
SWA_HD = 64
SWA_Q_DIM = SWA_Q_HEADS * SWA_HD
SWA_KV_DIM = SWA_KV_HEADS * SWA_HD
SWA_PROJ_DIM = SWA_Q_DIM + 2 * SWA_KV_DIM
WINDOW = 128
REL_BUCKETS = 32
REL_MAX_DIST = 128
D_FF = 2816
N_EXPERTS = 8
RMS_EPS = 1e-6

LANES = 128
SUBLANES = 8
VMEM_LIMIT = 56 * 1024 * 1024

TM = 512
NP_TOK = BATCH * SEQ
NS_TOK = DEC_BATCH * DEC_SEQ
TT = NP_TOK + NS_TOK
NP_TILES = NP_TOK // TM
NS_TILES = NS_TOK // TM
N_TILES = NP_TILES + NS_TILES
TILES_PER_SEQ = SEQ // TM
TF = 256
NF = D_FF // TF
TME = 1024
MOE_ROWS = 2 * TT + N_EXPERTS * TME
MOE_TILES = MOE_ROWS // TME
GDN_NB = 2
GDN_PROMPT_PAR = 4
GDN_SAMPLE_PAR = 4
GDN_SAMPLE_ROWS = 8
SWA_SB = 8
SWA_KPAD = WINDOW + 8

assert NP_TOK % TM == 0 and NS_TOK % TM == 0 and SEQ % TM == 0
assert D_FF % TF == 0 and DEC_SEQ <= GDN_SAMPLE_ROWS and DEC_SEQ <= 8


def _cparams(sem):
    return pltpu.CompilerParams(dimension_semantics=sem, vmem_limit_bytes=VMEM_LIMIT)


def _bdot(a, b):
    return jnp.dot(a.astype(BF16), b.astype(BF16), preferred_element_type=F32)


def _bdot_nt(a, b):
    return lax.dot_general(a.astype(BF16), b.astype(BF16), (((1,), (1,)), ((), ())),
                           preferred_element_type=F32)


def _silu(x):
    return x * jax.nn.sigmoid(x)


def _rms(x):
    return x * lax.rsqrt(jnp.mean(x * x, axis=-1, keepdims=True) + RMS_EPS)


def _mod_kernel(c_ref, w_ref, b_ref, o_ref):
    c = c_ref[...]
    o_ref[0] = _bdot(_silu(c), w_ref[0]) + b_ref[0]


def _mod_call(c_all, w_mod, b_mod):
    n = c_all.shape[0]
    tn = D
    return pl.pallas_call(
        _mod_kernel,
        grid=(DEPTH, 6 * D // tn),
        in_specs=[
            pl.BlockSpec((n, D), lambda l, j: (0, 0)),
            pl.BlockSpec((1, D, tn), lambda l, j: (l, 0, j)),
            pl.BlockSpec((1, 1, tn), lambda l, j: (l, 0, j)),
        ],
        out_specs=pl.BlockSpec((1, n, tn), lambda l, j: (l, 0, j)),
        out_shape=jax.ShapeDtypeStruct((DEPTH, n, 6 * D), F32),
        compiler_params=_cparams(("arbitrary", "arbitrary")),
        name="modulation",
    )(c_all, w_mod, b_mod.reshape(DEPTH, 1, 6 * D))


def _mod_specs(chunk, token_axis=0):
    p = pl.BlockSpec((1, 1, D), lambda *ids: (jnp.minimum(ids[token_axis] // TILES_PER_SEQ, BATCH - 1),
                                              0, chunk))
    s = pl.BlockSpec((TM, D), lambda *ids: (jnp.maximum(ids[token_axis] - NP_TILES, 0), chunk))
    return p, s


def _pick(i, p_ref, s_ref):
    return jnp.where(i < NP_TILES, p_ref[0], s_ref[...])


def _prenorm(i, x, g_ref, shp_ref, shs_ref, scp_ref, scs_ref):
    h = _rms(x) * g_ref[...]
    return h * (1.0 + _pick(i, scp_ref, scs_ref)) + _pick(i, shp_ref, shs_ref)


def _prenorm_router_kernel(x_ref, g_ref, shp_ref, shs_ref, scp_ref, scs_ref, wr_ref, br_ref,
                           h_ref, idx_ref, wt_ref):
    h = _prenorm(pl.program_id(0), x_ref[...], g_ref, shp_ref, shs_ref, scp_ref, scs_ref)
    h_ref[...] = h
    lane = lax.broadcasted_iota(I32, (TM, LANES), 1).astype(F32)
    logits = jnp.dot(h, wr_ref[...], precision=HIGHEST, preferred_element_type=F32) + br_ref[...]
    logits = jnp.where(lane < N_EXPERTS, logits, -jnp.inf)
    m1 = jnp.max(logits, axis=-1, keepdims=True)
    i1 = jnp.min(jnp.where(logits == m1, lane, float(LANES)), axis=-1, keepdims=True)
    rest = jnp.where(lane == i1, -jnp.inf, logits)
    m2 = jnp.max(rest, axis=-1, keepdims=True)
    i2 = jnp.min(jnp.where(rest == m2, lane, float(LANES)), axis=-1, keepdims=True)
    e2 = jnp.exp(m2 - m1)
    w1 = 1.0 / (1.0 + e2)
    w2 = e2 / (1.0 + e2)
    idx_ref[...] = jnp.where(lane == 0, i1, jnp.where(lane == 1, i2, 0.0)).astype(I32)
    wt_ref[...] = jnp.where(lane == 0, w1, jnp.where(lane == 1, w2, 0.0))


def _prenorm_router_call(x, g, mod_p, mod_s, sh_chunk, sc_chunk, w_router, b_router):
    shp, shs = _mod_specs(sh_chunk)
    scp, scs = _mod_specs(sc_chunk)
    tile = pl.BlockSpec((TM, D), lambda i: (i, 0))
    small = pl.BlockSpec((TM, LANES), lambda i: (i, 0))
    wr = jnp.pad(w_router, ((0, 0), (0, LANES - N_EXPERTS)))
    br = jnp.pad(b_router, (0, LANES - N_EXPERTS)).reshape(1, LANES)
    return pl.pallas_call(
        _prenorm_router_kernel,
        grid=(N_TILES,),
        in_specs=[tile, pl.BlockSpec((1, D), lambda i: (0, 0)), shp, shs, scp, scs,
                  pl.BlockSpec((D, LANES), lambda i: (0, 0)),
                  pl.BlockSpec((1, LANES), lambda i: (0, 0))],
        out_specs=[tile, small, small],
        out_shape=[jax.ShapeDtypeStruct((TT, D), F32),
                   jax.ShapeDtypeStruct((TT, LANES), I32),
                   jax.ShapeDtypeStruct((TT, LANES), F32)],
        compiler_params=_cparams(("arbitrary",)),
        name="prenorm_router",
    )(x, g.reshape(1, D), mod_p, mod_s, mod_p, mod_s, wr, br)


def _proj_kernel(x_ref, g_ref, shp_ref, shs_ref, scp_ref, scs_ref, w_ref, b_ref, o_ref, wbf_ref):
    i = pl.program_id(1)

    @pl.when(i == 0)
    def _():
        wbf_ref[...] = w_ref[...].astype(BF16)

    h = _prenorm(i, x_ref[...], g_ref, shp_ref, shs_ref, scp_ref, scs_ref).astype(BF16)
    o_ref[...] = jnp.dot(h, wbf_ref[...], preferred_element_type=F32) + b_ref[...]


def _proj_call(x, g, mod_p, mod_s, w, b, tn):
    k, n = w.shape
    shp, shs = _mod_specs(0, token_axis=1)
    scp, scs = _mod_specs(1, token_axis=1)
    return pl.pallas_call(
        _proj_kernel,
        grid=(n // tn, N_TILES),
        in_specs=[
            pl.BlockSpec((TM, k), lambda j, i: (i, 0)),
            pl.BlockSpec((1, k), lambda j, i: (0, 0)), shp, shs, scp, scs,
            pl.BlockSpec((k, tn), lambda j, i: (0, j)),
            pl.BlockSpec((1, tn), lambda j, i: (0, j)),
        ],
        out_specs=pl.BlockSpec((TM, tn), lambda j, i: (i, j)),
        out_shape=jax.ShapeDtypeStruct((TT, n), F32),
        scratch_shapes=[pltpu.VMEM((k, tn), BF16)],
        compiler_params=_cparams(("arbitrary", "arbitrary")),
        name="proj",
    )(x, g.reshape(1, k), mod_p, mod_s, mod_p, mod_s, w, b.reshape(1, n))


def _post_mm_kernel(x_ref, op_ref, os_ref, w_ref, b_ref, g_ref, gtp_ref, gts_ref, out_ref, wbf_ref):
    i = pl.program_id(0)

    @pl.when(i == 0)
    def _():
        wbf_ref[...] = w_ref[...].astype(BF16)

    o = jnp.where(i < NP_TILES, op_ref[...], os_ref[...]).astype(BF16)
    y = jnp.dot(o, wbf_ref[...], preferred_element_type=F32) + b_ref[...]
    out_ref[...] = x_ref[...] + _pick(i, gtp_ref, gts_ref) * (_rms(y) * g_ref[...])


def _post_mm_call(x, o_p, o_s, w, b, g, mod_p, mod_s, gate_chunk):
    gtp, gts = _mod_specs(gate_chunk)
    tile = pl.BlockSpec((TM, D), lambda i: (i, 0))
    row = pl.BlockSpec((1, D), lambda i: (0, 0))
    return pl.pallas_call(
        _post_mm_kernel,
        grid=(N_TILES,),
        in_specs=[tile,
                  pl.BlockSpec((TM, D), lambda i: (jnp.minimum(i, NP_TILES - 1), 0)),
                  pl.BlockSpec((TM, D), lambda i: (jnp.maximum(i - NP_TILES, 0), 0)),
                  pl.BlockSpec((D, D), lambda i: (0, 0)), row, row, gtp, gts],
        out_specs=tile,
        out_shape=jax.ShapeDtypeStruct((TT, D), F32),
        scratch_shapes=[pltpu.VMEM((D, D), BF16)],
        compiler_params=_cparams(("arbitrary",)),
        name="post_mm",
    )(x, o_p, o_s, w, b.reshape(1, D), g.reshape(1, D), mod_p, mod_s)


def _gdn_kernel(*refs, chunk, n_chunks, t_real, n_par):
    x_refs = refs[:n_par]
    ba_refs = refs[n_par:2 * n_par]
    (cw_ref, alog_ref, dt_ref, nw_ref, cb_ref, s0_ref, o_ref, sfin_ref, xbuf, s_scr) = refs[2 * n_par:]
    c_sz = chunk
    tb = chunk * n_chunks
    j = pl.program_id(1)
    rep = GDN_V_HEADS // GDN_QK_HEADS
    units = [(p, h) for p in range(n_par) for h in range(GDN_V_HEADS)]

    @pl.when(j == 0)
    def _():
        s_scr[...] = s0_ref[...]
        xbuf[:, 0:SUBLANES, :] = cb_ref[...]

    for p in range(n_par):
        xbuf[p, SUBLANES:SUBLANES + tb, :] = x_refs[p][:, 0:GDN_CONV_DIM]

    ii = lax.broadcasted_iota(I32, (c_sz, c_sz), 0)
    jj = lax.broadcasted_iota(I32, (c_sz, c_sz), 1)
    causal = ii >= jj
    strict = ii > jj
    tri = causal.astype(F32)
    eye_c = (ii == jj).astype(F32)
    pair_mask = [jnp.logical_and((ii >> (l + 1)) == (jj >> (l + 1)), (ii >> l) != (jj >> l))
                 for l in range(int(math.log2(c_sz)))]
    eye_t = (lax.broadcasted_iota(I32, (2 * SUBLANES, LANES), 0)
             == lax.broadcasted_iota(I32, (2 * SUBLANES, LANES), 1)).astype(F32)
    lane = lax.broadcasted_iota(I32, (c_sz, LANES), 1)
    row = lax.broadcasted_iota(I32, (c_sz, LANES), 0)
    cw = cw_ref[...]
    neg_a = -jnp.exp(alog_ref[...])
    dt = dt_ref[...]
    nw = nw_ref[...]

    def chunk_body(c, carry):
        r0 = pl.multiple_of(c * c_sz, c_sz)
        xc, beta, gc, rows_t = [], [], [], []
        for p in range(n_par):
            win = xbuf[p, pl.ds(r0, c_sz + SUBLANES), :]
            y = win[SUBLANES:SUBLANES + c_sz] * cw[3:4]
            for tap in range(1, CONV_WIDTH):
                y = y + win[SUBLANES - tap:SUBLANES - tap + c_sz] * cw[3 - tap:4 - tap]
            xc.append(_silu(y))
            ba = ba_refs[p][pl.ds(r0, c_sz), :]
            b_p = jax.nn.sigmoid(ba)
            g_p = neg_a * jax.nn.softplus(ba + dt)
            if t_real < tb:
                live = (row + r0) < t_real
                b_p = jnp.where(live, b_p, 0.0)
                g_p = jnp.where(live, g_p, 0.0)
            gc_p = jnp.dot(tri, g_p, precision=HIGHEST, preferred_element_type=F32)
            cols = jnp.where(lane < GDN_V_HEADS, b_p, gc_p)
            rows_t.append(lax.dot_general(eye_t, cols, (((1,), (1,)), ((), ())), precision=HIGHEST,
                                          preferred_element_type=F32))
            beta.append(b_p)
            gc.append(gc_p)

        qn, kn, kk, qk = {}, {}, {}, {}
        for p in range(n_par):
            for hq in range(GDN_QK_HEADS):
                q = xc[p][:, hq * GDN_HD:(hq + 1) * GDN_HD]
                k = xc[p][:, GDN_KEY_DIM + hq * GDN_HD:GDN_KEY_DIM + (hq + 1) * GDN_HD]
                q = q * lax.rsqrt(jnp.sum(q * q, axis=-1, keepdims=True) + RMS_EPS) * (GDN_HD ** -0.5)
                k = k * lax.rsqrt(jnp.sum(k * k, axis=-1, keepdims=True) + RMS_EPS)
                qn[p, hq], kn[p, hq] = q, k
        for key in qn:
            kk[key] = _bdot_nt(kn[key], kn[key])
            qk[key] = _bdot_nt(qn[key], kn[key])

        gcc, gcl, bcol, decay, a_mat, inv = {}, {}, {}, {}, {}, {}
        for (p, h) in units:
            gcc[p, h] = gc[p][:, GDN_V_HEADS + h:GDN_V_HEADS + h + 1]
            gcr = rows_t[p][GDN_V_HEADS + h:GDN_V_HEADS + h + 1, :]
            bcol[p, h] = beta[p][:, h:h + 1]
            gcl[p, h] = gcc[p, h][c_sz - 1:c_sz, :]
            decay[p, h] = jnp.exp(jnp.where(causal, gcc[p, h] - gcr, -jnp.inf))
            a_mat[p, h] = jnp.where(strict, bcol[p, h] * kk[p, h // rep] * decay[p, h], 0.0)
            inv[p, h] = eye_c - jnp.where(pair_mask[0], a_mat[p, h], 0.0)
        for lvl in range(1, len(pair_mask)):
            t1 = {u: _bdot(jnp.where(pair_mask[lvl], a_mat[u], 0.0), inv[u]) for u in units}
            t2 = {u: _bdot(inv[u], t1[u]) for u in units}
            inv = {u: inv[u] - t2[u] for u in units}

        egc = {u: jnp.exp(gcc[u]) for u in units}
        sol = {}
        for (p, h) in units:
            v = xc[p][:, 2 * GDN_KEY_DIM + h * GDN_HD:2 * GDN_KEY_DIM + (h + 1) * GDN_HD]
            rhs = jnp.concatenate([v * bcol[p, h], kn[p, h // rep] * (bcol[p, h] * egc[p, h])], axis=1)
            sol[p, h] = _bdot(inv[p, h], rhs)
        ws = {}
        for (p, h) in units:
            q_dec = qn[p, h // rep] * egc[p, h]
            ws[p, h] = _bdot(jnp.concatenate([sol[p, h][:, GDN_HD:], q_dec], axis=0), s_scr[p, h])
        u_new = {u: sol[u][:, :GDN_HD] - ws[u][:c_sz] for u in units}
        o_part, s_part = {}, {}
        for (p, h) in units:
            qkm = jnp.where(causal, qk[p, h // rep] * decay[p, h], 0.0)
            k_dec = kn[p, h // rep] * jnp.exp(gcl[p, h] - gcc[p, h])
            o_part[p, h] = _bdot(qkm, u_new[p, h])
            s_part[p, h] = lax.dot_general(k_dec.astype(BF16), u_new[p, h].astype(BF16),
                                           (((0,), (0,)), ((), ())), preferred_element_type=F32)
        for (p, h) in units:
            s_scr[p, h] = s_scr[p, h] * jnp.exp(gcl[p, h]) + s_part[p, h]
            o = ws[p, h][c_sz:] + o_part[p, h]
            z = x_refs[p][pl.ds(r0, c_sz), GDN_CONV_DIM + h * GDN_HD:GDN_CONV_DIM + (h + 1) * GDN_HD]
            o_ref[p, pl.ds(r0, c_sz), h * GDN_HD:(h + 1) * GDN_HD] = _rms(o) * nw * _silu(z)
        return carry

    lax.fori_loop(0, n_chunks, chunk_body, 0)
    xbuf[:, 0:SUBLANES, :] = xbuf[:, tb:tb + SUBLANES, :]

    @pl.when(j == pl.num_programs(1) - 1)
    def _():
        sfin_ref[...] = s_scr[...]


def _gdn_call(x, ba, conv_w, a_log, dt_bias, norm_w, conv_buf, s0, *, n_seq, n_steps, chunk,
              n_chunks, t_real, n_par, s0_first_seq=0):
    s0_block = s0_first_seq // n_par
    tb = chunk * n_chunks
    pad = LANES - 2 * GDN_V_HEADS
    alog_row = jnp.pad(a_log, (GDN_V_HEADS, pad)).reshape(1, LANES)
    dt_row = jnp.pad(dt_bias, (GDN_V_HEADS, pad)).reshape(1, LANES)

    def seq_block(width, p):
        return pl.BlockSpec((None, tb, width), lambda s, j: ((s * n_par + p) * n_steps + j, 0, 0))

    const = lambda s, j: (0, 0)
    state_spec = pl.BlockSpec((n_par, GDN_V_HEADS, GDN_HD, GDN_HD), lambda s, j: (s, 0, 0, 0))
    return pl.pallas_call(
        functools.partial(_gdn_kernel, chunk=chunk, n_chunks=n_chunks, t_real=t_real, n_par=n_par),
        grid=(n_seq // n_par, n_steps),
        in_specs=[seq_block(GDN_MAIN_DIM, p) for p in range(n_par)]
                 + [seq_block(LANES, p) for p in range(n_par)]
                 + [pl.BlockSpec((CONV_WIDTH, GDN_CONV_DIM), const),
                    pl.BlockSpec((1, LANES), const), pl.BlockSpec((1, LANES), const),
                    pl.BlockSpec((1, GDN_HD), const),
                    pl.BlockSpec((n_par, SUBLANES, GDN_CONV_DIM), lambda s, j: (s, 0, 0)),
                    pl.BlockSpec((n_par, GDN_V_HEADS, GDN_HD, GDN_HD),
                                 lambda s, j: (s + s0_block, 0, 0, 0))],
        out_specs=[pl.BlockSpec((n_par, None, tb, GDN_VAL_DIM), lambda s, j: (s, j, 0, 0)),
                   state_spec],
        out_shape=[jax.ShapeDtypeStruct((n_seq, n_steps, tb, GDN_VAL_DIM), F32),
                   jax.ShapeDtypeStruct((n_seq, GDN_V_HEADS, GDN_HD, GDN_HD), F32)],
        scratch_shapes=[pltpu.VMEM((n_par, tb + SUBLANES, GDN_CONV_DIM), F32),
                        pltpu.VMEM((n_par, GDN_V_HEADS, GDN_HD, GDN_HD), F32)],
        compiler_params=_cparams(("arbitrary", "arbitrary")),
        name="gdn",
    )(*([x] * n_par), *([ba] * n_par), conv_w, alog_row, dt_row, norm_w.reshape(1, GDN_HD),
      conv_buf, s0)


def _t5_bucket_np(dist):
    dist = np.maximum(dist, 0)
    max_exact = REL_BUCKETS // 2
    ratio = np.log(np.maximum(dist, max_exact).astype(np.float32) / np.float32(max_exact)) \
        / np.float32(math.log(REL_MAX_DIST / max_exact))
    large = max_exact + (ratio.astype(np.float32) * np.float32(REL_BUCKETS - max_exact)).astype(np.int32)
    return np.where(dist < max_exact, dist, np.minimum(large, REL_BUCKETS - 1)).astype(np.int32)


def _bucket_table(qpos, kpos, k_valid):
    dist = qpos[:, None] - kpos[None, :]
    ok = (dist >= 0) & (dist < WINDOW) & (kpos[None, :] >= 0) & k_valid[None, :]
    return np.where(ok, _t5_bucket_np(dist), -1).astype(np.int32)


def _bias_from_buckets(bkt, rb_ref, head):
    def body(b, acc):
        return jnp.where(bkt == b, rb_ref[b, head], acc)
    acc = lax.fori_loop(0, REL_BUCKETS, body, jnp.zeros(bkt.shape, F32))
    return jnp.where(bkt < 0, -jnp.inf, acc)


def _swa_prompt_kernel(rb_ref, sk_ref, bkt_ref, q_ref, kvp_ref, kvc_ref, o_ref, bias_scr):
    first = jnp.logical_and(pl.program_id(0) == 0, pl.program_id(1) == 0)
    n = pl.program_id(1)

    @pl.when(first)
    def _():
        bkt = bkt_ref[...]
        for h in range(SWA_Q_HEADS):
            kv, g = divmod(h, SWA_GROUP)
            bias_scr[kv, g * WINDOW:(g + 1) * WINDOW, :] = _bias_from_buckets(bkt, rb_ref, h)

    q = q_ref[...]
    scale = SWA_HD ** -0.5
    heads = range(SWA_KV_HEADS)
    s_prev, s_cur = [], []
    for kv in heads:
        qs = jnp.concatenate(
            [q[:, (kv * SWA_GROUP + g) * SWA_HD:(kv * SWA_GROUP + g + 1) * SWA_HD]
             for g in range(SWA_GROUP)], axis=0).astype(BF16)
        s_prev.append(_bdot_nt(qs, kvp_ref[:, kv * SWA_HD:(kv + 1) * SWA_HD]))
        s_cur.append(_bdot_nt(qs, kvc_ref[:, kv * SWA_HD:(kv + 1) * SWA_HD]))
    p_prev, p_cur, den = [], [], []
    for kv in heads:
        sp = s_prev[kv] * scale + bias_scr[kv, :, 0:WINDOW]
        sp = jnp.where(n > 0, sp, -jnp.inf)
        sc = s_cur[kv] * scale + bias_scr[kv, :, WINDOW:2 * WINDOW]
        sink = jnp.concatenate(
            [jnp.full((WINDOW, 1), sk_ref[kv * SWA_GROUP + g], F32) for g in range(SWA_GROUP)], axis=0)
        m = jnp.maximum(jnp.maximum(jnp.max(sp, axis=-1, keepdims=True),
                                    jnp.max(sc, axis=-1, keepdims=True)), sink)
        pp = jnp.exp(sp - m)
        pc = jnp.exp(sc - m)
        den.append(jnp.sum(pp, axis=-1, keepdims=True) + jnp.sum(pc, axis=-1, keepdims=True)
                   + jnp.exp(sink - m))
        p_prev.append(pp.astype(BF16))
        p_cur.append(pc.astype(BF16))
    outs = []
    for kv in heads:
        v_prev = kvp_ref[:, SWA_KV_DIM + kv * SWA_HD:SWA_KV_DIM + (kv + 1) * SWA_HD]
        v_cur = kvc_ref[:, SWA_KV_DIM + kv * SWA_HD:SWA_KV_DIM + (kv + 1) * SWA_HD]
        o = (_bdot(p_prev[kv], v_prev) + _bdot(p_cur[kv], v_cur)) / den[kv]
        outs.extend(o[g * WINDOW:(g + 1) * WINDOW] for g in range(SWA_GROUP))
    o_ref[...] = jnp.concatenate(outs, axis=1)


def _swa_prompt_call(qkv, rel_bias, sinks):
    nb = SEQ // WINDOW
    qpos = WINDOW + np.arange(WINDOW)
    kpos = np.arange(2 * WINDOW)
    bkt = jnp.asarray(_bucket_table(qpos, kpos, np.ones(2 * WINDOW, bool)))
    kv_col = SWA_Q_DIM // (2 * SWA_KV_DIM)
    smem = pl.BlockSpec(memory_space=pltpu.SMEM)
    return pl.pallas_call(
        _swa_prompt_kernel,
        grid=(BATCH, nb),
        in_specs=[smem, smem,
                  pl.BlockSpec((WINDOW, 2 * WINDOW), lambda b, n: (0, 0)),
                  pl.BlockSpec((WINDOW, SWA_Q_DIM), lambda b, n: (b * nb + n, 0)),
                  pl.BlockSpec((WINDOW, 2 * SWA_KV_DIM),
                               lambda b, n: (jnp.maximum(b * nb + n - 1, 0), kv_col)),
                  pl.BlockSpec((WINDOW, 2 * SWA_KV_DIM), lambda b, n: (b * nb + n, kv_col))],
        out_specs=pl.BlockSpec((WINDOW, SWA_Q_DIM), lambda b, n: (b * nb + n, 0)),
        out_shape=jax.ShapeDtypeStruct((NP_TOK, SWA_Q_DIM), F32),
        scratch_shapes=[pltpu.VMEM((SWA_KV_HEADS, SWA_GROUP * WINDOW, 2 * WINDOW), F32)],
        compiler_params=_cparams(("arbitrary", "arbitrary")),
        name="swa_prompt",
    )(rel_bias, sinks, bkt, qkv, qkv, qkv)


def _swa_sample_kernel(rb_ref, sk_ref, bkt_ref, q_ref, ck_ref, cv_ref, kn_ref, vn_ref,
                       o_ref, ck_out_ref, cv_out_ref, k_scr, v_scr, bias_scr, sink_scr):
    rows = SWA_GROUP * DEC_SEQ
    k_scr[:, 0:WINDOW, :] = ck_ref[...]
    k_scr[:, WINDOW:SWA_KPAD, :] = kn_ref[...]
    v_scr[:, 0:WINDOW, :] = cv_ref[...]
    v_scr[:, WINDOW:SWA_KPAD, :] = vn_ref[...]
    ck_out_ref[...] = k_scr[:, DEC_SEQ:DEC_SEQ + WINDOW, :]
    cv_out_ref[...] = v_scr[:, DEC_SEQ:DEC_SEQ + WINDOW, :]
    k_ref, v_ref = k_scr, v_scr

    def group_of_row(shape):
        r = lax.broadcasted_iota(I32, shape, 0)
        return sum((r >= g * DEC_SEQ).astype(I32) for g in range(1, SWA_GROUP))

    @pl.when(pl.program_id(0) == 0)
    def _():
        bkt = bkt_ref[...]
        grp = group_of_row((rows, 1))
        grp_full = group_of_row(bkt.shape)
        for kv in range(SWA_KV_HEADS):
            acc = jnp.zeros(bkt.shape, F32)
            snk = jnp.zeros((rows, 1), F32)
            for g in range(SWA_GROUP):
                head = kv * SWA_GROUP + g
                acc = jnp.where(grp_full == g, _bias_from_buckets(bkt, rb_ref, head), acc)
                snk = jnp.where(grp == g, sk_ref[head], snk)
            bias_scr[kv] = acc
            sink_scr[kv] = snk

    scale = SWA_HD ** -0.5
    units = [(s, kv) for s in range(SWA_SB) for kv in range(SWA_KV_HEADS)]
    scores = {(s, kv): _bdot_nt(q_ref[s, kv], k_ref[s, :, kv * SWA_HD:(kv + 1) * SWA_HD])
              for (s, kv) in units}
    probs, den = {}, {}
    for (s, kv) in units:
        sc = scores[s, kv] * scale + bias_scr[kv]
        sink = sink_scr[kv]
        m = jnp.maximum(jnp.max(sc, axis=-1, keepdims=True), sink)
        p = jnp.exp(sc - m)
        den[s, kv] = jnp.sum(p, axis=-1, keepdims=True) + jnp.exp(sink - m)
        probs[s, kv] = p.astype(BF16)
    for (s, kv) in units:
        o_ref[s, kv] = _bdot(probs[s, kv], v_ref[s, :, kv * SWA_HD:(kv + 1) * SWA_HD]) / den[s, kv]


def _swa_sample_call(q_st, cache_k, cache_v, k_new, v_new, layer_pair, rel_bias, sinks):
    rows = SWA_GROUP * DEC_SEQ
    n_keys = WINDOW + DEC_SEQ
    qpos = PAST_LEN + np.arange(DEC_SEQ)
    kpos = PAST_LEN - WINDOW + np.arange(SWA_KPAD)
    bkt4 = _bucket_table(qpos, kpos, np.arange(SWA_KPAD) < n_keys)
    bkt = jnp.asarray(np.tile(bkt4, (SWA_GROUP, 1)))
    steps = DEC_BATCH // SWA_SB
    smem = pl.BlockSpec(memory_space=pltpu.SMEM)
    q_spec = pl.BlockSpec((SWA_SB, SWA_KV_HEADS, rows, SWA_HD), lambda i: (i, 0, 0, 0))
    win_in = pl.BlockSpec((SWA_SB, WINDOW, SWA_KV_DIM), lambda i: (layer_pair * steps + i, 0, 0))
    win_out = pl.BlockSpec((SWA_SB, WINDOW, SWA_KV_DIM), lambda i: (i, 0, 0))
    new_spec = pl.BlockSpec((SWA_SB, SWA_KPAD - WINDOW, SWA_KV_DIM), lambda i: (i, 0, 0))
    win_shape = jax.ShapeDtypeStruct((DEC_BATCH, WINDOW, SWA_KV_DIM), F32)
    return pl.pallas_call(
        _swa_sample_kernel,
        grid=(steps,),
        in_specs=[smem, smem, pl.BlockSpec((rows, SWA_KPAD), lambda i: (0, 0)),
                  q_spec, win_in, win_in, new_spec, new_spec],
        out_specs=[q_spec, win_out, win_out],
        out_shape=[jax.ShapeDtypeStruct((DEC_BATCH, SWA_KV_HEADS, rows, SWA_HD), F32),
                   win_shape, win_shape],
        scratch_shapes=[pltpu.VMEM((SWA_SB, SWA_KPAD, SWA_KV_DIM), F32),
                        pltpu.VMEM((SWA_SB, SWA_KPAD, SWA_KV_DIM), F32),
                        pltpu.VMEM((SWA_KV_HEADS, rows, SWA_KPAD), F32),
                        pltpu.VMEM((SWA_KV_HEADS, rows, 1), F32)],
        compiler_params=_cparams(("arbitrary",)),
        name="swa_sample",
    )(rel_bias, sinks, bkt, q_st, cache_k, cache_v, k_new, v_new)


def _ffn_scratch(rows):
    return [pltpu.VMEM((rows, D), F32),
            pltpu.VMEM((NF, D, TF), BF16),
            pltpu.VMEM((NF, D, TF), BF16),
            pltpu.VMEM((NF, TF, D), BF16),
            pltpu.VMEM((2, D, TF), F32),
            pltpu.VMEM((2, D, TF), F32),
            pltpu.VMEM((2, TF, D), F32),
            pltpu.SemaphoreType.DMA((3, 2))]


def _swiglu_into(acc_ref, x, expert, load, w_hbm, res, stg, sem):
    wg_hbm, wu_hbm, wd_hbm = w_hbm
    res_g, res_u, res_d = res
    stg_g, stg_u, stg_d = stg

    def copies(f, slot):
        col = pl.multiple_of(f * TF, TF)
        return (pltpu.make_async_copy(wg_hbm.at[expert, :, pl.ds(col, TF)], stg_g.at[slot], sem.at[0, slot]),
                pltpu.make_async_copy(wu_hbm.at[expert, :, pl.ds(col, TF)], stg_u.at[slot], sem.at[1, slot]),
                pltpu.make_async_copy(wd_hbm.at[expert, pl.ds(col, TF), :], stg_d.at[slot], sem.at[2, slot]))

    def block(f):
        gate = jnp.dot(x, res_g[f], preferred_element_type=F32)
        up = jnp.dot(x, res_u[f], preferred_element_type=F32)
        act = (_silu(gate) * up).astype(BF16)
        acc_ref[...] += jnp.dot(act, res_d[f], preferred_element_type=F32)

    acc_ref[...] = jnp.zeros_like(acc_ref)

    @pl.when(load)
    def _():
        for c in copies(0, 0):
            c.start()

        def body(f, carry):
            slot = lax.rem(f, 2)

            @pl.when(f + 1 < NF)
            def _():
                for c in copies(f + 1, 1 - slot):
                    c.start()

            for c in copies(f, slot):
                c.wait()
            res_g[f] = stg_g[slot].astype(BF16)
            res_u[f] = stg_u[slot].astype(BF16)
            res_d[f] = stg_d[slot].astype(BF16)
            block(f)
            return carry

        lax.fori_loop(0, NF, body, 0)

    @pl.when(jnp.logical_not(load))
    def _():
        def body(f, carry):
            block(f)
            return carry

        lax.fori_loop(0, NF, body, 0)


def _ffn_kernel(te_ref, ld_ref, nt_ref, x_ref, wg_hbm, wu_hbm, wd_hbm, o_ref, acc_ref, *scratch):
    t = pl.program_id(0)

    @pl.when(t < nt_ref[0])
    def _():
        _swiglu_into(acc_ref, x_ref[...].astype(BF16), te_ref[t], ld_ref[t] != 0,
                     (wg_hbm, wu_hbm, wd_hbm), scratch[0:3], scratch[3:6], scratch[6])
        o_ref[...] = acc_ref[...]

    @pl.when(t >= nt_ref[0])
    def _():
        o_ref[...] = jnp.zeros_like(o_ref)


def _ffn_call(x, tile_expert, n_tiles_used, w_gate, w_up, w_down):
    n_rows = x.shape[0]
    n_tiles = n_rows // TME
    changed = jnp.concatenate([jnp.ones((1,), I32),
                               (tile_expert[1:] != tile_expert[:-1]).astype(I32)])
    hbm = pl.BlockSpec(memory_space=pl.ANY)
    grid_spec = pltpu.PrefetchScalarGridSpec(
        num_scalar_prefetch=3,
        grid=(n_tiles,),
        in_specs=[pl.BlockSpec((TME, D), lambda t, te, ld, nt: (jnp.minimum(t, nt[0] - 1), 0)),
                  hbm, hbm, hbm],
        out_specs=pl.BlockSpec((TME, D), lambda t, *_: (t, 0)),
        scratch_shapes=_ffn_scratch(TME),
    )
    return pl.pallas_call(
        _ffn_kernel,
        grid_spec=grid_spec,
        out_shape=jax.ShapeDtypeStruct((n_rows, D), F32),
        compiler_params=_cparams(("arbitrary",)),
        name="ffn",
    )(tile_expert, changed, n_tiles_used, x, w_gate, w_up, w_down)


def _dense_ffn_kernel(x_ref, gpre_ref, shp_ref, shs_ref, scp_ref, scs_ref, wg_hbm, wu_hbm, wd_hbm,
                      gpost_ref, gtp_ref, gts_ref, out_ref, acc_ref, *scratch, layer_pair):
    i = pl.program_id(0)
    x = x_ref[...]
    h = _prenorm(i, x, gpre_ref, shp_ref, shs_ref, scp_ref, scs_ref).astype(BF16)
    _swiglu_into(acc_ref, h, layer_pair, i == 0, (wg_hbm, wu_hbm, wd_hbm),
                 scratch[0:3], scratch[3:6], scratch[6])
    out_ref[...] = x + _pick(i, gtp_ref, gts_ref) * (_rms(acc_ref[...]) * gpost_ref[...])


def _dense_ffn_call(x, g_pre, g_post, mod_p, mod_s, w_gate, w_up, w_down, layer_pair):
    shp, shs = _mod_specs(3)
    scp, scs = _mod_specs(4)
    gtp, gts = _mod_specs(5)
    tile = pl.BlockSpec((TM, D), lambda i: (i, 0))
    row = pl.BlockSpec((1, D), lambda i: (0, 0))
    hbm = pl.BlockSpec(memory_space=pl.ANY)
    return pl.pallas_call(
        functools.partial(_dense_ffn_kernel, layer_pair=layer_pair),
        grid=(N_TILES,),
        in_specs=[tile, row, shp, shs, scp, scs, hbm, hbm, hbm, row, gtp, gts],
        out_specs=tile,
        out_shape=jax.ShapeDtypeStruct((TT, D), F32),
        scratch_shapes=_ffn_scratch(TM),
        compiler_params=_cparams(("arbitrary",)),
        name="dense_ffn",
    )(x, g_pre.reshape(1, D), mod_p, mod_s, mod_p, mod_s, w_gate, w_up, w_down,
      g_post.reshape(1, D), mod_p, mod_s)


DMA_UNROLL = 8


def _row_copy(src_ref, dst_ref, src_row, dst_row, sem):
    return pltpu.make_async_copy(src_ref.at[pl.ds(src_row, 1)], dst_ref.at[pl.ds(dst_row, 1)], sem)


def _dispatch_kernel(tail_ref, nt_ref, d0_ref, d1_ref, h_ref, out_ref, zero_ref, sem, zsem):
    @pl.when(pl.program_id(0) == 0)
    def _():
        zero_ref[...] = jnp.zeros_like(zero_ref)

        def fill_tile(row0):
            fill = pltpu.make_async_copy(zero_ref, out_ref.at[pl.ds(pl.multiple_of(row0, TME), TME)], zsem)
            fill.start()
            fill.wait()

        for e in range(N_EXPERTS):
            @pl.when(tail_ref[e] >= 0)
            def _():
                fill_tile(tail_ref[e])

            @pl.when(nt_ref[0] + e < MOE_TILES)
            def _():
                fill_tile((nt_ref[0] + e) * TME)

    def start(r, carry):
        _row_copy(h_ref, out_ref, r, d0_ref[0, 0, r], sem).start(priority=0)
        _row_copy(h_ref, out_ref, r, d1_ref[0, 0, r], sem).start(priority=1)
        return carry

    def wait(r, carry):
        _row_copy(h_ref, out_ref, 0, 0, sem).wait()
        _row_copy(h_ref, out_ref, 0, 0, sem).wait()
        return carry

    lax.fori_loop(0, TM, start, 0, unroll=DMA_UNROLL)
    lax.fori_loop(0, TM, wait, 0, unroll=DMA_UNROLL)


def _dispatch_call(h, dest, tail_rows, n_tiles_used):
    idx_spec = pl.BlockSpec((1, 1, TM), lambda i, *_: (i, 0, 0), memory_space=pltpu.SMEM)
    grid_spec = pltpu.PrefetchScalarGridSpec(
        num_scalar_prefetch=2,
        grid=(N_TILES,),
        in_specs=[idx_spec, idx_spec, pl.BlockSpec((TM, D), lambda i, *_: (i, 0))],
        out_specs=pl.BlockSpec(memory_space=pl.ANY),
        scratch_shapes=[pltpu.VMEM((TME, D), h.dtype), pltpu.SemaphoreType.DMA(()),
                        pltpu.SemaphoreType.DMA(())],
    )
    return pl.pallas_call(
        _dispatch_kernel,
        grid_spec=grid_spec,
        out_shape=jax.ShapeDtypeStruct((MOE_ROWS, D), h.dtype),
        compiler_params=_cparams(("arbitrary",)),
        name="moe_dispatch",
    )(tail_rows, n_tiles_used, dest[0].reshape(N_TILES, 1, TM), dest[1].reshape(N_TILES, 1, TM), h)


def _combine_kernel(d0_ref, d1_ref, x_ref, wt_ref, ys_ref, g_ref, gtp_ref, gts_ref, *rest, split):
    i = pl.program_id(0)
    buf, sem = rest[-2:]

    def start(r, carry):
        _row_copy(ys_ref, buf.at[0], d0_ref[0, 0, r], r, sem).start(priority=0)
        _row_copy(ys_ref, buf.at[1], d1_ref[0, 0, r], r, sem).start(priority=1)
        return carry

    def wait(r, carry):
        _row_copy(ys_ref, buf.at[0], 0, 0, sem).wait()
        _row_copy(ys_ref, buf.at[1], 0, 0, sem).wait()
        return carry

    lax.fori_loop(0, TM, start, 0, unroll=DMA_UNROLL)
    lax.fori_loop(0, TM, wait, 0, unroll=DMA_UNROLL)
    wt = wt_ref[...]
    y = wt[:, 0:1] * buf[0] + wt[:, 1:2] * buf[1]
    out = x_ref[...] + _pick(i, gtp_ref, gts_ref) * (_rms(y) * g_ref[...])
    if split:
        prompt_ref, sample_ref = rest[:2]

        @pl.when(i < NP_TILES)
        def _():
            prompt_ref[...] = out

        @pl.when(i >= NP_TILES)
        def _():
            sample_ref[...] = out
    else:
        rest[0][...] = out


def _combine_call(x, ys, dest, wts, g, mod_p, mod_s, gate_chunk, split):
    gtp, gts = _mod_specs(gate_chunk)
    tile = pl.BlockSpec((TM, D), lambda i: (i, 0))
    idx_spec = pl.BlockSpec((1, 1, TM), lambda i: (i, 0, 0), memory_space=pltpu.SMEM)
    if split:
        out_specs = [pl.BlockSpec((TM, D), lambda i: (jnp.minimum(i, NP_TILES - 1), 0)),
                     pl.BlockSpec((TM, D), lambda i: (jnp.maximum(i - NP_TILES, 0), 0))]
        out_shape = [jax.ShapeDtypeStruct((NP_TOK, D), F32), jax.ShapeDtypeStruct((NS_TOK, D), F32)]
    else:
        out_specs = tile
        out_shape = jax.ShapeDtypeStruct((TT, D), F32)
    return pl.pallas_call(
        functools.partial(_combine_kernel, split=split),
        grid=(N_TILES,),
        in_specs=[idx_spec, idx_spec, tile, pl.BlockSpec((TM, LANES), lambda i: (i, 0)),
                  pl.BlockSpec(memory_space=pl.ANY),
                  pl.BlockSpec((1, D), lambda i: (0, 0)), gtp, gts],
        out_specs=out_specs,
        out_shape=out_shape,
        scratch_shapes=[pltpu.VMEM((2, TM, D), F32), pltpu.SemaphoreType.DMA(())],
        compiler_params=_cparams(("arbitrary",)),
        name="moe_combine",
    )(dest[0].reshape(N_TILES, 1, TM), dest[1].reshape(N_TILES, 1, TM), x, wts, ys,
      g.reshape(1, D), mod_p, mod_s)


def _route(top_idx):
    flat_e = top_idx.T.reshape(-1)
    onehot = (flat_e[:, None] == jnp.arange(N_EXPERTS, dtype=I32)[None, :]).astype(I32)
    csum = jnp.cumsum(onehot, axis=0)
    rank = jnp.sum((csum - onehot) * onehot, axis=1)
    counts = csum[-1]
    padded = ((counts + TME - 1) // TME) * TME
    ends = jnp.cumsum(padded)
    starts = ends - padded
    dest = (jnp.sum(onehot * starts[None, :], axis=1) + rank).astype(I32)
    tile_start = jnp.arange(MOE_TILES, dtype=I32) * TME
    tile_expert = jnp.minimum(jnp.sum(tile_start[:, None] >= ends[None, :], axis=1),
                              N_EXPERTS - 1).astype(I32)
    n_used = (ends[-1] // TME).astype(I32).reshape(1)
    tail_rows = jnp.where(padded > 0, ends - TME, -1).astype(I32)
    return dest.reshape(2, TT), tile_expert, n_used, tail_rows


def _gdn_layer(x, g_pre, w_in, conv_w, a_log, dt_bias, norm_w, w_out, state_conv, state_rec_all,
               layer_pair, g_post, mod_p, mod_s):
    w_main = w_in[:, :GDN_MAIN_DIM]
    w_ba = jnp.pad(w_in[:, GDN_MAIN_DIM:], ((0, 0), (0, LANES - 2 * GDN_V_HEADS)))
    qkvz = _proj_call(x, g_pre, mod_p, mod_s, w_main, jnp.zeros((GDN_MAIN_DIM,), F32), 1024)
    ba = _proj_call(x, g_pre, mod_p, mod_s, w_ba, jnp.zeros((LANES,), F32), LANES)

    tb = GDN_CHUNK * GDN_NB
    n_steps = SEQ // tb
    o_p, rec_p = _gdn_call(
        qkvz.reshape(TT // tb, tb, GDN_MAIN_DIM), ba.reshape(TT // tb, tb, LANES),
        conv_w, a_log, dt_bias, norm_w,
        jnp.zeros((BATCH, SUBLANES, GDN_CONV_DIM), F32),
        jnp.zeros((BATCH, GDN_V_HEADS, GDN_HD, GDN_HD), F32),
        n_seq=BATCH, n_steps=n_steps, chunk=GDN_CHUNK, n_chunks=GDN_NB, t_real=tb,
        n_par=GDN_PROMPT_PAR)
    o_p = o_p.reshape(NP_TOK, GDN_VAL_DIM)

    row_pad = ((0, 0), (0, GDN_SAMPLE_ROWS - DEC_SEQ), (0, 0))
    qkvz_s = qkvz[NP_TOK:].reshape(DEC_BATCH, DEC_SEQ, GDN_MAIN_DIM)
    ba_s = ba[NP_TOK:].reshape(DEC_BATCH, DEC_SEQ, LANES)
    cbuf_s = jnp.pad(state_conv, ((0, 0), (SUBLANES - (CONV_WIDTH - 1), 0), (0, 0)))
    o_s, rec_s = _gdn_call(
        jnp.pad(qkvz_s, row_pad), jnp.pad(ba_s, row_pad), conv_w, a_log, dt_bias, norm_w,
        cbuf_s, state_rec_all,
        n_seq=DEC_BATCH, n_steps=1, chunk=GDN_SAMPLE_ROWS, n_chunks=1, t_real=DEC_SEQ,
        n_par=GDN_SAMPLE_PAR, s0_first_seq=layer_pair * DEC_BATCH)
    o_s = o_s[:, 0, :DEC_SEQ].reshape(NS_TOK, GDN_VAL_DIM)

    keep = CONV_WIDTH - 1
    pre_p = jnp.stack([qkvz[(b + 1) * SEQ - keep:(b + 1) * SEQ, :GDN_CONV_DIM] for b in range(BATCH)])
    pre_s = jnp.concatenate([state_conv, qkvz_s[:, :, :GDN_CONV_DIM]], axis=1)[:, -keep:]
    x = _post_mm_call(x, o_p, o_s, w_out, jnp.zeros((D,), F32), g_post, mod_p, mod_s, 2)
    return x, pre_p, rec_p, pre_s, rec_s


def _swa_layer(x, g_pre, w_in, b_in, sinks, w_out, b_out, rel_bias, cache_k_all, cache_v_all,
               layer_pair, g_post, mod_p, mod_s):
    qkv = _proj_call(x, g_pre, mod_p, mod_s, w_in, b_in, SWA_PROJ_DIM)
    o_p = _swa_prompt_call(qkv, rel_bias, sinks)
    last = jnp.stack([qkv[(b + 1) * SEQ - WINDOW:(b + 1) * SEQ, SWA_Q_DIM:] for b in range(BATCH)])
    new_k_p = last[:, :, :SWA_KV_DIM]
    new_v_p = last[:, :, SWA_KV_DIM:]

    qkv_s = qkv[NP_TOK:].reshape(DEC_BATCH, DEC_SEQ, SWA_PROJ_DIM)
    q_st = qkv_s[:, :, :SWA_Q_DIM].reshape(DEC_BATCH, DEC_SEQ, SWA_KV_HEADS, SWA_GROUP, SWA_HD)
    q_st = q_st.transpose(0, 2, 3, 1, 4).reshape(DEC_BATCH, SWA_KV_HEADS, SWA_GROUP * DEC_SEQ, SWA_HD)
    row_pad = ((0, 0), (0, SWA_KPAD - WINDOW - DEC_SEQ), (0, 0))
    k_new = jnp.pad(qkv_s[:, :, SWA_Q_DIM:SWA_Q_DIM + SWA_KV_DIM], row_pad)
    v_new = jnp.pad(qkv_s[:, :, SWA_Q_DIM + SWA_KV_DIM:], row_pad)
    o_st, new_k_s, new_v_s = _swa_sample_call(q_st, cache_k_all, cache_v_all, k_new, v_new,
                                              layer_pair, rel_bias, sinks)
    o_s = o_st.reshape(DEC_BATCH, SWA_KV_HEADS, SWA_GROUP, DEC_SEQ, SWA_HD)
    o_s = o_s.transpose(0, 3, 1, 2, 4).reshape(NS_TOK, SWA_Q_DIM)

    x = _post_mm_call(x, o_p, o_s, w_out, b_out, g_post, mod_p, mod_s, 2)
    shape_p = (BATCH, WINDOW, SWA_KV_HEADS, SWA_HD)
    shape_s = (DEC_BATCH, WINDOW, SWA_KV_HEADS, SWA_HD)
    return (x, new_k_p.reshape(shape_p), new_v_p.reshape(shape_p),
            new_k_s.reshape(shape_s), new_v_s.reshape(shape_s))


def kernel(x_prompt, x_sample, c_prompt, c_sample, state_conv, state_rec, cache_win_k, cache_win_v, w_mod, b_mod, g_pre_mix, g_post_mix, g_pre_ffn, g_post_ffn, gdn_w_in, gdn_conv_w, gdn_a_log, gdn_dt_bias, gdn_norm_w, gdn_w_out, swa_w_in, swa_b_in, swa_sinks, swa_w_out, swa_b_out, rel_bias, ffn_w_gate, ffn_w_up, ffn_w_down, moe_w_router, moe_b_router, moe_w_gate, moe_w_up, moe_w_down):
    x = jnp.concatenate([x_prompt.reshape(NP_TOK, D), x_sample.reshape(NS_TOK, D)], axis=0)
    n_c = BATCH + DEC_BATCH
    c_all = jnp.concatenate([c_prompt, c_sample, jnp.zeros((-n_c % SUBLANES, D), F32)], axis=0)
    m_all = _mod_call(c_all, w_mod, b_mod)
    ffn_w = (ffn_w_gate, ffn_w_up, ffn_w_down)
    moe_w = (moe_w_gate.reshape(-1, D, D_FF), moe_w_up.reshape(-1, D, D_FF),
             moe_w_down.reshape(-1, D_FF, D))
    state_rec_all = state_rec.reshape((-1,) + state_rec.shape[2:])
    assert cache_win_k.shape[2] == WINDOW
    cache_k_all = cache_win_k.reshape(-1, WINDOW, SWA_KV_DIM)
    cache_v_all = cache_win_v.reshape(-1, WINDOW, SWA_KV_DIM)

    conv_p, rec_p, conv_s, rec_s = [], [], [], []
    k_p, v_p, k_s, v_s = [], [], [], []
    for layer in range(DEPTH):
        j = layer // 2
        mod_p = m_all[layer, :BATCH].reshape(BATCH, 1, 6 * D)
        mod_s = jnp.repeat(m_all[layer, BATCH:n_c], DEC_SEQ, axis=0)
        if layer % 2 == 0:
            x, cp, rp, cs, rs = _gdn_layer(
                x, g_pre_mix[layer], gdn_w_in[j], gdn_conv_w[j], gdn_a_log[j], gdn_dt_bias[j],
                gdn_norm_w[j], gdn_w_out[j], state_conv[j], state_rec_all, j, g_post_mix[layer],
                mod_p, mod_s)
            conv_p.append(cp); rec_p.append(rp); conv_s.append(cs); rec_s.append(rs)
        else:
            x, kp, vp, ks, vs = _swa_layer(
                x, g_pre_mix[layer], swa_w_in[j], swa_b_in[j], swa_sinks[j], swa_w_out[j],
                swa_b_out[j], rel_bias, cache_k_all, cache_v_all, j, g_post_mix[layer],
                mod_p, mod_s)
            k_p.append(kp); v_p.append(vp); k_s.append(ks); v_s.append(vs)

        if layer % 2 == 0:
            x = _dense_ffn_call(x, g_pre_ffn[layer], g_post_ffn[layer], mod_p, mod_s, *ffn_w, j)
        else:
            h, idx, wts = _prenorm_router_call(x, g_pre_ffn[layer], mod_p, mod_s, 3, 4,
                                               moe_w_router[j], moe_b_router[j])
            dest, tile_expert, n_used, tail_rows = _route(idx[:, :2])
            xs = _dispatch_call(h, dest, tail_rows, n_used)
            ys = _ffn_call(xs, tile_expert + j * N_EXPERTS, n_used, *moe_w)
            x = _combine_call(x, ys, dest, wts, g_post_ffn[layer], mod_p, mod_s, 5,
                              split=layer == DEPTH - 1)

    x_p, x_s = x if isinstance(x, (list, tuple)) else (x[:NP_TOK], x[NP_TOK:])
    y_prompt = x_p.reshape(BATCH, SEQ, D)
    y_sample = x_s.reshape(DEC_BATCH, DEC_SEQ, D)
    return (y_prompt, y_sample, jnp.stack(conv_p), jnp.stack(rec_p), jnp.stack(k_p), jnp.stack(v_p),
            jnp.stack(conv_s), jnp.stack(rec_s), jnp.stack(k_s), jnp.stack(v_s))
```

```python
import functools
import math

import numpy as np
import jax
import jax.numpy as jnp
from jax import lax
from jax.experimental import pallas as pl
from jax.experimental.pallas import tpu as pltpu

F32 = jnp.float32
BF16 = jnp.bfloat16
I32 = jnp.int32
HIGHEST = lax.Precision.HIGHEST

D = 1024
BATCH = 4
SEQ = 4096
DEPTH = 4
DEC_BATCH = 128
DEC_SEQ = 4
PAST_LEN = 8192
GDN_QK_HEADS = 4
GDN_V_HEADS = 8
GDN_HD = 128
GDN_KEY_DIM = GDN_QK_HEADS * GDN_HD
GDN_VAL_DIM = GDN_V_HEADS * GDN_HD
GDN_CONV_DIM = 2 * GDN_KEY_DIM + GDN_VAL_DIM
GDN_MAIN_DIM = GDN_CONV_DIM + GDN_VAL_DIM
CONV_WIDTH = 4
GDN_CHUNK = 64
SWA_Q_HEADS = 16
SWA_KV_HEADS = 4
SWA_GROUP = SWA_Q_HEADS // SWA_KV_HEADS
SWA_HD = 64
SWA_Q_DIM = SWA_Q_HEADS * SWA_HD
SWA_KV_DIM = SWA_KV_HEADS * SWA_HD
SWA_PROJ_DIM = SWA_Q_DIM + 2 * SWA_KV_DIM
WINDOW = 128
REL_BUCKETS = 32
REL_MAX_DIST = 128
D_FF = 2816
N_EXPERTS = 8
RMS_EPS = 1e-6

LANES = 128
SUBLANES = 8
VMEM_LIMIT = 56 * 1024 * 1024

TM = 512
NP_TOK = BATCH * SEQ
NS_TOK = DEC_BATCH * DEC_SEQ
TT = NP_TOK + NS_TOK
NP_TILES = NP_TOK // TM
NS_TILES = NS_TOK // TM
N_TILES = NP_TILES + NS_TILES
TILES_PER_SEQ = SEQ // TM
TF = 256
NF = D_FF // TF
TME = 1024
MOE_ROWS = 2 * TT + N_EXPERTS * TME
MOE_TILES = MOE_ROWS // TME
GDN_NB = 2
GDN_PROMPT_PAR = 4
GDN_SAMPLE_PAR = 4
GDN_SAMPLE_ROWS = 8
SWA_SB = 8
SWA_KPAD = WINDOW + 8

assert NP_TOK % TM == 0 and NS_TOK % TM == 0 and SEQ % TM == 0
assert D_FF % TF == 0 and DEC_SEQ <= GDN_SAMPLE_ROWS and DEC_SEQ <= 8
assert math.log2(SWA_HD) % 2 == 0


def _cparams(sem):
    return pltpu.CompilerParams(dimension_semantics=sem, vmem_limit_bytes=VMEM_LIMIT)


def _bdot(a, b):
    return jnp.dot(a.astype(BF16), b.astype(BF16), preferred_element_type=F32)


def _bdot_nt(a, b):
    return lax.dot_general(a.astype(BF16), b.astype(BF16), (((1,), (1,)), ((), ())),
                           preferred_element_type=F32)


def _silu(x):
    return x * jax.nn.sigmoid(x)


def _rms(x):
    return x * lax.rsqrt(jnp.mean(x * x, axis=-1, keepdims=True) + RMS_EPS)


def _mod_kernel(c_ref, w_ref, b_ref, o_ref):
    c = c_ref[...]
    o_ref[0] = _bdot(_silu(c), w_ref[0]) + b_ref[0]


def _mod_call(c_all, w_mod, b_mod):
    n = c_all.shape[0]
    tn = D
    return pl.pallas_call(
        _mod_kernel,
        grid=(DEPTH, 6 * D // tn),
        in_specs=[
            pl.BlockSpec((n, D), lambda l, j: (0, 0)),
            pl.BlockSpec((1, D, tn), lambda l, j: (l, 0, j)),
            pl.BlockSpec((1, 1, tn), lambda l, j: (l, 0, j)),
        ],
        out_specs=pl.BlockSpec((1, n, tn), lambda l, j: (l, 0, j)),
        out_shape=jax.ShapeDtypeStruct((DEPTH, n, 6 * D), F32),
        compiler_params=_cparams(("arbitrary", "arbitrary")),
        name="modulation",
    )(c_all, w_mod, b_mod.reshape(DEPTH, 1, 6 * D))


def _mod_specs(chunk, token_axis=0):
    p = pl.BlockSpec((1, 1, D), lambda *ids: (jnp.minimum(ids[token_axis] // TILES_PER_SEQ, BATCH - 1),
                                              0, chunk))
    s = pl.BlockSpec((TM, D), lambda *ids: (jnp.maximum(ids[token_axis] - NP_TILES, 0), chunk))
    return p, s


def _pick(i, p_ref, s_ref):
    return jnp.where(i < NP_TILES, p_ref[0], s_ref[...])


def _prenorm(i, x, g_ref, shp_ref, shs_ref, scp_ref, scs_ref):
    h = _rms(x) * g_ref[...]
    return h * (1.0 + _pick(i, scp_ref, scs_ref)) + _pick(i, shp_ref, shs_ref)


def _prenorm_router_kernel(x_ref, g_ref, shp_ref, shs_ref, scp_ref, scs_ref, wr_ref, br_ref,
                           h_ref, idx_ref, wt_ref):
    h = _prenorm(pl.program_id(0), x_ref[...], g_ref, shp_ref, shs_ref, scp_ref, scs_ref)
    h_ref[...] = h
    lane = lax.broadcasted_iota(I32, (TM, LANES), 1).astype(F32)
    logits = jnp.dot(h, wr_ref[...], precision=HIGHEST, preferred_element_type=F32) + br_ref[...]
    logits = jnp.where(lane < N_EXPERTS, logits, -jnp.inf)
    m1 = jnp.max(logits, axis=-1, keepdims=True)
    i1 = jnp.min(jnp.where(logits == m1, lane, float(LANES)), axis=-1, keepdims=True)
    rest = jnp.where(lane == i1, -jnp.inf, logits)
    m2 = jnp.max(rest, axis=-1, keepdims=True)
    i2 = jnp.min(jnp.where(rest == m2, lane, float(LANES)), axis=-1, keepdims=True)
    e2 = jnp.exp(m2 - m1)
    w1 = 1.0 / (1.0 + e2)
    w2 = e2 / (1.0 + e2)
    idx_ref[...] = jnp.where(lane == 0, i1, jnp.where(lane == 1, i2, 0.0)).astype(I32)
    wt_ref[...] = jnp.where(lane == 0, w1, jnp.where(lane == 1, w2, 0.0))


def _prenorm_router_call(x, g, mod_p, mod_s, sh_chunk, sc_chunk, w_router, b_router):
    shp, shs = _mod_specs(sh_chunk)
    scp, scs = _mod_specs(sc_chunk)
    tile = pl.BlockSpec((TM, D), lambda i: (i, 0))
    small = pl.BlockSpec((TM, LANES), lambda i: (i, 0))
    wr = jnp.pad(w_router, ((0, 0), (0, LANES - N_EXPERTS)))
    br = jnp.pad(b_router, (0, LANES - N_EXPERTS)).reshape(1, LANES)
    return pl.pallas_call(
        _prenorm_router_kernel,
        grid=(N_TILES,),
        in_specs=[tile, pl.BlockSpec((1, D), lambda i: (0, 0)), shp, shs, scp, scs,
                  pl.BlockSpec((D, LANES), lambda i: (0, 0)),
                  pl.BlockSpec((1, LANES), lambda i: (0, 0))],
        out_specs=[tile, small, small],
        out_shape=[jax.ShapeDtypeStruct((TT, D), F32),
                   jax.ShapeDtypeStruct((TT, LANES), I32),
                   jax.ShapeDtypeStruct((TT, LANES), F32)],
        compiler_params=_cparams(("arbitrary",)),
        name="prenorm_router",
    )(x, g.reshape(1, D), mod_p, mod_s, mod_p, mod_s, wr, br)


def _proj_kernel(x_ref, g_ref, shp_ref, shs_ref, scp_ref, scs_ref, w_ref, b_ref, o_ref, wbf_ref):
    i = pl.program_id(1)

    @pl.when(i == 0)
    def _():
        wbf_ref[...] = w_ref[...].astype(BF16)

    h = _prenorm(i, x_ref[...], g_ref, shp_ref, shs_ref, scp_ref, scs_ref).astype(BF16)
    o_ref[...] = jnp.dot(h, wbf_ref[...], preferred_element_type=F32) + b_ref[...]


def _proj_call(x, g, mod_p, mod_s, w, b, tn):
    k, n = w.shape
    shp, shs = _mod_specs(0, token_axis=1)
    scp, scs = _mod_specs(1, token_axis=1)
    return pl.pallas_call(
        _proj_kernel,
        grid=(n // tn, N_TILES),
        in_specs=[
            pl.BlockSpec((TM, k), lambda j, i: (i, 0)),
            pl.BlockSpec((1, k), lambda j, i: (0, 0)), shp, shs, scp, scs,
            pl.BlockSpec((k, tn), lambda j, i: (0, j)),
            pl.BlockSpec((1, tn), lambda j, i: (0, j)),
        ],
        out_specs=pl.BlockSpec((TM, tn), lambda j, i: (i, j)),
        out_shape=jax.ShapeDtypeStruct((TT, n), F32),
        scratch_shapes=[pltpu.VMEM((k, tn), BF16)],
        compiler_params=_cparams(("arbitrary", "arbitrary")),
        name="proj",
    )(x, g.reshape(1, k), mod_p, mod_s, mod_p, mod_s, w, b.reshape(1, n))


def _post_mm_kernel(x_ref, op_ref, os_ref, w_ref, b_ref, g_ref, gtp_ref, gts_ref, out_ref, wbf_ref):
    i = pl.program_id(0)

    @pl.when(i == 0)
    def _():
        wbf_ref[...] = w_ref[...].astype(BF16)

    o = jnp.where(i < NP_TILES, op_ref[...], os_ref[...]).astype(BF16)
    y = jnp.dot(o, wbf_ref[...], preferred_element_type=F32) + b_ref[...]
    out_ref[...] = x_ref[...] + _pick(i, gtp_ref, gts_ref) * (_rms(y) * g_ref[...])


def _post_mm_call(x, o_p, o_s, w, b, g, mod_p, mod_s, gate_chunk):
    gtp, gts = _mod_specs(gate_chunk)
    tile = pl.BlockSpec((TM, D), lambda i: (i, 0))
    row = pl.BlockSpec((1, D), lambda i: (0, 0))
    return pl.pallas_call(
        _post_mm_kernel,
        grid=(N_TILES,),
        in_specs=[tile,
                  pl.BlockSpec((TM, D), lambda i: (jnp.minimum(i, NP_TILES - 1), 0)),
                  pl.BlockSpec((TM, D), lambda i: (jnp.maximum(i - NP_TILES, 0), 0)),
                  pl.BlockSpec((D, D), lambda i: (0, 0)), row, row, gtp, gts],
        out_specs=tile,
        out_shape=jax.ShapeDtypeStruct((TT, D), F32),
        scratch_shapes=[pltpu.VMEM((D, D), BF16)],
        compiler_params=_cparams(("arbitrary",)),
        name="post_mm",
    )(x, o_p, o_s, w, b.reshape(1, D), g.reshape(1, D), mod_p, mod_s)


def _gdn_kernel(*refs, chunk, n_chunks, t_real, n_par):
    x_refs = refs[:n_par]
    ba_refs = refs[n_par:2 * n_par]
    (cw_ref, alog_ref, dt_ref, nw_ref, cb_ref, s0_ref, o_ref, sfin_ref, xbuf, s_scr) = refs[2 * n_par:]
    c_sz = chunk
    tb = chunk * n_chunks
    j = pl.program_id(1)
    rep = GDN_V_HEADS // GDN_QK_HEADS
    units = [(p, h) for p in range(n_par) for h in range(GDN_V_HEADS)]

    @pl.when(j == 0)
    def _():
        s_scr[...] = s0_ref[...]
        xbuf[:, 0:SUBLANES, :] = cb_ref[...]

    for p in range(n_par):
        xbuf[p, SUBLANES:SUBLANES + tb, :] = x_refs[p][:, 0:GDN_CONV_DIM]

    ii = lax.broadcasted_iota(I32, (c_sz, c_sz), 0)
    jj = lax.broadcasted_iota(I32, (c_sz, c_sz), 1)
    causal = ii >= jj
    strict = ii > jj
    tri = causal.astype(F32)
    eye_c = (ii == jj).astype(F32)
    pair_mask = [jnp.logical_and((ii >> (l + 1)) == (jj >> (l + 1)), (ii >> l) != (jj >> l))
                 for l in range(int(math.log2(c_sz)))]
    eye_t = (lax.broadcasted_iota(I32, (2 * SUBLANES, LANES), 0)
             == lax.broadcasted_iota(I32, (2 * SUBLANES, LANES), 1)).astype(F32)
    lane = lax.broadcasted_iota(I32, (c_sz, LANES), 1)
    row = lax.broadcasted_iota(I32, (c_sz, LANES), 0)
    cw = cw_ref[...]
    neg_a = -jnp.exp(alog_ref[...])
    dt = dt_ref[...]
    nw = nw_ref[...]

    def chunk_body(c, carry):
        r0 = pl.multiple_of(c * c_sz, c_sz)
        xc, beta, gc, rows_t = [], [], [], []
        for p in range(n_par):
            win = xbuf[p, pl.ds(r0, c_sz + SUBLANES), :]
            y = win[SUBLANES:SUBLANES + c_sz] * cw[3:4]
            for tap in range(1, CONV_WIDTH):
                y = y + win[SUBLANES - tap:SUBLANES - tap + c_sz] * cw[3 - tap:4 - tap]
            xc.append(_silu(y))
            ba = ba_refs[p][pl.ds(r0, c_sz), :]
            b_p = jax.nn.sigmoid(ba)
            g_p = neg_a * jax.nn.softplus(ba + dt)
            if t_real < tb:
                live = (row + r0) < t_real
                b_p = jnp.where(live, b_p, 0.0)
                g_p = jnp.where(live, g_p, 0.0)
            gc_p = jnp.dot(tri, g_p, precision=HIGHEST, preferred_element_type=F32)
            cols = jnp.where(lane < GDN_V_HEADS, b_p, gc_p)
            rows_t.append(lax.dot_general(eye_t, cols, (((1,), (1,)), ((), ())), precision=HIGHEST,
                                          preferred_element_type=F32))
            beta.append(b_p)
            gc.append(gc_p)

        qn, kn, kk, qk = {}, {}, {}, {}
        for p in range(n_par):
            for hq in range(GDN_QK_HEADS):
                q = xc[p][:, hq * GDN_HD:(hq + 1) * GDN_HD]
                k = xc[p][:, GDN_KEY_DIM + hq * GDN_HD:GDN_KEY_DIM + (hq + 1) * GDN_HD]
                q = q * lax.rsqrt(jnp.sum(q * q, axis=-1, keepdims=True) + RMS_EPS) * (GDN_HD ** -0.5)
                k = k * lax.rsqrt(jnp.sum(k * k, axis=-1, keepdims=True) + RMS_EPS)
                qn[p, hq], kn[p, hq] = q, k
        for key in qn:
            kk[key] = _bdot_nt(kn[key], kn[key])
            qk[key] = _bdot_nt(qn[key], kn[key])

        gcc, gcl, bcol, decay, a_mat, inv = {}, {}, {}, {}, {}, {}
        for (p, h) in units:
            gcc[p, h] = gc[p][:, GDN_V_HEADS + h:GDN_V_HEADS + h + 1]
            gcr = rows_t[p][GDN_V_HEADS + h:GDN_V_HEADS + h + 1, :]
            bcol[p, h] = beta[p][:, h:h + 1]
            gcl[p, h] = gcc[p, h][c_sz - 1:c_sz, :]
            decay[p, h] = jnp.exp(jnp.where(causal, gcc[p, h] - gcr, -jnp.inf))
            a_mat[p, h] = jnp.where(strict, bcol[p, h] * kk[p, h // rep] * decay[p, h], 0.0)
            inv[p, h] = eye_c - jnp.where(pair_mask[0], a_mat[p, h], 0.0)
        for lvl in range(1, len(pair_mask)):
            t1 = {u: _bdot(jnp.where(pair_mask[lvl], a_mat[u], 0.0), inv[u]) for u in units}
            t2 = {u: _bdot(inv[u], t1[u]) for u in units}
            inv = {u: inv[u] - t2[u] for u in units}

        egc = {u: jnp.exp(gcc[u]) for u in units}
        sol = {}
        for (p, h) in units:
            v = xc[p][:, 2 * GDN_KEY_DIM + h * GDN_HD:2 * GDN_KEY_DIM + (h + 1) * GDN_HD]
            rhs = jnp.concatenate([v * bcol[p, h], kn[p, h // rep] * (bcol[p, h] * egc[p, h])], axis=1)
            sol[p, h] = _bdot(inv[p, h], rhs)
        ws = {}
        for (p, h) in units:
            q_dec = qn[p, h // rep] * egc[p, h]
            ws[p, h] = _bdot(jnp.concatenate([sol[p, h][:, GDN_HD:], q_dec], axis=0), s_scr[p, h])
        u_new = {u: sol[u][:, :GDN_HD] - ws[u][:c_sz] for u in units}
        o_part, s_part = {}, {}
        for (p, h) in units:
            qkm = jnp.where(causal, qk[p, h // rep] * decay[p, h], 0.0)
            k_dec = kn[p, h // rep] * jnp.exp(gcl[p, h] - gcc[p, h])
            o_part[p, h] = _bdot(qkm, u_new[p, h])
            s_part[p, h] = lax.dot_general(k_dec.astype(BF16), u_new[p, h].astype(BF16),
                                           (((0,), (0,)), ((), ())), preferred_element_type=F32)
        for (p, h) in units:
            s_scr[p, h] = s_scr[p, h] * jnp.exp(gcl[p, h]) + s_part[p, h]
            o = ws[p, h][c_sz:] + o_part[p, h]
            z = x_refs[p][pl.ds(r0, c_sz), GDN_CONV_DIM + h * GDN_HD:GDN_CONV_DIM + (h + 1) * GDN_HD]
            o_ref[p, pl.ds(r0, c_sz), h * GDN_HD:(h + 1) * GDN_HD] = _rms(o) * nw * _silu(z)
        return carry

    lax.fori_loop(0, n_chunks, chunk_body, 0)
    xbuf[:, 0:SUBLANES, :] = xbuf[:, tb:tb + SUBLANES, :]

    @pl.when(j == pl.num_programs(1) - 1)
    def _():
        sfin_ref[...] = s_scr[...]


def _gdn_call(x, ba, conv_w, a_log, dt_bias, norm_w, conv_buf, s0, *, n_seq, n_steps, chunk,
              n_chunks, t_real, n_par, s0_first_seq=0):
    s0_block = s0_first_seq // n_par
    tb = chunk * n_chunks
    pad = LANES - 2 * GDN_V_HEADS
    alog_row = jnp.pad(a_log, (GDN_V_HEADS, pad)).reshape(1, LANES)
    dt_row = jnp.pad(dt_bias, (GDN_V_HEADS, pad)).reshape(1, LANES)

    def seq_block(width, p):
        return pl.BlockSpec((None, tb, width), lambda s, j: ((s * n_par + p) * n_steps + j, 0, 0))

    const = lambda s, j: (0, 0)
    state_spec = pl.BlockSpec((n_par, GDN_V_HEADS, GDN_HD, GDN_HD), lambda s, j: (s, 0, 0, 0))
    return pl.pallas_call(
        functools.partial(_gdn_kernel, chunk=chunk, n_chunks=n_chunks, t_real=t_real, n_par=n_par),
        grid=(n_seq // n_par, n_steps),
        in_specs=[seq_block(GDN_MAIN_DIM, p) for p in range(n_par)]
                 + [seq_block(LANES, p) for p in range(n_par)]
                 + [pl.BlockSpec((CONV_WIDTH, GDN_CONV_DIM), const),
                    pl.BlockSpec((1, LANES), const), pl.BlockSpec((1, LANES), const),
                    pl.BlockSpec((1, GDN_HD), const),
                    pl.BlockSpec((n_par, SUBLANES, GDN_CONV_DIM), lambda s, j: (s, 0, 0)),
                    pl.BlockSpec((n_par, GDN_V_HEADS, GDN_HD, GDN_HD),
                                 lambda s, j: (s + s0_block, 0, 0, 0))],
        out_specs=[pl.BlockSpec((n_par, None, tb, GDN_VAL_DIM), lambda s, j: (s, j, 0, 0)),
                   state_spec],
        out_shape=[jax.ShapeDtypeStruct((n_seq, n_steps, tb, GDN_VAL_DIM), F32),
                   jax.ShapeDtypeStruct((n_seq, GDN_V_HEADS, GDN_HD, GDN_HD), F32)],
        scratch_shapes=[pltpu.VMEM((n_par, tb + SUBLANES, GDN_CONV_DIM), F32),
                        pltpu.VMEM((n_par, GDN_V_HEADS, GDN_HD, GDN_HD), F32)],
        compiler_params=_cparams(("arbitrary", "arbitrary")),
        name="gdn",
    )(*([x] * n_par), *([ba] * n_par), conv_w, alog_row, dt_row, norm_w.reshape(1, GDN_HD),
      conv_buf, s0)


def _t5_bucket_np(dist):
    dist = np.maximum(dist, 0)
    max_exact = REL_BUCKETS // 2
    ratio = np.log(np.maximum(dist, max_exact).astype(np.float32) / np.float32(max_exact)) \
        / np.float32(math.log(REL_MAX_DIST / max_exact))
    large = max_exact + (ratio.astype(np.float32) * np.float32(REL_BUCKETS - max_exact)).astype(np.int32)
    return np.where(dist < max_exact, dist, np.minimum(large, REL_BUCKETS - 1)).astype(np.int32)


def _bucket_table(qpos, kpos, k_valid):
    dist = qpos[:, None] - kpos[None, :]
    ok = (dist >= 0) & (dist < WINDOW) & (kpos[None, :] >= 0) & k_valid[None, :]
    return np.where(ok, _t5_bucket_np(dist), -1).astype(np.int32)


def _bias_from_buckets(bkt, rb_ref, head):
    def body(b, acc):
        return jnp.where(bkt == b, rb_ref[b, head], acc)
    acc = lax.fori_loop(0, REL_BUCKETS, body, jnp.zeros(bkt.shape, F32))
    return jnp.where(bkt < 0, -jnp.inf, acc)


def _swa_prompt_kernel(rb_ref, sk_ref, bkt_ref, q_ref, kvp_ref, kvc_ref, o_ref, bias_scr):
    first = jnp.logical_and(pl.program_id(0) == 0, pl.program_id(1) == 0)
    n = pl.program_id(1)

    @pl.when(first)
    def _():
        bkt = bkt_ref[...]
        for h in range(SWA_Q_HEADS):
            kv, g = divmod(h, SWA_GROUP)
            rows = slice(g * WINDOW, (g + 1) * WINDOW)
            table = _bias_from_buckets(bkt, rb_ref, h)
            bias_scr[1, kv, rows, :] = table
            bias_scr[0, kv, rows, 0:WINDOW] = jnp.full((WINDOW, WINDOW), -jnp.inf, F32)
            bias_scr[0, kv, rows, WINDOW:2 * WINDOW] = table[:, WINDOW:]

    has_prev = jnp.minimum(n, 1)
    q = q_ref[...] * (SWA_HD ** -0.5)
    heads = range(SWA_KV_HEADS)
    s_prev, s_cur = [], []
    for kv in heads:
        qs = jnp.concatenate(
            [q[:, (kv * SWA_GROUP + g) * SWA_HD:(kv * SWA_GROUP + g + 1) * SWA_HD]
             for g in range(SWA_GROUP)], axis=0).astype(BF16)
        s_prev.append(_bdot_nt(qs, kvp_ref[:, kv * SWA_HD:(kv + 1) * SWA_HD]))
        s_cur.append(_bdot_nt(qs, kvc_ref[:, kv * SWA_HD:(kv + 1) * SWA_HD]))
    p_prev, p_cur, den = [], [], []
    for kv in heads:
        sp = s_prev[kv] + bias_scr[has_prev, kv, :, 0:WINDOW]
        sc = s_cur[kv] + bias_scr[1, kv, :, WINDOW:2 * WINDOW]
        sink = jnp.concatenate(
            [jnp.full((WINDOW, 1), sk_ref[kv * SWA_GROUP + g], F32) for g in range(SWA_GROUP)], axis=0)
        m = jnp.maximum(jnp.maximum(jnp.max(sp, axis=-1, keepdims=True),
                                    jnp.max(sc, axis=-1, keepdims=True)), sink)
        pp = jnp.exp(sp - m)
        pc = jnp.exp(sc - m)
        den.append(jnp.sum(pp, axis=-1, keepdims=True) + jnp.sum(pc, axis=-1, keepdims=True)
                   + jnp.exp(sink - m))
        p_prev.append(pp.astype(BF16))
        p_cur.append(pc.astype(BF16))
    outs = []
    for kv in heads:
        v_prev = kvp_ref[:, SWA_KV_DIM + kv * SWA_HD:SWA_KV_DIM + (kv + 1) * SWA_HD]
        v_cur = kvc_ref[:, SWA_KV_DIM + kv * SWA_HD:SWA_KV_DIM + (kv + 1) * SWA_HD]
        o = (_bdot(p_prev[kv], v_prev) + _bdot(p_cur[kv], v_cur)) / den[kv]
        outs.extend(o[g * WINDOW:(g + 1) * WINDOW] for g in range(SWA_GROUP))
    o_ref[...] = jnp.concatenate(outs, axis=1)


def _swa_prompt_call(qkv, rel_bias, sinks):
    nb = SEQ // WINDOW
    qpos = WINDOW + np.arange(WINDOW)
    kpos = np.arange(2 * WINDOW)
    bkt = jnp.asarray(_bucket_table(qpos, kpos, np.ones(2 * WINDOW, bool)))
    kv_col = SWA_Q_DIM // (2 * SWA_KV_DIM)
    smem = pl.BlockSpec(memory_space=pltpu.SMEM)
    return pl.pallas_call(
        _swa_prompt_kernel,
        grid=(BATCH, nb),
        in_specs=[smem, smem,
                  pl.BlockSpec((WINDOW, 2 * WINDOW), lambda b, n: (0, 0)),
                  pl.BlockSpec((WINDOW, SWA_Q_DIM), lambda b, n: (b * nb + n, 0)),
                  pl.BlockSpec((WINDOW, 2 * SWA_KV_DIM),
                               lambda b, n: (jnp.maximum(b * nb + n - 1, 0), kv_col)),
                  pl.BlockSpec((WINDOW, 2 * SWA_KV_DIM), lambda b, n: (b * nb + n, kv_col))],
        out_specs=pl.BlockSpec((WINDOW, SWA_Q_DIM), lambda b, n: (b * nb + n, 0)),
        out_shape=jax.ShapeDtypeStruct((NP_TOK, SWA_Q_DIM), F32),
        scratch_shapes=[pltpu.VMEM((2, SWA_KV_HEADS, SWA_GROUP * WINDOW, 2 * WINDOW), F32)],
        compiler_params=_cparams(("arbitrary", "arbitrary")),
        name="swa_prompt",
    )(rel_bias, sinks, bkt, qkv, qkv, qkv)


def _swa_sample_kernel(rb_ref, sk_ref, bkt_ref, q_ref, ck_ref, cv_ref, kn_ref, vn_ref,
                       o_ref, ck_out_ref, cv_out_ref, k_scr, v_scr, bias_scr, sink_scr):
    rows = SWA_GROUP * DEC_SEQ
    k_scr[:, 0:WINDOW, :] = ck_ref[...]
    k_scr[:, WINDOW:SWA_KPAD, :] = kn_ref[...]
    v_scr[:, 0:WINDOW, :] = cv_ref[...]
    v_scr[:, WINDOW:SWA_KPAD, :] = vn_ref[...]
    ck_out_ref[...] = k_scr[:, DEC_SEQ:DEC_SEQ + WINDOW, :]
    cv_out_ref[...] = v_scr[:, DEC_SEQ:DEC_SEQ + WINDOW, :]
    k_ref, v_ref = k_scr, v_scr

    def group_of_row(shape):
        r = lax.broadcasted_iota(I32, shape, 0)
        return sum((r >= g * DEC_SEQ).astype(I32) for g in range(1, SWA_GROUP))

    @pl.when(pl.program_id(0) == 0)
    def _():
        bkt = bkt_ref[...]
        grp = group_of_row((rows, 1))
        grp_full = group_of_row(bkt.shape)
        for kv in range(SWA_KV_HEADS):
            acc = jnp.zeros(bkt.shape, F32)
            snk = jnp.zeros((rows, 1), F32)
            for g in range(SWA_GROUP):
                head = kv * SWA_GROUP + g
                acc = jnp.where(grp_full == g, _bias_from_buckets(bkt, rb_ref, head), acc)
                snk = jnp.where(grp == g, sk_ref[head], snk)
            bias_scr[kv] = acc
            sink_scr[kv] = snk

    scale = SWA_HD ** -0.5
    units = [(s, kv) for s in range(SWA_SB) for kv in range(SWA_KV_HEADS)]
    scores = {(s, kv): _bdot_nt(q_ref[s, kv], k_ref[s, :, kv * SWA_HD:(kv + 1) * SWA_HD])
              for (s, kv) in units}
    probs, den = {}, {}
    for (s, kv) in units:
        sc = scores[s, kv] * scale + bias_scr[kv]
        sink = sink_scr[kv]
        m = jnp.maximum(jnp.max(sc, axis=-1, keepdims=True), sink)
        p = jnp.exp(sc - m)
        den[s, kv] = jnp.sum(p, axis=-1, keepdims=True) + jnp.exp(sink - m)
        probs[s, kv] = p.astype(BF16)
    for (s, kv) in units:
        o_ref[s, kv] = _bdot(probs[s, kv], v_ref[s, :, kv * SWA_HD:(kv + 1) * SWA_HD]) / den[s, kv]


def _swa_sample_call(q_st, cache_k, cache_v, k_new, v_new, layer_pair, rel_bias, sinks):
    rows = SWA_GROUP * DEC_SEQ
    n_keys = WINDOW + DEC_SEQ
    qpos = PAST_LEN + np.arange(DEC_SEQ)
    kpos = PAST_LEN - WINDOW + np.arange(SWA_KPAD)
    bkt4 = _bucket_table(qpos, kpos, np.arange(SWA_KPAD) < n_keys)
    bkt = jnp.asarray(np.tile(bkt4, (SWA_GROUP, 1)))
    steps = DEC_BATCH // SWA_SB
    smem = pl.BlockSpec(memory_space=pltpu.SMEM)
    q_spec = pl.BlockSpec((SWA_SB, SWA_KV_HEADS, rows, SWA_HD), lambda i: (i, 0, 0, 0))
    win_in = pl.BlockSpec((SWA_SB, WINDOW, SWA_KV_DIM), lambda i: (layer_pair * steps + i, 0, 0))
    win_out = pl.BlockSpec((SWA_SB, WINDOW, SWA_KV_DIM), lambda i: (i, 0, 0))
    new_spec = pl.BlockSpec((SWA_SB, SWA_KPAD - WINDOW, SWA_KV_DIM), lambda i: (i, 0, 0))
    win_shape = jax.ShapeDtypeStruct((DEC_BATCH, WINDOW, SWA_KV_DIM), F32)
    return pl.pallas_call(
        _swa_sample_kernel,
        grid=(steps,),
        in_specs=[smem, smem, pl.BlockSpec((rows, SWA_KPAD), lambda i: (0, 0)),
                  q_spec, win_in, win_in, new_spec, new_spec],
        out_specs=[q_spec, win_out, win_out],
        out_shape=[jax.ShapeDtypeStruct((DEC_BATCH, SWA_KV_HEADS, rows, SWA_HD), F32),
                   win_shape, win_shape],
        scratch_shapes=[pltpu.VMEM((SWA_SB, SWA_KPAD, SWA_KV_DIM), F32),
                        pltpu.VMEM((SWA_SB, SWA_KPAD, SWA_KV_DIM), F32),
                        pltpu.VMEM((SWA_KV_HEADS, rows, SWA_KPAD), F32),
                        pltpu.VMEM((SWA_KV_HEADS, rows, 1), F32)],
        compiler_params=_cparams(("arbitrary",)),
        name="swa_sample",
    )(rel_bias, sinks, bkt, q_st, cache_k, cache_v, k_new, v_new)


def _ffn_scratch(rows):
    return [pltpu.VMEM((rows, D), F32),
            pltpu.VMEM((NF, D, TF), BF16),
            pltpu.VMEM((NF, D, TF), BF16),
            pltpu.VMEM((NF, TF, D), BF16),
            pltpu.VMEM((2, D, TF), F32),
            pltpu.VMEM((2, D, TF), F32),
            pltpu.VMEM((2, TF, D), F32),
            pltpu.SemaphoreType.DMA((3, 2))]


def _swiglu_into(acc_ref, x, expert, load, w_hbm, res, stg, sem):
    wg_hbm, wu_hbm, wd_hbm = w_hbm
    res_g, res_u, res_d = res
    stg_g, stg_u, stg_d = stg

    def copies(f, slot):
        col = pl.multiple_of(f * TF, TF)
        return (pltpu.make_async_copy(wg_hbm.at[expert, :, pl.ds(col, TF)], stg_g.at[slot], sem.at[0, slot]),
                pltpu.make_async_copy(wu_hbm.at[expert, :, pl.ds(col, TF)], stg_u.at[slot], sem.at[1, slot]),
                pltpu.make_async_copy(wd_hbm.at[expert, pl.ds(col, TF), :], stg_d.at[slot], sem.at[2, slot]))

    def block(f):
        gate = jnp.dot(x, res_g[f], preferred_element_type=F32)
        up = jnp.dot(x, res_u[f], preferred_element_type=F32)
        act = (_silu(gate) * up).astype(BF16)
        acc_ref[...] += jnp.dot(act, res_d[f], preferred_element_type=F32)

    acc_ref[...] = jnp.zeros(acc_ref.shape, acc_ref.dtype)

    @pl.when(load)
    def _():
        for c in copies(0, 0):
            c.start()

        def body(f, carry):
            slot = lax.rem(f, 2)

            @pl.when(f + 1 < NF)
            def _():
                for c in copies(f + 1, 1 - slot):
                    c.start()

            for c in copies(f, slot):
                c.wait()
            res_g[f] = stg_g[slot].astype(BF16)
            res_u[f] = stg_u[slot].astype(BF16)
            res_d[f] = stg_d[slot].astype(BF16)
            block(f)
            return carry

        lax.fori_loop(0, NF, body, 0)

    @pl.when(jnp.logical_not(load))
    def _():
        def body(f, carry):
            block(f)
            return carry

        lax.fori_loop(0, NF, body, 0)


def _ffn_kernel(te_ref, ld_ref, hf_ref, nt_ref, x_ref, wg_hbm, wu_hbm, wd_hbm, o_ref, acc_ref, *scratch):
    t = pl.program_id(0)
    used = t < nt_ref[0]
    half = hf_ref[t] != 0
    half_rows = TME // 2

    def run(rows):
        _swiglu_into(acc_ref.at[pl.ds(0, rows)], x_ref[0:rows, :].astype(BF16), te_ref[t],
                     ld_ref[t] != 0, (wg_hbm, wu_hbm, wd_hbm), scratch[0:3], scratch[3:6], scratch[6])
        o_ref[0:rows, :] = acc_ref[0:rows, :]

    @pl.when(jnp.logical_and(used, jnp.logical_not(half)))
    def _():
        run(TME)

    @pl.when(jnp.logical_and(used, half))
    def _():
        run(half_rows)
        o_ref[half_rows:, :] = jnp.zeros((TME - half_rows, D), F32)

    @pl.when(jnp.logical_not(used))
    def _():
        o_ref[...] = jnp.zeros_like(o_ref)


def _ffn_call(x, tile_expert, tile_half, n_tiles_used, w_gate, w_up, w_down):
    n_rows = x.shape[0]
    n_tiles = n_rows // TME
    changed = jnp.concatenate([jnp.ones((1,), I32),
                               (tile_expert[1:] != tile_expert[:-1]).astype(I32)])
    hbm = pl.BlockSpec(memory_space=pl.ANY)
    grid_spec = pltpu.PrefetchScalarGridSpec(
        num_scalar_prefetch=4,
        grid=(n_tiles,),
        in_specs=[pl.BlockSpec((TME, D), lambda t, te, ld, hf, nt: (jnp.minimum(t, nt[0] - 1), 0)),
                  hbm, hbm, hbm],
        out_specs=pl.BlockSpec((TME, D), lambda t, *_: (t, 0)),
        scratch_shapes=_ffn_scratch(TME),
    )
    return pl.pallas_call(
        _ffn_kernel,
        grid_spec=grid_spec,
        out_shape=jax.ShapeDtypeStruct((n_rows, D), F32),
        compiler_params=_cparams(("arbitrary",)),
        name="ffn",
    )(tile_expert, changed, tile_half, n_tiles_used, x, w_gate, w_up, w_down)


def _dense_ffn_kernel(x_ref, gpre_ref, shp_ref, shs_ref, scp_ref, scs_ref, wg_hbm, wu_hbm, wd_hbm,
                      gpost_ref, gtp_ref, gts_ref, out_ref, acc_ref, *scratch, layer_pair):
    i = pl.program_id(0)
    x = x_ref[...]
    h = _prenorm(i, x, gpre_ref, shp_ref, shs_ref, scp_ref, scs_ref).astype(BF16)
    _swiglu_into(acc_ref, h, layer_pair, i == 0, (wg_hbm, wu_hbm, wd_hbm),
                 scratch[0:3], scratch[3:6], scratch[6])
    out_ref[...] = x + _pick(i, gtp_ref, gts_ref) * (_rms(acc_ref[...]) * gpost_ref[...])


def _dense_ffn_call(x, g_pre, g_post, mod_p, mod_s, w_gate, w_up, w_down, layer_pair):
    shp, shs = _mod_specs(3)
    scp, scs = _mod_specs(4)
    gtp, gts = _mod_specs(5)
    tile = pl.BlockSpec((TM, D), lambda i: (i, 0))
    row = pl.BlockSpec((1, D), lambda i: (0, 0))
    hbm = pl.BlockSpec(memory_space=pl.ANY)
    return pl.pallas_call(
        functools.partial(_dense_ffn_kernel, layer_pair=layer_pair),
        grid=(N_TILES,),
        in_specs=[tile, row, shp, shs, scp, scs, hbm, hbm, hbm, row, gtp, gts],
        out_specs=tile,
        out_shape=jax.ShapeDtypeStruct((TT, D), F32),
        scratch_shapes=_ffn_scratch(TM),
        compiler_params=_cparams(("arbitrary",)),
        name="dense_ffn",
    )(x, g_pre.reshape(1, D), mod_p, mod_s, mod_p, mod_s, w_gate, w_up, w_down,
      g_post.reshape(1, D), mod_p, mod_s)


DMA_UNROLL = 8


def _row_copy(src_ref, dst_ref, src_row, dst_row, sem):
    return pltpu.make_async_copy(src_ref.at[pl.ds(src_row, 1)], dst_ref.at[pl.ds(dst_row, 1)], sem)


def _dispatch_kernel(tail_ref, nt_ref, d0_ref, d1_ref, h_ref, out_ref, zero_ref, sem, zsem):
    @pl.when(pl.program_id(0) == 0)
    def _():
        zero_ref[...] = jnp.zeros_like(zero_ref)

        def fill_tile(row0):
            fill = pltpu.make_async_copy(zero_ref, out_ref.at[pl.ds(pl.multiple_of(row0, TME), TME)], zsem)
            fill.start()
            fill.wait()

        for e in range(N_EXPERTS):
            @pl.when(tail_ref[e] >= 0)
            def _():
                fill_tile(tail_ref[e])

            @pl.when(nt_ref[0] + e < MOE_TILES)
            def _():
                fill_tile((nt_ref[0] + e) * TME)

    def start(r, carry):
        _row_copy(h_ref, out_ref, r, d0_ref[0, 0, r], sem).start(priority=0)
        _row_copy(h_ref, out_ref, r, d1_ref[0, 0, r], sem).start(priority=1)
        return carry

    def wait(r, carry):
        _row_copy(h_ref, out_ref, 0, 0, sem).wait()
        _row_copy(h_ref, out_ref, 0, 0, sem).wait()
        return carry

    lax.fori_loop(0, TM, start, 0, unroll=DMA_UNROLL)
    lax.fori_loop(0, TM, wait, 0, unroll=DMA_UNROLL)


def _dispatch_call(h, dest, tail_rows, n_tiles_used):
    idx_spec = pl.BlockSpec((1, 1, TM), lambda i, *_: (i, 0, 0), memory_space=pltpu.SMEM)
    grid_spec = pltpu.PrefetchScalarGridSpec(
        num_scalar_prefetch=2,
        grid=(N_TILES,),
        in_specs=[idx_spec, idx_spec, pl.BlockSpec((TM, D), lambda i, *_: (i, 0))],
        out_specs=pl.BlockSpec(memory_space=pl.ANY),
        scratch_shapes=[pltpu.VMEM((TME, D), h.dtype), pltpu.SemaphoreType.DMA(()),
                        pltpu.SemaphoreType.DMA(())],
    )
    return pl.pallas_call(
        _dispatch_kernel,
        grid_spec=grid_spec,
        out_shape=jax.ShapeDtypeStruct((MOE_ROWS, D), h.dtype),
        compiler_params=_cparams(("arbitrary",)),
        name="moe_dispatch",
    )(tail_rows, n_tiles_used, dest[0].reshape(N_TILES, 1, TM), dest[1].reshape(N_TILES, 1, TM), h)


def _combine_kernel(d0_ref, d1_ref, x_ref, wt_ref, ys_ref, g_ref, gtp_ref, gts_ref, *rest, split):
    i = pl.program_id(0)
    buf, sem = rest[-2:]

    def start(r, carry):
        _row_copy(ys_ref, buf.at[0], d0_ref[0, 0, r], r, sem).start(priority=0)
        _row_copy(ys_ref, buf.at[1], d1_ref[0, 0, r], r, sem).start(priority=1)
        return carry

    def wait(r, carry):
        _row_copy(ys_ref, buf.at[0], 0, 0, sem).wait()
        _row_copy(ys_ref, buf.at[1], 0, 0, sem).wait()
        return carry

    lax.fori_loop(0, TM, start, 0, unroll=DMA_UNROLL)
    lax.fori_loop(0, TM, wait, 0, unroll=DMA_UNROLL)
    wt = wt_ref[...]
    y = wt[:, 0:1] * buf[0] + wt[:, 1:2] * buf[1]
    out = x_ref[...] + _pick(i, gtp_ref, gts_ref) * (_rms(y) * g_ref[...])
    if split:
        prompt_ref, sample_ref = rest[:2]

        @pl.when(i < NP_TILES)
        def _():
            prompt_ref[...] = out

        @pl.when(i >= NP_TILES)
        def _():
            sample_ref[...] = out
    else:
        rest[0][...] = out


def _combine_call(x, ys, dest, wts, g, mod_p, mod_s, gate_chunk, split):
    gtp, gts = _mod_specs(gate_chunk)
    tile = pl.BlockSpec((TM, D), lambda i: (i, 0))
    idx_spec = pl.BlockSpec((1, 1, TM), lambda i: (i, 0, 0), memory_space=pltpu.SMEM)
    if split:
        out_specs = [pl.BlockSpec((TM, D), lambda i: (jnp.minimum(i, NP_TILES - 1), 0)),
                     pl.BlockSpec((TM, D), lambda i: (jnp.maximum(i - NP_TILES, 0), 0))]
        out_shape = [jax.ShapeDtypeStruct((NP_TOK, D), F32), jax.ShapeDtypeStruct((NS_TOK, D), F32)]
    else:
        out_specs = tile
        out_shape = jax.ShapeDtypeStruct((TT, D), F32)
    return pl.pallas_call(
        functools.partial(_combine_kernel, split=split),
        grid=(N_TILES,),
        in_specs=[idx_spec, idx_spec, tile, pl.BlockSpec((TM, LANES), lambda i: (i, 0)),
                  pl.BlockSpec(memory_space=pl.ANY),
                  pl.BlockSpec((1, D), lambda i: (0, 0)), gtp, gts],
        out_specs=out_specs,
        out_shape=out_shape,
        scratch_shapes=[pltpu.VMEM((2, TM, D), F32), pltpu.SemaphoreType.DMA(())],
        compiler_params=_cparams(("arbitrary",)),
        name="moe_combine",
    )(dest[0].reshape(N_TILES, 1, TM), dest[1].reshape(N_TILES, 1, TM), x, wts, ys,
      g.reshape(1, D), mod_p, mod_s)


def _route(top_idx):
    flat_e = top_idx.T.reshape(-1)
    onehot = (flat_e[:, None] == jnp.arange(N_EXPERTS, dtype=I32)[None, :]).astype(I32)
    csum = jnp.cumsum(onehot, axis=0)
    rank = jnp.sum((csum - onehot) * onehot, axis=1)
    counts = csum[-1]
    padded = ((counts + TME - 1) // TME) * TME
    ends = jnp.cumsum(padded)
    starts = ends - padded
    dest = (jnp.sum(onehot * starts[None, :], axis=1) + rank).astype(I32)
    tile_start = jnp.arange(MOE_TILES, dtype=I32) * TME
    tile_expert = jnp.minimum(jnp.sum(tile_start[:, None] >= ends[None, :], axis=1),
                              N_EXPERTS - 1).astype(I32)
    n_used = (ends[-1] // TME).astype(I32).reshape(1)
    tail_rows = jnp.where(padded > 0, ends - TME, -1).astype(I32)
    in_tile = jnp.sum((tile_expert[:, None] == jnp.arange(N_EXPERTS, dtype=I32)[None, :])
                      * (starts + counts)[None, :], axis=1) - tile_start
    tile_half = (in_tile <= TME // 2).astype(I32)
    return dest.reshape(2, TT), tile_expert, tile_half, n_used, tail_rows


def _gdn_layer(x, g_pre, w_in, conv_w, a_log, dt_bias, norm_w, w_out, state_conv, state_rec_all,
               layer_pair, g_post, mod_p, mod_s):
    w_main = w_in[:, :GDN_MAIN_DIM]
    w_ba = jnp.pad(w_in[:, GDN_MAIN_DIM:], ((0, 0), (0, LANES - 2 * GDN_V_HEADS)))
    qkvz = _proj_call(x, g_pre, mod_p, mod_s, w_main, jnp.zeros((GDN_MAIN_DIM,), F32), 1024)
    ba = _proj_call(x, g_pre, mod_p, mod_s, w_ba, jnp.zeros((LANES,), F32), LANES)

    tb = GDN_CHUNK * GDN_NB
    n_steps = SEQ // tb
    o_p, rec_p = _gdn_call(
        qkvz.reshape(TT // tb, tb, GDN_MAIN_DIM), ba.reshape(TT // tb, tb, LANES),
        conv_w, a_log, dt_bias, norm_w,
        jnp.zeros((BATCH, SUBLANES, GDN_CONV_DIM), F32),
        jnp.zeros((BATCH, GDN_V_HEADS, GDN_HD, GDN_HD), F32),
        n_seq=BATCH, n_steps=n_steps, chunk=GDN_CHUNK, n_chunks=GDN_NB, t_real=tb,
        n_par=GDN_PROMPT_PAR)
    o_p = o_p.reshape(NP_TOK, GDN_VAL_DIM)

    row_pad = ((0, 0), (0, GDN_SAMPLE_ROWS - DEC_SEQ), (0, 0))
    qkvz_s = qkvz[NP_TOK:].reshape(DEC_BATCH, DEC_SEQ, GDN_MAIN_DIM)
    ba_s = ba[NP_TOK:].reshape(DEC_BATCH, DEC_SEQ, LANES)
    cbuf_s = jnp.pad(state_conv, ((0, 0), (SUBLANES - (CONV_WIDTH - 1), 0), (0, 0)))
    o_s, rec_s = _gdn_call(
        jnp.pad(qkvz_s, row_pad), jnp.pad(ba_s, row_pad), conv_w, a_log, dt_bias, norm_w,
        cbuf_s, state_rec_all,
        n_seq=DEC_BATCH, n_steps=1, chunk=GDN_SAMPLE_ROWS, n_chunks=1, t_real=DEC_SEQ,
        n_par=GDN_SAMPLE_PAR, s0_first_seq=layer_pair * DEC_BATCH)
    o_s = o_s[:, 0, :DEC_SEQ].reshape(NS_TOK, GDN_VAL_DIM)

    keep = CONV_WIDTH - 1
    pre_p = jnp.stack([qkvz[(b + 1) * SEQ - keep:(b + 1) * SEQ, :GDN_CONV_DIM] for b in range(BATCH)])
    pre_s = jnp.concatenate([state_conv, qkvz_s[:, :, :GDN_CONV_DIM]], axis=1)[:, -keep:]
    x = _post_mm_call(x, o_p, o_s, w_out, jnp.zeros((D,), F32), g_post, mod_p, mod_s, 2)
    return x, pre_p, rec_p, pre_s, rec_s


def _swa_layer(x, g_pre, w_in, b_in, sinks, w_out, b_out, rel_bias, cache_k_all, cache_v_all,
               layer_pair, g_post, mod_p, mod_s):
    qkv = _proj_call(x, g_pre, mod_p, mod_s, w_in, b_in, SWA_PROJ_DIM)
    o_p = _swa_prompt_call(qkv, rel_bias, sinks)
    last = jnp.stack([qkv[(b + 1) * SEQ - WINDOW:(b + 1) * SEQ, SWA_Q_DIM:] for b in range(BATCH)])
    new_k_p = last[:, :, :SWA_KV_DIM]
    new_v_p = last[:, :, SWA_KV_DIM:]

    qkv_s = qkv[NP_TOK:].reshape(DEC_BATCH, DEC_SEQ, SWA_PROJ_DIM)
    q_st = qkv_s[:, :, :SWA_Q_DIM].reshape(DEC_BATCH, DEC_SEQ, SWA_KV_HEADS, SWA_GROUP, SWA_HD)
    q_st = q_st.transpose(0, 2, 3, 1, 4).reshape(DEC_BATCH, SWA_KV_HEADS, SWA_GROUP * DEC_SEQ, SWA_HD)
    row_pad = ((0, 0), (0, SWA_KPAD - WINDOW - DEC_SEQ), (0, 0))
    k_new = jnp.pad(qkv_s[:, :, SWA_Q_DIM:SWA_Q_DIM + SWA_KV_DIM], row_pad)
    v_new = jnp.pad(qkv_s[:, :, SWA_Q_DIM + SWA_KV_DIM:], row_pad)
    o_st, new_k_s, new_v_s = _swa_sample_call(q_st, cache_k_all, cache_v_all, k_new, v_new,
                                              layer_pair, rel_bias, sinks)
    o_s = o_st.reshape(DEC_BATCH, SWA_KV_HEADS, SWA_GROUP, DEC_SEQ, SWA_HD)
    o_s = o_s.transpose(0, 3, 1, 2, 4).reshape(NS_TOK, SWA_Q_DIM)

    x = _post_mm_call(x, o_p, o_s, w_out, b_out, g_post, mod_p, mod_s, 2)
    shape_p = (BATCH, WINDOW, SWA_KV_HEADS, SWA_HD)
    shape_s = (DEC_BATCH, WINDOW, SWA_KV_HEADS, SWA_HD)
    return (x, new_k_p.reshape(shape_p), new_v_p.reshape(shape_p),
            new_k_s.reshape(shape_s), new_v_s.reshape(shape_s))


def kernel(x_prompt, x_sample, c_prompt, c_sample, state_conv, state_rec, cache_win_k, cache_win_v, w_mod, b_mod, g_pre_mix, g_post_mix, g_pre_ffn, g_post_ffn, gdn_w_in, gdn_conv_w, gdn_a_log, gdn_dt_bias, gdn_norm_w, gdn_w_out, swa_w_in, swa_b_in, swa_sinks, swa_w_out, swa_b_out, rel_bias, ffn_w_gate, ffn_w_up, ffn_w_down, moe_w_router, moe_b_router, moe_w_gate, moe_w_up, moe_w_down):
    x = jnp.concatenate([x_prompt.reshape(NP_TOK, D), x_sample.reshape(NS_TOK, D)], axis=0)
    n_c = BATCH + DEC_BATCH
    c_all = jnp.concatenate([c_prompt, c_sample, jnp.zeros((-n_c % SUBLANES, D), F32)], axis=0)
    m_all = _mod_call(c_all, w_mod, b_mod)
    mod_s_all = jnp.repeat(m_all[:, BATCH:n_c], DEC_SEQ, axis=1)
    ffn_w = (ffn_w_gate, ffn_w_up, ffn_w_down)
    moe_w = (moe_w_gate.reshape(-1, D, D_FF), moe_w_up.reshape(-1, D, D_FF),
             moe_w_down.reshape(-1, D_FF, D))
    state_rec_all = state_rec.reshape((-1,) + state_rec.shape[2:])
    assert cache_win_k.shape[2] == WINDOW
    cache_k_all = cache_win_k.reshape(-1, WINDOW, SWA_KV_DIM)
    cache_v_all = cache_win_v.reshape(-1, WINDOW, SWA_KV_DIM)

    conv_p, rec_p, conv_s, rec_s = [], [], [], []
    k_p, v_p, k_s, v_s = [], [], [], []
    for layer in range(DEPTH):
        j = layer // 2
        mod_p = m_all[layer, :BATCH].reshape(BATCH, 1, 6 * D)
        mod_s = mod_s_all[layer]
        if layer % 2 == 0:
            x, cp, rp, cs, rs = _gdn_layer(
                x, g_pre_mix[layer], gdn_w_in[j], gdn_conv_w[j], gdn_a_log[j], gdn_dt_bias[j],
                gdn_norm_w[j], gdn_w_out[j], state_conv[j], state_rec_all, j, g_post_mix[layer],
                mod_p, mod_s)
            conv_p.append(cp); rec_p.append(rp); conv_s.append(cs); rec_s.append(rs)
        else:
            x, kp, vp, ks, vs = _swa_layer(
                x, g_pre_mix[layer], swa_w_in[j], swa_b_in[j], swa_sinks[j], swa_w_out[j],
                swa_b_out[j], rel_bias, cache_k_all, cache_v_all, j, g_post_mix[layer],
                mod_p, mod_s)
            k_p.append(kp); v_p.append(vp); k_s.append(ks); v_s.append(vs)

        if layer % 2 == 0:
            x = _dense_ffn_call(x, g_pre_ffn[layer], g_post_ffn[layer], mod_p, mod_s, *ffn_w, j)
        else:
            h, idx, wts = _prenorm_router_call(x, g_pre_ffn[layer], mod_p, mod_s, 3, 4,
                                               moe_w_router[j], moe_b_router[j])
            dest, tile_expert, tile_half, n_used, tail_rows = _route(idx[:, :2])
            xs = _dispatch_call(h, dest, tail_rows, n_used)
            ys = _ffn_call(xs, tile_expert + j * N_EXPERTS, tile_half, n_used, *moe_w)
            x = _combine_call(x, ys, dest, wts, g_post_ffn[layer], mod_p, mod_s, 5,
                              split=layer == DEPTH - 1)

    x_p, x_s = x if isinstance(x, (list, tuple)) else (x[:NP_TOK], x[NP_TOK:])
    y_prompt = x_p.reshape(BATCH, SEQ, D)
    y_sample = x_s.reshape(DEC_BATCH, DEC_SEQ, D)
    return (y_prompt, y_sample, jnp.stack(conv_p), jnp.stack(rec_p), jnp.stack(k_p), jnp.stack(v_p),
            jnp.stack(conv_s), jnp.stack(rec_s), jnp.stack(k_s), jnp.stack(v_s))
```

```python
import functools
import math

import numpy as np
import jax
import jax.numpy as jnp
from jax import lax
from jax.experimental import pallas as pl
from jax.experimental.pallas import tpu as pltpu

F32 = jnp.float32
BF16 = jnp.bfloat16
I32 = jnp.int32
HIGHEST = lax.Precision.HIGHEST

D = 1024
BATCH = 4
SEQ = 4096
DEPTH = 4
DEC_BATCH = 128
DEC_SEQ = 4
PAST_LEN = 8192
GDN_QK_HEADS = 4
GDN_V_HEADS = 8
GDN_HD = 128
GDN_KEY_DIM = GDN_QK_HEADS * GDN_HD
GDN_VAL_DIM = GDN_V_HEADS * GDN_HD
GDN_CONV_DIM = 2 * GDN_KEY_DIM + GDN_VAL_DIM
GDN_MAIN_DIM = GDN_CONV_DIM + GDN_VAL_DIM
CONV_WIDTH = 4
GDN_CHUNK = 64
SWA_Q_HEADS = 16
SWA_KV_HEADS = 4
SWA_GROUP = SWA_Q_HEADS // SWA_KV_HEADS
SWA_HD = 64
SWA_Q_DIM = SWA_Q_HEADS * SWA_HD
SWA_KV_DIM = SWA_KV_HEADS * SWA_HD
SWA_PROJ_DIM = SWA_Q_DIM + 2 * SWA_KV_DIM
WINDOW = 128
REL_BUCKETS = 32
REL_MAX_DIST = 128
D_FF = 2816
N_EXPERTS = 8
RMS_EPS = 1e-6

LANES = 128
SUBLANES = 8
VMEM_LIMIT = 56 * 1024 * 1024

TM = 512
NP_TOK = BATCH * SEQ
NS_TOK = DEC_BATCH * DEC_SEQ
TT = NP_TOK + NS_TOK
NP_TILES = NP_TOK // TM
NS_TILES = NS_TOK // TM
N_TILES = NP_TILES + NS_TILES
TILES_PER_SEQ = SEQ // TM
TF = 256
NF = D_FF // TF
TME = 1024
MOE_ROWS = 2 * TT + N_EXPERTS * TME
MOE_TILES = MOE_ROWS // TME
GDN_NB = 2
GDN_PROMPT_PAR = 4
GDN_SAMPLE_PAR = 4
GDN_SAMPLE_ROWS = 8
SWA_SB = 8
SWA_KPAD = WINDOW + 8

assert NP_TOK % TM == 0 and NS_TOK % TM == 0 and SEQ % TM == 0
assert D_FF % TF == 0 and DEC_SEQ <= GDN_SAMPLE_ROWS and DEC_SEQ <= 8


def _cparams(sem):
    return pltpu.CompilerParams(dimension_semantics=sem, vmem_limit_bytes=VMEM_LIMIT)


def _bdot(a, b):
    return jnp.dot(a.astype(BF16), b.astype(BF16), preferred_element_type=F32)


def _bdot_nt(a, b):
    return lax.dot_general(a.astype(BF16), b.astype(BF16), (((1,), (1,)), ((), ())),
                           preferred_element_type=F32)


def _silu(x):
    return x * jax.nn.sigmoid(x)


def _rms(x):
    return x * lax.rsqrt(jnp.mean(x * x, axis=-1, keepdims=True) + RMS_EPS)


def _mod_kernel(c_ref, w_ref, b_ref, o_ref):
    c = c_ref[...]
    o_ref[0] = _bdot(_silu(c), w_ref[0]) + b_ref[0]


def _mod_call(c_all, w_mod, b_mod):
    n = c_all.shape[0]
    tn = D
    return pl.pallas_call(
        _mod_kernel,
        grid=(DEPTH, 6 * D // tn),
        in_specs=[
            pl.BlockSpec((n, D), lambda l, j: (0, 0)),
            pl.BlockSpec((1, D, tn), lambda l, j: (l, 0, j)),
            pl.BlockSpec((1, 1, tn), lambda l, j: (l, 0, j)),
        ],
        out_specs=pl.BlockSpec((1, n, tn), lambda l, j: (l, 0, j)),
        out_shape=jax.ShapeDtypeStruct((DEPTH, n, 6 * D), F32),
        compiler_params=_cparams(("arbitrary", "arbitrary")),
        name="modulation",
    )(c_all, w_mod, b_mod.reshape(DEPTH, 1, 6 * D))


def _mod_specs(chunk, token_axis=0):
    p = pl.BlockSpec((1, 1, D), lambda *ids: (jnp.minimum(ids[token_axis] // TILES_PER_SEQ, BATCH - 1),
                                              0, chunk))
    s = pl.BlockSpec((TM, D), lambda *ids: (jnp.maximum(ids[token_axis] - NP_TILES, 0), chunk))
    return p, s


def _pick(i, p_ref, s_ref):
    return jnp.where(i < NP_TILES, p_ref[0], s_ref[...])


def _prenorm(i, x, g_ref, shp_ref, shs_ref, scp_ref, scs_ref):
    h = _rms(x) * g_ref[...]
    return h * (1.0 + _pick(i, scp_ref, scs_ref)) + _pick(i, shp_ref, shs_ref)


def _prenorm_router_kernel(x_ref, g_ref, shp_ref, shs_ref, scp_ref, scs_ref, wr_ref, br_ref,
                           h_ref, idx_ref, wt_ref):
    h = _prenorm(pl.program_id(0), x_ref[...], g_ref, shp_ref, shs_ref, scp_ref, scs_ref)
    h_ref[...] = h
    lane = lax.broadcasted_iota(I32, (TM, LANES), 1).astype(F32)
    logits = jnp.dot(h, wr_ref[...], precision=HIGHEST, preferred_element_type=F32) + br_ref[...]
    logits = jnp.where(lane < N_EXPERTS, logits, -jnp.inf)
    m1 = jnp.max(logits, axis=-1, keepdims=True)
    i1 = jnp.min(jnp.where(logits == m1, lane, float(LANES)), axis=-1, keepdims=True)
    rest = jnp.where(lane == i1, -jnp.inf, logits)
    m2 = jnp.max(rest, axis=-1, keepdims=True)
    i2 = jnp.min(jnp.where(rest == m2, lane, float(LANES)), axis=-1, keepdims=True)
    e2 = jnp.exp(m2 - m1)
    w1 = 1.0 / (1.0 + e2)
    w2 = e2 / (1.0 + e2)
    idx_ref[...] = jnp.where(lane == 0, i1, jnp.where(lane == 1, i2, 0.0)).astype(I32)
    wt_ref[...] = jnp.where(lane == 0, w1, jnp.where(lane == 1, w2, 0.0))


def _prenorm_router_call(x, g, mod_p, mod_s, sh_chunk, sc_chunk, w_router, b_router):
    shp, shs = _mod_specs(sh_chunk)
    scp, scs = _mod_specs(sc_chunk)
    tile = pl.BlockSpec((TM, D), lambda i: (i, 0))
    small = pl.BlockSpec((TM, LANES), lambda i: (i, 0))
    wr = jnp.pad(w_router, ((0, 0), (0, LANES - N_EXPERTS)))
    br = jnp.pad(b_router, (0, LANES - N_EXPERTS)).reshape(1, LANES)
    return pl.pallas_call(
        _prenorm_router_kernel,
        grid=(N_TILES,),
        in_specs=[tile, pl.BlockSpec((1, D), lambda i: (0, 0)), shp, shs, scp, scs,
                  pl.BlockSpec((D, LANES), lambda i: (0, 0)),
                  pl.BlockSpec((1, LANES), lambda i: (0, 0))],
        out_specs=[tile, small, small],
        out_shape=[jax.ShapeDtypeStruct((TT, D), F32),
                   jax.ShapeDtypeStruct((TT, LANES), I32),
                   jax.ShapeDtypeStruct((TT, LANES), F32)],
        compiler_params=_cparams(("arbitrary",)),
        name="prenorm_router",
    )(x, g.reshape(1, D), mod_p, mod_s, mod_p, mod_s, wr, br)


def _proj_kernel(x_ref, g_ref, shp_ref, shs_ref, scp_ref, scs_ref, w_ref, b_ref, o_ref, wbf_ref):
    i = pl.program_id(1)

    @pl.when(i == 0)
    def _():
        wbf_ref[...] = w_ref[...].astype(BF16)

    h = _prenorm(i, x_ref[...], g_ref, shp_ref, shs_ref, scp_ref, scs_ref).astype(BF16)
    o_ref[...] = jnp.dot(h, wbf_ref[...], preferred_element_type=F32) + b_ref[...]


def _proj_call(x, g, mod_p, mod_s, w, b, tn):
    k, n = w.shape
    shp, shs = _mod_specs(0, token_axis=1)
    scp, scs = _mod_specs(1, token_axis=1)
    return pl.pallas_call(
        _proj_kernel,
        grid=(n // tn, N_TILES),
        in_specs=[
            pl.BlockSpec((TM, k), lambda j, i: (i, 0)),
            pl.BlockSpec((1, k), lambda j, i: (0, 0)), shp, shs, scp, scs,
            pl.BlockSpec((k, tn), lambda j, i: (0, j)),
            pl.BlockSpec((1, tn), lambda j, i: (0, j)),
        ],
        out_specs=pl.BlockSpec((TM, tn), lambda j, i: (i, j)),
        out_shape=jax.ShapeDtypeStruct((TT, n), F32),
        scratch_shapes=[pltpu.VMEM((k, tn), BF16)],
        compiler_params=_cparams(("arbitrary", "arbitrary")),
        name="proj",
    )(x, g.reshape(1, k), mod_p, mod_s, mod_p, mod_s, w, b.reshape(1, n))


def _post_mm_kernel(x_ref, op_ref, os_ref, w_ref, b_ref, g_ref, gtp_ref, gts_ref, out_ref, wbf_ref):
    i = pl.program_id(0)

    @pl.when(i == 0)
    def _():
        wbf_ref[...] = w_ref[...].astype(BF16)

    o = jnp.where(i < NP_TILES, op_ref[...], os_ref[...]).astype(BF16)
    y = jnp.dot(o, wbf_ref[...], preferred_element_type=F32) + b_ref[...]
    out_ref[...] = x_ref[...] + _pick(i, gtp_ref, gts_ref) * (_rms(y) * g_ref[...])


def _post_mm_call(x, o_p, o_s, w, b, g, mod_p, mod_s, gate_chunk):
    gtp, gts = _mod_specs(gate_chunk)
    tile = pl.BlockSpec((TM, D), lambda i: (i, 0))
    row = pl.BlockSpec((1, D), lambda i: (0, 0))
    return pl.pallas_call(
        _post_mm_kernel,
        grid=(N_TILES,),
        in_specs=[tile,
                  pl.BlockSpec((TM, D), lambda i: (jnp.minimum(i, NP_TILES - 1), 0)),
                  pl.BlockSpec((TM, D), lambda i: (jnp.maximum(i - NP_TILES, 0), 0)),
                  pl.BlockSpec((D, D), lambda i: (0, 0)), row, row, gtp, gts],
        out_specs=tile,
        out_shape=jax.ShapeDtypeStruct((TT, D), F32),
        scratch_shapes=[pltpu.VMEM((D, D), BF16)],
        compiler_params=_cparams(("arbitrary",)),
        name="post_mm",
    )(x, o_p, o_s, w, b.reshape(1, D), g.reshape(1, D), mod_p, mod_s)


def _gdn_kernel(*refs, chunk, n_chunks, t_real, n_par):
    x_refs = refs[:n_par]
    ba_refs = refs[n_par:2 * n_par]
    (cw_ref, alog_ref, dt_ref, nw_ref, cb_ref, s0_ref, o_ref, sfin_ref, xbuf, s_scr) = refs[2 * n_par:]
    c_sz = chunk
    tb = chunk * n_chunks
    j = pl.program_id(1)
    rep = GDN_V_HEADS // GDN_QK_HEADS
    units = [(p, h) for p in range(n_par) for h in range(GDN_V_HEADS)]

    @pl.when(j == 0)
    def _():
        s_scr[...] = s0_ref[...]
        xbuf[:, 0:SUBLANES, :] = cb_ref[...]

    for p in range(n_par):
        xbuf[p, SUBLANES:SUBLANES + tb, :] = x_refs[p][:, 0:GDN_CONV_DIM]

    ii = lax.broadcasted_iota(I32, (c_sz, c_sz), 0)
    jj = lax.broadcasted_iota(I32, (c_sz, c_sz), 1)
    causal = ii >= jj
    strict = ii > jj
    tri = causal.astype(F32)
    eye_c = (ii == jj).astype(F32)
    pair_mask = [jnp.logical_and((ii >> (l + 1)) == (jj >> (l + 1)), (ii >> l) != (jj >> l))
                 for l in range(int(math.log2(c_sz)))]
    eye_t = (lax.broadcasted_iota(I32, (2 * SUBLANES, LANES), 0)
             == lax.broadcasted_iota(I32, (2 * SUBLANES, LANES), 1)).astype(F32)
    lane = lax.broadcasted_iota(I32, (c_sz, LANES), 1)
    row = lax.broadcasted_iota(I32, (c_sz, LANES), 0)
    cw = cw_ref[...]
    neg_a = -jnp.exp(alog_ref[...])
    dt = dt_ref[...]
    nw = nw_ref[...]

    def chunk_body(c, carry):
        r0 = pl.multiple_of(c * c_sz, c_sz)
        xc, beta, gc, rows_t = [], [], [], []
        for p in range(n_par):
            win = xbuf[p, pl.ds(r0, c_sz + SUBLANES), :]
            y = win[SUBLANES:SUBLANES + c_sz] * cw[3:4]
            for tap in range(1, CONV_WIDTH):
                y = y + win[SUBLANES - tap:SUBLANES - tap + c_sz] * cw[3 - tap:4 - tap]
            xc.append(_silu(y))
            ba = ba_refs[p][pl.ds(r0, c_sz), :]
            b_p = jax.nn.sigmoid(ba)
            g_p = neg_a * jax.nn.softplus(ba + dt)
            if t_real < tb:
                live = (row + r0) < t_real
                b_p = jnp.where(live, b_p, 0.0)
                g_p = jnp.where(live, g_p, 0.0)
            gc_p = jnp.dot(tri, g_p, precision=HIGHEST, preferred_element_type=F32)
            cols = jnp.where(lane < GDN_V_HEADS, b_p, gc_p)
            rows_t.append(lax.dot_general(eye_t, cols, (((1,), (1,)), ((), ())), precision=HIGHEST,
                                          preferred_element_type=F32))
            beta.append(b_p)
            gc.append(gc_p)

        qn, kn, kk, qk = {}, {}, {}, {}
        for p in range(n_par):
            for hq in range(GDN_QK_HEADS):
                q = xc[p][:, hq * GDN_HD:(hq + 1) * GDN_HD]
                k = xc[p][:, GDN_KEY_DIM + hq * GDN_HD:GDN_KEY_DIM + (hq + 1) * GDN_HD]
                q = q * lax.rsqrt(jnp.sum(q * q, axis=-1, keepdims=True) + RMS_EPS) * (GDN_HD ** -0.5)
                k = k * lax.rsqrt(jnp.sum(k * k, axis=-1, keepdims=True) + RMS_EPS)
                qn[p, hq], kn[p, hq] = q, k
        for key in qn:
            kk[key] = _bdot_nt(kn[key], kn[key])
            qk[key] = _bdot_nt(qn[key], kn[key])

        gcc, gcl, bcol, decay, a_mat, inv = {}, {}, {}, {}, {}, {}
        for (p, h) in units:
            gcc[p, h] = gc[p][:, GDN_V_HEADS + h:GDN_V_HEADS + h + 1]
            gcr = rows_t[p][GDN_V_HEADS + h:GDN_V_HEADS + h + 1, :]
            bcol[p, h] = beta[p][:, h:h + 1]
            gcl[p, h] = gcc[p, h][c_sz - 1:c_sz, :]
            decay[p, h] = jnp.exp(jnp.where(causal, gcc[p, h] - gcr, -jnp.inf))
            a_mat[p, h] = jnp.where(strict, bcol[p, h] * kk[p, h // rep] * decay[p, h], 0.0)
            inv[p, h] = eye_c - jnp.where(pair_mask[0], a_mat[p, h], 0.0)
        for lvl in range(1, len(pair_mask)):
            t1 = {u: _bdot(jnp.where(pair_mask[lvl], a_mat[u], 0.0), inv[u]) for u in units}
            t2 = {u: _bdot(inv[u], t1[u]) for u in units}
            inv = {u: inv[u] - t2[u] for u in units}

        egc = {u: jnp.exp(gcc[u]) for u in units}
        sol = {}
        for (p, h) in units:
            v = xc[p][:, 2 * GDN_KEY_DIM + h * GDN_HD:2 * GDN_KEY_DIM + (h + 1) * GDN_HD]
            rhs = jnp.concatenate([v * bcol[p, h], kn[p, h // rep] * (bcol[p, h] * egc[p, h])], axis=1)
            sol[p, h] = _bdot(inv[p, h], rhs)
        ws = {}
        for (p, h) in units:
            q_dec = qn[p, h // rep] * egc[p, h]
            ws[p, h] = _bdot(jnp.concatenate([sol[p, h][:, GDN_HD:], q_dec], axis=0), s_scr[p, h])
        u_new = {u: sol[u][:, :GDN_HD] - ws[u][:c_sz] for u in units}
        o_part, s_part = {}, {}
        for (p, h) in units:
            qkm = jnp.where(causal, qk[p, h // rep] * decay[p, h], 0.0)
            k_dec = kn[p, h // rep] * jnp.exp(gcl[p, h] - gcc[p, h])
            o_part[p, h] = _bdot(qkm, u_new[p, h])
            s_part[p, h] = lax.dot_general(k_dec.astype(BF16), u_new[p, h].astype(BF16),
                                           (((0,), (0,)), ((), ())), preferred_element_type=F32)
        for (p, h) in units:
            s_scr[p, h] = s_scr[p, h] * jnp.exp(gcl[p, h]) + s_part[p, h]
            o = ws[p, h][c_sz:] + o_part[p, h]
            z = x_refs[p][pl.ds(r0, c_sz), GDN_CONV_DIM + h * GDN_HD:GDN_CONV_DIM + (h + 1) * GDN_HD]
            o_ref[p, pl.ds(r0, c_sz), h * GDN_HD:(h + 1) * GDN_HD] = _rms(o) * nw * _silu(z)
        return carry

    lax.fori_loop(0, n_chunks, chunk_body, 0)
    xbuf[:, 0:SUBLANES, :] = xbuf[:, tb:tb + SUBLANES, :]

    @pl.when(j == pl.num_programs(1) - 1)
    def _():
        sfin_ref[...] = s_scr[...]


def _gdn_call(x, ba, conv_w, a_log, dt_bias, norm_w, conv_buf, s0, *, n_seq, n_steps, chunk,
              n_chunks, t_real, n_par, s0_first_seq=0):
    s0_block = s0_first_seq // n_par
    tb = chunk * n_chunks
    pad = LANES - 2 * GDN_V_HEADS
    alog_row = jnp.pad(a_log, (GDN_V_HEADS, pad)).reshape(1, LANES)
    dt_row = jnp.pad(dt_bias, (GDN_V_HEADS, pad)).reshape(1, LANES)

    def seq_block(width, p):
        return pl.BlockSpec((None, tb, width), lambda s, j: ((s * n_par + p) * n_steps + j, 0, 0))

    const = lambda s, j: (0, 0)
    state_spec = pl.BlockSpec((n_par, GDN_V_HEADS, GDN_HD, GDN_HD), lambda s, j: (s, 0, 0, 0))
    return pl.pallas_call(
        functools.partial(_gdn_kernel, chunk=chunk, n_chunks=n_chunks, t_real=t_real, n_par=n_par),
        grid=(n_seq // n_par, n_steps),
        in_specs=[seq_block(GDN_MAIN_DIM, p) for p in range(n_par)]
                 + [seq_block(LANES, p) for p in range(n_par)]
                 + [pl.BlockSpec((CONV_WIDTH, GDN_CONV_DIM), const),
                    pl.BlockSpec((1, LANES), const), pl.BlockSpec((1, LANES), const),
                    pl.BlockSpec((1, GDN_HD), const),
                    pl.BlockSpec((n_par, SUBLANES, GDN_CONV_DIM), lambda s, j: (s, 0, 0)),
                    pl.BlockSpec((n_par, GDN_V_HEADS, GDN_HD, GDN_HD),
                                 lambda s, j: (s + s0_block, 0, 0, 0))],
        out_specs=[pl.BlockSpec((n_par, None, tb, GDN_VAL_DIM), lambda s, j: (s, j, 0, 0)),
                   state_spec],
        out_shape=[jax.ShapeDtypeStruct((n_seq, n_steps, tb, GDN_VAL_DIM), F32),
                   jax.ShapeDtypeStruct((n_seq, GDN_V_HEADS, GDN_HD, GDN_HD), F32)],
        scratch_shapes=[pltpu.VMEM((n_par, tb + SUBLANES, GDN_CONV_DIM), F32),
                        pltpu.VMEM((n_par, GDN_V_HEADS, GDN_HD, GDN_HD), F32)],
        compiler_params=_cparams(("arbitrary", "arbitrary")),
        name="gdn",
    )(*([x] * n_par), *([ba] * n_par), conv_w, alog_row, dt_row, norm_w.reshape(1, GDN_HD),
      conv_buf, s0)


def _t5_bucket_np(dist):
    dist = np.maximum(dist, 0)
    max_exact = REL_BUCKETS // 2
    ratio = np.log(np.maximum(dist, max_exact).astype(np.float32) / np.float32(max_exact)) \
        / np.float32(math.log(REL_MAX_DIST / max_exact))
    large = max_exact + (ratio.astype(np.float32) * np.float32(REL_BUCKETS - max_exact)).astype(np.int32)
    return np.where(dist < max_exact, dist, np.minimum(large, REL_BUCKETS - 1)).astype(np.int32)


def _bucket_table(qpos, kpos, k_valid):
    dist = qpos[:, None] - kpos[None, :]
    ok = (dist >= 0) & (dist < WINDOW) & (kpos[None, :] >= 0) & k_valid[None, :]
    return np.where(ok, _t5_bucket_np(dist), -1).astype(np.int32)


def _bias_from_buckets(bkt, rb_ref, head):
    def body(b, acc):
        return jnp.where(bkt == b, rb_ref[b, head], acc)
    acc = lax.fori_loop(0, REL_BUCKETS, body, jnp.zeros(bkt.shape, F32))
    return jnp.where(bkt < 0, -jnp.inf, acc)


def _swa_prompt_kernel(rb_ref, sk_ref, bkt_ref, q_ref, kvp_ref, kvc_ref, o_ref, bias_scr):
    first = jnp.logical_and(pl.program_id(0) == 0, pl.program_id(1) == 0)
    n = pl.program_id(1)

    @pl.when(first)
    def _():
        bkt = bkt_ref[...]
        for h in range(SWA_Q_HEADS):
            kv, g = divmod(h, SWA_GROUP)
            bias_scr[kv, g * WINDOW:(g + 1) * WINDOW, :] = _bias_from_buckets(bkt, rb_ref, h)

    q = q_ref[...]
    scale = SWA_HD ** -0.5
    heads = range(SWA_KV_HEADS)
    s_prev, s_cur = [], []
    for kv in heads:
        qs = jnp.concatenate(
            [q[:, (kv * SWA_GROUP + g) * SWA_HD:(kv * SWA_GROUP + g + 1) * SWA_HD]
             for g in range(SWA_GROUP)], axis=0).astype(BF16)
        s_prev.append(_bdot_nt(qs, kvp_ref[:, kv * SWA_HD:(kv + 1) * SWA_HD]))
        s_cur.append(_bdot_nt(qs, kvc_ref[:, kv * SWA_HD:(kv + 1) * SWA_HD]))
    p_prev, p_cur, den = [], [], []
    for kv in heads:
        sp = s_prev[kv] * scale + bias_scr[kv, :, 0:WINDOW]
        sp = jnp.where(n > 0, sp, -jnp.inf)
        sc = s_cur[kv] * scale + bias_scr[kv, :, WINDOW:2 * WINDOW]
        sink = jnp.concatenate(
            [jnp.full((WINDOW, 1), sk_ref[kv * SWA_GROUP + g], F32) for g in range(SWA_GROUP)], axis=0)
        m = jnp.maximum(jnp.maximum(jnp.max(sp, axis=-1, keepdims=True),
                                    jnp.max(sc, axis=-1, keepdims=True)), sink)
        pp = jnp.exp(sp - m)
        pc = jnp.exp(sc - m)
        den.append(jnp.sum(pp, axis=-1, keepdims=True) + jnp.sum(pc, axis=-1, keepdims=True)
                   + jnp.exp(sink - m))
        p_prev.append(pp.astype(BF16))
        p_cur.append(pc.astype(BF16))
    outs = []
    for kv in heads:
        v_prev = kvp_ref[:, SWA_KV_DIM + kv * SWA_HD:SWA_KV_DIM + (kv + 1) * SWA_HD]
        v_cur = kvc_ref[:, SWA_KV_DIM + kv * SWA_HD:SWA_KV_DIM + (kv + 1) * SWA_HD]
        o = (_bdot(p_prev[kv], v_prev) + _bdot(p_cur[kv], v_cur)) / den[kv]
        outs.extend(o[g * WINDOW:(g + 1) * WINDOW] for g in range(SWA_GROUP))
    o_ref[...] = jnp.concatenate(outs, axis=1)


def _swa_prompt_call(qkv, rel_bias, sinks):
    nb = SEQ // WINDOW
    qpos = WINDOW + np.arange(WINDOW)
    kpos = np.arange(2 * WINDOW)
    bkt = jnp.asarray(_bucket_table(qpos, kpos, np.ones(2 * WINDOW, bool)))
    kv_col = SWA_Q_DIM // (2 * SWA_KV_DIM)
    smem = pl.BlockSpec(memory_space=pltpu.SMEM)
    return pl.pallas_call(
        _swa_prompt_kernel,
        grid=(BATCH, nb),
        in_specs=[smem, smem,
                  pl.BlockSpec((WINDOW, 2 * WINDOW), lambda b, n: (0, 0)),
                  pl.BlockSpec((WINDOW, SWA_Q_DIM), lambda b, n: (b * nb + n, 0)),
                  pl.BlockSpec((WINDOW, 2 * SWA_KV_DIM),
                               lambda b, n: (jnp.maximum(b * nb + n - 1, 0), kv_col)),
                  pl.BlockSpec((WINDOW, 2 * SWA_KV_DIM), lambda b, n: (b * nb + n, kv_col))],
        out_specs=pl.BlockSpec((WINDOW, SWA_Q_DIM), lambda b, n: (b * nb + n, 0)),
        out_shape=jax.ShapeDtypeStruct((NP_TOK, SWA_Q_DIM), F32),
        scratch_shapes=[pltpu.VMEM((SWA_KV_HEADS, SWA_GROUP * WINDOW, 2 * WINDOW), F32)],
        compiler_params=_cparams(("arbitrary", "arbitrary")),
        name="swa_prompt",
    )(rel_bias, sinks, bkt, qkv, qkv, qkv)


def _swa_sample_kernel(rb_ref, sk_ref, bkt_ref, q_ref, ck_ref, cv_ref, kn_ref, vn_ref,
                       o_ref, ck_out_ref, cv_out_ref, k_scr, v_scr, bias_scr, sink_scr):
    rows = SWA_GROUP * DEC_SEQ
    k_scr[:, 0:WINDOW, :] = ck_ref[...]
    k_scr[:, WINDOW:SWA_KPAD, :] = kn_ref[...]
    v_scr[:, 0:WINDOW, :] = cv_ref[...]
    v_scr[:, WINDOW:SWA_KPAD, :] = vn_ref[...]
    ck_out_ref[...] = k_scr[:, DEC_SEQ:DEC_SEQ + WINDOW, :]
    cv_out_ref[...] = v_scr[:, DEC_SEQ:DEC_SEQ + WINDOW, :]
    k_ref, v_ref = k_scr, v_scr

    def group_of_row(shape):
        r = lax.broadcasted_iota(I32, shape, 0)
        return sum((r >= g * DEC_SEQ).astype(I32) for g in range(1, SWA_GROUP))

    @pl.when(pl.program_id(0) == 0)
    def _():
        bkt = bkt_ref[...]
        grp = group_of_row((rows, 1))
        grp_full = group_of_row(bkt.shape)
        for kv in range(SWA_KV_HEADS):
            acc = jnp.zeros(bkt.shape, F32)
            snk = jnp.zeros((rows, 1), F32)
            for g in range(SWA_GROUP):
                head = kv * SWA_GROUP + g
                acc = jnp.where(grp_full == g, _bias_from_buckets(bkt, rb_ref, head), acc)
                snk = jnp.where(grp == g, sk_ref[head], snk)
            bias_scr[kv] = acc
            sink_scr[kv] = snk

    scale = SWA_HD ** -0.5
    units = [(s, kv) for s in range(SWA_SB) for kv in range(SWA_KV_HEADS)]
    scores = {(s, kv): _bdot_nt(q_ref[s, kv], k_ref[s, :, kv * SWA_HD:(kv + 1) * SWA_HD])
              for (s, kv) in units}
    probs, den = {}, {}
    for (s, kv) in units:
        sc = scores[s, kv] * scale + bias_scr[kv]
        sink = sink_scr[kv]
        m = jnp.maximum(jnp.max(sc, axis=-1, keepdims=True), sink)
        p = jnp.exp(sc - m)
        den[s, kv] = jnp.sum(p, axis=-1, keepdims=True) + jnp.exp(sink - m)
        probs[s, kv] = p.astype(BF16)
    for (s, kv) in units:
        o_ref[s, kv] = _bdot(probs[s, kv], v_ref[s, :, kv * SWA_HD:(kv + 1) * SWA_HD]) / den[s, kv]


def _swa_sample_call(q_st, cache_k, cache_v, k_new, v_new, layer_pair, rel_bias, sinks):
    rows = SWA_GROUP * DEC_SEQ
    n_keys = WINDOW + DEC_SEQ
    qpos = PAST_LEN + np.arange(DEC_SEQ)
    kpos = PAST_LEN - WINDOW + np.arange(SWA_KPAD)
    bkt4 = _bucket_table(qpos, kpos, np.arange(SWA_KPAD) < n_keys)
    bkt = jnp.asarray(np.tile(bkt4, (SWA_GROUP, 1)))
    steps = DEC_BATCH // SWA_SB
    smem = pl.BlockSpec(memory_space=pltpu.SMEM)
    q_spec = pl.BlockSpec((SWA_SB, SWA_KV_HEADS, rows, SWA_HD), lambda i: (i, 0, 0, 0))
    win_in = pl.BlockSpec((SWA_SB, WINDOW, SWA_KV_DIM), lambda i: (layer_pair * steps + i, 0, 0))
    win_out = pl.BlockSpec((SWA_SB, WINDOW, SWA_KV_DIM), lambda i: (i, 0, 0))
    new_spec = pl.BlockSpec((SWA_SB, SWA_KPAD - WINDOW, SWA_KV_DIM), lambda i: (i, 0, 0))
    win_shape = jax.ShapeDtypeStruct((DEC_BATCH, WINDOW, SWA_KV_DIM), F32)
    return pl.pallas_call(
        _swa_sample_kernel,
        grid=(steps,),
        in_specs=[smem, smem, pl.BlockSpec((rows, SWA_KPAD), lambda i: (0, 0)),
                  q_spec, win_in, win_in, new_spec, new_spec],
        out_specs=[q_spec, win_out, win_out],
        out_shape=[jax.ShapeDtypeStruct((DEC_BATCH, SWA_KV_HEADS, rows, SWA_HD), F32),
                   win_shape, win_shape],
        scratch_shapes=[pltpu.VMEM((SWA_SB, SWA_KPAD, SWA_KV_DIM), F32),
                        pltpu.VMEM((SWA_SB, SWA_KPAD, SWA_KV_DIM), F32),
                        pltpu.VMEM((SWA_KV_HEADS, rows, SWA_KPAD), F32),
                        pltpu.VMEM((SWA_KV_HEADS, rows, 1), F32)],
        compiler_params=_cparams(("arbitrary",)),
        name="swa_sample",
    )(rel_bias, sinks, bkt, q_st, cache_k, cache_v, k_new, v_new)


def _ffn_scratch(rows):
    return [pltpu.VMEM((rows, D), F32),
            pltpu.VMEM((NF, D, TF), BF16),
            pltpu.VMEM((NF, D, TF), BF16),
            pltpu.VMEM((NF, TF, D), BF16),
            pltpu.VMEM((2, D, TF), F32),
            pltpu.VMEM((2, D, TF), F32),
            pltpu.VMEM((2, TF, D), F32),
            pltpu.SemaphoreType.DMA((3, 2))]


def _swiglu_into(acc_ref, x, expert, load, w_hbm, res, stg, sem):
    wg_hbm, wu_hbm, wd_hbm = w_hbm
    res_g, res_u, res_d = res
    stg_g, stg_u, stg_d = stg

    def copies(f, slot):
        col = pl.multiple_of(f * TF, TF)
        return (pltpu.make_async_copy(wg_hbm.at[expert, :, pl.ds(col, TF)], stg_g.at[slot], sem.at[0, slot]),
                pltpu.make_async_copy(wu_hbm.at[expert, :, pl.ds(col, TF)], stg_u.at[slot], sem.at[1, slot]),
                pltpu.make_async_copy(wd_hbm.at[expert, pl.ds(col, TF), :], stg_d.at[slot], sem.at[2, slot]))

    def block(f):
        gate = jnp.dot(x, res_g[f], preferred_element_type=F32)
        up = jnp.dot(x, res_u[f], preferred_element_type=F32)
        act = (_silu(gate) * up).astype(BF16)
        acc_ref[...] += jnp.dot(act, res_d[f], preferred_element_type=F32)

    acc_ref[...] = jnp.zeros(acc_ref.shape, acc_ref.dtype)

    @pl.when(load)
    def _():
        for c in copies(0, 0):
            c.start()

        def body(f, carry):
            slot = lax.rem(f, 2)

            @pl.when(f + 1 < NF)
            def _():
                for c in copies(f + 1, 1 - slot):
                    c.start()

            for c in copies(f, slot):
                c.wait()
            res_g[f] = stg_g[slot].astype(BF16)
            res_u[f] = stg_u[slot].astype(BF16)
            res_d[f] = stg_d[slot].astype(BF16)
            block(f)
            return carry

        lax.fori_loop(0, NF, body, 0)

    @pl.when(jnp.logical_not(load))
    def _():
        def body(f, carry):
            block(f)
            return carry

        lax.fori_loop(0, NF, body, 0)


def _ffn_kernel(te_ref, ld_ref, hf_ref, nt_ref, x_ref, wg_hbm, wu_hbm, wd_hbm, o_ref, acc_ref, *scratch):
    t = pl.program_id(0)
    used = t < nt_ref[0]
    half = hf_ref[t] != 0
    half_rows = TME // 2

    def run(rows):
        _swiglu_into(acc_ref.at[pl.ds(0, rows)], x_ref[0:rows, :].astype(BF16), te_ref[t],
                     ld_ref[t] != 0, (wg_hbm, wu_hbm, wd_hbm), scratch[0:3], scratch[3:6], scratch[6])
        o_ref[0:rows, :] = acc_ref[0:rows, :]

    @pl.when(jnp.logical_and(used, jnp.logical_not(half)))
    def _():
        run(TME)

    @pl.when(jnp.logical_and(used, half))
    def _():
        run(half_rows)
        o_ref[half_rows:, :] = jnp.zeros((TME - half_rows, D), F32)

    @pl.when(jnp.logical_not(used))
    def _():
        o_ref[...] = jnp.zeros_like(o_ref)


def _ffn_call(x, tile_expert, tile_half, n_tiles_used, w_gate, w_up, w_down):
    n_rows = x.shape[0]
    n_tiles = n_rows // TME
    changed = jnp.concatenate([jnp.ones((1,), I32),
                               (tile_expert[1:] != tile_expert[:-1]).astype(I32)])
    hbm = pl.BlockSpec(memory_space=pl.ANY)
    grid_spec = pltpu.PrefetchScalarGridSpec(
        num_scalar_prefetch=4,
        grid=(n_tiles,),
        in_specs=[pl.BlockSpec((TME, D), lambda t, te, ld, hf, nt: (jnp.minimum(t, nt[0] - 1), 0)),
                  hbm, hbm, hbm],
        out_specs=pl.BlockSpec((TME, D), lambda t, *_: (t, 0)),
        scratch_shapes=_ffn_scratch(TME),
    )
    return pl.pallas_call(
        _ffn_kernel,
        grid_spec=grid_spec,
        out_shape=jax.ShapeDtypeStruct((n_rows, D), F32),
        compiler_params=_cparams(("arbitrary",)),
        name="ffn",
    )(tile_expert, changed, tile_half, n_tiles_used, x, w_gate, w_up, w_down)


def _dense_ffn_kernel(x_ref, gpre_ref, shp_ref, shs_ref, scp_ref, scs_ref, wg_hbm, wu_hbm, wd_hbm,
                      gpost_ref, gtp_ref, gts_ref, out_ref, acc_ref, *scratch, layer_pair):
    i = pl.program_id(0)
    x = x_ref[...]
    h = _prenorm(i, x, gpre_ref, shp_ref, shs_ref, scp_ref, scs_ref).astype(BF16)
    _swiglu_into(acc_ref, h, layer_pair, i == 0, (wg_hbm, wu_hbm, wd_hbm),
                 scratch[0:3], scratch[3:6], scratch[6])
    out_ref[...] = x + _pick(i, gtp_ref, gts_ref) * (_rms(acc_ref[...]) * gpost_ref[...])


def _dense_ffn_call(x, g_pre, g_post, mod_p, mod_s, w_gate, w_up, w_down, layer_pair):
    shp, shs = _mod_specs(3)
    scp, scs = _mod_specs(4)
    gtp, gts = _mod_specs(5)
    tile = pl.BlockSpec((TM, D), lambda i: (i, 0))
    row = pl.BlockSpec((1, D), lambda i: (0, 0))
    hbm = pl.BlockSpec(memory_space=pl.ANY)
    return pl.pallas_call(
        functools.partial(_dense_ffn_kernel, layer_pair=layer_pair),
        grid=(N_TILES,),
        in_specs=[tile, row, shp, shs, scp, scs, hbm, hbm, hbm, row, gtp, gts],
        out_specs=tile,
        out_shape=jax.ShapeDtypeStruct((TT, D), F32),
        scratch_shapes=_ffn_scratch(TM),
        compiler_params=_cparams(("arbitrary",)),
        name="dense_ffn",
    )(x, g_pre.reshape(1, D), mod_p, mod_s, mod_p, mod_s, w_gate, w_up, w_down,
      g_post.reshape(1, D), mod_p, mod_s)


DMA_UNROLL = 8


def _row_copy(src_ref, dst_ref, src_row, dst_row, sem):
    return pltpu.make_async_copy(src_ref.at[pl.ds(src_row, 1)], dst_ref.at[pl.ds(dst_row, 1)], sem)


def _dispatch_kernel(tail_ref, nt_ref, d0_ref, d1_ref, h_ref, out_ref, zero_ref, sem, zsem):
    @pl.when(pl.program_id(0) == 0)
    def _():
        zero_ref[...] = jnp.zeros_like(zero_ref)

        def fill_tile(row0):
            fill = pltpu.make_async_copy(zero_ref, out_ref.at[pl.ds(pl.multiple_of(row0, TME), TME)], zsem)
            fill.start()
            fill.wait()

        for e in range(N_EXPERTS):
            @pl.when(tail_ref[e] >= 0)
            def _():
                fill_tile(tail_ref[e])

            @pl.when(nt_ref[0] + e < MOE_TILES)
            def _():
                fill_tile((nt_ref[0] + e) * TME)

    def start(r, carry):
        _row_copy(h_ref, out_ref, r, d0_ref[0, 0, r], sem).start(priority=0)
        _row_copy(h_ref, out_ref, r, d1_ref[0, 0, r], sem).start(priority=1)
        return carry

    def wait(r, carry):
        _row_copy(h_ref, out_ref, 0, 0, sem).wait()
        _row_copy(h_ref, out_ref, 0, 0, sem).wait()
        return carry

    lax.fori_loop(0, TM, start, 0, unroll=DMA_UNROLL)
    lax.fori_loop(0, TM, wait, 0, unroll=DMA_UNROLL)


def _dispatch_call(h, dest, tail_rows, n_tiles_used):
    idx_spec = pl.BlockSpec((1, 1, TM), lambda i, *_: (i, 0, 0), memory_space=pltpu.SMEM)
    grid_spec = pltpu.PrefetchScalarGridSpec(
        num_scalar_prefetch=2,
        grid=(N_TILES,),
        in_specs=[idx_spec, idx_spec, pl.BlockSpec((TM, D), lambda i, *_: (i, 0))],
        out_specs=pl.BlockSpec(memory_space=pl.ANY),
        scratch_shapes=[pltpu.VMEM((TME, D), h.dtype), pltpu.SemaphoreType.DMA(()),
                        pltpu.SemaphoreType.DMA(())],
    )
    return pl.pallas_call(
        _dispatch_kernel,
        grid_spec=grid_spec,
        out_shape=jax.ShapeDtypeStruct((MOE_ROWS, D), h.dtype),
        compiler_params=_cparams(("arbitrary",)),
        name="moe_dispatch",
    )(tail_rows, n_tiles_used, dest[0].reshape(N_TILES, 1, TM), dest[1].reshape(N_TILES, 1, TM), h)


def _combine_kernel(d0_ref, d1_ref, x_ref, wt_ref, ys_ref, g_ref, gtp_ref, gts_ref, *rest, split):
    i = pl.program_id(0)
    buf, sem = rest[-2:]

    def start(r, carry):
        _row_copy(ys_ref, buf.at[0], d0_ref[0, 0, r], r, sem).start(priority=0)
        _row_copy(ys_ref, buf.at[1], d1_ref[0, 0, r], r, sem).start(priority=1)
        return carry

    def wait(r, carry):
        _row_copy(ys_ref, buf.at[0], 0, 0, sem).wait()
        _row_copy(ys_ref, buf.at[1], 0, 0, sem).wait()
        return carry

    lax.fori_loop(0, TM, start, 0, unroll=DMA_UNROLL)
    lax.fori_loop(0, TM, wait, 0, unroll=DMA_UNROLL)
    wt = wt_ref[...]
    y = wt[:, 0:1] * buf[0] + wt[:, 1:2] * buf[1]
    out = x_ref[...] + _pick(i, gtp_ref, gts_ref) * (_rms(y) * g_ref[...])
    if split:
        prompt_ref, sample_ref = rest[:2]

        @pl.when(i < NP_TILES)
        def _():
            prompt_ref[...] = out

        @pl.when(i >= NP_TILES)
        def _():
            sample_ref[...] = out
    else:
        rest[0][...] = out


def _combine_call(x, ys, dest, wts, g, mod_p, mod_s, gate_chunk, split):
    gtp, gts = _mod_specs(gate_chunk)
    tile = pl.BlockSpec((TM, D), lambda i: (i, 0))
    idx_spec = pl.BlockSpec((1, 1, TM), lambda i: (i, 0, 0), memory_space=pltpu.SMEM)
    if split:
        out_specs = [pl.BlockSpec((TM, D), lambda i: (jnp.minimum(i, NP_TILES - 1), 0)),
                     pl.BlockSpec((TM, D), lambda i: (jnp.maximum(i - NP_TILES, 0), 0))]
        out_shape = [jax.ShapeDtypeStruct((NP_TOK, D), F32), jax.ShapeDtypeStruct((NS_TOK, D), F32)]
    else:
        out_specs = tile
        out_shape = jax.ShapeDtypeStruct((TT, D), F32)
    return pl.pallas_call(
        functools.partial(_combine_kernel, split=split),
        grid=(N_TILES,),
        in_specs=[idx_spec, idx_spec, tile, pl.BlockSpec((TM, LANES), lambda i: (i, 0)),
                  pl.BlockSpec(memory_space=pl.ANY),
                  pl.BlockSpec((1, D), lambda i: (0, 0)), gtp, gts],
        out_specs=out_specs,
        out_shape=out_shape,
        scratch_shapes=[pltpu.VMEM((2, TM, D), F32), pltpu.SemaphoreType.DMA(())],
        compiler_params=_cparams(("arbitrary",)),
        name="moe_combine",
    )(dest[0].reshape(N_TILES, 1, TM), dest[1].reshape(N_TILES, 1, TM), x, wts, ys,
      g.reshape(1, D), mod_p, mod_s)


def _route(top_idx):
    flat_e = top_idx.T.reshape(-1)
    onehot = (flat_e[:, None] == jnp.arange(N_EXPERTS, dtype=I32)[None, :]).astype(I32)
    csum = jnp.cumsum(onehot, axis=0)
    rank = jnp.sum((csum - onehot) * onehot, axis=1)
    counts = csum[-1]
    padded = ((counts + TME - 1) // TME) * TME
    ends = jnp.cumsum(padded)
    starts = ends - padded
    dest = (jnp.sum(onehot * starts[None, :], axis=1) + rank).astype(I32)
    tile_start = jnp.arange(MOE_TILES, dtype=I32) * TME
    tile_expert = jnp.minimum(jnp.sum(tile_start[:, None] >= ends[None, :], axis=1),
                              N_EXPERTS - 1).astype(I32)
    n_used = (ends[-1] // TME).astype(I32).reshape(1)
    tail_rows = jnp.where(padded > 0, ends - TME, -1).astype(I32)
    in_tile = jnp.sum((tile_expert[:, None] == jnp.arange(N_EXPERTS, dtype=I32)[None, :])
                      * (starts + counts)[None, :], axis=1) - tile_start
    tile_half = (in_tile <= TME // 2).astype(I32)
    return dest.reshape(2, TT), tile_expert, tile_half, n_used, tail_rows


def _gdn_layer(x, g_pre, w_in, conv_w, a_log, dt_bias, norm_w, w_out, state_conv, state_rec_all,
               layer_pair, g_post, mod_p, mod_s):
    w_main = w_in[:, :GDN_MAIN_DIM]
    w_ba = jnp.pad(w_in[:, GDN_MAIN_DIM:], ((0, 0), (0, LANES - 2 * GDN_V_HEADS)))
    qkvz = _proj_call(x, g_pre, mod_p, mod_s, w_main, jnp.zeros((GDN_MAIN_DIM,), F32), 1024)
    ba = _proj_call(x, g_pre, mod_p, mod_s, w_ba, jnp.zeros((LANES,), F32), LANES)

    tb = GDN_CHUNK * GDN_NB
    n_steps = SEQ // tb
    o_p, rec_p = _gdn_call(
        qkvz.reshape(TT // tb, tb, GDN_MAIN_DIM), ba.reshape(TT // tb, tb, LANES),
        conv_w, a_log, dt_bias, norm_w,
        jnp.zeros((BATCH, SUBLANES, GDN_CONV_DIM), F32),
        jnp.zeros((BATCH, GDN_V_HEADS, GDN_HD, GDN_HD), F32),
        n_seq=BATCH, n_steps=n_steps, chunk=GDN_CHUNK, n_chunks=GDN_NB, t_real=tb,
        n_par=GDN_PROMPT_PAR)
    o_p = o_p.reshape(NP_TOK, GDN_VAL_DIM)

    row_pad = ((0, 0), (0, GDN_SAMPLE_ROWS - DEC_SEQ), (0, 0))
    qkvz_s = qkvz[NP_TOK:].reshape(DEC_BATCH, DEC_SEQ, GDN_MAIN_DIM)
    ba_s = ba[NP_TOK:].reshape(DEC_BATCH, DEC_SEQ, LANES)
    cbuf_s = jnp.pad(state_conv, ((0, 0), (SUBLANES - (CONV_WIDTH - 1), 0), (0, 0)))
    o_s, rec_s = _gdn_call(
        jnp.pad(qkvz_s, row_pad), jnp.pad(ba_s, row_pad), conv_w, a_log, dt_bias, norm_w,
        cbuf_s, state_rec_all,
        n_seq=DEC_BATCH, n_steps=1, chunk=GDN_SAMPLE_ROWS, n_chunks=1, t_real=DEC_SEQ,
        n_par=GDN_SAMPLE_PAR, s0_first_seq=layer_pair * DEC_BATCH)
    o_s = o_s[:, 0, :DEC_SEQ].reshape(NS_TOK, GDN_VAL_DIM)

    keep = CONV_WIDTH - 1
    pre_p = jnp.stack([qkvz[(b + 1) * SEQ - keep:(b + 1) * SEQ, :GDN_CONV_DIM] for b in range(BATCH)])
    pre_s = jnp.concatenate([state_conv, qkvz_s[:, :, :GDN_CONV_DIM]], axis=1)[:, -keep:]
    x = _post_mm_call(x, o_p, o_s, w_out, jnp.zeros((D,), F32), g_post, mod_p, mod_s, 2)
    return x, pre_p, rec_p, pre_s, rec_s


def _swa_layer(x, g_pre, w_in, b_in, sinks, w_out, b_out, rel_bias, cache_k_all, cache_v_all,
               layer_pair, g_post, mod_p, mod_s):
    qkv = _proj_call(x, g_pre, mod_p, mod_s, w_in, b_in, SWA_PROJ_DIM)
    o_p = _swa_prompt_call(qkv, rel_bias, sinks)
    last = jnp.stack([qkv[(b + 1) * SEQ - WINDOW:(b + 1) * SEQ, SWA_Q_DIM:] for b in range(BATCH)])
    new_k_p = last[:, :, :SWA_KV_DIM]
    new_v_p = last[:, :, SWA_KV_DIM:]

    qkv_s = qkv[NP_TOK:].reshape(DEC_BATCH, DEC_SEQ, SWA_PROJ_DIM)
    q_st = qkv_s[:, :, :SWA_Q_DIM].reshape(DEC_BATCH, DEC_SEQ, SWA_KV_HEADS, SWA_GROUP, SWA_HD)
    q_st = q_st.transpose(0, 2, 3, 1, 4).reshape(DEC_BATCH, SWA_KV_HEADS, SWA_GROUP * DEC_SEQ, SWA_HD)
    row_pad = ((0, 0), (0, SWA_KPAD - WINDOW - DEC_SEQ), (0, 0))
    k_new = jnp.pad(qkv_s[:, :, SWA_Q_DIM:SWA_Q_DIM + SWA_KV_DIM], row_pad)
    v_new = jnp.pad(qkv_s[:, :, SWA_Q_DIM + SWA_KV_DIM:], row_pad)
    o_st, new_k_s, new_v_s = _swa_sample_call(q_st, cache_k_all, cache_v_all, k_new, v_new,
                                              layer_pair, rel_bias, sinks)
    o_s = o_st.reshape(DEC_BATCH, SWA_KV_HEADS, SWA_GROUP, DEC_SEQ, SWA_HD)
    o_s = o_s.transpose(0, 3, 1, 2, 4).reshape(NS_TOK, SWA_Q_DIM)

    x = _post_mm_call(x, o_p, o_s, w_out, b_out, g_post, mod_p, mod_s, 2)
    shape_p = (BATCH, WINDOW, SWA_KV_HEADS, SWA_HD)
    shape_s = (DEC_BATCH, WINDOW, SWA_KV_HEADS, SWA_HD)
    return (x, new_k_p.reshape(shape_p), new_v_p.reshape(shape_p),
            new_k_s.reshape(shape_s), new_v_s.reshape(shape_s))


def kernel(x_prompt, x_sample, c_prompt, c_sample, state_conv, state_rec, cache_win_k, cache_win_v, w_mod, b_mod, g_pre_mix, g_post_mix, g_pre_ffn, g_post_ffn, gdn_w_in, gdn_conv_w, gdn_a_log, gdn_dt_bias, gdn_norm_w, gdn_w_out, swa_w_in, swa_b_in, swa_sinks, swa_w_out, swa_b_out, rel_bias, ffn_w_gate, ffn_w_up, ffn_w_down, moe_w_router, moe_b_router, moe_w_gate, moe_w_up, moe_w_down):
    x = jnp.concatenate([x_prompt.reshape(NP_TOK, D), x_sample.reshape(NS_TOK, D)], axis=0)
    n_c = BATCH + DEC_BATCH
    c_all = jnp.concatenate([c_prompt, c_sample, jnp.zeros((-n_c % SUBLANES, D), F32)], axis=0)
    m_all = _mod_call(c_all, w_mod, b_mod)
    mod_s_all = jnp.repeat(m_all[:, BATCH:n_c], DEC_SEQ, axis=1)
    ffn_w = (ffn_w_gate, ffn_w_up, ffn_w_down)
    moe_w = (moe_w_gate.reshape(-1, D, D_FF), moe_w_up.reshape(-1, D, D_FF),
             moe_w_down.reshape(-1, D_FF, D))
    state_rec_all = state_rec.reshape((-1,) + state_rec.shape[2:])
    assert cache_win_k.shape[2] == WINDOW
    cache_k_all = cache_win_k.reshape(-1, WINDOW, SWA_KV_DIM)
    cache_v_all = cache_win_v.reshape(-1, WINDOW, SWA_KV_DIM)

    conv_p, rec_p, conv_s, rec_s = [], [], [], []
    k_p, v_p, k_s, v_s = [], [], [], []
    for layer in range(DEPTH):
        j = layer // 2
        mod_p = m_all[layer, :BATCH].reshape(BATCH, 1, 6 * D)
        mod_s = mod_s_all[layer]
        if layer % 2 == 0:
            x, cp, rp, cs, rs = _gdn_layer(
                x, g_pre_mix[layer], gdn_w_in[j], gdn_conv_w[j], gdn_a_log[j], gdn_dt_bias[j],
                gdn_norm_w[j], gdn_w_out[j], state_conv[j], state_rec_all, j, g_post_mix[layer],
                mod_p, mod_s)
            conv_p.append(cp); rec_p.append(rp); conv_s.append(cs); rec_s.append(rs)
        else:
            x, kp, vp, ks, vs = _swa_layer(
                x, g_pre_mix[layer], swa_w_in[j], swa_b_in[j], swa_sinks[j], swa_w_out[j],
                swa_b_out[j], rel_bias, cache_k_all, cache_v_all, j, g_post_mix[layer],
                mod_p, mod_s)
            k_p.append(kp); v_p.append(vp); k_s.append(ks); v_s.append(vs)

        if layer % 2 == 0:
            x = _dense_ffn_call(x, g_pre_ffn[layer], g_post_ffn[layer], mod_p, mod_s, *ffn_w, j)
        else:
            h, idx, wts = _prenorm_router_call(x, g_pre_ffn[layer], mod_p, mod_s, 3, 4,
                                               moe_w_router[j], moe_b_router[j])
            dest, tile_expert, tile_half, n_used, tail_rows = _route(idx[:, :2])
            xs = _dispatch_call(h, dest, tail_rows, n_used)
            ys = _ffn_call(xs, tile_expert + j * N_EXPERTS, tile_half, n_used, *moe_w)
            x = _combine_call(x, ys, dest, wts, g_post_ffn[layer], mod_p, mod_s, 5,
                              split=layer == DEPTH - 1)

    x_p, x_s = x if isinstance(x, (list, tuple)) else (x[:NP_TOK], x[NP_TOK:])
    y_prompt = x_p.reshape(BATCH, SEQ, D)
    y_sample = x_s.reshape(DEC_BATCH, DEC_SEQ, D)
    return (y_prompt, y_sample, jnp.stack(conv_p), jnp.stack(rec_p), jnp.stack(k_p), jnp.stack(v_p),
            jnp.stack(conv_s), jnp.stack(rec_s), jnp.stack(k_s), jnp.stack(v_s))
```

```python
import functools
import math

import numpy as np
import jax
import jax.numpy as jnp
from jax import lax
from jax.experimental import pallas as pl
from jax.experimental.pallas import tpu as pltpu

F32 = jnp.float32
BF16 = jnp.bfloat16
I32 = jnp.int32
HIGHEST = lax.Precision.HIGHEST

D = 1024
BATCH = 4
SEQ = 4096
DEPTH = 4
DEC_BATCH = 128
DEC_SEQ = 4
PAST_LEN = 8192
GDN_QK_HEADS = 4
GDN_V_HEADS = 8
GDN_HD = 128
GDN_KEY_DIM = GDN_QK_HEADS * GDN_HD
GDN_VAL_DIM = GDN_V_HEADS * GDN_HD
GDN_CONV_DIM = 2 * GDN_KEY_DIM + GDN_VAL_DIM
GDN_MAIN_DIM = GDN_CONV_DIM + GDN_VAL_DIM
CONV_WIDTH = 4
GDN_CHUNK = 64
SWA_Q_HEADS = 16
SWA_KV_HEADS = 4
SWA_GROUP = SWA_Q_HEADS // SWA_KV_HEADS
SWA_HD = 64
SWA_Q_DIM = SWA_Q_HEADS * SWA_HD
SWA_KV_DIM = SWA_KV_HEADS * SWA_HD
SWA_PROJ_DIM = SWA_Q_DIM + 2 * SWA_KV_DIM
WINDOW = 128
REL_BUCKETS = 32
REL_MAX_DIST = 128
D_FF = 2816
N_EXPERTS = 8
RMS_EPS = 1e-6

LANES = 128
SUBLANES = 8
VMEM_LIMIT = 56 * 1024 * 1024

TM = 512
NP_TOK = BATCH * SEQ
NS_TOK = DEC_BATCH * DEC_SEQ
TT = NP_TOK + NS_TOK
NP_TILES = NP_TOK // TM
NS_TILES = NS_TOK // TM
N_TILES = NP_TILES + NS_TILES
TILES_PER_SEQ = SEQ // TM
TF = 256
NF = D_FF // TF
TME = 1024
MOE_ROWS = 2 * TT + N_EXPERTS * TME
MOE_TILES = MOE_ROWS // TME
GDN_NB = 2
GDN_PROMPT_PAR = 4
GDN_SAMPLE_PAR = 4
GDN_SAMPLE_ROWS = 8
SWA_SB = 8
SWA_KPAD = WINDOW + 8

assert NP_TOK % TM == 0 and NS_TOK % TM == 0 and SEQ % TM == 0
assert D_FF % TF == 0 and DEC_SEQ <= GDN_SAMPLE_ROWS and DEC_SEQ <= 8


def _cparams(sem):
    return pltpu.CompilerParams(dimension_semantics=sem, vmem_limit_bytes=VMEM_LIMIT)


def _bdot(a, b):
    return jnp.dot(a.astype(BF16), b.astype(BF16), preferred_element_type=F32)


def _bdot_nt(a, b):
    return lax.dot_general(a.astype(BF16), b.astype(BF16), (((1,), (1,)), ((), ())),
                           preferred_element_type=F32)


def _silu(x):
    return x * jax.nn.sigmoid(x)


def _rms(x):
    return x * lax.rsqrt(jnp.mean(x * x, axis=-1, keepdims=True) + RMS_EPS)


def _mod_kernel(c_ref, w_ref, b_ref, o_ref):
    c = c_ref[...]
    o_ref[0] = _bdot(_silu(c), w_ref[0]) + b_ref[0]


def _mod_call(c_all, w_mod, b_mod):
    n = c_all.shape[0]
    tn = D
    return pl.pallas_call(
        _mod_kernel,
        grid=(DEPTH, 6 * D // tn),
        in_specs=[
            pl.BlockSpec((n, D), lambda l, j: (0, 0)),
            pl.BlockSpec((1, D, tn), lambda l, j: (l, 0, j)),
            pl.BlockSpec((1, 1, tn), lambda l, j: (l, 0, j)),
        ],
        out_specs=pl.BlockSpec((1, n, tn), lambda l, j: (l, 0, j)),
        out_shape=jax.ShapeDtypeStruct((DEPTH, n, 6 * D), F32),
        compiler_params=_cparams(("arbitrary", "arbitrary")),
        name="modulation",
    )(c_all, w_mod, b_mod.reshape(DEPTH, 1, 6 * D))


def _mod_specs(chunk, token_axis=0):
    p = pl.BlockSpec((1, 1, D), lambda *ids: (jnp.minimum(ids[token_axis] // TILES_PER_SEQ, BATCH - 1),
                                              0, chunk))
    s = pl.BlockSpec((TM, D), lambda *ids: (jnp.maximum(ids[token_axis] - NP_TILES, 0), chunk))
    return p, s


def _pick(i, p_ref, s_ref):
    return jnp.where(i < NP_TILES, p_ref[0], s_ref[...])


def _prenorm(i, x, g_ref, shp_ref, shs_ref, scp_ref, scs_ref):
    h = _rms(x) * g_ref[...]
    return h * (1.0 + _pick(i, scp_ref, scs_ref)) + _pick(i, shp_ref, shs_ref)


def _prenorm_router_kernel(x_ref, g_ref, shp_ref, shs_ref, scp_ref, scs_ref, wr_ref, br_ref,
                           h_ref, idx_ref, wt_ref):
    h = _prenorm(pl.program_id(0), x_ref[...], g_ref, shp_ref, shs_ref, scp_ref, scs_ref)
    h_ref[...] = h
    lane = lax.broadcasted_iota(I32, (TM, LANES), 1).astype(F32)
    logits = jnp.dot(h, wr_ref[...], precision=HIGHEST, preferred_element_type=F32) + br_ref[...]
    logits = jnp.where(lane < N_EXPERTS, logits, -jnp.inf)
    m1 = jnp.max(logits, axis=-1, keepdims=True)
    i1 = jnp.min(jnp.where(logits == m1, lane, float(LANES)), axis=-1, keepdims=True)
    rest = jnp.where(lane == i1, -jnp.inf, logits)
    m2 = jnp.max(rest, axis=-1, keepdims=True)
    i2 = jnp.min(jnp.where(rest == m2, lane, float(LANES)), axis=-1, keepdims=True)
    e2 = jnp.exp(m2 - m1)
    w1 = 1.0 / (1.0 + e2)
    w2 = e2 / (1.0 + e2)
    idx_ref[...] = jnp.where(lane == 0, i1, jnp.where(lane == 1, i2, 0.0)).astype(I32)
    wt_ref[...] = jnp.where(lane == 0, w1, jnp.where(lane == 1, w2, 0.0))


def _prenorm_router_call(x, g, mod_p, mod_s, sh_chunk, sc_chunk, w_router, b_router):
    shp, shs = _mod_specs(sh_chunk)
    scp, scs = _mod_specs(sc_chunk)
    tile = pl.BlockSpec((TM, D), lambda i: (i, 0))
    small = pl.BlockSpec((TM, LANES), lambda i: (i, 0))
    wr = jnp.pad(w_router, ((0, 0), (0, LANES - N_EXPERTS)))
    br = jnp.pad(b_router, (0, LANES - N_EXPERTS)).reshape(1, LANES)
    return pl.pallas_call(
        _prenorm_router_kernel,
        grid=(N_TILES,),
        in_specs=[tile, pl.BlockSpec((1, D), lambda i: (0, 0)), shp, shs, scp, scs,
                  pl.BlockSpec((D, LANES), lambda i: (0, 0)),
                  pl.BlockSpec((1, LANES), lambda i: (0, 0))],
        out_specs=[tile, small, small],
        out_shape=[jax.ShapeDtypeStruct((TT, D), F32),
                   jax.ShapeDtypeStruct((TT, LANES), I32),
                   jax.ShapeDtypeStruct((TT, LANES), F32)],
        compiler_params=_cparams(("arbitrary",)),
        name="prenorm_router",
    )(x, g.reshape(1, D), mod_p, mod_s, mod_p, mod_s, wr, br)


def _proj_kernel(x_ref, g_ref, shp_ref, shs_ref, scp_ref, scs_ref, w_ref, b_ref, o_ref, wbf_ref):
    i = pl.program_id(1)

    @pl.when(i == 0)
    def _():
        wbf_ref[...] = w_ref[...].astype(BF16)

    h = _prenorm(i, x_ref[...], g_ref, shp_ref, shs_ref, scp_ref, scs_ref).astype(BF16)
    o_ref[...] = jnp.dot(h, wbf_ref[...], preferred_element_type=F32) + b_ref[...]


def _proj_call(x, g, mod_p, mod_s, w, b, tn):
    k, n = w.shape
    shp, shs = _mod_specs(0, token_axis=1)
    scp, scs = _mod_specs(1, token_axis=1)
    return pl.pallas_call(
        _proj_kernel,
        grid=(n // tn, N_TILES),
        in_specs=[
            pl.BlockSpec((TM, k), lambda j, i: (i, 0)),
            pl.BlockSpec((1, k), lambda j, i: (0, 0)), shp, shs, scp, scs,
            pl.BlockSpec((k, tn), lambda j, i: (0, j)),
            pl.BlockSpec((1, tn), lambda j, i: (0, j)),
        ],
        out_specs=pl.BlockSpec((TM, tn), lambda j, i: (i, j)),
        out_shape=jax.ShapeDtypeStruct((TT, n), F32),
        scratch_shapes=[pltpu.VMEM((k, tn), BF16)],
        compiler_params=_cparams(("arbitrary", "arbitrary")),
        name="proj",
    )(x, g.reshape(1, k), mod_p, mod_s, mod_p, mod_s, w, b.reshape(1, n))


def _post_mm_kernel(x_ref, op_ref, os_ref, w_ref, b_ref, g_ref, gtp_ref, gts_ref, out_ref, wbf_ref):
    i = pl.program_id(0)

    @pl.when(i == 0)
    def _():
        wbf_ref[...] = w_ref[...].astype(BF16)

    o = jnp.where(i < NP_TILES, op_ref[...], os_ref[...]).astype(BF16)
    y = jnp.dot(o, wbf_ref[...], preferred_element_type=F32) + b_ref[...]
    out_ref[...] = x_ref[...] + _pick(i, gtp_ref, gts_ref) * (_rms(y) * g_ref[...])


def _post_mm_call(x, o_p, o_s, w, b, g, mod_p, mod_s, gate_chunk):
    gtp, gts = _mod_specs(gate_chunk)
    tile = pl.BlockSpec((TM, D), lambda i: (i, 0))
    row = pl.BlockSpec((1, D), lambda i: (0, 0))
    return pl.pallas_call(
        _post_mm_kernel,
        grid=(N_TILES,),
        in_specs=[tile,
                  pl.BlockSpec((TM, D), lambda i: (jnp.minimum(i, NP_TILES - 1), 0)),
                  pl.BlockSpec((TM, D), lambda i: (jnp.maximum(i - NP_TILES, 0), 0)),
                  pl.BlockSpec((D, D), lambda i: (0, 0)), row, row, gtp, gts],
        out_specs=tile,
        out_shape=jax.ShapeDtypeStruct((TT, D), F32),
        scratch_shapes=[pltpu.VMEM((D, D), BF16)],
        compiler_params=_cparams(("arbitrary",)),
        name="post_mm",
    )(x, o_p, o_s, w, b.reshape(1, D), g.reshape(1, D), mod_p, mod_s)


def _gdn_kernel(*refs, chunk, n_chunks, t_real, n_par):
    x_refs = refs[:n_par]
    ba_refs = refs[n_par:2 * n_par]
    (cw_ref, alog_ref, dt_ref, nw_ref, cb_ref, s0_ref, o_ref, sfin_ref, xbuf, s_scr) = refs[2 * n_par:]
    c_sz = chunk
    tb = chunk * n_chunks
    j = pl.program_id(1)
    rep = GDN_V_HEADS // GDN_QK_HEADS
    units = [(p, h) for p in range(n_par) for h in range(GDN_V_HEADS)]

    @pl.when(j == 0)
    def _():
        s_scr[...] = s0_ref[...]
        xbuf[:, 0:SUBLANES, :] = cb_ref[...]

    for p in range(n_par):
        xbuf[p, SUBLANES:SUBLANES + tb, :] = x_refs[p][:, 0:GDN_CONV_DIM]

    ii = lax.broadcasted_iota(I32, (c_sz, c_sz), 0)
    jj = lax.broadcasted_iota(I32, (c_sz, c_sz), 1)
    causal = ii >= jj
    strict = ii > jj
    tri = causal.astype(F32)
    eye_c = (ii == jj).astype(F32)
    pair_mask = [jnp.logical_and((ii >> (l + 1)) == (jj >> (l + 1)), (ii >> l) != (jj >> l))
                 for l in range(int(math.log2(c_sz)))]
    eye_t = (lax.broadcasted_iota(I32, (2 * SUBLANES, LANES), 0)
             == lax.broadcasted_iota(I32, (2 * SUBLANES, LANES), 1)).astype(F32)
    lane = lax.broadcasted_iota(I32, (c_sz, LANES), 1)
    row = lax.broadcasted_iota(I32, (c_sz, LANES), 0)
    cw = cw_ref[...]
    neg_a = -jnp.exp(alog_ref[...])
    dt = dt_ref[...]
    nw = nw_ref[...]

    def chunk_body(c, carry):
        r0 = pl.multiple_of(c * c_sz, c_sz)
        xc, beta, gc, rows_t = [], [], [], []
        for p in range(n_par):
            win = xbuf[p, pl.ds(r0, c_sz + SUBLANES), :]
            y = win[SUBLANES:SUBLANES + c_sz] * cw[3:4]
            for tap in range(1, CONV_WIDTH):
                y = y + win[SUBLANES - tap:SUBLANES - tap + c_sz] * cw[3 - tap:4 - tap]
            xc.append(_silu(y))
            ba = ba_refs[p][pl.ds(r0, c_sz), :]
            b_p = jax.nn.sigmoid(ba)
            g_p = neg_a * jax.nn.softplus(ba + dt)
            if t_real < tb:
                live = (row + r0) < t_real
                b_p = jnp.where(live, b_p, 0.0)
                g_p = jnp.where(live, g_p, 0.0)
            gc_p = jnp.dot(tri, g_p, precision=HIGHEST, preferred_element_type=F32)
            cols = jnp.where(lane < GDN_V_HEADS, b_p, gc_p)
            rows_t.append(lax.dot_general(eye_t, cols, (((1,), (1,)), ((), ())), precision=HIGHEST,
                                          preferred_element_type=F32))
            beta.append(b_p)
            gc.append(gc_p)

        qn, kn, kk, qk = {}, {}, {}, {}
        for p in range(n_par):
            for hq in range(GDN_QK_HEADS):
                q = xc[p][:, hq * GDN_HD:(hq + 1) * GDN_HD]
                k = xc[p][:, GDN_KEY_DIM + hq * GDN_HD:GDN_KEY_DIM + (hq + 1) * GDN_HD]
                q = q * lax.rsqrt(jnp.sum(q * q, axis=-1, keepdims=True) + RMS_EPS) * (GDN_HD ** -0.5)
                k = k * lax.rsqrt(jnp.sum(k * k, axis=-1, keepdims=True) + RMS_EPS)
                qn[p, hq], kn[p, hq] = q, k
        for key in qn:
            kk[key] = _bdot_nt(kn[key], kn[key])
            qk[key] = _bdot_nt(qn[key], kn[key])

        gcc, gcl, bcol, decay, a_mat, inv = {}, {}, {}, {}, {}, {}
        for (p, h) in units:
            gcc[p, h] = gc[p][:, GDN_V_HEADS + h:GDN_V_HEADS + h + 1]
            gcr = rows_t[p][GDN_V_HEADS + h:GDN_V_HEADS + h + 1, :]
            bcol[p, h] = beta[p][:, h:h + 1]
            gcl[p, h] = gcc[p, h][c_sz - 1:c_sz, :]
            decay[p, h] = jnp.exp(jnp.where(causal, gcc[p, h] - gcr, -jnp.inf))
            a_mat[p, h] = jnp.where(strict, bcol[p, h] * kk[p, h // rep] * decay[p, h], 0.0)
            inv[p, h] = eye_c - jnp.where(pair_mask[0], a_mat[p, h], 0.0)
        for lvl in range(1, len(pair_mask)):
            t1 = {u: _bdot(jnp.where(pair_mask[lvl], a_mat[u], 0.0), inv[u]) for u in units}
            t2 = {u: _bdot(inv[u], t1[u]) for u in units}
            inv = {u: inv[u] - t2[u] for u in units}

        egc = {u: jnp.exp(gcc[u]) for u in units}
        sol = {}
        for (p, h) in units:
            v = xc[p][:, 2 * GDN_KEY_DIM + h * GDN_HD:2 * GDN_KEY_DIM + (h + 1) * GDN_HD]
            rhs = jnp.concatenate([v * bcol[p, h], kn[p, h // rep] * (bcol[p, h] * egc[p, h])], axis=1)
            sol[p, h] = _bdot(inv[p, h], rhs)
        ws = {}
        for (p, h) in units:
            q_dec = qn[p, h // rep] * egc[p, h]
            ws[p, h] = _bdot(jnp.concatenate([sol[p, h][:, GDN_HD:], q_dec], axis=0), s_scr[p, h])
        u_new = {u: sol[u][:, :GDN_HD] - ws[u][:c_sz] for u in units}
        o_part, s_part = {}, {}
        for (p, h) in units:
            qkm = jnp.where(causal, qk[p, h // rep] * decay[p, h], 0.0)
            k_dec = kn[p, h // rep] * jnp.exp(gcl[p, h] - gcc[p, h])
            o_part[p, h] = _bdot(qkm, u_new[p, h])
            s_part[p, h] = lax.dot_general(k_dec.astype(BF16), u_new[p, h].astype(BF16),
                                           (((0,), (0,)), ((), ())), preferred_element_type=F32)
        for (p, h) in units:
            s_scr[p, h] = s_scr[p, h] * jnp.exp(gcl[p, h]) + s_part[p, h]
            o = ws[p, h][c_sz:] + o_part[p, h]
            z = x_refs[p][pl.ds(r0, c_sz), GDN_CONV_DIM + h * GDN_HD:GDN_CONV_DIM + (h + 1) * GDN_HD]
            o_ref[p, pl.ds(r0, c_sz), h * GDN_HD:(h + 1) * GDN_HD] = _rms(o) * nw * _silu(z)
        return carry

    lax.fori_loop(0, n_chunks, chunk_body, 0)
    xbuf[:, 0:SUBLANES, :] = xbuf[:, tb:tb + SUBLANES, :]

    @pl.when(j == pl.num_programs(1) - 1)
    def _():
        sfin_ref[...] = s_scr[...]


def _gdn_call(x, ba, conv_w, a_log, dt_bias, norm_w, conv_buf, s0, *, n_seq, n_steps, chunk,
              n_chunks, t_real, n_par, s0_first_seq=0):
    s0_block = s0_first_seq // n_par
    tb = chunk * n_chunks
    pad = LANES - 2 * GDN_V_HEADS
    alog_row = jnp.pad(a_log, (GDN_V_HEADS, pad)).reshape(1, LANES)
    dt_row = jnp.pad(dt_bias, (GDN_V_HEADS, pad)).reshape(1, LANES)

    def seq_block(width, p):
        return pl.BlockSpec((None, tb, width), lambda s, j: ((s * n_par + p) * n_steps + j, 0, 0))

    const = lambda s, j: (0, 0)
    state_spec = pl.BlockSpec((n_par, GDN_V_HEADS, GDN_HD, GDN_HD), lambda s, j: (s, 0, 0, 0))
    return pl.pallas_call(
        functools.partial(_gdn_kernel, chunk=chunk, n_chunks=n_chunks, t_real=t_real, n_par=n_par),
        grid=(n_seq // n_par, n_steps),
        in_specs=[seq_block(GDN_MAIN_DIM, p) for p in range(n_par)]
                 + [seq_block(LANES, p) for p in range(n_par)]
                 + [pl.BlockSpec((CONV_WIDTH, GDN_CONV_DIM), const),
                    pl.BlockSpec((1, LANES), const), pl.BlockSpec((1, LANES), const),
                    pl.BlockSpec((1, GDN_HD), const),
                    pl.BlockSpec((n_par, SUBLANES, GDN_CONV_DIM), lambda s, j: (s, 0, 0)),
                    pl.BlockSpec((n_par, GDN_V_HEADS, GDN_HD, GDN_HD),
                                 lambda s, j: (s + s0_block, 0, 0, 0))],
        out_specs=[pl.BlockSpec((n_par, None, tb, GDN_VAL_DIM), lambda s, j: (s, j, 0, 0)),
                   state_spec],
        out_shape=[jax.ShapeDtypeStruct((n_seq, n_steps, tb, GDN_VAL_DIM), F32),
                   jax.ShapeDtypeStruct((n_seq, GDN_V_HEADS, GDN_HD, GDN_HD), F32)],
        scratch_shapes=[pltpu.VMEM((n_par, tb + SUBLANES, GDN_CONV_DIM), F32),
                        pltpu.VMEM((n_par, GDN_V_HEADS, GDN_HD, GDN_HD), F32)],
        compiler_params=_cparams(("arbitrary", "arbitrary")),
        name="gdn",
    )(*([x] * n_par), *([ba] * n_par), conv_w, alog_row, dt_row, norm_w.reshape(1, GDN_HD),
      conv_buf, s0)


def _t5_bucket_np(dist):
    dist = np.maximum(dist, 0)
    max_exact = REL_BUCKETS // 2
    ratio = np.log(np.maximum(dist, max_exact).astype(np.float32) / np.float32(max_exact)) \
        / np.float32(math.log(REL_MAX_DIST / max_exact))
    large = max_exact + (ratio.astype(np.float32) * np.float32(REL_BUCKETS - max_exact)).astype(np.int32)
    return np.where(dist < max_exact, dist, np.minimum(large, REL_BUCKETS - 1)).astype(np.int32)


def _bucket_table(qpos, kpos, k_valid):
    dist = qpos[:, None] - kpos[None, :]
    ok = (dist >= 0) & (dist < WINDOW) & (kpos[None, :] >= 0) & k_valid[None, :]
    return np.where(ok, _t5_bucket_np(dist), -1).astype(np.int32)


def _bias_from_buckets(bkt, rb_ref, head):
    def body(b, acc):
        return jnp.where(bkt == b, rb_ref[b, head], acc)
    acc = lax.fori_loop(0, REL_BUCKETS, body, jnp.zeros(bkt.shape, F32))
    return jnp.where(bkt < 0, -jnp.inf, acc)


def _swa_prompt_kernel(rb_ref, sk_ref, bkt_ref, q_ref, kvp_ref, kvc_ref, o_ref, bias_scr):
    first = jnp.logical_and(pl.program_id(0) == 0, pl.program_id(1) == 0)
    n = pl.program_id(1)

    @pl.when(first)
    def _():
        bkt = bkt_ref[...]
        for h in range(SWA_Q_HEADS):
            kv, g = divmod(h, SWA_GROUP)
            bias_scr[kv, g * WINDOW:(g + 1) * WINDOW, :] = _bias_from_buckets(bkt, rb_ref, h)

    q = q_ref[...]
    scale = SWA_HD ** -0.5
    heads = range(SWA_KV_HEADS)
    s_prev, s_cur = [], []
    for kv in heads:
        qs = jnp.concatenate(
            [q[:, (kv * SWA_GROUP + g) * SWA_HD:(kv * SWA_GROUP + g + 1) * SWA_HD]
             for g in range(SWA_GROUP)], axis=0).astype(BF16)
        s_prev.append(_bdot_nt(qs, kvp_ref[:, kv * SWA_HD:(kv + 1) * SWA_HD]))
        s_cur.append(_bdot_nt(qs, kvc_ref[:, kv * SWA_HD:(kv + 1) * SWA_HD]))
    p_prev, p_cur, den = [], [], []
    for kv in heads:
        sp = s_prev[kv] * scale + bias_scr[kv, :, 0:WINDOW]
        sp = jnp.where(n > 0, sp, -jnp.inf)
        sc = s_cur[kv] * scale + bias_scr[kv, :, WINDOW:2 * WINDOW]
        sink = jnp.concatenate(
            [jnp.full((WINDOW, 1), sk_ref[kv * SWA_GROUP + g], F32) for g in range(SWA_GROUP)], axis=0)
        m = jnp.maximum(jnp.maximum(jnp.max(sp, axis=-1, keepdims=True),
                                    jnp.max(sc, axis=-1, keepdims=True)), sink)
        pp = jnp.exp(sp - m)
        pc = jnp.exp(sc - m)
        den.append(jnp.sum(pp, axis=-1, keepdims=True) + jnp.sum(pc, axis=-1, keepdims=True)
                   + jnp.exp(sink - m))
        p_prev.append(pp.astype(BF16))
        p_cur.append(pc.astype(BF16))
    outs = []
    for kv in heads:
        v_prev = kvp_ref[:, SWA_KV_DIM + kv * SWA_HD:SWA_KV_DIM + (kv + 1) * SWA_HD]
        v_cur = kvc_ref[:, SWA_KV_DIM + kv * SWA_HD:SWA_KV_DIM + (kv + 1) * SWA_HD]
        o = (_bdot(p_prev[kv], v_prev) + _bdot(p_cur[kv], v_cur)) / den[kv]
        outs.extend(o[g * WINDOW:(g + 1) * WINDOW] for g in range(SWA_GROUP))
    o_ref[...] = jnp.concatenate(outs, axis=1)


def _swa_prompt_call(qkv, rel_bias, sinks):
    nb = SEQ // WINDOW
    qpos = WINDOW + np.arange(WINDOW)
    kpos = np.arange(2 * WINDOW)
    bkt = jnp.asarray(_bucket_table(qpos, kpos, np.ones(2 * WINDOW, bool)))
    kv_col = SWA_Q_DIM // (2 * SWA_KV_DIM)
    smem = pl.BlockSpec(memory_space=pltpu.SMEM)
    return pl.pallas_call(
        _swa_prompt_kernel,
        grid=(BATCH, nb),
        in_specs=[smem, smem,
                  pl.BlockSpec((WINDOW, 2 * WINDOW), lambda b, n: (0, 0)),
                  pl.BlockSpec((WINDOW, SWA_Q_DIM), lambda b, n: (b * nb + n, 0)),
                  pl.BlockSpec((WINDOW, 2 * SWA_KV_DIM),
                               lambda b, n: (jnp.maximum(b * nb + n - 1, 0), kv_col)),
                  pl.BlockSpec((WINDOW, 2 * SWA_KV_DIM), lambda b, n: (b * nb + n, kv_col))],
        out_specs=pl.BlockSpec((WINDOW, SWA_Q_DIM), lambda b, n: (b * nb + n, 0)),
        out_shape=jax.ShapeDtypeStruct((NP_TOK, SWA_Q_DIM), F32),
        scratch_shapes=[pltpu.VMEM((SWA_KV_HEADS, SWA_GROUP * WINDOW, 2 * WINDOW), F32)],
        compiler_params=_cparams(("arbitrary", "arbitrary")),
        name="swa_prompt",
    )(rel_bias, sinks, bkt, qkv, qkv, qkv)


def _swa_sample_kernel(rb_ref, sk_ref, bkt_ref, q_ref, ck_ref, cv_ref, kn_ref, vn_ref,
                       o_ref, ck_out_ref, cv_out_ref, k_scr, v_scr, bias_scr, sink_scr):
    rows = SWA_GROUP * DEC_SEQ
    k_scr[:, 0:WINDOW, :] = ck_ref[...]
    k_scr[:, WINDOW:SWA_KPAD, :] = kn_ref[...]
    v_scr[:, 0:WINDOW, :] = cv_ref[...]
    v_scr[:, WINDOW:SWA_KPAD, :] = vn_ref[...]
    ck_out_ref[...] = k_scr[:, DEC_SEQ:DEC_SEQ + WINDOW, :]
    cv_out_ref[...] = v_scr[:, DEC_SEQ:DEC_SEQ + WINDOW, :]
    k_ref, v_ref = k_scr, v_scr

    def group_of_row(shape):
        r = lax.broadcasted_iota(I32, shape, 0)
        return sum((r >= g * DEC_SEQ).astype(I32) for g in range(1, SWA_GROUP))

    @pl.when(pl.program_id(0) == 0)
    def _():
        bkt = bkt_ref[...]
        grp = group_of_row((rows, 1))
        grp_full = group_of_row(bkt.shape)
        for kv in range(SWA_KV_HEADS):
            acc = jnp.zeros(bkt.shape, F32)
            snk = jnp.zeros((rows, 1), F32)
            for g in range(SWA_GROUP):
                head = kv * SWA_GROUP + g
                acc = jnp.where(grp_full == g, _bias_from_buckets(bkt, rb_ref, head), acc)
                snk = jnp.where(grp == g, sk_ref[head], snk)
            bias_scr[kv] = acc
            sink_scr[kv] = snk

    scale = SWA_HD ** -0.5
    units = [(s, kv) for s in range(SWA_SB) for kv in range(SWA_KV_HEADS)]
    scores = {(s, kv): _bdot_nt(q_ref[s, kv], k_ref[s, :, kv * SWA_HD:(kv + 1) * SWA_HD])
              for (s, kv) in units}
    probs, den = {}, {}
    for (s, kv) in units:
        sc = scores[s, kv] * scale + bias_scr[kv]
        sink = sink_scr[kv]
        m = jnp.maximum(jnp.max(sc, axis=-1, keepdims=True), sink)
        p = jnp.exp(sc - m)
        den[s, kv] = jnp.sum(p, axis=-1, keepdims=True) + jnp.exp(sink - m)
        probs[s, kv] = p.astype(BF16)
    for (s, kv) in units:
        o_ref[s, kv] = _bdot(probs[s, kv], v_ref[s, :, kv * SWA_HD:(kv + 1) * SWA_HD]) / den[s, kv]


def _swa_sample_call(q_st, cache_k, cache_v, k_new, v_new, layer_pair, rel_bias, sinks):
    rows = SWA_GROUP * DEC_SEQ
    n_keys = WINDOW + DEC_SEQ
    qpos = PAST_LEN + np.arange(DEC_SEQ)
    kpos = PAST_LEN - WINDOW + np.arange(SWA_KPAD)
    bkt4 = _bucket_table(qpos, kpos, np.arange(SWA_KPAD) < n_keys)
    bkt = jnp.asarray(np.tile(bkt4, (SWA_GROUP, 1)))
    steps = DEC_BATCH // SWA_SB
    smem = pl.BlockSpec(memory_space=pltpu.SMEM)
    q_spec = pl.BlockSpec((SWA_SB, SWA_KV_HEADS, rows, SWA_HD), lambda i: (i, 0, 0, 0))
    win_in = pl.BlockSpec((SWA_SB, WINDOW, SWA_KV_DIM), lambda i: (layer_pair * steps + i, 0, 0))
    win_out = pl.BlockSpec((SWA_SB, WINDOW, SWA_KV_DIM), lambda i: (i, 0, 0))
    new_spec = pl.BlockSpec((SWA_SB, SWA_KPAD - WINDOW, SWA_KV_DIM), lambda i: (i, 0, 0))
    win_shape = jax.ShapeDtypeStruct((DEC_BATCH, WINDOW, SWA_KV_DIM), F32)
    return pl.pallas_call(
        _swa_sample_kernel,
        grid=(steps,),
        in_specs=[smem, smem, pl.BlockSpec((rows, SWA_KPAD), lambda i: (0, 0)),
                  q_spec, win_in, win_in, new_spec, new_spec],
        out_specs=[q_spec, win_out, win_out],
        out_shape=[jax.ShapeDtypeStruct((DEC_BATCH, SWA_KV_HEADS, rows, SWA_HD), F32),
                   win_shape, win_shape],
        scratch_shapes=[pltpu.VMEM((SWA_SB, SWA_KPAD, SWA_KV_DIM), F32),
                        pltpu.VMEM((SWA_SB, SWA_KPAD, SWA_KV_DIM), F32),
                        pltpu.VMEM((SWA_KV_HEADS, rows, SWA_KPAD), F32),
                        pltpu.VMEM((SWA_KV_HEADS, rows, 1), F32)],
        compiler_params=_cparams(("arbitrary",)),
        name="swa_sample",
    )(rel_bias, sinks, bkt, q_st, cache_k, cache_v, k_new, v_new)


def _ffn_scratch(rows):
    return [pltpu.VMEM((rows, D_FF), BF16),
            pltpu.VMEM((NF, D, TF), BF16),
            pltpu.VMEM((NF, D, TF), BF16),
            pltpu.VMEM((D_FF, D), BF16),
            pltpu.VMEM((2, D, TF), F32),
            pltpu.VMEM((2, D, TF), F32),
            pltpu.VMEM((2, TF, D), F32),
            pltpu.SemaphoreType.DMA((3, 2))]


def _swiglu(act_ref, x, expert, load, w_hbm, res, stg, sem):
    wg_hbm, wu_hbm, wd_hbm = w_hbm
    res_g, res_u, res_d = res
    stg_g, stg_u, stg_d = stg

    def copies(f, slot):
        col = pl.multiple_of(f * TF, TF)
        return (pltpu.make_async_copy(wg_hbm.at[expert, :, pl.ds(col, TF)], stg_g.at[slot], sem.at[0, slot]),
                pltpu.make_async_copy(wu_hbm.at[expert, :, pl.ds(col, TF)], stg_u.at[slot], sem.at[1, slot]),
                pltpu.make_async_copy(wd_hbm.at[expert, pl.ds(col, TF), :], stg_d.at[slot], sem.at[2, slot]))

    def block(f):
        col = pl.multiple_of(f * TF, TF)
        gate = jnp.dot(x, res_g[f], preferred_element_type=F32)
        up = jnp.dot(x, res_u[f], preferred_element_type=F32)
        act_ref[:, pl.ds(col, TF)] = (_silu(gate) * up).astype(BF16)

    @pl.when(load)
    def _():
        for c in copies(0, 0):
            c.start()

        def body(f, carry):
            slot = lax.rem(f, 2)

            @pl.when(f + 1 < NF)
            def _():
                for c in copies(f + 1, 1 - slot):
                    c.start()

            for c in copies(f, slot):
                c.wait()
            res_g[f] = stg_g[slot].astype(BF16)
            res_u[f] = stg_u[slot].astype(BF16)
            res_d[pl.ds(pl.multiple_of(f * TF, TF), TF), :] = stg_d[slot].astype(BF16)
            block(f)
            return carry

        lax.fori_loop(0, NF, body, 0)

    @pl.when(jnp.logical_not(load))
    def _():
        def body(f, carry):
            block(f)
            return carry

        lax.fori_loop(0, NF, body, 0)

    return jnp.dot(act_ref[...], res_d[...], preferred_element_type=F32)


def _ffn_kernel(te_ref, ld_ref, hf_ref, nt_ref, x_ref, wg_hbm, wu_hbm, wd_hbm, o_ref, act_ref, *scratch):
    t = pl.program_id(0)
    used = t < nt_ref[0]
    half = hf_ref[t] != 0
    half_rows = TME // 2

    def run(rows):
        o_ref[0:rows, :] = _swiglu(
            act_ref.at[pl.ds(0, rows)], x_ref[0:rows, :].astype(BF16), te_ref[t], ld_ref[t] != 0,
            (wg_hbm, wu_hbm, wd_hbm), scratch[0:3], scratch[3:6], scratch[6])

    @pl.when(jnp.logical_and(used, jnp.logical_not(half)))
    def _():
        run(TME)

    @pl.when(jnp.logical_and(used, half))
    def _():
        run(half_rows)
        o_ref[half_rows:, :] = jnp.zeros((TME - half_rows, D), F32)

    @pl.when(jnp.logical_not(used))
    def _():
        o_ref[...] = jnp.zeros_like(o_ref)


def _ffn_call(x, tile_expert, tile_half, n_tiles_used, w_gate, w_up, w_down):
    n_rows = x.shape[0]
    n_tiles = n_rows // TME
    changed = jnp.concatenate([jnp.ones((1,), I32),
                               (tile_expert[1:] != tile_expert[:-1]).astype(I32)])
    hbm = pl.BlockSpec(memory_space=pl.ANY)
    grid_spec = pltpu.PrefetchScalarGridSpec(
        num_scalar_prefetch=4,
        grid=(n_tiles,),
        in_specs=[pl.BlockSpec((TME, D), lambda t, te, ld, hf, nt: (jnp.minimum(t, nt[0] - 1), 0)),
                  hbm, hbm, hbm],
        out_specs=pl.BlockSpec((TME, D), lambda t, *_: (t, 0)),
        scratch_shapes=_ffn_scratch(TME),
    )
    return pl.pallas_call(
        _ffn_kernel,
        grid_spec=grid_spec,
        out_shape=jax.ShapeDtypeStruct((n_rows, D), F32),
        compiler_params=_cparams(("arbitrary",)),
        name="ffn",
    )(tile_expert, changed, tile_half, n_tiles_used, x, w_gate, w_up, w_down)


def _dense_ffn_kernel(x_ref, gpre_ref, shp_ref, shs_ref, scp_ref, scs_ref, wg_hbm, wu_hbm, wd_hbm,
                      gpost_ref, gtp_ref, gts_ref, out_ref, act_ref, *scratch, layer_pair):
    i = pl.program_id(0)
    x = x_ref[...]
    h = _prenorm(i, x, gpre_ref, shp_ref, shs_ref, scp_ref, scs_ref).astype(BF16)
    y = _swiglu(act_ref, h, layer_pair, i == 0, (wg_hbm, wu_hbm, wd_hbm),
                scratch[0:3], scratch[3:6], scratch[6])
    out_ref[...] = x + _pick(i, gtp_ref, gts_ref) * (_rms(y) * gpost_ref[...])


def _dense_ffn_call(x, g_pre, g_post, mod_p, mod_s, w_gate, w_up, w_down, layer_pair):
    shp, shs = _mod_specs(3)
    scp, scs = _mod_specs(4)
    gtp, gts = _mod_specs(5)
    tile = pl.BlockSpec((TM, D), lambda i: (i, 0))
    row = pl.BlockSpec((1, D), lambda i: (0, 0))
    hbm = pl.BlockSpec(memory_space=pl.ANY)
    return pl.pallas_call(
        functools.partial(_dense_ffn_kernel, layer_pair=layer_pair),
        grid=(N_TILES,),
        in_specs=[tile, row, shp, shs, scp, scs, hbm, hbm, hbm, row, gtp, gts],
        out_specs=tile,
        out_shape=jax.ShapeDtypeStruct((TT, D), F32),
        scratch_shapes=_ffn_scratch(TM),
        compiler_params=_cparams(("arbitrary",)),
        name="dense_ffn",
    )(x, g_pre.reshape(1, D), mod_p, mod_s, mod_p, mod_s, w_gate, w_up, w_down,
      g_post.reshape(1, D), mod_p, mod_s)


DMA_UNROLL = 8


def _row_copy(src_ref, dst_ref, src_row, dst_row, sem):
    return pltpu.make_async_copy(src_ref.at[pl.ds(src_row, 1)], dst_ref.at[pl.ds(dst_row, 1)], sem)


def _dispatch_kernel(tail_ref, nt_ref, d0_ref, d1_ref, h_ref, out_ref, zero_ref, sem, zsem):
    @pl.when(pl.program_id(0) == 0)
    def _():
        zero_ref[...] = jnp.zeros_like(zero_ref)

        def fill_tile(row0):
            fill = pltpu.make_async_copy(zero_ref, out_ref.at[pl.ds(pl.multiple_of(row0, TME), TME)], zsem)
            fill.start()
            fill.wait()

        for e in range(N_EXPERTS):
            @pl.when(tail_ref[e] >= 0)
            def _():
                fill_tile(tail_ref[e])

            @pl.when(nt_ref[0] + e < MOE_TILES)
            def _():
                fill_tile((nt_ref[0] + e) * TME)

    def start(r, carry):
        _row_copy(h_ref, out_ref, r, d0_ref[0, 0, r], sem).start(priority=0)
        _row_copy(h_ref, out_ref, r, d1_ref[0, 0, r], sem).start(priority=1)
        return carry

    def wait(r, carry):
        _row_copy(h_ref, out_ref, 0, 0, sem).wait()
        _row_copy(h_ref, out_ref, 0, 0, sem).wait()
        return carry

    lax.fori_loop(0, TM, start, 0, unroll=DMA_UNROLL)
    lax.fori_loop(0, TM, wait, 0, unroll=DMA_UNROLL)


def _dispatch_call(h, dest, tail_rows, n_tiles_used):
    idx_spec = pl.BlockSpec((1, 1, TM), lambda i, *_: (i, 0, 0), memory_space=pltpu.SMEM)
    grid_spec = pltpu.PrefetchScalarGridSpec(
        num_scalar_prefetch=2,
        grid=(N_TILES,),
        in_specs=[idx_spec, idx_spec, pl.BlockSpec((TM, D), lambda i, *_: (i, 0))],
        out_specs=pl.BlockSpec(memory_space=pl.ANY),
        scratch_shapes=[pltpu.VMEM((TME, D), h.dtype), pltpu.SemaphoreType.DMA(()),
                        pltpu.SemaphoreType.DMA(())],
    )
    return pl.pallas_call(
        _dispatch_kernel,
        grid_spec=grid_spec,
        out_shape=jax.ShapeDtypeStruct((MOE_ROWS, D), h.dtype),
        compiler_params=_cparams(("arbitrary",)),
        name="moe_dispatch",
    )(tail_rows, n_tiles_used, dest[0].reshape(N_TILES, 1, TM), dest[1].reshape(N_TILES, 1, TM), h)


def _combine_kernel(d0_ref, d1_ref, x_ref, wt_ref, ys_ref, g_ref, gtp_ref, gts_ref, *rest, split):
    i = pl.program_id(0)
    buf, sem = rest[-2:]

    def start(r, carry):
        _row_copy(ys_ref, buf.at[0], d0_ref[0, 0, r], r, sem).start(priority=0)
        _row_copy(ys_ref, buf.at[1], d1_ref[0, 0, r], r, sem).start(priority=1)
        return carry

    def wait(r, carry):
        _row_copy(ys_ref, buf.at[0], 0, 0, sem).wait()
        _row_copy(ys_ref, buf.at[1], 0, 0, sem).wait()
        return carry

    lax.fori_loop(0, TM, start, 0, unroll=DMA_UNROLL)
    lax.fori_loop(0, TM, wait, 0, unroll=DMA_UNROLL)
    wt = wt_ref[...]
    y = wt[:, 0:1] * buf[0] + wt[:, 1:2] * buf[1]
    out = x_ref[...] + _pick(i, gtp_ref, gts_ref) * (_rms(y) * g_ref[...])
    if split:
        prompt_ref, sample_ref = rest[:2]

        @pl.when(i < NP_TILES)
        def _():
            prompt_ref[...] = out

        @pl.when(i >= NP_TILES)
        def _():
            sample_ref[...] = out
    else:
        rest[0][...] = out


def _combine_call(x, ys, dest, wts, g, mod_p, mod_s, gate_chunk, split):
    gtp, gts = _mod_specs(gate_chunk)
    tile = pl.BlockSpec((TM, D), lambda i: (i, 0))
    idx_spec = pl.BlockSpec((1, 1, TM), lambda i: (i, 0, 0), memory_space=pltpu.SMEM)
    if split:
        out_specs = [pl.BlockSpec((TM, D), lambda i: (jnp.minimum(i, NP_TILES - 1), 0)),
                     pl.BlockSpec((TM, D), lambda i: (jnp.maximum(i - NP_TILES, 0), 0))]
        out_shape = [jax.ShapeDtypeStruct((NP_TOK, D), F32), jax.ShapeDtypeStruct((NS_TOK, D), F32)]
    else:
        out_specs = tile
        out_shape = jax.ShapeDtypeStruct((TT, D), F32)
    return pl.pallas_call(
        functools.partial(_combine_kernel, split=split),
        grid=(N_TILES,),
        in_specs=[idx_spec, idx_spec, tile, pl.BlockSpec((TM, LANES), lambda i: (i, 0)),
                  pl.BlockSpec(memory_space=pl.ANY),
                  pl.BlockSpec((1, D), lambda i: (0, 0)), gtp, gts],
        out_specs=out_specs,
        out_shape=out_shape,
        scratch_shapes=[pltpu.VMEM((2, TM, D), F32), pltpu.SemaphoreType.DMA(())],
        compiler_params=_cparams(("arbitrary",)),
        name="moe_combine",
    )(dest[0].reshape(N_TILES, 1, TM), dest[1].reshape(N_TILES, 1, TM), x, wts, ys,
      g.reshape(1, D), mod_p, mod_s)


def _route(top_idx):
    flat_e = top_idx.T.reshape(-1)
    onehot = (flat_e[:, None] == jnp.arange(N_EXPERTS, dtype=I32)[None, :]).astype(I32)
    csum = jnp.cumsum(onehot, axis=0)
    rank = jnp.sum((csum - onehot) * onehot, axis=1)
    counts = csum[-1]
    padded = ((counts + TME - 1) // TME) * TME
    ends = jnp.cumsum(padded)
    starts = ends - padded
    dest = (jnp.sum(onehot * starts[None, :], axis=1) + rank).astype(I32)
    tile_start = jnp.arange(MOE_TILES, dtype=I32) * TME
    tile_expert = jnp.minimum(jnp.sum(tile_start[:, None] >= ends[None, :], axis=1),
                              N_EXPERTS - 1).astype(I32)
    n_used = (ends[-1] // TME).astype(I32).reshape(1)
    tail_rows = jnp.where(padded > 0, ends - TME, -1).astype(I32)
    in_tile = jnp.sum((tile_expert[:, None] == jnp.arange(N_EXPERTS, dtype=I32)[None, :])
                      * (starts + counts)[None, :], axis=1) - tile_start
    tile_half = (in_tile <= TME // 2).astype(I32)
    return dest.reshape(2, TT), tile_expert, tile_half, n_used, tail_rows


def _gdn_layer(x, g_pre, w_in, conv_w, a_log, dt_bias, norm_w, w_out, state_conv, state_rec_all,
               layer_pair, g_post, mod_p, mod_s):
    w_main = w_in[:, :GDN_MAIN_DIM]
    w_ba = jnp.pad(w_in[:, GDN_MAIN_DIM:], ((0, 0), (0, LANES - 2 * GDN_V_HEADS)))
    qkvz = _proj_call(x, g_pre, mod_p, mod_s, w_main, jnp.zeros((GDN_MAIN_DIM,), F32), 1024)
    ba = _proj_call(x, g_pre, mod_p, mod_s, w_ba, jnp.zeros((LANES,), F32), LANES)

    tb = GDN_CHUNK * GDN_NB
    n_steps = SEQ // tb
    o_p, rec_p = _gdn_call(
        qkvz.reshape(TT // tb, tb, GDN_MAIN_DIM), ba.reshape(TT // tb, tb, LANES),
        conv_w, a_log, dt_bias, norm_w,
        jnp.zeros((BATCH, SUBLANES, GDN_CONV_DIM), F32),
        jnp.zeros((BATCH, GDN_V_HEADS, GDN_HD, GDN_HD), F32),
        n_seq=BATCH, n_steps=n_steps, chunk=GDN_CHUNK, n_chunks=GDN_NB, t_real=tb,
        n_par=GDN_PROMPT_PAR)
    o_p = o_p.reshape(NP_TOK, GDN_VAL_DIM)

    row_pad = ((0, 0), (0, GDN_SAMPLE_ROWS - DEC_SEQ), (0, 0))
    qkvz_s = qkvz[NP_TOK:].reshape(DEC_BATCH, DEC_SEQ, GDN_MAIN_DIM)
    ba_s = ba[NP_TOK:].reshape(DEC_BATCH, DEC_SEQ, LANES)
    cbuf_s = jnp.pad(state_conv, ((0, 0), (SUBLANES - (CONV_WIDTH - 1), 0), (0, 0)))
    o_s, rec_s = _gdn_call(
        jnp.pad(qkvz_s, row_pad), jnp.pad(ba_s, row_pad), conv_w, a_log, dt_bias, norm_w,
        cbuf_s, state_rec_all,
        n_seq=DEC_BATCH, n_steps=1, chunk=GDN_SAMPLE_ROWS, n_chunks=1, t_real=DEC_SEQ,
        n_par=GDN_SAMPLE_PAR, s0_first_seq=layer_pair * DEC_BATCH)
    o_s = o_s[:, 0, :DEC_SEQ].reshape(NS_TOK, GDN_VAL_DIM)

    keep = CONV_WIDTH - 1
    pre_p = jnp.stack([qkvz[(b + 1) * SEQ - keep:(b + 1) * SEQ, :GDN_CONV_DIM] for b in range(BATCH)])
    pre_s = jnp.concatenate([state_conv, qkvz_s[:, :, :GDN_CONV_DIM]], axis=1)[:, -keep:]
    x = _post_mm_call(x, o_p, o_s, w_out, jnp.zeros((D,), F32), g_post, mod_p, mod_s, 2)
    return x, pre_p, rec_p, pre_s, rec_s


def _swa_layer(x, g_pre, w_in, b_in, sinks, w_out, b_out, rel_bias, cache_k_all, cache_v_all,
               layer_pair, g_post, mod_p, mod_s):
    qkv = _proj_call(x, g_pre, mod_p, mod_s, w_in, b_in, SWA_PROJ_DIM)
    o_p = _swa_prompt_call(qkv, rel_bias, sinks)
    last = jnp.stack([qkv[(b + 1) * SEQ - WINDOW:(b + 1) * SEQ, SWA_Q_DIM:] for b in range(BATCH)])
    new_k_p = last[:, :, :SWA_KV_DIM]
    new_v_p = last[:, :, SWA_KV_DIM:]

    qkv_s = qkv[NP_TOK:].reshape(DEC_BATCH, DEC_SEQ, SWA_PROJ_DIM)
    q_st = qkv_s[:, :, :SWA_Q_DIM].reshape(DEC_BATCH, DEC_SEQ, SWA_KV_HEADS, SWA_GROUP, SWA_HD)
    q_st = q_st.transpose(0, 2, 3, 1, 4).reshape(DEC_BATCH, SWA_KV_HEADS, SWA_GROUP * DEC_SEQ, SWA_HD)
    row_pad = ((0, 0), (0, SWA_KPAD - WINDOW - DEC_SEQ), (0, 0))
    k_new = jnp.pad(qkv_s[:, :, SWA_Q_DIM:SWA_Q_DIM + SWA_KV_DIM], row_pad)
    v_new = jnp.pad(qkv_s[:, :, SWA_Q_DIM + SWA_KV_DIM:], row_pad)
    o_st, new_k_s, new_v_s = _swa_sample_call(q_st, cache_k_all, cache_v_all, k_new, v_new,
                                              layer_pair, rel_bias, sinks)
    o_s = o_st.reshape(DEC_BATCH, SWA_KV_HEADS, SWA_GROUP, DEC_SEQ, SWA_HD)
    o_s = o_s.transpose(0, 3, 1, 2, 4).reshape(NS_TOK, SWA_Q_DIM)

    x = _post_mm_call(x, o_p, o_s, w_out, b_out, g_post, mod_p, mod_s, 2)
    shape_p = (BATCH, WINDOW, SWA_KV_HEADS, SWA_HD)
    shape_s = (DEC_BATCH, WINDOW, SWA_KV_HEADS, SWA_HD)
    return (x, new_k_p.reshape(shape_p), new_v_p.reshape(shape_p),
            new_k_s.reshape(shape_s), new_v_s.reshape(shape_s))


def kernel(x_prompt, x_sample, c_prompt, c_sample, state_conv, state_rec, cache_win_k, cache_win_v, w_mod, b_mod, g_pre_mix, g_post_mix, g_pre_ffn, g_post_ffn, gdn_w_in, gdn_conv_w, gdn_a_log, gdn_dt_bias, gdn_norm_w, gdn_w_out, swa_w_in, swa_b_in, swa_sinks, swa_w_out, swa_b_out, rel_bias, ffn_w_gate, ffn_w_up, ffn_w_down, moe_w_router, moe_b_router, moe_w_gate, moe_w_up, moe_w_down):
    x = jnp.concatenate([x_prompt.reshape(NP_TOK, D), x_sample.reshape(NS_TOK, D)], axis=0)
    n_c = BATCH + DEC_BATCH
    c_all = jnp.concatenate([c_prompt, c_sample, jnp.zeros((-n_c % SUBLANES, D), F32)], axis=0)
    m_all = _mod_call(c_all, w_mod, b_mod)
    mod_s_all = jnp.repeat(m_all[:, BATCH:n_c], DEC_SEQ, axis=1)
    ffn_w = (ffn_w_gate, ffn_w_up, ffn_w_down)
    moe_w = (moe_w_gate.reshape(-1, D, D_FF), moe_w_up.reshape(-1, D, D_FF),
             moe_w_down.reshape(-1, D_FF, D))
    state_rec_all = state_rec.reshape((-1,) + state_rec.shape[2:])
    assert cache_win_k.shape[2] == WINDOW
    cache_k_all = cache_win_k.reshape(-1, WINDOW, SWA_KV_DIM)
    cache_v_all = cache_win_v.reshape(-1, WINDOW, SWA_KV_DIM)

    conv_p, rec_p, conv_s, rec_s = [], [], [], []
    k_p, v_p, k_s, v_s = [], [], [], []
    for layer in range(DEPTH):
        j = layer // 2
        mod_p = m_all[layer, :BATCH].reshape(BATCH, 1, 6 * D)
        mod_s = mod_s_all[layer]
        if layer % 2 == 0:
            x, cp, rp, cs, rs = _gdn_layer(
                x, g_pre_mix[layer], gdn_w_in[j], gdn_conv_w[j], gdn_a_log[j], gdn_dt_bias[j],
                gdn_norm_w[j], gdn_w_out[j], state_conv[j], state_rec_all, j, g_post_mix[layer],
                mod_p, mod_s)
            conv_p.append(cp); rec_p.append(rp); conv_s.append(cs); rec_s.append(rs)
        else:
            x, kp, vp, ks, vs = _swa_layer(
                x, g_pre_mix[layer], swa_w_in[j], swa_b_in[j], swa_sinks[j], swa_w_out[j],
                swa_b_out[j], rel_bias, cache_k_all, cache_v_all, j, g_post_mix[layer],
                mod_p, mod_s)
            k_p.append(kp); v_p.append(vp); k_s.append(ks); v_s.append(vs)

        if layer % 2 == 0:
            x = _dense_ffn_call(x, g_pre_ffn[layer], g_post_ffn[layer], mod_p, mod_s, *ffn_w, j)
        else:
            h, idx, wts = _prenorm_router_call(x, g_pre_ffn[layer], mod_p, mod_s, 3, 4,
                                               moe_w_router[j], moe_b_router[j])
            dest, tile_expert, tile_half, n_used, tail_rows = _route(idx[:, :2])
            xs = _dispatch_call(h, dest, tail_rows, n_used)
            ys = _ffn_call(xs, tile_expert + j * N_EXPERTS, tile_half, n_used, *moe_w)
            x = _combine_call(x, ys, dest, wts, g_post_ffn[layer], mod_p, mod_s, 5,
                              split=layer == DEPTH - 1)

    x_p, x_s = x if isinstance(x, (list, tuple)) else (x[:NP_TOK], x[NP_TOK:])
    y_prompt = x_p.reshape(BATCH, SEQ, D)
    y_sample = x_s.reshape(DEC_BATCH, DEC_SEQ, D)
    return (y_prompt, y_sample, jnp.stack(conv_p), jnp.stack(rec_p), jnp.stack(k_p), jnp.stack(v_p),
            jnp.stack(conv_s), jnp.stack(rec_s), jnp.stack(k_s), jnp.stack(v_s))
```

```python
import functools
import math

import numpy as np
import jax
import jax.numpy as jnp
from jax import lax
from jax.experimental import pallas as pl
from jax.experimental.pallas import tpu as pltpu

F32 = jnp.float32
BF16 = jnp.bfloat16
I32 = jnp.int32
HIGHEST = lax.Precision.HIGHEST

D = 1024
BATCH = 4
SEQ = 4096
DEPTH = 4
DEC_BATCH = 128
DEC_SEQ = 4
PAST_LEN = 8192
GDN_QK_HEADS = 4
GDN_V_HEADS = 8
GDN_HD = 128
GDN_KEY_DIM = GDN_QK_HEADS * GDN_HD
GDN_VAL_DIM = GDN_V_HEADS * GDN_HD
GDN_CONV_DIM = 2 * GDN_KEY_DIM + GDN_VAL_DIM
GDN_MAIN_DIM = GDN_CONV_DIM + GDN_VAL_DIM
CONV_WIDTH = 4
GDN_CHUNK = 64
SWA_Q_HEADS = 16
SWA_KV_HEADS = 4
SWA_GROUP = SWA_Q_HEADS // SWA_KV_HEADS
SWA_HD = 64
SWA_Q_DIM = SWA_Q_HEADS * SWA_HD
SWA_KV_DIM = SWA_KV_HEADS * SWA_HD
SWA_PROJ_DIM = SWA_Q_DIM + 2 * SWA_KV_DIM
WINDOW = 128
REL_BUCKETS = 32
REL_MAX_DIST = 128
D_FF = 2816
N_EXPERTS = 8
RMS_EPS = 1e-6

LANES = 128
SUBLANES = 8
VMEM_LIMIT = 56 * 1024 * 1024

TM = 512
NP_TOK = BATCH * SEQ
NS_TOK = DEC_BATCH * DEC_SEQ
TT = NP_TOK + NS_TOK
NP_TILES = NP_TOK // TM
NS_TILES = NS_TOK // TM
N_TILES = NP_TILES + NS_TILES
TILES_PER_SEQ = SEQ // TM
TF = 256
NF = D_FF // TF
TME = 1024
MOE_ROWS = 2 * TT + N_EXPERTS * TME
MOE_TILES = MOE_ROWS // TME
GDN_NB = 2
GDN_PROMPT_PAR = 4
GDN_SAMPLE_PAR = 4
GDN_SAMPLE_ROWS = 8
SWA_SB = 8
SWA_KPAD = WINDOW + 8

assert NP_TOK % TM == 0 and NS_TOK % TM == 0 and SEQ % TM == 0
assert D_FF % TF == 0 and DEC_SEQ <= GDN_SAMPLE_ROWS and DEC_SEQ <= 8


def _cparams(sem):
    return pltpu.CompilerParams(dimension_semantics=sem, vmem_limit_bytes=VMEM_LIMIT)


def _bdot(a, b):
    return jnp.dot(a.astype(BF16), b.astype(BF16), preferred_element_type=F32)


def _bdot_nt(a, b):
    return lax.dot_general(a.astype(BF16), b.astype(BF16), (((1,), (1,)), ((), ())),
                           preferred_element_type=F32)


def _silu(x):
    return x * jax.nn.sigmoid(x)


def _rms(x):
    return x * lax.rsqrt(jnp.mean(x * x, axis=-1, keepdims=True) + RMS_EPS)


def _mod_kernel(c_ref, w_ref, b_ref, o_ref):
    c = c_ref[...]
    o_ref[0] = _bdot(_silu(c), w_ref[0]) + b_ref[0]


def _mod_call(c_all, w_mod, b_mod):
    n = c_all.shape[0]
    tn = D
    return pl.pallas_call(
        _mod_kernel,
        grid=(DEPTH, 6 * D // tn),
        in_specs=[
            pl.BlockSpec((n, D), lambda l, j: (0, 0)),
            pl.BlockSpec((1, D, tn), lambda l, j: (l, 0, j)),
            pl.BlockSpec((1, 1, tn), lambda l, j: (l, 0, j)),
        ],
        out_specs=pl.BlockSpec((1, n, tn), lambda l, j: (l, 0, j)),
        out_shape=jax.ShapeDtypeStruct((DEPTH, n, 6 * D), F32),
        compiler_params=_cparams(("arbitrary", "arbitrary")),
        name="modulation",
    )(c_all, w_mod, b_mod.reshape(DEPTH, 1, 6 * D))


def _mod_specs(chunk, token_axis=0):
    p = pl.BlockSpec((1, 1, D), lambda *ids: (jnp.minimum(ids[token_axis] // TILES_PER_SEQ, BATCH - 1),
                                              0, chunk))
    s = pl.BlockSpec((TM, D), lambda *ids: (jnp.maximum(ids[token_axis] - NP_TILES, 0), chunk))
    return p, s


def _pick(i, p_ref, s_ref):
    return jnp.where(i < NP_TILES, p_ref[0], s_ref[...])


def _prenorm(i, x, g_ref, shp_ref, shs_ref, scp_ref, scs_ref):
    h = _rms(x) * g_ref[...]
    return h * (1.0 + _pick(i, scp_ref, scs_ref)) + _pick(i, shp_ref, shs_ref)


def _prenorm_router_kernel(x_ref, g_ref, shp_ref, shs_ref, scp_ref, scs_ref, wr_ref, br_ref,
                           h_ref, idx_ref, wt_ref):
    h = _prenorm(pl.program_id(0), x_ref[...], g_ref, shp_ref, shs_ref, scp_ref, scs_ref)
    h_ref[...] = h
    lane = lax.broadcasted_iota(I32, (TM, LANES), 1).astype(F32)
    w = wr_ref[...]
    h_hi, w_hi = h.astype(BF16), w.astype(BF16)
    h_lo = (h - h_hi.astype(F32)).astype(BF16)
    w_lo = (w - w_hi.astype(F32)).astype(BF16)
    logits = (jnp.dot(h_hi, w_hi, preferred_element_type=F32)
              + (jnp.dot(h_lo, w_hi, preferred_element_type=F32)
                 + jnp.dot(h_hi, w_lo, preferred_element_type=F32))) + br_ref[...]
    logits = jnp.where(lane < N_EXPERTS, logits, -jnp.inf)
    m1 = jnp.max(logits, axis=-1, keepdims=True)
    i1 = jnp.min(jnp.where(logits == m1, lane, float(LANES)), axis=-1, keepdims=True)
    rest = jnp.where(lane == i1, -jnp.inf, logits)
    m2 = jnp.max(rest, axis=-1, keepdims=True)
    i2 = jnp.min(jnp.where(rest == m2, lane, float(LANES)), axis=-1, keepdims=True)
    e2 = jnp.exp(m2 - m1)
    w1 = 1.0 / (1.0 + e2)
    w2 = e2 / (1.0 + e2)
    idx_ref[...] = jnp.where(lane == 0, i1, jnp.where(lane == 1, i2, 0.0)).astype(I32)
    wt_ref[...] = jnp.where(lane == 0, w1, jnp.where(lane == 1, w2, 0.0))


def _prenorm_router_call(x, g, mod_p, mod_s, sh_chunk, sc_chunk, w_router, b_router):
    shp, shs = _mod_specs(sh_chunk)
    scp, scs = _mod_specs(sc_chunk)
    tile = pl.BlockSpec((TM, D), lambda i: (i, 0))
    small = pl.BlockSpec((TM, LANES), lambda i: (i, 0))
    wr = jnp.pad(w_router, ((0, 0), (0, LANES - N_EXPERTS)))
    br = jnp.pad(b_router, (0, LANES - N_EXPERTS)).reshape(1, LANES)
    return pl.pallas_call(
        _prenorm_router_kernel,
        grid=(N_TILES,),
        in_specs=[tile, pl.BlockSpec((1, D), lambda i: (0, 0)), shp, shs, scp, scs,
                  pl.BlockSpec((D, LANES), lambda i: (0, 0)),
                  pl.BlockSpec((1, LANES), lambda i: (0, 0))],
        out_specs=[tile, small, small],
        out_shape=[jax.ShapeDtypeStruct((TT, D), F32),
                   jax.ShapeDtypeStruct((TT, LANES), I32),
                   jax.ShapeDtypeStruct((TT, LANES), F32)],
        compiler_params=_cparams(("arbitrary",)),
        name="prenorm_router",
    )(x, g.reshape(1, D), mod_p, mod_s, mod_p, mod_s, wr, br)


def _proj_kernel(x_ref, g_ref, shp_ref, shs_ref, scp_ref, scs_ref, w_ref, b_ref, o_ref, wbf_ref):
    i = pl.program_id(1)

    @pl.when(i == 0)
    def _():
        wbf_ref[...] = w_ref[...].astype(BF16)

    h = _prenorm(i, x_ref[...], g_ref, shp_ref, shs_ref, scp_ref, scs_ref).astype(BF16)
    o_ref[...] = jnp.dot(h, wbf_ref[...], preferred_element_type=F32) + b_ref[...]


def _proj_call(x, g, mod_p, mod_s, w, b, tn):
    k, n = w.shape
    shp, shs = _mod_specs(0, token_axis=1)
    scp, scs = _mod_specs(1, token_axis=1)
    return pl.pallas_call(
        _proj_kernel,
        grid=(n // tn, N_TILES),
        in_specs=[
            pl.BlockSpec((TM, k), lambda j, i: (i, 0)),
            pl.BlockSpec((1, k), lambda j, i: (0, 0)), shp, shs, scp, scs,
            pl.BlockSpec((k, tn), lambda j, i: (0, j)),
            pl.BlockSpec((1, tn), lambda j, i: (0, j)),
        ],
        out_specs=pl.BlockSpec((TM, tn), lambda j, i: (i, j)),
        out_shape=jax.ShapeDtypeStruct((TT, n), F32),
        scratch_shapes=[pltpu.VMEM((k, tn), BF16)],
        compiler_params=_cparams(("arbitrary", "arbitrary")),
        name="proj",
    )(x, g.reshape(1, k), mod_p, mod_s, mod_p, mod_s, w, b.reshape(1, n))


def _post_mm_kernel(x_ref, op_ref, os_ref, w_ref, b_ref, g_ref, gtp_ref, gts_ref, out_ref, wbf_ref):
    i = pl.program_id(0)

    @pl.when(i == 0)
    def _():
        wbf_ref[...] = w_ref[...].astype(BF16)

    o = jnp.where(i < NP_TILES, op_ref[...], os_ref[...]).astype(BF16)
    y = jnp.dot(o, wbf_ref[...], preferred_element_type=F32) + b_ref[...]
    out_ref[...] = x_ref[...] + _pick(i, gtp_ref, gts_ref) * (_rms(y) * g_ref[...])


def _post_mm_call(x, o_p, o_s, w, b, g, mod_p, mod_s, gate_chunk):
    gtp, gts = _mod_specs(gate_chunk)
    tile = pl.BlockSpec((TM, D), lambda i: (i, 0))
    row = pl.BlockSpec((1, D), lambda i: (0, 0))
    return pl.pallas_call(
        _post_mm_kernel,
        grid=(N_TILES,),
        in_specs=[tile,
                  pl.BlockSpec((TM, D), lambda i: (jnp.minimum(i, NP_TILES - 1), 0)),
                  pl.BlockSpec((TM, D), lambda i: (jnp.maximum(i - NP_TILES, 0), 0)),
                  pl.BlockSpec((D, D), lambda i: (0, 0)), row, row, gtp, gts],
        out_specs=tile,
        out_shape=jax.ShapeDtypeStruct((TT, D), F32),
        scratch_shapes=[pltpu.VMEM((D, D), BF16)],
        compiler_params=_cparams(("arbitrary",)),
        name="post_mm",
    )(x, o_p, o_s, w, b.reshape(1, D), g.reshape(1, D), mod_p, mod_s)


def _gdn_kernel(*refs, chunk, n_chunks, t_real, n_par):
    x_refs = refs[:n_par]
    ba_refs = refs[n_par:2 * n_par]
    (cw_ref, alog_ref, dt_ref, nw_ref, cb_ref, s0_ref, o_ref, sfin_ref, xbuf, s_scr) = refs[2 * n_par:]
    c_sz = chunk
    tb = chunk * n_chunks
    j = pl.program_id(1)
    rep = GDN_V_HEADS // GDN_QK_HEADS
    units = [(p, h) for p in range(n_par) for h in range(GDN_V_HEADS)]

    @pl.when(j == 0)
    def _():
        s_scr[...] = s0_ref[...]
        xbuf[:, 0:SUBLANES, :] = cb_ref[...]

    for p in range(n_par):
        xbuf[p, SUBLANES:SUBLANES + tb, :] = x_refs[p][:, 0:GDN_CONV_DIM]

    ii = lax.broadcasted_iota(I32, (c_sz, c_sz), 0)
    jj = lax.broadcasted_iota(I32, (c_sz, c_sz), 1)
    causal = ii >= jj
    strict = ii > jj
    tri = causal.astype(F32)
    eye_c = (ii == jj).astype(F32)
    pair_mask = [jnp.logical_and((ii >> (l + 1)) == (jj >> (l + 1)), (ii >> l) != (jj >> l))
                 for l in range(int(math.log2(c_sz)))]
    eye_t = (lax.broadcasted_iota(I32, (2 * SUBLANES, LANES), 0)
             == lax.broadcasted_iota(I32, (2 * SUBLANES, LANES), 1)).astype(F32)
    lane = lax.broadcasted_iota(I32, (c_sz, LANES), 1)
    row = lax.broadcasted_iota(I32, (c_sz, LANES), 0)
    cw = cw_ref[...]
    neg_a = -jnp.exp(alog_ref[...])
    dt = dt_ref[...]
    nw = nw_ref[...]

    def chunk_body(c, carry):
        r0 = pl.multiple_of(c * c_sz, c_sz)
        xc, beta, gc, rows_t = [], [], [], []
        for p in range(n_par):
            win = xbuf[p, pl.ds(r0, c_sz + SUBLANES), :]
            y = win[SUBLANES:SUBLANES + c_sz] * cw[3:4]
            for tap in range(1, CONV_WIDTH):
                y = y + win[SUBLANES - tap:SUBLANES - tap + c_sz] * cw[3 - tap:4 - tap]
            xc.append(_silu(y))
            ba = ba_refs[p][pl.ds(r0, c_sz), :]
            b_p = jax.nn.sigmoid(ba)
            g_p = neg_a * jax.nn.softplus(ba + dt)
            if t_real < tb:
                live = (row + r0) < t_real
                b_p = jnp.where(live, b_p, 0.0)
                g_p = jnp.where(live, g_p, 0.0)
            gc_p = jnp.dot(tri, g_p, precision=HIGHEST, preferred_element_type=F32)
            cols = jnp.where(lane < GDN_V_HEADS, b_p, gc_p)
            rows_t.append(lax.dot_general(eye_t, cols, (((1,), (1,)), ((), ())), precision=HIGHEST,
                                          preferred_element_type=F32))
            beta.append(b_p)
            gc.append(gc_p)

        qn, kn, kk, qk = {}, {}, {}, {}
        for p in range(n_par):
            for hq in range(GDN_QK_HEADS):
                q = xc[p][:, hq * GDN_HD:(hq + 1) * GDN_HD]
                k = xc[p][:, GDN_KEY_DIM + hq * GDN_HD:GDN_KEY_DIM + (hq + 1) * GDN_HD]
                q = q * lax.rsqrt(jnp.sum(q * q, axis=-1, keepdims=True) + RMS_EPS) * (GDN_HD ** -0.5)
                k = k * lax.rsqrt(jnp.sum(k * k, axis=-1, keepdims=True) + RMS_EPS)
                qn[p, hq], kn[p, hq] = q, k
        for key in qn:
            kk[key] = _bdot_nt(kn[key], kn[key])
            qk[key] = _bdot_nt(qn[key], kn[key])

        gcc, gcl, bcol, decay, a_mat, inv = {}, {}, {}, {}, {}, {}
        for (p, h) in units:
            gcc[p, h] = gc[p][:, GDN_V_HEADS + h:GDN_V_HEADS + h + 1]
            gcr = rows_t[p][GDN_V_HEADS + h:GDN_V_HEADS + h + 1, :]
            bcol[p, h] = beta[p][:, h:h + 1]
            gcl[p, h] = gcc[p, h][c_sz - 1:c_sz, :]
            decay[p, h] = jnp.exp(jnp.where(causal, gcc[p, h] - gcr, -jnp.inf))
            a_mat[p, h] = jnp.where(strict, bcol[p, h] * kk[p, h // rep] * decay[p, h], 0.0)
            inv[p, h] = eye_c - jnp.where(pair_mask[0], a_mat[p, h], 0.0)
        for lvl in range(1, len(pair_mask)):
            t1 = {u: _bdot(jnp.where(pair_mask[lvl], a_mat[u], 0.0), inv[u]) for u in units}
            t2 = {u: _bdot(inv[u], t1[u]) for u in units}
            inv = {u: inv[u] - t2[u] for u in units}

        egc = {u: jnp.exp(gcc[u]) for u in units}
        sol = {}
        for (p, h) in units:
            v = xc[p][:, 2 * GDN_KEY_DIM + h * GDN_HD:2 * GDN_KEY_DIM + (h + 1) * GDN_HD]
            rhs = jnp.concatenate([v * bcol[p, h], kn[p, h // rep] * (bcol[p, h] * egc[p, h])], axis=1)
            sol[p, h] = _bdot(inv[p, h], rhs)
        ws = {}
        for (p, h) in units:
            q_dec = qn[p, h // rep] * egc[p, h]
            ws[p, h] = _bdot(jnp.concatenate([sol[p, h][:, GDN_HD:], q_dec], axis=0), s_scr[p, h])
        u_new = {u: sol[u][:, :GDN_HD] - ws[u][:c_sz] for u in units}
        o_part, s_part = {}, {}
        for (p, h) in units:
            qkm = jnp.where(causal, qk[p, h // rep] * decay[p, h], 0.0)
            k_dec = kn[p, h // rep] * jnp.exp(gcl[p, h] - gcc[p, h])
            o_part[p, h] = _bdot(qkm, u_new[p, h])
            s_part[p, h] = lax.dot_general(k_dec.astype(BF16), u_new[p, h].astype(BF16),
                                           (((0,), (0,)), ((), ())), preferred_element_type=F32)
        for (p, h) in units:
            s_scr[p, h] = s_scr[p, h] * jnp.exp(gcl[p, h]) + s_part[p, h]
            o = ws[p, h][c_sz:] + o_part[p, h]
            z = x_refs[p][pl.ds(r0, c_sz), GDN_CONV_DIM + h * GDN_HD:GDN_CONV_DIM + (h + 1) * GDN_HD]
            o_ref[p, pl.ds(r0, c_sz), h * GDN_HD:(h + 1) * GDN_HD] = _rms(o) * nw * _silu(z)
        return carry

    lax.fori_loop(0, n_chunks, chunk_body, 0)
    xbuf[:, 0:SUBLANES, :] = xbuf[:, tb:tb + SUBLANES, :]

    @pl.when(j == pl.num_programs(1) - 1)
    def _():
        sfin_ref[...] = s_scr[...]


def _gdn_call(x, ba, conv_w, a_log, dt_bias, norm_w, conv_buf, s0, *, n_seq, n_steps, chunk,
              n_chunks, t_real, n_par, s0_first_seq=0):
    s0_block = s0_first_seq // n_par
    tb = chunk * n_chunks
    pad = LANES - 2 * GDN_V_HEADS
    alog_row = jnp.pad(a_log, (GDN_V_HEADS, pad)).reshape(1, LANES)
    dt_row = jnp.pad(dt_bias, (GDN_V_HEADS, pad)).reshape(1, LANES)

    def seq_block(width, p):
        return pl.BlockSpec((None, tb, width), lambda s, j: ((s * n_par + p) * n_steps + j, 0, 0))

    const = lambda s, j: (0, 0)
    state_spec = pl.BlockSpec((n_par, GDN_V_HEADS, GDN_HD, GDN_HD), lambda s, j: (s, 0, 0, 0))
    return pl.pallas_call(
        functools.partial(_gdn_kernel, chunk=chunk, n_chunks=n_chunks, t_real=t_real, n_par=n_par),
        grid=(n_seq // n_par, n_steps),
        in_specs=[seq_block(GDN_MAIN_DIM, p) for p in range(n_par)]
                 + [seq_block(LANES, p) for p in range(n_par)]
                 + [pl.BlockSpec((CONV_WIDTH, GDN_CONV_DIM), const),
                    pl.BlockSpec((1, LANES), const), pl.BlockSpec((1, LANES), const),
                    pl.BlockSpec((1, GDN_HD), const),
                    pl.BlockSpec((n_par, SUBLANES, GDN_CONV_DIM), lambda s, j: (s, 0, 0)),
                    pl.BlockSpec((n_par, GDN_V_HEADS, GDN_HD, GDN_HD),
                                 lambda s, j: (s + s0_block, 0, 0, 0))],
        out_specs=[pl.BlockSpec((n_par, None, tb, GDN_VAL_DIM), lambda s, j: (s, j, 0, 0)),
                   state_spec],
        out_shape=[jax.ShapeDtypeStruct((n_seq, n_steps, tb, GDN_VAL_DIM), F32),
                   jax.ShapeDtypeStruct((n_seq, GDN_V_HEADS, GDN_HD, GDN_HD), F32)],
        scratch_shapes=[pltpu.VMEM((n_par, tb + SUBLANES, GDN_CONV_DIM), F32),
                        pltpu.VMEM((n_par, GDN_V_HEADS, GDN_HD, GDN_HD), F32)],
        compiler_params=_cparams(("arbitrary", "arbitrary")),
        name="gdn",
    )(*([x] * n_par), *([ba] * n_par), conv_w, alog_row, dt_row, norm_w.reshape(1, GDN_HD),
      conv_buf, s0)


def _t5_bucket_np(dist):
    dist = np.maximum(dist, 0)
    max_exact = REL_BUCKETS // 2
    ratio = np.log(np.maximum(dist, max_exact).astype(np.float32) / np.float32(max_exact)) \
        / np.float32(math.log(REL_MAX_DIST / max_exact))
    large = max_exact + (ratio.astype(np.float32) * np.float32(REL_BUCKETS - max_exact)).astype(np.int32)
    return np.where(dist < max_exact, dist, np.minimum(large, REL_BUCKETS - 1)).astype(np.int32)


def _bucket_table(qpos, kpos, k_valid):
    dist = qpos[:, None] - kpos[None, :]
    ok = (dist >= 0) & (dist < WINDOW) & (kpos[None, :] >= 0) & k_valid[None, :]
    return np.where(ok, _t5_bucket_np(dist), -1).astype(np.int32)


def _bias_from_buckets(bkt, rb_ref, head):
    def body(b, acc):
        return jnp.where(bkt == b, rb_ref[b, head], acc)
    acc = lax.fori_loop(0, REL_BUCKETS, body, jnp.zeros(bkt.shape, F32))
    return jnp.where(bkt < 0, -jnp.inf, acc)


def _swa_prompt_kernel(rb_ref, sk_ref, bkt_ref, q_ref, kvp_ref, kvc_ref, o_ref, bias_scr):
    first = jnp.logical_and(pl.program_id(0) == 0, pl.program_id(1) == 0)
    n = pl.program_id(1)

    @pl.when(first)
    def _():
        bkt = bkt_ref[...]
        for h in range(SWA_Q_HEADS):
            kv, g = divmod(h, SWA_GROUP)
            bias_scr[kv, g * WINDOW:(g + 1) * WINDOW, :] = _bias_from_buckets(bkt, rb_ref, h)

    q = q_ref[...]
    scale = SWA_HD ** -0.5
    heads = range(SWA_KV_HEADS)
    s_prev, s_cur = [], []
    for kv in heads:
        qs = jnp.concatenate(
            [q[:, (kv * SWA_GROUP + g) * SWA_HD:(kv * SWA_GROUP + g + 1) * SWA_HD]
             for g in range(SWA_GROUP)], axis=0).astype(BF16)
        s_prev.append(_bdot_nt(qs, kvp_ref[:, kv * SWA_HD:(kv + 1) * SWA_HD]))
        s_cur.append(_bdot_nt(qs, kvc_ref[:, kv * SWA_HD:(kv + 1) * SWA_HD]))
    p_prev, p_cur, den = [], [], []
    for kv in heads:
        sp = s_prev[kv] * scale + bias_scr[kv, :, 0:WINDOW]
        sp = jnp.where(n > 0, sp, -jnp.inf)
        sc = s_cur[kv] * scale + bias_scr[kv, :, WINDOW:2 * WINDOW]
        sink = jnp.concatenate(
            [jnp.full((WINDOW, 1), sk_ref[kv * SWA_GROUP + g], F32) for g in range(SWA_GROUP)], axis=0)
        m = jnp.maximum(jnp.maximum(jnp.max(sp, axis=-1, keepdims=True),
                                    jnp.max(sc, axis=-1, keepdims=True)), sink)
        pp = jnp.exp(sp - m)
        pc = jnp.exp(sc - m)
        den.append(jnp.sum(pp, axis=-1, keepdims=True) + jnp.sum(pc, axis=-1, keepdims=True)
                   + jnp.exp(sink - m))
        p_prev.append(pp.astype(BF16))
        p_cur.append(pc.astype(BF16))
    outs = []
    for kv in heads:
        v_prev = kvp_ref[:, SWA_KV_DIM + kv * SWA_HD:SWA_KV_DIM + (kv + 1) * SWA_HD]
        v_cur = kvc_ref[:, SWA_KV_DIM + kv * SWA_HD:SWA_KV_DIM + (kv + 1) * SWA_HD]
        o = (_bdot(p_prev[kv], v_prev) + _bdot(p_cur[kv], v_cur)) / den[kv]
        outs.extend(o[g * WINDOW:(g + 1) * WINDOW] for g in range(SWA_GROUP))
    o_ref[...] = jnp.concatenate(outs, axis=1)


def _swa_prompt_call(qkv, rel_bias, sinks):
    nb = SEQ // WINDOW
    qpos = WINDOW + np.arange(WINDOW)
    kpos = np.arange(2 * WINDOW)
    bkt = jnp.asarray(_bucket_table(qpos, kpos, np.ones(2 * WINDOW, bool)))
    kv_col = SWA_Q_DIM // (2 * SWA_KV_DIM)
    smem = pl.BlockSpec(memory_space=pltpu.SMEM)
    return pl.pallas_call(
        _swa_prompt_kernel,
        grid=(BATCH, nb),
        in_specs=[smem, smem,
                  pl.BlockSpec((WINDOW, 2 * WINDOW), lambda b, n: (0, 0)),
                  pl.BlockSpec((WINDOW, SWA_Q_DIM), lambda b, n: (b * nb + n, 0)),
                  pl.BlockSpec((WINDOW, 2 * SWA_KV_DIM),
                               lambda b, n: (jnp.maximum(b * nb + n - 1, 0), kv_col)),
                  pl.BlockSpec((WINDOW, 2 * SWA_KV_DIM), lambda b, n: (b * nb + n, kv_col))],
        out_specs=pl.BlockSpec((WINDOW, SWA_Q_DIM), lambda b, n: (b * nb + n, 0)),
        out_shape=jax.ShapeDtypeStruct((NP_TOK, SWA_Q_DIM), F32),
        scratch_shapes=[pltpu.VMEM((SWA_KV_HEADS, SWA_GROUP * WINDOW, 2 * WINDOW), F32)],
        compiler_params=_cparams(("arbitrary", "arbitrary")),
        name="swa_prompt",
    )(rel_bias, sinks, bkt, qkv, qkv, qkv)


def _swa_sample_kernel(rb_ref, sk_ref, bkt_ref, q_ref, ck_ref, cv_ref, kn_ref, vn_ref,
                       o_ref, ck_out_ref, cv_out_ref, k_scr, v_scr, bias_scr, sink_scr):
    rows = SWA_GROUP * DEC_SEQ
    k_scr[:, 0:WINDOW, :] = ck_ref[...]
    k_scr[:, WINDOW:SWA_KPAD, :] = kn_ref[...]
    v_scr[:, 0:WINDOW, :] = cv_ref[...]
    v_scr[:, WINDOW:SWA_KPAD, :] = vn_ref[...]
    ck_out_ref[...] = k_scr[:, DEC_SEQ:DEC_SEQ + WINDOW, :]
    cv_out_ref[...] = v_scr[:, DEC_SEQ:DEC_SEQ + WINDOW, :]
    k_ref, v_ref = k_scr, v_scr

    def group_of_row(shape):
        r = lax.broadcasted_iota(I32, shape, 0)
        return sum((r >= g * DEC_SEQ).astype(I32) for g in range(1, SWA_GROUP))

    @pl.when(pl.program_id(0) == 0)
    def _():
        bkt = bkt_ref[...]
        grp = group_of_row((rows, 1))
        grp_full = group_of_row(bkt.shape)
        for kv in range(SWA_KV_HEADS):
            acc = jnp.zeros(bkt.shape, F32)
            snk = jnp.zeros((rows, 1), F32)
            for g in range(SWA_GROUP):
                head = kv * SWA_GROUP + g
                acc = jnp.where(grp_full == g, _bias_from_buckets(bkt, rb_ref, head), acc)
                snk = jnp.where(grp == g, sk_ref[head], snk)
            bias_scr[kv] = acc
            sink_scr[kv] = snk

    scale = SWA_HD ** -0.5
    units = [(s, kv) for s in range(SWA_SB) for kv in range(SWA_KV_HEADS)]
    scores = {(s, kv): _bdot_nt(q_ref[s, kv], k_ref[s, :, kv * SWA_HD:(kv + 1) * SWA_HD])
              for (s, kv) in units}
    probs, den = {}, {}
    for (s, kv) in units:
        sc = scores[s, kv] * scale + bias_scr[kv]
        sink = sink_scr[kv]
        m = jnp.maximum(jnp.max(sc, axis=-1, keepdims=True), sink)
        p = jnp.exp(sc - m)
        den[s, kv] = jnp.sum(p, axis=-1, keepdims=True) + jnp.exp(sink - m)
        probs[s, kv] = p.astype(BF16)
    for (s, kv) in units:
        o_ref[s, kv] = _bdot(probs[s, kv], v_ref[s, :, kv * SWA_HD:(kv + 1) * SWA_HD]) / den[s, kv]


def _swa_sample_call(q_st, cache_k, cache_v, k_new, v_new, layer_pair, rel_bias, sinks):
    rows = SWA_GROUP * DEC_SEQ
    n_keys = WINDOW + DEC_SEQ
    qpos = PAST_LEN + np.arange(DEC_SEQ)
    kpos = PAST_LEN - WINDOW + np.arange(SWA_KPAD)
    bkt4 = _bucket_table(qpos, kpos, np.arange(SWA_KPAD) < n_keys)
    bkt = jnp.asarray(np.tile(bkt4, (SWA_GROUP, 1)))
    steps = DEC_BATCH // SWA_SB
    smem = pl.BlockSpec(memory_space=pltpu.SMEM)
    q_spec = pl.BlockSpec((SWA_SB, SWA_KV_HEADS, rows, SWA_HD), lambda i: (i, 0, 0, 0))
    win_in = pl.BlockSpec((SWA_SB, WINDOW, SWA_KV_DIM), lambda i: (layer_pair * steps + i, 0, 0))
    win_out = pl.BlockSpec((SWA_SB, WINDOW, SWA_KV_DIM), lambda i: (i, 0, 0))
    new_spec = pl.BlockSpec((SWA_SB, SWA_KPAD - WINDOW, SWA_KV_DIM), lambda i: (i, 0, 0))
    win_shape = jax.ShapeDtypeStruct((DEC_BATCH, WINDOW, SWA_KV_DIM), F32)
    return pl.pallas_call(
        _swa_sample_kernel,
        grid=(steps,),
        in_specs=[smem, smem, pl.BlockSpec((rows, SWA_KPAD), lambda i: (0, 0)),
                  q_spec, win_in, win_in, new_spec, new_spec],
        out_specs=[q_spec, win_out, win_out],
        out_shape=[jax.ShapeDtypeStruct((DEC_BATCH, SWA_KV_HEADS, rows, SWA_HD), F32),
                   win_shape, win_shape],
        scratch_shapes=[pltpu.VMEM((SWA_SB, SWA_KPAD, SWA_KV_DIM), F32),
                        pltpu.VMEM((SWA_SB, SWA_KPAD, SWA_KV_DIM), F32),
                        pltpu.VMEM((SWA_KV_HEADS, rows, SWA_KPAD), F32),
                        pltpu.VMEM((SWA_KV_HEADS, rows, 1), F32)],
        compiler_params=_cparams(("arbitrary",)),
        name="swa_sample",
    )(rel_bias, sinks, bkt, q_st, cache_k, cache_v, k_new, v_new)


def _ffn_scratch(rows):
    return [pltpu.VMEM((rows, D), F32),
            pltpu.VMEM((NF, D, TF), BF16),
            pltpu.VMEM((NF, D, TF), BF16),
            pltpu.VMEM((NF, TF, D), BF16),
            pltpu.VMEM((2, D, TF), F32),
            pltpu.VMEM((2, D, TF), F32),
            pltpu.VMEM((2, TF, D), F32),
            pltpu.SemaphoreType.DMA((3, 2))]


def _swiglu_into(acc_ref, x, expert, load, w_hbm, res, stg, sem):
    wg_hbm, wu_hbm, wd_hbm = w_hbm
    res_g, res_u, res_d = res
    stg_g, stg_u, stg_d = stg

    def copies(f, slot):
        col = pl.multiple_of(f * TF, TF)
        return (pltpu.make_async_copy(wg_hbm.at[expert, :, pl.ds(col, TF)], stg_g.at[slot], sem.at[0, slot]),
                pltpu.make_async_copy(wu_hbm.at[expert, :, pl.ds(col, TF)], stg_u.at[slot], sem.at[1, slot]),
                pltpu.make_async_copy(wd_hbm.at[expert, pl.ds(col, TF), :], stg_d.at[slot], sem.at[2, slot]))

    def block(f):
        gate = jnp.dot(x, res_g[f], preferred_element_type=F32)
        up = jnp.dot(x, res_u[f], preferred_element_type=F32)
        act = (_silu(gate) * up).astype(BF16)
        acc_ref[...] += jnp.dot(act, res_d[f], preferred_element_type=F32)

    acc_ref[...] = jnp.zeros(acc_ref.shape, acc_ref.dtype)

    @pl.when(load)
    def _():
        for c in copies(0, 0):
            c.start()

        def body(f, carry):
            slot = lax.rem(f, 2)

            @pl.when(f + 1 < NF)
            def _():
                for c in copies(f + 1, 1 - slot):
                    c.start()

            for c in copies(f, slot):
                c.wait()
            res_g[f] = stg_g[slot].astype(BF16)
            res_u[f] = stg_u[slot].astype(BF16)
            res_d[f] = stg_d[slot].astype(BF16)
            block(f)
            return carry

        lax.fori_loop(0, NF, body, 0)

    @pl.when(jnp.logical_not(load))
    def _():
        def body(f, carry):
            block(f)
            return carry

        lax.fori_loop(0, NF, body, 0)


def _ffn_kernel(te_ref, ld_ref, hf_ref, nt_ref, x_ref, wg_hbm, wu_hbm, wd_hbm, o_ref, acc_ref, *scratch):
    t = pl.program_id(0)
    used = t < nt_ref[0]
    half = hf_ref[t] != 0
    half_rows = TME // 2

    def run(rows):
        _swiglu_into(acc_ref.at[pl.ds(0, rows)], x_ref[0:rows, :].astype(BF16), te_ref[t],
                     ld_ref[t] != 0, (wg_hbm, wu_hbm, wd_hbm), scratch[0:3], scratch[3:6], scratch[6])
        o_ref[0:rows, :] = acc_ref[0:rows, :]

    @pl.when(jnp.logical_and(used, jnp.logical_not(half)))
    def _():
        run(TME)

    @pl.when(jnp.logical_and(used, half))
    def _():
        run(half_rows)
        o_ref[half_rows:, :] = jnp.zeros((TME - half_rows, D), F32)

    @pl.when(jnp.logical_not(used))
    def _():
        o_ref[...] = jnp.zeros_like(o_ref)


def _ffn_call(x, tile_expert, tile_half, n_tiles_used, w_gate, w_up, w_down):
    n_rows = x.shape[0]
    n_tiles = n_rows // TME
    changed = jnp.concatenate([jnp.ones((1,), I32),
                               (tile_expert[1:] != tile_expert[:-1]).astype(I32)])
    hbm = pl.BlockSpec(memory_space=pl.ANY)
    grid_spec = pltpu.PrefetchScalarGridSpec(
        num_scalar_prefetch=4,
        grid=(n_tiles,),
        in_specs=[pl.BlockSpec((TME, D), lambda t, te, ld, hf, nt: (jnp.minimum(t, nt[0] - 1), 0)),
                  hbm, hbm, hbm],
        out_specs=pl.BlockSpec((TME, D), lambda t, *_: (t, 0)),
        scratch_shapes=_ffn_scratch(TME),
    )
    return pl.pallas_call(
        _ffn_kernel,
        grid_spec=grid_spec,
        out_shape=jax.ShapeDtypeStruct((n_rows, D), F32),
        compiler_params=_cparams(("arbitrary",)),
        name="ffn",
    )(tile_expert, changed, tile_half, n_tiles_used, x, w_gate, w_up, w_down)


def _dense_ffn_kernel(x_ref, gpre_ref, shp_ref, shs_ref, scp_ref, scs_ref, wg_hbm, wu_hbm, wd_hbm,
                      gpost_ref, gtp_ref, gts_ref, out_ref, acc_ref, *scratch, layer_pair):
    i = pl.program_id(0)
    x = x_ref[...]
    h = _prenorm(i, x, gpre_ref, shp_ref, shs_ref, scp_ref, scs_ref).astype(BF16)
    _swiglu_into(acc_ref, h, layer_pair, i == 0, (wg_hbm, wu_hbm, wd_hbm),
                 scratch[0:3], scratch[3:6], scratch[6])
    out_ref[...] = x + _pick(i, gtp_ref, gts_ref) * (_rms(acc_ref[...]) * gpost_ref[...])


def _dense_ffn_call(x, g_pre, g_post, mod_p, mod_s, w_gate, w_up, w_down, layer_pair):
    shp, shs = _mod_specs(3)
    scp, scs = _mod_specs(4)
    gtp, gts = _mod_specs(5)
    tile = pl.BlockSpec((TM, D), lambda i: (i, 0))
    row = pl.BlockSpec((1, D), lambda i: (0, 0))
    hbm = pl.BlockSpec(memory_space=pl.ANY)
    return pl.pallas_call(
        functools.partial(_dense_ffn_kernel, layer_pair=layer_pair),
        grid=(N_TILES,),
        in_specs=[tile, row, shp, shs, scp, scs, hbm, hbm, hbm, row, gtp, gts],
        out_specs=tile,
        out_shape=jax.ShapeDtypeStruct((TT, D), F32),
        scratch_shapes=_ffn_scratch(TM),
        compiler_params=_cparams(("arbitrary",)),
        name="dense_ffn",
    )(x, g_pre.reshape(1, D), mod_p, mod_s, mod_p, mod_s, w_gate, w_up, w_down,
      g_post.reshape(1, D), mod_p, mod_s)


DMA_UNROLL = 8


def _row_copy(src_ref, dst_ref, src_row, dst_row, sem):
    return pltpu.make_async_copy(src_ref.at[pl.ds(src_row, 1)], dst_ref.at[pl.ds(dst_row, 1)], sem)


def _dispatch_kernel(tail_ref, nt_ref, d0_ref, d1_ref, h_ref, out_ref, zero_ref, sem, zsem):
    @pl.when(pl.program_id(0) == 0)
    def _():
        zero_ref[...] = jnp.zeros_like(zero_ref)

        def fill_tile(row0):
            fill = pltpu.make_async_copy(zero_ref, out_ref.at[pl.ds(pl.multiple_of(row0, TME), TME)], zsem)
            fill.start()
            fill.wait()

        for e in range(N_EXPERTS):
            @pl.when(tail_ref[e] >= 0)
            def _():
                fill_tile(tail_ref[e])

            @pl.when(nt_ref[0] + e < MOE_TILES)
            def _():
                fill_tile((nt_ref[0] + e) * TME)

    def start(r, carry):
        _row_copy(h_ref, out_ref, r, d0_ref[0, 0, r], sem).start(priority=0)
        _row_copy(h_ref, out_ref, r, d1_ref[0, 0, r], sem).start(priority=1)
        return carry

    def wait(r, carry):
        _row_copy(h_ref, out_ref, 0, 0, sem).wait()
        _row_copy(h_ref, out_ref, 0, 0, sem).wait()
        return carry

    lax.fori_loop(0, TM, start, 0, unroll=DMA_UNROLL)
    lax.fori_loop(0, TM, wait, 0, unroll=DMA_UNROLL)


def _dispatch_call(h, dest, tail_rows, n_tiles_used):
    idx_spec = pl.BlockSpec((1, 1, TM), lambda i, *_: (i, 0, 0), memory_space=pltpu.SMEM)
    grid_spec = pltpu.PrefetchScalarGridSpec(
        num_scalar_prefetch=2,
        grid=(N_TILES,),
        in_specs=[idx_spec, idx_spec, pl.BlockSpec((TM, D), lambda i, *_: (i, 0))],
        out_specs=pl.BlockSpec(memory_space=pl.ANY),
        scratch_shapes=[pltpu.VMEM((TME, D), h.dtype), pltpu.SemaphoreType.DMA(()),
                        pltpu.SemaphoreType.DMA(())],
    )
    return pl.pallas_call(
        _dispatch_kernel,
        grid_spec=grid_spec,
        out_shape=jax.ShapeDtypeStruct((MOE_ROWS, D), h.dtype),
        compiler_params=_cparams(("arbitrary",)),
        name="moe_dispatch",
    )(tail_rows, n_tiles_used, dest[0].reshape(N_TILES, 1, TM), dest[1].reshape(N_TILES, 1, TM), h)


def _combine_kernel(d0_ref, d1_ref, x_ref, wt_ref, ys_ref, g_ref, gtp_ref, gts_ref, *rest, split):
    i = pl.program_id(0)
    buf, sem = rest[-2:]

    def start(r, carry):
        _row_copy(ys_ref, buf.at[0], d0_ref[0, 0, r], r, sem).start(priority=0)
        _row_copy(ys_ref, buf.at[1], d1_ref[0, 0, r], r, sem).start(priority=1)
        return carry

    def wait(r, carry):
        _row_copy(ys_ref, buf.at[0], 0, 0, sem).wait()
        _row_copy(ys_ref, buf.at[1], 0, 0, sem).wait()
        return carry

    lax.fori_loop(0, TM, start, 0, unroll=DMA_UNROLL)
    lax.fori_loop(0, TM, wait, 0, unroll=DMA_UNROLL)
    wt = wt_ref[...]
    y = wt[:, 0:1] * buf[0] + wt[:, 1:2] * buf[1]
    out = x_ref[...] + _pick(i, gtp_ref, gts_ref) * (_rms(y) * g_ref[...])
    if split:
        prompt_ref, sample_ref = rest[:2]

        @pl.when(i < NP_TILES)
        def _():
            prompt_ref[...] = out

        @pl.when(i >= NP_TILES)
        def _():
            sample_ref[...] = out
    else:
        rest[0][...] = out


def _combine_call(x, ys, dest, wts, g, mod_p, mod_s, gate_chunk, split):
    gtp, gts = _mod_specs(gate_chunk)
    tile = pl.BlockSpec((TM, D), lambda i: (i, 0))
    idx_spec = pl.BlockSpec((1, 1, TM), lambda i: (i, 0, 0), memory_space=pltpu.SMEM)
    if split:
        out_specs = [pl.BlockSpec((TM, D), lambda i: (jnp.minimum(i, NP_TILES - 1), 0)),
                     pl.BlockSpec((TM, D), lambda i: (jnp.maximum(i - NP_TILES, 0), 0))]
        out_shape = [jax.ShapeDtypeStruct((NP_TOK, D), F32), jax.ShapeDtypeStruct((NS_TOK, D), F32)]
    else:
        out_specs = tile
        out_shape = jax.ShapeDtypeStruct((TT, D), F32)
    return pl.pallas_call(
        functools.partial(_combine_kernel, split=split),
        grid=(N_TILES,),
        in_specs=[idx_spec, idx_spec, tile, pl.BlockSpec((TM, LANES), lambda i: (i, 0)),
                  pl.BlockSpec(memory_space=pl.ANY),
                  pl.BlockSpec((1, D), lambda i: (0, 0)), gtp, gts],
        out_specs=out_specs,
        out_shape=out_shape,
        scratch_shapes=[pltpu.VMEM((2, TM, D), F32), pltpu.SemaphoreType.DMA(())],
        compiler_params=_cparams(("arbitrary",)),
        name="moe_combine",
    )(dest[0].reshape(N_TILES, 1, TM), dest[1].reshape(N_TILES, 1, TM), x, wts, ys,
      g.reshape(1, D), mod_p, mod_s)


def _route(top_idx):
    flat_e = top_idx.T.reshape(-1)
    onehot = (flat_e[:, None] == jnp.arange(N_EXPERTS, dtype=I32)[None, :]).astype(I32)
    csum = jnp.cumsum(onehot, axis=0)
    rank = jnp.sum((csum - onehot) * onehot, axis=1)
    counts = csum[-1]
    padded = ((counts + TME - 1) // TME) * TME
    ends = jnp.cumsum(padded)
    starts = ends - padded
    dest = (jnp.sum(onehot * starts[None, :], axis=1) + rank).astype(I32)
    tile_start = jnp.arange(MOE_TILES, dtype=I32) * TME
    tile_expert = jnp.minimum(jnp.sum(tile_start[:, None] >= ends[None, :], axis=1),
                              N_EXPERTS - 1).astype(I32)
    n_used = (ends[-1] // TME).astype(I32).reshape(1)
    tail_rows = jnp.where(padded > 0, ends - TME, -1).astype(I32)
    in_tile = jnp.sum((tile_expert[:, None] == jnp.arange(N_EXPERTS, dtype=I32)[None, :])
                      * (starts + counts)[None, :], axis=1) - tile_start
    tile_half = (in_tile <= TME // 2).astype(I32)
    return dest.reshape(2, TT), tile_expert, tile_half, n_used, tail_rows


def _gdn_layer(x, g_pre, w_in, conv_w, a_log, dt_bias, norm_w, w_out, state_conv, state_rec_all,
               layer_pair, g_post, mod_p, mod_s):
    w_main = w_in[:, :GDN_MAIN_DIM]
    w_ba = jnp.pad(w_in[:, GDN_MAIN_DIM:], ((0, 0), (0, LANES - 2 * GDN_V_HEADS)))
    qkvz = _proj_call(x, g_pre, mod_p, mod_s, w_main, jnp.zeros((GDN_MAIN_DIM,), F32),
                      GDN_MAIN_DIM // 2)
    ba = _proj_call(x, g_pre, mod_p, mod_s, w_ba, jnp.zeros((LANES,), F32), LANES)

    tb = GDN_CHUNK * GDN_NB
    n_steps = SEQ // tb
    o_p, rec_p = _gdn_call(
        qkvz.reshape(TT // tb, tb, GDN_MAIN_DIM), ba.reshape(TT // tb, tb, LANES),
        conv_w, a_log, dt_bias, norm_w,
        jnp.zeros((BATCH, SUBLANES, GDN_CONV_DIM), F32),
        jnp.zeros((BATCH, GDN_V_HEADS, GDN_HD, GDN_HD), F32),
        n_seq=BATCH, n_steps=n_steps, chunk=GDN_CHUNK, n_chunks=GDN_NB, t_real=tb,
        n_par=GDN_PROMPT_PAR)
    o_p = o_p.reshape(NP_TOK, GDN_VAL_DIM)

    row_pad = ((0, 0), (0, GDN_SAMPLE_ROWS - DEC_SEQ), (0, 0))
    qkvz_s = qkvz[NP_TOK:].reshape(DEC_BATCH, DEC_SEQ, GDN_MAIN_DIM)
    ba_s = ba[NP_TOK:].reshape(DEC_BATCH, DEC_SEQ, LANES)
    cbuf_s = jnp.pad(state_conv, ((0, 0), (SUBLANES - (CONV_WIDTH - 1), 0), (0, 0)))
    o_s, rec_s = _gdn_call(
        jnp.pad(qkvz_s, row_pad), jnp.pad(ba_s, row_pad), conv_w, a_log, dt_bias, norm_w,
        cbuf_s, state_rec_all,
        n_seq=DEC_BATCH, n_steps=1, chunk=GDN_SAMPLE_ROWS, n_chunks=1, t_real=DEC_SEQ,
        n_par=GDN_SAMPLE_PAR, s0_first_seq=layer_pair * DEC_BATCH)
    o_s = o_s[:, 0, :DEC_SEQ].reshape(NS_TOK, GDN_VAL_DIM)

    keep = CONV_WIDTH - 1
    pre_p = jnp.stack([qkvz[(b + 1) * SEQ - keep:(b + 1) * SEQ, :GDN_CONV_DIM] for b in range(BATCH)])
    pre_s = jnp.concatenate([state_conv, qkvz_s[:, :, :GDN_CONV_DIM]], axis=1)[:, -keep:]
    x = _post_mm_call(x, o_p, o_s, w_out, jnp.zeros((D,), F32), g_post, mod_p, mod_s, 2)
    return x, pre_p, rec_p, pre_s, rec_s


def _swa_layer(x, g_pre, w_in, b_in, sinks, w_out, b_out, rel_bias, cache_k_all, cache_v_all,
               layer_pair, g_post, mod_p, mod_s):
    qkv = _proj_call(x, g_pre, mod_p, mod_s, w_in, b_in, SWA_PROJ_DIM)
    o_p = _swa_prompt_call(qkv, rel_bias, sinks)
    last = jnp.stack([qkv[(b + 1) * SEQ - WINDOW:(b + 1) * SEQ, SWA_Q_DIM:] for b in range(BATCH)])
    new_k_p = last[:, :, :SWA_KV_DIM]
    new_v_p = last[:, :, SWA_KV_DIM:]

    qkv_s = qkv[NP_TOK:].reshape(DEC_BATCH, DEC_SEQ, SWA_PROJ_DIM)
    q_st = qkv_s[:, :, :SWA_Q_DIM].reshape(DEC_BATCH, DEC_SEQ, SWA_KV_HEADS, SWA_GROUP, SWA_HD)
    q_st = q_st.transpose(0, 2, 3, 1, 4).reshape(DEC_BATCH, SWA_KV_HEADS, SWA_GROUP * DEC_SEQ, SWA_HD)
    row_pad = ((0, 0), (0, SWA_KPAD - WINDOW - DEC_SEQ), (0, 0))
    k_new = jnp.pad(qkv_s[:, :, SWA_Q_DIM:SWA_Q_DIM + SWA_KV_DIM], row_pad)
    v_new = jnp.pad(qkv_s[:, :, SWA_Q_DIM + SWA_KV_DIM:], row_pad)
    o_st, new_k_s, new_v_s = _swa_sample_call(q_st, cache_k_all, cache_v_all, k_new, v_new,
                                              layer_pair, rel_bias, sinks)
    o_s = o_st.reshape(DEC_BATCH, SWA_KV_HEADS, SWA_GROUP, DEC_SEQ, SWA_HD)
    o_s = o_s.transpose(0, 3, 1, 2, 4).reshape(NS_TOK, SWA_Q_DIM)

    x = _post_mm_call(x, o_p, o_s, w_out, b_out, g_post, mod_p, mod_s, 2)
    shape_p = (BATCH, WINDOW, SWA_KV_HEADS, SWA_HD)
    shape_s = (DEC_BATCH, WINDOW, SWA_KV_HEADS, SWA_HD)
    return (x, new_k_p.reshape(shape_p), new_v_p.reshape(shape_p),
            new_k_s.reshape(shape_s), new_v_s.reshape(shape_s))


def kernel(x_prompt, x_sample, c_prompt, c_sample, state_conv, state_rec, cache_win_k, cache_win_v, w_mod, b_mod, g_pre_mix, g_post_mix, g_pre_ffn, g_post_ffn, gdn_w_in, gdn_conv_w, gdn_a_log, gdn_dt_bias, gdn_norm_w, gdn_w_out, swa_w_in, swa_b_in, swa_sinks, swa_w_out, swa_b_out, rel_bias, ffn_w_gate, ffn_w_up, ffn_w_down, moe_w_router, moe_b_router, moe_w_gate, moe_w_up, moe_w_down):
    x = jnp.concatenate([x_prompt.reshape(NP_TOK, D), x_sample.reshape(NS_TOK, D)], axis=0)
    n_c = BATCH + DEC_BATCH
    c_all = jnp.concatenate([c_prompt, c_sample, jnp.zeros((-n_c % SUBLANES, D), F32)], axis=0)
    m_all = _mod_call(c_all, w_mod, b_mod)
    mod_s_all = jnp.repeat(m_all[:, BATCH:n_c], DEC_SEQ, axis=1)
    ffn_w = (ffn_w_gate, ffn_w_up, ffn_w_down)
    moe_w = (moe_w_gate.reshape(-1, D, D_FF), moe_w_up.reshape(-1, D, D_FF),
             moe_w_down.reshape(-1, D_FF, D))
    state_rec_all = state_rec.reshape((-1,) + state_rec.shape[2:])
    assert cache_win_k.shape[2] == WINDOW
    cache_k_all = cache_win_k.reshape(-1, WINDOW, SWA_KV_DIM)
    cache_v_all = cache_win_v.reshape(-1, WINDOW, SWA_KV_DIM)

    conv_p, rec_p, conv_s, rec_s = [], [], [], []
    k_p, v_p, k_s, v_s = [], [], [], []
    for layer in range(DEPTH):
        j = layer // 2
        mod_p = m_all[layer, :BATCH].reshape(BATCH, 1, 6 * D)
        mod_s = mod_s_all[layer]
        if layer % 2 == 0:
            x, cp, rp, cs, rs = _gdn_layer(
                x, g_pre_mix[layer], gdn_w_in[j], gdn_conv_w[j], gdn_a_log[j], gdn_dt_bias[j],
                gdn_norm_w[j], gdn_w_out[j], state_conv[j], state_rec_all, j, g_post_mix[layer],
                mod_p, mod_s)
            conv_p.append(cp); rec_p.append(rp); conv_s.append(cs); rec_s.append(rs)
        else:
            x, kp, vp, ks, vs = _swa_layer(
                x, g_pre_mix[layer], swa_w_in[j], swa_b_in[j], swa_sinks[j], swa_w_out[j],
                swa_b_out[j], rel_bias, cache_k_all, cache_v_all, j, g_post_mix[layer],
                mod_p, mod_s)
            k_p.append(kp); v_p.append(vp); k_s.append(ks); v_s.append(vs)

        if layer % 2 == 0:
            x = _dense_ffn_call(x, g_pre_ffn[layer], g_post_ffn[layer], mod_p, mod_s, *ffn_w, j)
        else:
            h, idx, wts = _prenorm_router_call(x, g_pre_ffn[layer], mod_p, mod_s, 3, 4,
                                               moe_w_router[j], moe_b_router[j])
            dest, tile_expert, tile_half, n_used, tail_rows = _route(idx[:, :2])
            xs = _dispatch_call(h, dest, tail_rows, n_used)
            ys = _ffn_call(xs, tile_expert + j * N_EXPERTS, tile_half, n_used, *moe_w)
            x = _combine_call(x, ys, dest, wts, g_post_ffn[layer], mod_p, mod_s, 5,
                              split=layer == DEPTH - 1)

    x_p, x_s = x if isinstance(x, (list, tuple)) else (x[:NP_TOK], x[NP_TOK:])
    y_prompt = x_p.reshape(BATCH, SEQ, D)
    y_sample = x_s.reshape(DEC_BATCH, DEC_SEQ, D)
    return (y_prompt, y_sample, jnp.stack(conv_p), jnp.stack(rec_p), jnp.stack(k_p), jnp.stack(v_p),
            jnp.stack(conv_s), jnp.stack(rec_s), jnp.stack(k_s), jnp.stack(v_s))
```

```python
import functools
import math

import numpy as np
import jax
import jax.numpy as jnp
from jax import lax
from jax.experimental import pallas as pl
from jax.experimental.pallas import tpu as pltpu

F32 = jnp.float32
BF16 = jnp.bfloat16
I32 = jnp.int32
HIGHEST = lax.Precision.HIGHEST

D = 1024
BATCH = 4
SEQ = 4096
DEPTH = 4
DEC_BATCH = 128
DEC_SEQ = 4
PAST_LEN = 8192
GDN_QK_HEADS = 4
GDN_V_HEADS = 8
GDN_HD = 128
GDN_KEY_DIM = GDN_QK_HEADS * GDN_HD
GDN_VAL_DIM = GDN_V_HEADS * GDN_HD
GDN_CONV_DIM = 2 * GDN_KEY_DIM + GDN_VAL_DIM
GDN_MAIN_DIM = GDN_CONV_DIM + GDN_VAL_DIM
CONV_WIDTH = 4
GDN_CHUNK = 64
SWA_Q_HEADS = 16
SWA_KV_HEADS = 4
SWA_GROUP = SWA_Q_HEADS // SWA_KV_HEADS
SWA_HD = 64
SWA_Q_DIM = SWA_Q_HEADS * SWA_HD
SWA_KV_DIM = SWA_KV_HEADS * SWA_HD
SWA_PROJ_DIM = SWA_Q_DIM + 2 * SWA_KV_DIM
WINDOW = 128
REL_BUCKETS = 32
REL_MAX_DIST = 128
D_FF = 2816
N_EXPERTS = 8
RMS_EPS = 1e-6

LANES = 128
SUBLANES = 8
VMEM_LIMIT = 56 * 1024 * 1024

TM = 512
NP_TOK = BATCH * SEQ
NS_TOK = DEC_BATCH * DEC_SEQ
TT = NP_TOK + NS_TOK
NP_TILES = NP_TOK // TM
NS_TILES = NS_TOK // TM
N_TILES = NP_TILES + NS_TILES
TILES_PER_SEQ = SEQ // TM
TF = 256
NF = D_FF // TF
TME = 1024
MOE_ROWS = 2 * TT + N_EXPERTS * TME
MOE_TILES = MOE_ROWS // TME
GDN_NB = 2
GDN_PROMPT_PAR = 4
GDN_SAMPLE_PAR = 4
GDN_SAMPLE_ROWS = 8
SWA_SB = 8
SWA_KPAD = WINDOW + 8

assert NP_TOK % TM == 0 and NS_TOK % TM == 0 and SEQ % TM == 0
assert D_FF % TF == 0 and DEC_SEQ <= GDN_SAMPLE_ROWS and DEC_SEQ <= 8


def _cparams(sem):
    return pltpu.CompilerParams(dimension_semantics=sem, vmem_limit_bytes=VMEM_LIMIT)


def _bdot(a, b):
    return jnp.dot(a.astype(BF16), b.astype(BF16), preferred_element_type=F32)


def _bdot_nt(a, b):
    return lax.dot_general(a.astype(BF16), b.astype(BF16), (((1,), (1,)), ((), ())),
                           preferred_element_type=F32)


def _silu(x):
    return x * jax.nn.sigmoid(x)


def _rms(x):
    return x * lax.rsqrt(jnp.mean(x * x, axis=-1, keepdims=True) + RMS_EPS)


def _mod_kernel(c_ref, w_ref, b_ref, o_ref):
    c = c_ref[...]
    o_ref[0] = _bdot(_silu(c), w_ref[0]) + b_ref[0]


def _mod_call(c_all, w_mod, b_mod):
    n = c_all.shape[0]
    tn = D
    return pl.pallas_call(
        _mod_kernel,
        grid=(DEPTH, 6 * D // tn),
        in_specs=[
            pl.BlockSpec((n, D), lambda l, j: (0, 0)),
            pl.BlockSpec((1, D, tn), lambda l, j: (l, 0, j)),
            pl.BlockSpec((1, 1, tn), lambda l, j: (l, 0, j)),
        ],
        out_specs=pl.BlockSpec((1, n, tn), lambda l, j: (l, 0, j)),
        out_shape=jax.ShapeDtypeStruct((DEPTH, n, 6 * D), F32),
        compiler_params=_cparams(("arbitrary", "arbitrary")),
        name="modulation",
    )(c_all, w_mod, b_mod.reshape(DEPTH, 1, 6 * D))


def _mod_specs(chunk, token_axis=0):
    p = pl.BlockSpec((1, 1, D), lambda *ids: (jnp.minimum(ids[token_axis] // TILES_PER_SEQ, BATCH - 1),
                                              0, chunk))
    s = pl.BlockSpec((TM, D), lambda *ids: (jnp.maximum(ids[token_axis] - NP_TILES, 0), chunk))
    return p, s


def _pick(i, p_ref, s_ref):
    return jnp.where(i < NP_TILES, p_ref[0], s_ref[...])


def _prenorm(i, x, g_ref, shp_ref, shs_ref, scp_ref, scs_ref):
    h = _rms(x) * g_ref[...]
    return h * (1.0 + _pick(i, scp_ref, scs_ref)) + _pick(i, shp_ref, shs_ref)


def _prenorm_router_kernel(x_ref, g_ref, shp_ref, shs_ref, scp_ref, scs_ref, wr_ref, br_ref,
                           h_ref, idx_ref, wt_ref):
    h = _prenorm(pl.program_id(0), x_ref[...], g_ref, shp_ref, shs_ref, scp_ref, scs_ref)
    h_ref[...] = h
    lane = lax.broadcasted_iota(I32, (TM, LANES), 1).astype(F32)
    w = wr_ref[...]
    h_hi, w_hi = h.astype(BF16), w.astype(BF16)
    h_lo = (h - h_hi.astype(F32)).astype(BF16)
    w_lo = (w - w_hi.astype(F32)).astype(BF16)
    logits = (jnp.dot(h_hi, w_hi, preferred_element_type=F32)
              + (jnp.dot(h_lo, w_hi, preferred_element_type=F32)
                 + jnp.dot(h_hi, w_lo, preferred_element_type=F32))) + br_ref[...]
    logits = jnp.where(lane < N_EXPERTS, logits, -jnp.inf)
    m1 = jnp.max(logits, axis=-1, keepdims=True)
    i1 = jnp.min(jnp.where(logits == m1, lane, float(LANES)), axis=-1, keepdims=True)
    rest = jnp.where(lane == i1, -jnp.inf, logits)
    m2 = jnp.max(rest, axis=-1, keepdims=True)
    i2 = jnp.min(jnp.where(rest == m2, lane, float(LANES)), axis=-1, keepdims=True)
    e2 = jnp.exp(m2 - m1)
    w1 = 1.0 / (1.0 + e2)
    w2 = e2 / (1.0 + e2)
    idx_ref[...] = jnp.where(lane == 0, i1, jnp.where(lane == 1, i2, 0.0)).astype(I32)
    wt_ref[...] = jnp.where(lane == 0, w1, jnp.where(lane == 1, w2, 0.0))


def _prenorm_router_call(x, g, mod_p, mod_s, sh_chunk, sc_chunk, w_router, b_router):
    shp, shs = _mod_specs(sh_chunk)
    scp, scs = _mod_specs(sc_chunk)
    tile = pl.BlockSpec((TM, D), lambda i: (i, 0))
    small = pl.BlockSpec((TM, LANES), lambda i: (i, 0))
    wr = jnp.pad(w_router, ((0, 0), (0, LANES - N_EXPERTS)))
    br = jnp.pad(b_router, (0, LANES - N_EXPERTS)).reshape(1, LANES)
    return pl.pallas_call(
        _prenorm_router_kernel,
        grid=(N_TILES,),
        in_specs=[tile, pl.BlockSpec((1, D), lambda i: (0, 0)), shp, shs, scp, scs,
                  pl.BlockSpec((D, LANES), lambda i: (0, 0)),
                  pl.BlockSpec((1, LANES), lambda i: (0, 0))],
        out_specs=[tile, small, small],
        out_shape=[jax.ShapeDtypeStruct((TT, D), F32),
                   jax.ShapeDtypeStruct((TT, LANES), I32),
                   jax.ShapeDtypeStruct((TT, LANES), F32)],
        compiler_params=_cparams(("arbitrary",)),
        name="prenorm_router",
    )(x, g.reshape(1, D), mod_p, mod_s, mod_p, mod_s, wr, br)


def _proj_kernel(x_ref, g_ref, shp_ref, shs_ref, scp_ref, scs_ref, w_ref, b_ref, o_ref, wbf_ref):
    i = pl.program_id(1)

    @pl.when(i == 0)
    def _():
        wbf_ref[...] = w_ref[...].astype(BF16)

    h = _prenorm(i, x_ref[...], g_ref, shp_ref, shs_ref, scp_ref, scs_ref).astype(BF16)
    o_ref[...] = jnp.dot(h, wbf_ref[...], preferred_element_type=F32) + b_ref[...]


def _proj_call(x, g, mod_p, mod_s, w, b, tn):
    k, n = w.shape
    shp, shs = _mod_specs(0, token_axis=1)
    scp, scs = _mod_specs(1, token_axis=1)
    return pl.pallas_call(
        _proj_kernel,
        grid=(n // tn, N_TILES),
        in_specs=[
            pl.BlockSpec((TM, k), lambda j, i: (i, 0)),
            pl.BlockSpec((1, k), lambda j, i: (0, 0)), shp, shs, scp, scs,
            pl.BlockSpec((k, tn), lambda j, i: (0, j), pipeline_mode=pl.Buffered(1)),
            pl.BlockSpec((1, tn), lambda j, i: (0, j)),
        ],
        out_specs=pl.BlockSpec((TM, tn), lambda j, i: (i, j)),
        out_shape=jax.ShapeDtypeStruct((TT, n), F32),
        scratch_shapes=[pltpu.VMEM((k, tn), BF16)],
        compiler_params=_cparams(("arbitrary", "arbitrary")),
        name="proj",
    )(x, g.reshape(1, k), mod_p, mod_s, mod_p, mod_s, w, b.reshape(1, n))


def _post_mm_kernel(x_ref, op_ref, os_ref, w_ref, b_ref, g_ref, gtp_ref, gts_ref, out_ref, wbf_ref):
    i = pl.program_id(0)

    @pl.when(i == 0)
    def _():
        wbf_ref[...] = w_ref[...].astype(BF16)

    o = jnp.where(i < NP_TILES, op_ref[...], os_ref[...]).astype(BF16)
    y = jnp.dot(o, wbf_ref[...], preferred_element_type=F32) + b_ref[...]
    out_ref[...] = x_ref[...] + _pick(i, gtp_ref, gts_ref) * (_rms(y) * g_ref[...])


def _post_mm_call(x, o_p, o_s, w, b, g, mod_p, mod_s, gate_chunk):
    gtp, gts = _mod_specs(gate_chunk)
    tile = pl.BlockSpec((TM, D), lambda i: (i, 0))
    row = pl.BlockSpec((1, D), lambda i: (0, 0))
    return pl.pallas_call(
        _post_mm_kernel,
        grid=(N_TILES,),
        in_specs=[tile,
                  pl.BlockSpec((TM, D), lambda i: (jnp.minimum(i, NP_TILES - 1), 0)),
                  pl.BlockSpec((TM, D), lambda i: (jnp.maximum(i - NP_TILES, 0), 0)),
                  pl.BlockSpec((D, D), lambda i: (0, 0)), row, row, gtp, gts],
        out_specs=tile,
        out_shape=jax.ShapeDtypeStruct((TT, D), F32),
        scratch_shapes=[pltpu.VMEM((D, D), BF16)],
        compiler_params=_cparams(("arbitrary",)),
        name="post_mm",
    )(x, o_p, o_s, w, b.reshape(1, D), g.reshape(1, D), mod_p, mod_s)


def _gdn_kernel(*refs, chunk, n_chunks, t_real, n_par):
    x_refs = refs[:n_par]
    ba_refs = refs[n_par:2 * n_par]
    (cw_ref, alog_ref, dt_ref, nw_ref, cb_ref, s0_ref, o_ref, sfin_ref, xbuf, s_scr) = refs[2 * n_par:]
    c_sz = chunk
    tb = chunk * n_chunks
    j = pl.program_id(1)
    rep = GDN_V_HEADS // GDN_QK_HEADS
    units = [(p, h) for p in range(n_par) for h in range(GDN_V_HEADS)]

    @pl.when(j == 0)
    def _():
        s_scr[...] = s0_ref[...]
        xbuf[:, 0:SUBLANES, :] = cb_ref[...]

    for p in range(n_par):
        xbuf[p, SUBLANES:SUBLANES + tb, :] = x_refs[p][:, 0:GDN_CONV_DIM]

    ii = lax.broadcasted_iota(I32, (c_sz, c_sz), 0)
    jj = lax.broadcasted_iota(I32, (c_sz, c_sz), 1)
    causal = ii >= jj
    strict = ii > jj
    tri = causal.astype(F32)
    eye_c = (ii == jj).astype(F32)
    pair_mask = [jnp.logical_and((ii >> (l + 1)) == (jj >> (l + 1)), (ii >> l) != (jj >> l))
                 for l in range(int(math.log2(c_sz)))]
    eye_t = (lax.broadcasted_iota(I32, (2 * SUBLANES, LANES), 0)
             == lax.broadcasted_iota(I32, (2 * SUBLANES, LANES), 1)).astype(F32)
    lane = lax.broadcasted_iota(I32, (c_sz, LANES), 1)
    row = lax.broadcasted_iota(I32, (c_sz, LANES), 0)
    cw = cw_ref[...]
    neg_a = -jnp.exp(alog_ref[...])
    dt = dt_ref[...]
    nw = nw_ref[...]

    def chunk_body(c, carry):
        r0 = pl.multiple_of(c * c_sz, c_sz)
        xc, beta, gc, rows_t = [], [], [], []
        for p in range(n_par):
            win = xbuf[p, pl.ds(r0, c_sz + SUBLANES), :]
            y = win[SUBLANES:SUBLANES + c_sz] * cw[3:4]
            for tap in range(1, CONV_WIDTH):
                y = y + win[SUBLANES - tap:SUBLANES - tap + c_sz] * cw[3 - tap:4 - tap]
            xc.append(_silu(y))
            ba = ba_refs[p][pl.ds(r0, c_sz), :]
            b_p = jax.nn.sigmoid(ba)
            g_p = neg_a * jax.nn.softplus(ba + dt)
            if t_real < tb:
                live = (row + r0) < t_real
                b_p = jnp.where(live, b_p, 0.0)
                g_p = jnp.where(live, g_p, 0.0)
            gc_p = jnp.dot(tri, g_p, precision=HIGHEST, preferred_element_type=F32)
            cols = jnp.where(lane < GDN_V_HEADS, b_p, gc_p)
            rows_t.append(lax.dot_general(eye_t, cols, (((1,), (1,)), ((), ())), precision=HIGHEST,
                                          preferred_element_type=F32))
            beta.append(b_p)
            gc.append(gc_p)

        qn, kn, kk, qk = {}, {}, {}, {}
        for p in range(n_par):
            for hq in range(GDN_QK_HEADS):
                q = xc[p][:, hq * GDN_HD:(hq + 1) * GDN_HD]
                k = xc[p][:, GDN_KEY_DIM + hq * GDN_HD:GDN_KEY_DIM + (hq + 1) * GDN_HD]
                q = q * lax.rsqrt(jnp.sum(q * q, axis=-1, keepdims=True) + RMS_EPS) * (GDN_HD ** -0.5)
                k = k * lax.rsqrt(jnp.sum(k * k, axis=-1, keepdims=True) + RMS_EPS)
                qn[p, hq], kn[p, hq] = q, k
        for key in qn:
            kk[key] = _bdot_nt(kn[key], kn[key])
            qk[key] = _bdot_nt(qn[key], kn[key])

        gcc, gcl, bcol, decay, a_mat, inv = {}, {}, {}, {}, {}, {}
        for (p, h) in units:
            gcc[p, h] = gc[p][:, GDN_V_HEADS + h:GDN_V_HEADS + h + 1]
            gcr = rows_t[p][GDN_V_HEADS + h:GDN_V_HEADS + h + 1, :]
            bcol[p, h] = beta[p][:, h:h + 1]
            gcl[p, h] = gcc[p, h][c_sz - 1:c_sz, :]
            decay[p, h] = jnp.exp(jnp.where(causal, gcc[p, h] - gcr, -jnp.inf))
            a_mat[p, h] = jnp.where(strict, bcol[p, h] * kk[p, h // rep] * decay[p, h], 0.0)
            inv[p, h] = eye_c - jnp.where(pair_mask[0], a_mat[p, h], 0.0)
        for lvl in range(1, len(pair_mask)):
            t1 = {u: _bdot(jnp.where(pair_mask[lvl], a_mat[u], 0.0), inv[u]) for u in units}
            t2 = {u: _bdot(inv[u], t1[u]) for u in units}
            inv = {u: inv[u] - t2[u] for u in units}

        egc = {u: jnp.exp(gcc[u]) for u in units}
        sol = {}
        for (p, h) in units:
            v = xc[p][:, 2 * GDN_KEY_DIM + h * GDN_HD:2 * GDN_KEY_DIM + (h + 1) * GDN_HD]
            rhs = jnp.concatenate([v * bcol[p, h], kn[p, h // rep] * (bcol[p, h] * egc[p, h])], axis=1)
            sol[p, h] = _bdot(inv[p, h], rhs)
        ws = {}
        for (p, h) in units:
            q_dec = qn[p, h // rep] * egc[p, h]
            ws[p, h] = _bdot(jnp.concatenate([sol[p, h][:, GDN_HD:], q_dec], axis=0), s_scr[p, h])
        u_new = {u: sol[u][:, :GDN_HD] - ws[u][:c_sz] for u in units}
        o_part, s_part = {}, {}
        for (p, h) in units:
            qkm = jnp.where(causal, qk[p, h // rep] * decay[p, h], 0.0)
            k_dec = kn[p, h // rep] * jnp.exp(gcl[p, h] - gcc[p, h])
            o_part[p, h] = _bdot(qkm, u_new[p, h])
            s_part[p, h] = lax.dot_general(k_dec.astype(BF16), u_new[p, h].astype(BF16),
                                           (((0,), (0,)), ((), ())), preferred_element_type=F32)
        for (p, h) in units:
            s_scr[p, h] = s_scr[p, h] * jnp.exp(gcl[p, h]) + s_part[p, h]
            o = ws[p, h][c_sz:] + o_part[p, h]
            z = x_refs[p][pl.ds(r0, c_sz), GDN_CONV_DIM + h * GDN_HD:GDN_CONV_DIM + (h + 1) * GDN_HD]
            o_ref[p, pl.ds(r0, c_sz), h * GDN_HD:(h + 1) * GDN_HD] = _rms(o) * nw * _silu(z)
        return carry

    lax.fori_loop(0, n_chunks, chunk_body, 0)
    xbuf[:, 0:SUBLANES, :] = xbuf[:, tb:tb + SUBLANES, :]

    @pl.when(j == pl.num_programs(1) - 1)
    def _():
        sfin_ref[...] = s_scr[...]


def _gdn_call(x, ba, conv_w, a_log, dt_bias, norm_w, conv_buf, s0, *, n_seq, n_steps, chunk,
              n_chunks, t_real, n_par, s0_first_seq=0):
    s0_block = s0_first_seq // n_par
    tb = chunk * n_chunks
    pad = LANES - 2 * GDN_V_HEADS
    alog_row = jnp.pad(a_log, (GDN_V_HEADS, pad)).reshape(1, LANES)
    dt_row = jnp.pad(dt_bias, (GDN_V_HEADS, pad)).reshape(1, LANES)

    def seq_block(width, p):
        return pl.BlockSpec((None, tb, width), lambda s, j: ((s * n_par + p) * n_steps + j, 0, 0))

    const = lambda s, j: (0, 0)
    state_spec = pl.BlockSpec((n_par, GDN_V_HEADS, GDN_HD, GDN_HD), lambda s, j: (s, 0, 0, 0))
    return pl.pallas_call(
        functools.partial(_gdn_kernel, chunk=chunk, n_chunks=n_chunks, t_real=t_real, n_par=n_par),
        grid=(n_seq // n_par, n_steps),
        in_specs=[seq_block(GDN_MAIN_DIM, p) for p in range(n_par)]
                 + [seq_block(LANES, p) for p in range(n_par)]
                 + [pl.BlockSpec((CONV_WIDTH, GDN_CONV_DIM), const),
                    pl.BlockSpec((1, LANES), const), pl.BlockSpec((1, LANES), const),
                    pl.BlockSpec((1, GDN_HD), const),
                    pl.BlockSpec((n_par, SUBLANES, GDN_CONV_DIM), lambda s, j: (s, 0, 0)),
                    pl.BlockSpec((n_par, GDN_V_HEADS, GDN_HD, GDN_HD),
                                 lambda s, j: (s + s0_block, 0, 0, 0))],
        out_specs=[pl.BlockSpec((n_par, None, tb, GDN_VAL_DIM), lambda s, j: (s, j, 0, 0)),
                   state_spec],
        out_shape=[jax.ShapeDtypeStruct((n_seq, n_steps, tb, GDN_VAL_DIM), F32),
                   jax.ShapeDtypeStruct((n_seq, GDN_V_HEADS, GDN_HD, GDN_HD), F32)],
        scratch_shapes=[pltpu.VMEM((n_par, tb + SUBLANES, GDN_CONV_DIM), F32),
                        pltpu.VMEM((n_par, GDN_V_HEADS, GDN_HD, GDN_HD), F32)],
        compiler_params=_cparams(("arbitrary", "arbitrary")),
        name="gdn",
    )(*([x] * n_par), *([ba] * n_par), conv_w, alog_row, dt_row, norm_w.reshape(1, GDN_HD),
      conv_buf, s0)


def _t5_bucket_np(dist):
    dist = np.maximum(dist, 0)
    max_exact = REL_BUCKETS // 2
    ratio = np.log(np.maximum(dist, max_exact).astype(np.float32) / np.float32(max_exact)) \
        / np.float32(math.log(REL_MAX_DIST / max_exact))
    large = max_exact + (ratio.astype(np.float32) * np.float32(REL_BUCKETS - max_exact)).astype(np.int32)
    return np.where(dist < max_exact, dist, np.minimum(large, REL_BUCKETS - 1)).astype(np.int32)


def _bucket_table(qpos, kpos, k_valid):
    dist = qpos[:, None] - kpos[None, :]
    ok = (dist >= 0) & (dist < WINDOW) & (kpos[None, :] >= 0) & k_valid[None, :]
    return np.where(ok, _t5_bucket_np(dist), -1).astype(np.int32)


def _bias_from_buckets(bkt, rb_ref, head):
    def body(b, acc):
        return jnp.where(bkt == b, rb_ref[b, head], acc)
    acc = lax.fori_loop(0, REL_BUCKETS, body, jnp.zeros(bkt.shape, F32))
    return jnp.where(bkt < 0, -jnp.inf, acc)


def _swa_prompt_kernel(rb_ref, sk_ref, bkt_ref, q_ref, kvp_ref, kvc_ref, o_ref, bias_scr):
    first = jnp.logical_and(pl.program_id(0) == 0, pl.program_id(1) == 0)
    n = pl.program_id(1)

    @pl.when(first)
    def _():
        bkt = bkt_ref[...]
        for h in range(SWA_Q_HEADS):
            kv, g = divmod(h, SWA_GROUP)
            bias_scr[kv, g * WINDOW:(g + 1) * WINDOW, :] = _bias_from_buckets(bkt, rb_ref, h)

    q = q_ref[...]
    scale = SWA_HD ** -0.5
    heads = range(SWA_KV_HEADS)
    s_prev, s_cur = [], []
    for kv in heads:
        qs = jnp.concatenate(
            [q[:, (kv * SWA_GROUP + g) * SWA_HD:(kv * SWA_GROUP + g + 1) * SWA_HD]
             for g in range(SWA_GROUP)], axis=0).astype(BF16)
        s_prev.append(_bdot_nt(qs, kvp_ref[:, kv * SWA_HD:(kv + 1) * SWA_HD]))
        s_cur.append(_bdot_nt(qs, kvc_ref[:, kv * SWA_HD:(kv + 1) * SWA_HD]))
    p_prev, p_cur, den = [], [], []
    for kv in heads:
        sp = s_prev[kv] * scale + bias_scr[kv, :, 0:WINDOW]
        sp = jnp.where(n > 0, sp, -jnp.inf)
        sc = s_cur[kv] * scale + bias_scr[kv, :, WINDOW:2 * WINDOW]
        sink = jnp.concatenate(
            [jnp.full((WINDOW, 1), sk_ref[kv * SWA_GROUP + g], F32) for g in range(SWA_GROUP)], axis=0)
        m = jnp.maximum(jnp.maximum(jnp.max(sp, axis=-1, keepdims=True),
                                    jnp.max(sc, axis=-1, keepdims=True)), sink)
        pp = jnp.exp(sp - m)
        pc = jnp.exp(sc - m)
        den.append(jnp.sum(pp, axis=-1, keepdims=True) + jnp.sum(pc, axis=-1, keepdims=True)
                   + jnp.exp(sink - m))
        p_prev.append(pp.astype(BF16))
        p_cur.append(pc.astype(BF16))
    outs = []
    for kv in heads:
        v_prev = kvp_ref[:, SWA_KV_DIM + kv * SWA_HD:SWA_KV_DIM + (kv + 1) * SWA_HD]
        v_cur = kvc_ref[:, SWA_KV_DIM + kv * SWA_HD:SWA_KV_DIM + (kv + 1) * SWA_HD]
        o = (_bdot(p_prev[kv], v_prev) + _bdot(p_cur[kv], v_cur)) / den[kv]
        outs.extend(o[g * WINDOW:(g + 1) * WINDOW] for g in range(SWA_GROUP))
    o_ref[...] = jnp.concatenate(outs, axis=1)


def _swa_prompt_call(qkv, rel_bias, sinks):
    nb = SEQ // WINDOW
    qpos = WINDOW + np.arange(WINDOW)
    kpos = np.arange(2 * WINDOW)
    bkt = jnp.asarray(_bucket_table(qpos, kpos, np.ones(2 * WINDOW, bool)))
    kv_col = SWA_Q_DIM // (2 * SWA_KV_DIM)
    smem = pl.BlockSpec(memory_space=pltpu.SMEM)
    return pl.pallas_call(
        _swa_prompt_kernel,
        grid=(BATCH, nb),
        in_specs=[smem, smem,
                  pl.BlockSpec((WINDOW, 2 * WINDOW), lambda b, n: (0, 0)),
                  pl.BlockSpec((WINDOW, SWA_Q_DIM), lambda b, n: (b * nb + n, 0)),
                  pl.BlockSpec((WINDOW, 2 * SWA_KV_DIM),
                               lambda b, n: (jnp.maximum(b * nb + n - 1, 0), kv_col)),
                  pl.BlockSpec((WINDOW, 2 * SWA_KV_DIM), lambda b, n: (b * nb + n, kv_col))],
        out_specs=pl.BlockSpec((WINDOW, SWA_Q_DIM), lambda b, n: (b * nb + n, 0)),
        out_shape=jax.ShapeDtypeStruct((NP_TOK, SWA_Q_DIM), F32),
        scratch_shapes=[pltpu.VMEM((SWA_KV_HEADS, SWA_GROUP * WINDOW, 2 * WINDOW), F32)],
        compiler_params=_cparams(("arbitrary", "arbitrary")),
        name="swa_prompt",
    )(rel_bias, sinks, bkt, qkv, qkv, qkv)


def _swa_sample_kernel(rb_ref, sk_ref, bkt_ref, q_ref, ck_ref, cv_ref, kn_ref, vn_ref,
                       o_ref, ck_out_ref, cv_out_ref, k_scr, v_scr, bias_scr, sink_scr):
    rows = SWA_GROUP * DEC_SEQ
    k_scr[:, 0:WINDOW, :] = ck_ref[...]
    k_scr[:, WINDOW:SWA_KPAD, :] = kn_ref[...]
    v_scr[:, 0:WINDOW, :] = cv_ref[...]
    v_scr[:, WINDOW:SWA_KPAD, :] = vn_ref[...]
    ck_out_ref[...] = k_scr[:, DEC_SEQ:DEC_SEQ + WINDOW, :]
    cv_out_ref[...] = v_scr[:, DEC_SEQ:DEC_SEQ + WINDOW, :]
    k_ref, v_ref = k_scr, v_scr

    def group_of_row(shape):
        r = lax.broadcasted_iota(I32, shape, 0)
        return sum((r >= g * DEC_SEQ).astype(I32) for g in range(1, SWA_GROUP))

    @pl.when(pl.program_id(0) == 0)
    def _():
        bkt = bkt_ref[...]
        grp = group_of_row((rows, 1))
        grp_full = group_of_row(bkt.shape)
        for kv in range(SWA_KV_HEADS):
            acc = jnp.zeros(bkt.shape, F32)
            snk = jnp.zeros((rows, 1), F32)
            for g in range(SWA_GROUP):
                head = kv * SWA_GROUP + g
                acc = jnp.where(grp_full == g, _bias_from_buckets(bkt, rb_ref, head), acc)
                snk = jnp.where(grp == g, sk_ref[head], snk)
            bias_scr[kv] = acc
            sink_scr[kv] = snk

    scale = SWA_HD ** -0.5
    units = [(s, kv) for s in range(SWA_SB) for kv in range(SWA_KV_HEADS)]
    scores = {(s, kv): _bdot_nt(q_ref[s, kv], k_ref[s, :, kv * SWA_HD:(kv + 1) * SWA_HD])
              for (s, kv) in units}
    probs, den = {}, {}
    for (s, kv) in units:
        sc = scores[s, kv] * scale + bias_scr[kv]
        sink = sink_scr[kv]
        m = jnp.maximum(jnp.max(sc, axis=-1, keepdims=True), sink)
        p = jnp.exp(sc - m)
        den[s, kv] = jnp.sum(p, axis=-1, keepdims=True) + jnp.exp(sink - m)
        probs[s, kv] = p.astype(BF16)
    for (s, kv) in units:
        o_ref[s, kv] = _bdot(probs[s, kv], v_ref[s, :, kv * SWA_HD:(kv + 1) * SWA_HD]) / den[s, kv]


def _swa_sample_call(q_st, cache_k, cache_v, k_new, v_new, layer_pair, rel_bias, sinks):
    rows = SWA_GROUP * DEC_SEQ
    n_keys = WINDOW + DEC_SEQ
    qpos = PAST_LEN + np.arange(DEC_SEQ)
    kpos = PAST_LEN - WINDOW + np.arange(SWA_KPAD)
    bkt4 = _bucket_table(qpos, kpos, np.arange(SWA_KPAD) < n_keys)
    bkt = jnp.asarray(np.tile(bkt4, (SWA_GROUP, 1)))
    steps = DEC_BATCH // SWA_SB
    smem = pl.BlockSpec(memory_space=pltpu.SMEM)
    q_spec = pl.BlockSpec((SWA_SB, SWA_KV_HEADS, rows, SWA_HD), lambda i: (i, 0, 0, 0))
    win_in = pl.BlockSpec((SWA_SB, WINDOW, SWA_KV_DIM), lambda i: (layer_pair * steps + i, 0, 0))
    win_out = pl.BlockSpec((SWA_SB, WINDOW, SWA_KV_DIM), lambda i: (i, 0, 0))
    new_spec = pl.BlockSpec((SWA_SB, SWA_KPAD - WINDOW, SWA_KV_DIM), lambda i: (i, 0, 0))
    win_shape = jax.ShapeDtypeStruct((DEC_BATCH, WINDOW, SWA_KV_DIM), F32)
    return pl.pallas_call(
        _swa_sample_kernel,
        grid=(steps,),
        in_specs=[smem, smem, pl.BlockSpec((rows, SWA_KPAD), lambda i: (0, 0)),
                  q_spec, win_in, win_in, new_spec, new_spec],
        out_specs=[q_spec, win_out, win_out],
        out_shape=[jax.ShapeDtypeStruct((DEC_BATCH, SWA_KV_HEADS, rows, SWA_HD), F32),
                   win_shape, win_shape],
        scratch_shapes=[pltpu.VMEM((SWA_SB, SWA_KPAD, SWA_KV_DIM), F32),
                        pltpu.VMEM((SWA_SB, SWA_KPAD, SWA_KV_DIM), F32),
                        pltpu.VMEM((SWA_KV_HEADS, rows, SWA_KPAD), F32),
                        pltpu.VMEM((SWA_KV_HEADS, rows, 1), F32)],
        compiler_params=_cparams(("arbitrary",)),
        name="swa_sample",
    )(rel_bias, sinks, bkt, q_st, cache_k, cache_v, k_new, v_new)


def _ffn_scratch(rows):
    return [pltpu.VMEM((rows, D), F32),
            pltpu.VMEM((NF, D, TF), BF16),
            pltpu.VMEM((NF, D, TF), BF16),
            pltpu.VMEM((NF, TF, D), BF16),
            pltpu.VMEM((2, D, TF), F32),
            pltpu.VMEM((2, D, TF), F32),
            pltpu.VMEM((2, TF, D), F32),
            pltpu.SemaphoreType.DMA((3, 2))]


def _swiglu_into(acc_ref, x, expert, load, w_hbm, res, stg, sem):
    wg_hbm, wu_hbm, wd_hbm = w_hbm
    res_g, res_u, res_d = res
    stg_g, stg_u, stg_d = stg

    def copies(f, slot):
        col = pl.multiple_of(f * TF, TF)
        return (pltpu.make_async_copy(wg_hbm.at[expert, :, pl.ds(col, TF)], stg_g.at[slot], sem.at[0, slot]),
                pltpu.make_async_copy(wu_hbm.at[expert, :, pl.ds(col, TF)], stg_u.at[slot], sem.at[1, slot]),
                pltpu.make_async_copy(wd_hbm.at[expert, pl.ds(col, TF), :], stg_d.at[slot], sem.at[2, slot]))

    def block(f):
        gate = jnp.dot(x, res_g[f], preferred_element_type=F32)
        up = jnp.dot(x, res_u[f], preferred_element_type=F32)
        act = (_silu(gate) * up).astype(BF16)
        acc_ref[...] += jnp.dot(act, res_d[f], preferred_element_type=F32)

    acc_ref[...] = jnp.zeros(acc_ref.shape, acc_ref.dtype)

    @pl.when(load)
    def _():
        for c in copies(0, 0):
            c.start()

        def body(f, carry):
            slot = lax.rem(f, 2)

            @pl.when(f + 1 < NF)
            def _():
                for c in copies(f + 1, 1 - slot):
                    c.start()

            for c in copies(f, slot):
                c.wait()
            res_g[f] = stg_g[slot].astype(BF16)
            res_u[f] = stg_u[slot].astype(BF16)
            res_d[f] = stg_d[slot].astype(BF16)
            block(f)
            return carry

        lax.fori_loop(0, NF, body, 0)

    @pl.when(jnp.logical_not(load))
    def _():
        def body(f, carry):
            block(f)
            return carry

        lax.fori_loop(0, NF, body, 0)


def _ffn_kernel(te_ref, ld_ref, hf_ref, nt_ref, x_ref, wg_hbm, wu_hbm, wd_hbm, o_ref, acc_ref, *scratch):
    t = pl.program_id(0)
    used = t < nt_ref[0]
    half = hf_ref[t] != 0
    half_rows = TME // 2

    def run(rows):
        _swiglu_into(acc_ref.at[pl.ds(0, rows)], x_ref[0:rows, :].astype(BF16), te_ref[t],
                     ld_ref[t] != 0, (wg_hbm, wu_hbm, wd_hbm), scratch[0:3], scratch[3:6], scratch[6])
        o_ref[0:rows, :] = acc_ref[0:rows, :]

    @pl.when(jnp.logical_and(used, jnp.logical_not(half)))
    def _():
        run(TME)

    @pl.when(jnp.logical_and(used, half))
    def _():
        run(half_rows)
        o_ref[half_rows:, :] = jnp.zeros((TME - half_rows, D), F32)

    @pl.when(jnp.logical_not(used))
    def _():
        o_ref[...] = jnp.zeros_like(o_ref)


def _ffn_call(x, tile_expert, tile_half, n_tiles_used, w_gate, w_up, w_down):
    n_rows = x.shape[0]
    n_tiles = n_rows // TME
    changed = jnp.concatenate([jnp.ones((1,), I32),
                               (tile_expert[1:] != tile_expert[:-1]).astype(I32)])
    hbm = pl.BlockSpec(memory_space=pl.ANY)
    grid_spec = pltpu.PrefetchScalarGridSpec(
        num_scalar_prefetch=4,
        grid=(n_tiles,),
        in_specs=[pl.BlockSpec((TME, D), lambda t, te, ld, hf, nt: (jnp.minimum(t, nt[0] - 1), 0)),
                  hbm, hbm, hbm],
        out_specs=pl.BlockSpec((TME, D), lambda t, *_: (t, 0)),
        scratch_shapes=_ffn_scratch(TME),
    )
    return pl.pallas_call(
        _ffn_kernel,
        grid_spec=grid_spec,
        out_shape=jax.ShapeDtypeStruct((n_rows, D), F32),
        compiler_params=_cparams(("arbitrary",)),
        name="ffn",
    )(tile_expert, changed, tile_half, n_tiles_used, x, w_gate, w_up, w_down)


def _dense_ffn_kernel(x_ref, gpre_ref, shp_ref, shs_ref, scp_ref, scs_ref, wg_hbm, wu_hbm, wd_hbm,
                      gpost_ref, gtp_ref, gts_ref, out_ref, acc_ref, *scratch, layer_pair):
    i = pl.program_id(0)
    x = x_ref[...]
    h = _prenorm(i, x, gpre_ref, shp_ref, shs_ref, scp_ref, scs_ref).astype(BF16)
    _swiglu_into(acc_ref, h, layer_pair, i == 0, (wg_hbm, wu_hbm, wd_hbm),
                 scratch[0:3], scratch[3:6], scratch[6])
    out_ref[...] = x + _pick(i, gtp_ref, gts_ref) * (_rms(acc_ref[...]) * gpost_ref[...])


def _dense_ffn_call(x, g_pre, g_post, mod_p, mod_s, w_gate, w_up, w_down, layer_pair):
    shp, shs = _mod_specs(3)
    scp, scs = _mod_specs(4)
    gtp, gts = _mod_specs(5)
    tile = pl.BlockSpec((TM, D), lambda i: (i, 0))
    row = pl.BlockSpec((1, D), lambda i: (0, 0))
    hbm = pl.BlockSpec(memory_space=pl.ANY)
    return pl.pallas_call(
        functools.partial(_dense_ffn_kernel, layer_pair=layer_pair),
        grid=(N_TILES,),
        in_specs=[tile, row, shp, shs, scp, scs, hbm, hbm, hbm, row, gtp, gts],
        out_specs=tile,
        out_shape=jax.ShapeDtypeStruct((TT, D), F32),
        scratch_shapes=_ffn_scratch(TM),
        compiler_params=_cparams(("arbitrary",)),
        name="dense_ffn",
    )(x, g_pre.reshape(1, D), mod_p, mod_s, mod_p, mod_s, w_gate, w_up, w_down,
      g_post.reshape(1, D), mod_p, mod_s)


DMA_UNROLL = 8


def _row_copy(src_ref, dst_ref, src_row, dst_row, sem):
    return pltpu.make_async_copy(src_ref.at[pl.ds(src_row, 1)], dst_ref.at[pl.ds(dst_row, 1)], sem)


def _dispatch_kernel(tail_ref, nt_ref, d0_ref, d1_ref, h_ref, out_ref, zero_ref, sem, zsem):
    @pl.when(pl.program_id(0) == 0)
    def _():
        zero_ref[...] = jnp.zeros_like(zero_ref)

        def fill_tile(row0):
            fill = pltpu.make_async_copy(zero_ref, out_ref.at[pl.ds(pl.multiple_of(row0, TME), TME)], zsem)
            fill.start()
            fill.wait()

        for e in range(N_EXPERTS):
            @pl.when(tail_ref[e] >= 0)
            def _():
                fill_tile(tail_ref[e])

            @pl.when(nt_ref[0] + e < MOE_TILES)
            def _():
                fill_tile((nt_ref[0] + e) * TME)

    def start(r, carry):
        _row_copy(h_ref, out_ref, r, d0_ref[0, 0, r], sem).start(priority=0)
        _row_copy(h_ref, out_ref, r, d1_ref[0, 0, r], sem).start(priority=1)
        return carry

    def wait(r, carry):
        _row_copy(h_ref, out_ref, 0, 0, sem).wait()
        _row_copy(h_ref, out_ref, 0, 0, sem).wait()
        return carry

    lax.fori_loop(0, TM, start, 0, unroll=DMA_UNROLL)
    lax.fori_loop(0, TM, wait, 0, unroll=DMA_UNROLL)


def _dispatch_call(h, dest, tail_rows, n_tiles_used):
    idx_spec = pl.BlockSpec((1, 1, TM), lambda i, *_: (i, 0, 0), memory_space=pltpu.SMEM)
    grid_spec = pltpu.PrefetchScalarGridSpec(
        num_scalar_prefetch=2,
        grid=(N_TILES,),
        in_specs=[idx_spec, idx_spec, pl.BlockSpec((TM, D), lambda i, *_: (i, 0))],
        out_specs=pl.BlockSpec(memory_space=pl.ANY),
        scratch_shapes=[pltpu.VMEM((TME, D), h.dtype), pltpu.SemaphoreType.DMA(()),
                        pltpu.SemaphoreType.DMA(())],
    )
    return pl.pallas_call(
        _dispatch_kernel,
        grid_spec=grid_spec,
        out_shape=jax.ShapeDtypeStruct((MOE_ROWS, D), h.dtype),
        compiler_params=_cparams(("arbitrary",)),
        name="moe_dispatch",
    )(tail_rows, n_tiles_used, dest[0].reshape(N_TILES, 1, TM), dest[1].reshape(N_TILES, 1, TM), h)


def _combine_kernel(d0_ref, d1_ref, x_ref, wt_ref, ys_ref, g_ref, gtp_ref, gts_ref, *rest, split):
    i = pl.program_id(0)
    buf, sem = rest[-2:]

    def start(r, carry):
        _row_copy(ys_ref, buf.at[0], d0_ref[0, 0, r], r, sem).start(priority=0)
        _row_copy(ys_ref, buf.at[1], d1_ref[0, 0, r], r, sem).start(priority=1)
        return carry

    def wait(r, carry):
        _row_copy(ys_ref, buf.at[0], 0, 0, sem).wait()
        _row_copy(ys_ref, buf.at[1], 0, 0, sem).wait()
        return carry

    lax.fori_loop(0, TM, start, 0, unroll=DMA_UNROLL)
    lax.fori_loop(0, TM, wait, 0, unroll=DMA_UNROLL)
    wt = wt_ref[...]
    y = wt[:, 0:1] * buf[0] + wt[:, 1:2] * buf[1]
    out = x_ref[...] + _pick(i, gtp_ref, gts_ref) * (_rms(y) * g_ref[...])
    if split:
        prompt_ref, sample_ref = rest[:2]

        @pl.when(i < NP_TILES)
        def _():
            prompt_ref[...] = out

        @pl.when(i >= NP_TILES)
        def _():
            sample_ref[...] = out
    else:
        rest[0][...] = out


def _combine_call(x, ys, dest, wts, g, mod_p, mod_s, gate_chunk, split):
    gtp, gts = _mod_specs(gate_chunk)
    tile = pl.BlockSpec((TM, D), lambda i: (i, 0))
    idx_spec = pl.BlockSpec((1, 1, TM), lambda i: (i, 0, 0), memory_space=pltpu.SMEM)
    if split:
        out_specs = [pl.BlockSpec((TM, D), lambda i: (jnp.minimum(i, NP_TILES - 1), 0)),
                     pl.BlockSpec((TM, D), lambda i: (jnp.maximum(i - NP_TILES, 0), 0))]
        out_shape = [jax.ShapeDtypeStruct((NP_TOK, D), F32), jax.ShapeDtypeStruct((NS_TOK, D), F32)]
    else:
        out_specs = tile
        out_shape = jax.ShapeDtypeStruct((TT, D), F32)
    return pl.pallas_call(
        functools.partial(_combine_kernel, split=split),
        grid=(N_TILES,),
        in_specs=[idx_spec, idx_spec, tile, pl.BlockSpec((TM, LANES), lambda i: (i, 0)),
                  pl.BlockSpec(memory_space=pl.ANY),
                  pl.BlockSpec((1, D), lambda i: (0, 0)), gtp, gts],
        out_specs=out_specs,
        out_shape=out_shape,
        scratch_shapes=[pltpu.VMEM((2, TM, D), F32), pltpu.SemaphoreType.DMA(())],
        compiler_params=_cparams(("arbitrary",)),
        name="moe_combine",
    )(dest[0].reshape(N_TILES, 1, TM), dest[1].reshape(N_TILES, 1, TM), x, wts, ys,
      g.reshape(1, D), mod_p, mod_s)


def _route(top_idx):
    flat_e = top_idx.T.reshape(-1)
    onehot = (flat_e[:, None] == jnp.arange(N_EXPERTS, dtype=I32)[None, :]).astype(I32)
    csum = jnp.cumsum(onehot, axis=0)
    rank = jnp.sum((csum - onehot) * onehot, axis=1)
    counts = csum[-1]
    padded = ((counts + TME - 1) // TME) * TME
    ends = jnp.cumsum(padded)
    starts = ends - padded
    dest = (jnp.sum(onehot * starts[None, :], axis=1) + rank).astype(I32)
    tile_start = jnp.arange(MOE_TILES, dtype=I32) * TME
    tile_expert = jnp.minimum(jnp.sum(tile_start[:, None] >= ends[None, :], axis=1),
                              N_EXPERTS - 1).astype(I32)
    n_used = (ends[-1] // TME).astype(I32).reshape(1)
    tail_rows = jnp.where(padded > 0, ends - TME, -1).astype(I32)
    in_tile = jnp.sum((tile_expert[:, None] == jnp.arange(N_EXPERTS, dtype=I32)[None, :])
                      * (starts + counts)[None, :], axis=1) - tile_start
    tile_half = (in_tile <= TME // 2).astype(I32)
    return dest.reshape(2, TT), tile_expert, tile_half, n_used, tail_rows


def _gdn_layer(x, g_pre, w_in, conv_w, a_log, dt_bias, norm_w, w_out, state_conv, state_rec_all,
               layer_pair, g_post, mod_p, mod_s):
    w_main = w_in[:, :GDN_MAIN_DIM]
    w_ba = jnp.pad(w_in[:, GDN_MAIN_DIM:], ((0, 0), (0, LANES - 2 * GDN_V_HEADS)))
    qkvz = _proj_call(x, g_pre, mod_p, mod_s, w_main, jnp.zeros((GDN_MAIN_DIM,), F32),
                      GDN_MAIN_DIM)
    ba = _proj_call(x, g_pre, mod_p, mod_s, w_ba, jnp.zeros((LANES,), F32), LANES)

    tb = GDN_CHUNK * GDN_NB
    n_steps = SEQ // tb
    o_p, rec_p = _gdn_call(
        qkvz.reshape(TT // tb, tb, GDN_MAIN_DIM), ba.reshape(TT // tb, tb, LANES),
        conv_w, a_log, dt_bias, norm_w,
        jnp.zeros((BATCH, SUBLANES, GDN_CONV_DIM), F32),
        jnp.zeros((BATCH, GDN_V_HEADS, GDN_HD, GDN_HD), F32),
        n_seq=BATCH, n_steps=n_steps, chunk=GDN_CHUNK, n_chunks=GDN_NB, t_real=tb,
        n_par=GDN_PROMPT_PAR)
    o_p = o_p.reshape(NP_TOK, GDN_VAL_DIM)

    row_pad = ((0, 0), (0, GDN_SAMPLE_ROWS - DEC_SEQ), (0, 0))
    qkvz_s = qkvz[NP_TOK:].reshape(DEC_BATCH, DEC_SEQ, GDN_MAIN_DIM)
    ba_s = ba[NP_TOK:].reshape(DEC_BATCH, DEC_SEQ, LANES)
    cbuf_s = jnp.pad(state_conv, ((0, 0), (SUBLANES - (CONV_WIDTH - 1), 0), (0, 0)))
    o_s, rec_s = _gdn_call(
        jnp.pad(qkvz_s, row_pad), jnp.pad(ba_s, row_pad), conv_w, a_log, dt_bias, norm_w,
        cbuf_s, state_rec_all,
        n_seq=DEC_BATCH, n_steps=1, chunk=GDN_SAMPLE_ROWS, n_chunks=1, t_real=DEC_SEQ,
        n_par=GDN_SAMPLE_PAR, s0_first_seq=layer_pair * DEC_BATCH)
    o_s = o_s[:, 0, :DEC_SEQ].reshape(NS_TOK, GDN_VAL_DIM)

    keep = CONV_WIDTH - 1
    pre_p = jnp.stack([qkvz[(b + 1) * SEQ - keep:(b + 1) * SEQ, :GDN_CONV_DIM] for b in range(BATCH)])
    pre_s = jnp.concatenate([state_conv, qkvz_s[:, :, :GDN_CONV_DIM]], axis=1)[:, -keep:]
    x = _post_mm_call(x, o_p, o_s, w_out, jnp.zeros((D,), F32), g_post, mod_p, mod_s, 2)
    return x, pre_p, rec_p, pre_s, rec_s


def _swa_layer(x, g_pre, w_in, b_in, sinks, w_out, b_out, rel_bias, cache_k_all, cache_v_all,
               layer_pair, g_post, mod_p, mod_s):
    qkv = _proj_call(x, g_pre, mod_p, mod_s, w_in, b_in, SWA_PROJ_DIM)
    o_p = _swa_prompt_call(qkv, rel_bias, sinks)
    last = jnp.stack([qkv[(b + 1) * SEQ - WINDOW:(b + 1) * SEQ, SWA_Q_DIM:] for b in range(BATCH)])
    new_k_p = last[:, :, :SWA_KV_DIM]
    new_v_p = last[:, :, SWA_KV_DIM:]

    qkv_s = qkv[NP_TOK:].reshape(DEC_BATCH, DEC_SEQ, SWA_PROJ_DIM)
    q_st = qkv_s[:, :, :SWA_Q_DIM].reshape(DEC_BATCH, DEC_SEQ, SWA_KV_HEADS, SWA_GROUP, SWA_HD)
    q_st = q_st.transpose(0, 2, 3, 1, 4).reshape(DEC_BATCH, SWA_KV_HEADS, SWA_GROUP * DEC_SEQ, SWA_HD)
    row_pad = ((0, 0), (0, SWA_KPAD - WINDOW - DEC_SEQ), (0, 0))
    k_new = jnp.pad(qkv_s[:, :, SWA_Q_DIM:SWA_Q_DIM + SWA_KV_DIM], row_pad)
    v_new = jnp.pad(qkv_s[:, :, SWA_Q_DIM + SWA_KV_DIM:], row_pad)
    o_st, new_k_s, new_v_s = _swa_sample_call(q_st, cache_k_all, cache_v_all, k_new, v_new,
                                              layer_pair, rel_bias, sinks)
    o_s = o_st.reshape(DEC_BATCH, SWA_KV_HEADS, SWA_GROUP, DEC_SEQ, SWA_HD)
    o_s = o_s.transpose(0, 3, 1, 2, 4).reshape(NS_TOK, SWA_Q_DIM)

    x = _post_mm_call(x, o_p, o_s, w_out, b_out, g_post, mod_p, mod_s, 2)
    shape_p = (BATCH, WINDOW, SWA_KV_HEADS, SWA_HD)
    shape_s = (DEC_BATCH, WINDOW, SWA_KV_HEADS, SWA_HD)
    return (x, new_k_p.reshape(shape_p), new_v_p.reshape(shape_p),
            new_k_s.reshape(shape_s), new_v_s.reshape(shape_s))


def kernel(x_prompt, x_sample, c_prompt, c_sample, state_conv, state_rec, cache_win_k, cache_win_v, w_mod, b_mod, g_pre_mix, g_post_mix, g_pre_ffn, g_post_ffn, gdn_w_in, gdn_conv_w, gdn_a_log, gdn_dt_bias, gdn_norm_w, gdn_w_out, swa_w_in, swa_b_in, swa_sinks, swa_w_out, swa_b_out, rel_bias, ffn_w_gate, ffn_w_up, ffn_w_down, moe_w_router, moe_b_router, moe_w_gate, moe_w_up, moe_w_down):
    x = jnp.concatenate([x_prompt.reshape(NP_TOK, D), x_sample.reshape(NS_TOK, D)], axis=0)
    n_c = BATCH + DEC_BATCH
    c_all = jnp.concatenate([c_prompt, c_sample, jnp.zeros((-n_c % SUBLANES, D), F32)], axis=0)
    m_all = _mod_call(c_all, w_mod, b_mod)
    mod_s_all = jnp.repeat(m_all[:, BATCH:n_c], DEC_SEQ, axis=1)
    ffn_w = (ffn_w_gate, ffn_w_up, ffn_w_down)
    moe_w = (moe_w_gate.reshape(-1, D, D_FF), moe_w_up.reshape(-1, D, D_FF),
             moe_w_down.reshape(-1, D_FF, D))
    state_rec_all = state_rec.reshape((-1,) + state_rec.shape[2:])
    assert cache_win_k.shape[2] == WINDOW
    cache_k_all = cache_win_k.reshape(-1, WINDOW, SWA_KV_DIM)
    cache_v_all = cache_win_v.reshape(-1, WINDOW, SWA_KV_DIM)

    conv_p, rec_p, conv_s, rec_s = [], [], [], []
    k_p, v_p, k_s, v_s = [], [], [], []
    for layer in range(DEPTH):
        j = layer // 2
        mod_p = m_all[layer, :BATCH].reshape(BATCH, 1, 6 * D)
        mod_s = mod_s_all[layer]
        if layer % 2 == 0:
            x, cp, rp, cs, rs = _gdn_layer(
                x, g_pre_mix[layer], gdn_w_in[j], gdn_conv_w[j], gdn_a_log[j], gdn_dt_bias[j],
                gdn_norm_w[j], gdn_w_out[j], state_conv[j], state_rec_all, j, g_post_mix[layer],
                mod_p, mod_s)
            conv_p.append(cp); rec_p.append(rp); conv_s.append(cs); rec_s.append(rs)
        else:
            x, kp, vp, ks, vs = _swa_layer(
                x, g_pre_mix[layer], swa_w_in[j], swa_b_in[j], swa_sinks[j], swa_w_out[j],
                swa_b_out[j], rel_bias, cache_k_all, cache_v_all, j, g_post_mix[layer],
                mod_p, mod_s)
            k_p.append(kp); v_p.append(vp); k_s.append(ks); v_s.append(vs)

        if layer % 2 == 0:
            x = _dense_ffn_call(x, g_pre_ffn[layer], g_post_ffn[layer], mod_p, mod_s, *ffn_w, j)
        else:
            h, idx, wts = _prenorm_router_call(x, g_pre_ffn[layer], mod_p, mod_s, 3, 4,
                                               moe_w_router[j], moe_b_router[j])
            dest, tile_expert, tile_half, n_used, tail_rows = _route(idx[:, :2])
            xs = _dispatch_call(h, dest, tail_rows, n_used)
            ys = _ffn_call(xs, tile_expert + j * N_EXPERTS, tile_half, n_used, *moe_w)
            x = _combine_call(x, ys, dest, wts, g_post_ffn[layer], mod_p, mod_s, 5,
                              split=layer == DEPTH - 1)

    x_p, x_s = x if isinstance(x, (list, tuple)) else (x[:NP_TOK], x[NP_TOK:])
    y_prompt = x_p.reshape(BATCH, SEQ, D)
    y_sample = x_s.reshape(DEC_BATCH, DEC_SEQ, D)
    return (y_prompt, y_sample, jnp.stack(conv_p), jnp.stack(rec_p), jnp.stack(k_p), jnp.stack(v_p),
            jnp.stack(conv_s), jnp.stack(rec_s), jnp.stack(k_s), jnp.stack(v_s))
```

```python
import functools
import math

import numpy as np
import jax
import jax.numpy as jnp
from jax import lax
from jax.experimental import pallas as pl
from jax.experimental.pallas import tpu as pltpu

F32 = jnp.float32
BF16 = jnp.bfloat16
I32 = jnp.int32
HIGHEST = lax.Precision.HIGHEST

D = 1024
BATCH = 4
SEQ = 4096
DEPTH = 4
DEC_BATCH = 128
DEC_SEQ = 4
PAST_LEN = 8192
GDN_QK_HEADS = 4
GDN_V_HEADS = 8
GDN_HD = 128
GDN_KEY_DIM = GDN_QK_HEADS * GDN_HD
GDN_VAL_DIM = GDN_V_HEADS * GDN_HD
GDN_CONV_DIM = 2 * GDN_KEY_DIM + GDN_VAL_DIM
GDN_MAIN_DIM = GDN_CONV_DIM + GDN_VAL_DIM
GDN_PROJ_PAD = GDN_MAIN_DIM + 128
CONV_WIDTH = 4
GDN_CHUNK = 64
SWA_Q_HEADS = 16
SWA_KV_HEADS = 4
SWA_GROUP = SWA_Q_HEADS // SWA_KV_HEADS
SWA_HD = 64
SWA_Q_DIM = SWA_Q_HEADS * SWA_HD
SWA_KV_DIM = SWA_KV_HEADS * SWA_HD
SWA_PROJ_DIM = SWA_Q_DIM + 2 * SWA_KV_DIM
WINDOW = 128
REL_BUCKETS = 32
REL_MAX_DIST = 128
D_FF = 2816
N_EXPERTS = 8
RMS_EPS = 1e-6

LANES = 128
SUBLANES = 8
VMEM_LIMIT = 56 * 1024 * 1024

TM = 512
NP_TOK = BATCH * SEQ
NS_TOK = DEC_BATCH * DEC_SEQ
TT = NP_TOK + NS_TOK
NP_TILES = NP_TOK // TM
NS_TILES = NS_TOK // TM
N_TILES = NP_TILES + NS_TILES
TILES_PER_SEQ = SEQ // TM
TF = 256
NF = D_FF // TF
TME = 1024
MOE_ROWS = 2 * TT + N_EXPERTS * TME
MOE_TILES = MOE_ROWS // TME
GDN_NB = 2
GDN_PROMPT_PAR = 4
GDN_SAMPLE_PAR = 4
GDN_SAMPLE_ROWS = 8
SWA_SB = 8
SWA_KPAD = WINDOW + 8

assert NP_TOK % TM == 0 and NS_TOK % TM == 0 and SEQ % TM == 0
assert D_FF % TF == 0 and DEC_SEQ <= GDN_SAMPLE_ROWS and DEC_SEQ <= 8


def _cparams(sem):
    return pltpu.CompilerParams(dimension_semantics=sem, vmem_limit_bytes=VMEM_LIMIT)


def _bdot(a, b):
    return jnp.dot(a.astype(BF16), b.astype(BF16), preferred_element_type=F32)


def _bdot_nt(a, b):
    return lax.dot_general(a.astype(BF16), b.astype(BF16), (((1,), (1,)), ((), ())),
                           preferred_element_type=F32)


def _silu(x):
    return x * jax.nn.sigmoid(x)


def _rms(x):
    return x * lax.rsqrt(jnp.mean(x * x, axis=-1, keepdims=True) + RMS_EPS)


def _mod_kernel(c_ref, w_ref, b_ref, o_ref):
    c = c_ref[...]
    o_ref[0] = _bdot(_silu(c), w_ref[0]) + b_ref[0]


def _mod_call(c_all, w_mod, b_mod):
    n = c_all.shape[0]
    tn = D
    return pl.pallas_call(
        _mod_kernel,
        grid=(DEPTH, 6 * D // tn),
        in_specs=[
            pl.BlockSpec((n, D), lambda l, j: (0, 0)),
            pl.BlockSpec((1, D, tn), lambda l, j: (l, 0, j)),
            pl.BlockSpec((1, 1, tn), lambda l, j: (l, 0, j)),
        ],
        out_specs=pl.BlockSpec((1, n, tn), lambda l, j: (l, 0, j)),
        out_shape=jax.ShapeDtypeStruct((DEPTH, n, 6 * D), F32),
        compiler_params=_cparams(("arbitrary", "arbitrary")),
        name="modulation",
    )(c_all, w_mod, b_mod.reshape(DEPTH, 1, 6 * D))


def _mod_specs(chunk, token_axis=0):
    p = pl.BlockSpec((1, 1, D), lambda *ids: (jnp.minimum(ids[token_axis] // TILES_PER_SEQ, BATCH - 1),
                                              0, chunk))
    s = pl.BlockSpec((TM, D), lambda *ids: (jnp.maximum(ids[token_axis] - NP_TILES, 0), chunk))
    return p, s


def _pick(i, p_ref, s_ref):
    return jnp.where(i < NP_TILES, p_ref[0], s_ref[...])


def _prenorm(i, x, g_ref, shp_ref, shs_ref, scp_ref, scs_ref):
    h = _rms(x) * g_ref[...]
    return h * (1.0 + _pick(i, scp_ref, scs_ref)) + _pick(i, shp_ref, shs_ref)


def _prenorm_router_kernel(x_ref, g_ref, shp_ref, shs_ref, scp_ref, scs_ref, wr_ref, br_ref,
                           h_ref, idx_ref, wt_ref):
    h = _prenorm(pl.program_id(0), x_ref[...], g_ref, shp_ref, shs_ref, scp_ref, scs_ref)
    h_ref[...] = h
    lane = lax.broadcasted_iota(I32, (TM, LANES), 1).astype(F32)
    w = wr_ref[...]
    h_hi, w_hi = h.astype(BF16), w.astype(BF16)
    h_lo = (h - h_hi.astype(F32)).astype(BF16)
    w_lo = (w - w_hi.astype(F32)).astype(BF16)
    logits = (jnp.dot(h_hi, w_hi, preferred_element_type=F32)
              + (jnp.dot(h_lo, w_hi, preferred_element_type=F32)
                 + jnp.dot(h_hi, w_lo, preferred_element_type=F32))) + br_ref[...]
    logits = jnp.where(lane < N_EXPERTS, logits, -jnp.inf)
    m1 = jnp.max(logits, axis=-1, keepdims=True)
    i1 = jnp.min(jnp.where(logits == m1, lane, float(LANES)), axis=-1, keepdims=True)
    rest = jnp.where(lane == i1, -jnp.inf, logits)
    m2 = jnp.max(rest, axis=-1, keepdims=True)
    i2 = jnp.min(jnp.where(rest == m2, lane, float(LANES)), axis=-1, keepdims=True)
    e2 = jnp.exp(m2 - m1)
    w1 = 1.0 / (1.0 + e2)
    w2 = e2 / (1.0 + e2)
    idx_ref[...] = jnp.where(lane == 0, i1, jnp.where(lane == 1, i2, 0.0)).astype(I32)
    wt_ref[...] = jnp.where(lane == 0, w1, jnp.where(lane == 1, w2, 0.0))


def _prenorm_router_call(x, g, mod_p, mod_s, sh_chunk, sc_chunk, w_router, b_router):
    shp, shs = _mod_specs(sh_chunk)
    scp, scs = _mod_specs(sc_chunk)
    tile = pl.BlockSpec((TM, D), lambda i: (i, 0))
    small = pl.BlockSpec((TM, LANES), lambda i: (i, 0))
    wr = jnp.pad(w_router, ((0, 0), (0, LANES - N_EXPERTS)))
    br = jnp.pad(b_router, (0, LANES - N_EXPERTS)).reshape(1, LANES)
    return pl.pallas_call(
        _prenorm_router_kernel,
        grid=(N_TILES,),
        in_specs=[tile, pl.BlockSpec((1, D), lambda i: (0, 0)), shp, shs, scp, scs,
                  pl.BlockSpec((D, LANES), lambda i: (0, 0)),
                  pl.BlockSpec((1, LANES), lambda i: (0, 0))],
        out_specs=[tile, small, small],
        out_shape=[jax.ShapeDtypeStruct((TT, D), F32),
                   jax.ShapeDtypeStruct((TT, LANES), I32),
                   jax.ShapeDtypeStruct((TT, LANES), F32)],
        compiler_params=_cparams(("arbitrary",)),
        name="prenorm_router",
    )(x, g.reshape(1, D), mod_p, mod_s, mod_p, mod_s, wr, br)


def _proj_kernel(x_ref, g_ref, shp_ref, shs_ref, scp_ref, scs_ref, w_ref, b_ref, o_ref, wbf_ref):
    i = pl.program_id(1)

    @pl.when(i == 0)
    def _():
        wbf_ref[...] = w_ref[...].astype(BF16)

    h = _prenorm(i, x_ref[...], g_ref, shp_ref, shs_ref, scp_ref, scs_ref).astype(BF16)
    o_ref[...] = jnp.dot(h, wbf_ref[...], preferred_element_type=F32) + b_ref[...]


def _proj_call(x, g, mod_p, mod_s, w, b, tn):
    k, n = w.shape
    shp, shs = _mod_specs(0, token_axis=1)
    scp, scs = _mod_specs(1, token_axis=1)
    return pl.pallas_call(
        _proj_kernel,
        grid=(n // tn, N_TILES),
        in_specs=[
            pl.BlockSpec((TM, k), lambda j, i: (i, 0)),
            pl.BlockSpec((1, k), lambda j, i: (0, 0)), shp, shs, scp, scs,
            pl.BlockSpec((k, tn), lambda j, i: (0, j), pipeline_mode=pl.Buffered(1)),
            pl.BlockSpec((1, tn), lambda j, i: (0, j)),
        ],
        out_specs=pl.BlockSpec((TM, tn), lambda j, i: (i, j)),
        out_shape=jax.ShapeDtypeStruct((TT, n), F32),
        scratch_shapes=[pltpu.VMEM((k, tn), BF16)],
        compiler_params=_cparams(("arbitrary", "arbitrary")),
        name="proj",
    )(x, g.reshape(1, k), mod_p, mod_s, mod_p, mod_s, w, b.reshape(1, n))


def _post_mm_kernel(x_ref, op_ref, os_ref, w_ref, b_ref, g_ref, gtp_ref, gts_ref, out_ref, wbf_ref):
    i = pl.program_id(0)

    @pl.when(i == 0)
    def _():
        wbf_ref[...] = w_ref[...].astype(BF16)

    o = jnp.where(i < NP_TILES, op_ref[...], os_ref[...]).astype(BF16)
    y = jnp.dot(o, wbf_ref[...], preferred_element_type=F32) + b_ref[...]
    out_ref[...] = x_ref[...] + _pick(i, gtp_ref, gts_ref) * (_rms(y) * g_ref[...])


def _post_mm_call(x, o_p, o_s, w, b, g, mod_p, mod_s, gate_chunk):
    gtp, gts = _mod_specs(gate_chunk)
    tile = pl.BlockSpec((TM, D), lambda i: (i, 0))
    row = pl.BlockSpec((1, D), lambda i: (0, 0))
    return pl.pallas_call(
        _post_mm_kernel,
        grid=(N_TILES,),
        in_specs=[tile,
                  pl.BlockSpec((TM, D), lambda i: (jnp.minimum(i, NP_TILES - 1), 0)),
                  pl.BlockSpec((TM, D), lambda i: (jnp.maximum(i - NP_TILES, 0), 0)),
                  pl.BlockSpec((D, D), lambda i: (0, 0)), row, row, gtp, gts],
        out_specs=tile,
        out_shape=jax.ShapeDtypeStruct((TT, D), F32),
        scratch_shapes=[pltpu.VMEM((D, D), BF16)],
        compiler_params=_cparams(("arbitrary",)),
        name="post_mm",
    )(x, o_p, o_s, w, b.reshape(1, D), g.reshape(1, D), mod_p, mod_s)


def _gdn_kernel(*refs, chunk, n_chunks, t_real, n_par):
    x_refs = refs[:n_par]
    (cw_ref, alog_ref, dt_ref, nw_ref, cb_ref, s0_ref, o_ref, sfin_ref, xbuf, s_scr) = refs[n_par:]
    c_sz = chunk
    tb = chunk * n_chunks
    j = pl.program_id(1)
    rep = GDN_V_HEADS // GDN_QK_HEADS
    units = [(p, h) for p in range(n_par) for h in range(GDN_V_HEADS)]

    @pl.when(j == 0)
    def _():
        s_scr[...] = s0_ref[...]
        xbuf[:, 0:SUBLANES, :] = cb_ref[...]

    for p in range(n_par):
        xbuf[p, SUBLANES:SUBLANES + tb, :] = x_refs[p][:, 0:GDN_CONV_DIM]

    ii = lax.broadcasted_iota(I32, (c_sz, c_sz), 0)
    jj = lax.broadcasted_iota(I32, (c_sz, c_sz), 1)
    causal = ii >= jj
    strict = ii > jj
    tri = causal.astype(F32)
    eye_c = (ii == jj).astype(F32)
    pair_mask = [jnp.logical_and((ii >> (l + 1)) == (jj >> (l + 1)), (ii >> l) != (jj >> l))
                 for l in range(int(math.log2(c_sz)))]
    eye_t = (lax.broadcasted_iota(I32, (2 * SUBLANES, LANES), 0)
             == lax.broadcasted_iota(I32, (2 * SUBLANES, LANES), 1)).astype(F32)
    lane = lax.broadcasted_iota(I32, (c_sz, LANES), 1)
    row = lax.broadcasted_iota(I32, (c_sz, LANES), 0)
    cw = cw_ref[...]
    neg_a = -jnp.exp(alog_ref[...])
    dt = dt_ref[...]
    nw = nw_ref[...]

    def chunk_body(c, carry):
        r0 = pl.multiple_of(c * c_sz, c_sz)
        xc, beta, gc, rows_t = [], [], [], []
        for p in range(n_par):
            win = xbuf[p, pl.ds(r0, c_sz + SUBLANES), :]
            y = win[SUBLANES:SUBLANES + c_sz] * cw[3:4]
            for tap in range(1, CONV_WIDTH):
                y = y + win[SUBLANES - tap:SUBLANES - tap + c_sz] * cw[3 - tap:4 - tap]
            xc.append(_silu(y))
            ba = x_refs[p][pl.ds(r0, c_sz), GDN_MAIN_DIM:GDN_PROJ_PAD]
            b_p = jax.nn.sigmoid(ba)
            g_p = neg_a * jax.nn.softplus(ba + dt)
            if t_real < tb:
                live = (row + r0) < t_real
                b_p = jnp.where(live, b_p, 0.0)
                g_p = jnp.where(live, g_p, 0.0)
            gc_p = jnp.dot(tri, g_p, precision=HIGHEST, preferred_element_type=F32)
            cols = jnp.where(lane < GDN_V_HEADS, b_p, gc_p)
            rows_t.append(lax.dot_general(eye_t, cols, (((1,), (1,)), ((), ())), precision=HIGHEST,
                                          preferred_element_type=F32))
            beta.append(b_p)
            gc.append(gc_p)

        qn, kn, kk, qk = {}, {}, {}, {}
        for p in range(n_par):
            for hq in range(GDN_QK_HEADS):
                q = xc[p][:, hq * GDN_HD:(hq + 1) * GDN_HD]
                k = xc[p][:, GDN_KEY_DIM + hq * GDN_HD:GDN_KEY_DIM + (hq + 1) * GDN_HD]
                q = q * lax.rsqrt(jnp.sum(q * q, axis=-1, keepdims=True) + RMS_EPS) * (GDN_HD ** -0.5)
                k = k * lax.rsqrt(jnp.sum(k * k, axis=-1, keepdims=True) + RMS_EPS)
                qn[p, hq], kn[p, hq] = q, k
        for key in qn:
            kk[key] = _bdot_nt(kn[key], kn[key])
            qk[key] = _bdot_nt(qn[key], kn[key])

        gcc, gcl, bcol, decay, a_mat, inv = {}, {}, {}, {}, {}, {}
        for (p, h) in units:
            gcc[p, h] = gc[p][:, GDN_V_HEADS + h:GDN_V_HEADS + h + 1]
            gcr = rows_t[p][GDN_V_HEADS + h:GDN_V_HEADS + h + 1, :]
            bcol[p, h] = beta[p][:, h:h + 1]
            gcl[p, h] = gcc[p, h][c_sz - 1:c_sz, :]
            decay[p, h] = jnp.exp(jnp.where(causal, gcc[p, h] - gcr, -jnp.inf))
            a_mat[p, h] = jnp.where(strict, bcol[p, h] * kk[p, h // rep] * decay[p, h], 0.0)
            inv[p, h] = eye_c - jnp.where(pair_mask[0], a_mat[p, h], 0.0)
        for lvl in range(1, len(pair_mask)):
            t1 = {u: _bdot(jnp.where(pair_mask[lvl], a_mat[u], 0.0), inv[u]) for u in units}
            t2 = {u: _bdot(inv[u], t1[u]) for u in units}
            inv = {u: inv[u] - t2[u] for u in units}

        egc = {u: jnp.exp(gcc[u]) for u in units}
        sol = {}
        for (p, h) in units:
            v = xc[p][:, 2 * GDN_KEY_DIM + h * GDN_HD:2 * GDN_KEY_DIM + (h + 1) * GDN_HD]
            rhs = jnp.concatenate([v * bcol[p, h], kn[p, h // rep] * (bcol[p, h] * egc[p, h])], axis=1)
            sol[p, h] = _bdot(inv[p, h], rhs)
        ws = {}
        for (p, h) in units:
            q_dec = qn[p, h // rep] * egc[p, h]
            ws[p, h] = _bdot(jnp.concatenate([sol[p, h][:, GDN_HD:], q_dec], axis=0), s_scr[p, h])
        u_new = {u: sol[u][:, :GDN_HD] - ws[u][:c_sz] for u in units}
        o_part, s_part = {}, {}
        for (p, h) in units:
            qkm = jnp.where(causal, qk[p, h // rep] * decay[p, h], 0.0)
            k_dec = kn[p, h // rep] * jnp.exp(gcl[p, h] - gcc[p, h])
            o_part[p, h] = _bdot(qkm, u_new[p, h])
            s_part[p, h] = lax.dot_general(k_dec.astype(BF16), u_new[p, h].astype(BF16),
                                           (((0,), (0,)), ((), ())), preferred_element_type=F32)
        for (p, h) in units:
            s_scr[p, h] = s_scr[p, h] * jnp.exp(gcl[p, h]) + s_part[p, h]
            o = ws[p, h][c_sz:] + o_part[p, h]
            z = x_refs[p][pl.ds(r0, c_sz), GDN_CONV_DIM + h * GDN_HD:GDN_CONV_DIM + (h + 1) * GDN_HD]
            o_ref[p, pl.ds(r0, c_sz), h * GDN_HD:(h + 1) * GDN_HD] = _rms(o) * nw * _silu(z)
        return carry

    lax.fori_loop(0, n_chunks, chunk_body, 0)
    xbuf[:, 0:SUBLANES, :] = xbuf[:, tb:tb + SUBLANES, :]

    @pl.when(j == pl.num_programs(1) - 1)
    def _():
        sfin_ref[...] = s_scr[...]


def _gdn_call(x, conv_w, a_log, dt_bias, norm_w, conv_buf, s0, *, n_seq, n_steps, chunk,
              n_chunks, t_real, n_par, s0_first_seq=0):
    s0_block = s0_first_seq // n_par
    tb = chunk * n_chunks
    pad = LANES - 2 * GDN_V_HEADS
    alog_row = jnp.pad(a_log, (GDN_V_HEADS, pad)).reshape(1, LANES)
    dt_row = jnp.pad(dt_bias, (GDN_V_HEADS, pad)).reshape(1, LANES)

    def seq_block(width, p):
        return pl.BlockSpec((None, tb, width), lambda s, j: ((s * n_par + p) * n_steps + j, 0, 0))

    const = lambda s, j: (0, 0)
    state_spec = pl.BlockSpec((n_par, GDN_V_HEADS, GDN_HD, GDN_HD), lambda s, j: (s, 0, 0, 0))
    return pl.pallas_call(
        functools.partial(_gdn_kernel, chunk=chunk, n_chunks=n_chunks, t_real=t_real, n_par=n_par),
        grid=(n_seq // n_par, n_steps),
        in_specs=[seq_block(GDN_PROJ_PAD, p) for p in range(n_par)]
                 + [pl.BlockSpec((CONV_WIDTH, GDN_CONV_DIM), const),
                    pl.BlockSpec((1, LANES), const), pl.BlockSpec((1, LANES), const),
                    pl.BlockSpec((1, GDN_HD), const),
                    pl.BlockSpec((n_par, SUBLANES, GDN_CONV_DIM), lambda s, j: (s, 0, 0)),
                    pl.BlockSpec((n_par, GDN_V_HEADS, GDN_HD, GDN_HD),
                                 lambda s, j: (s + s0_block, 0, 0, 0))],
        out_specs=[pl.BlockSpec((n_par, None, tb, GDN_VAL_DIM), lambda s, j: (s, j, 0, 0)),
                   state_spec],
        out_shape=[jax.ShapeDtypeStruct((n_seq, n_steps, tb, GDN_VAL_DIM), F32),
                   jax.ShapeDtypeStruct((n_seq, GDN_V_HEADS, GDN_HD, GDN_HD), F32)],
        scratch_shapes=[pltpu.VMEM((n_par, tb + SUBLANES, GDN_CONV_DIM), F32),
                        pltpu.VMEM((n_par, GDN_V_HEADS, GDN_HD, GDN_HD), F32)],
        compiler_params=_cparams(("arbitrary", "arbitrary")),
        name="gdn",
    )(*([x] * n_par), conv_w, alog_row, dt_row, norm_w.reshape(1, GDN_HD), conv_buf, s0)


def _t5_bucket_np(dist):
    dist = np.maximum(dist, 0)
    max_exact = REL_BUCKETS // 2
    ratio = np.log(np.maximum(dist, max_exact).astype(np.float32) / np.float32(max_exact)) \
        / np.float32(math.log(REL_MAX_DIST / max_exact))
    large = max_exact + (ratio.astype(np.float32) * np.float32(REL_BUCKETS - max_exact)).astype(np.int32)
    return np.where(dist < max_exact, dist, np.minimum(large, REL_BUCKETS - 1)).astype(np.int32)


def _bucket_table(qpos, kpos, k_valid):
    dist = qpos[:, None] - kpos[None, :]
    ok = (dist >= 0) & (dist < WINDOW) & (kpos[None, :] >= 0) & k_valid[None, :]
    return np.where(ok, _t5_bucket_np(dist), -1).astype(np.int32)


def _bias_from_buckets(bkt, rb_ref, head):
    def body(b, acc):
        return jnp.where(bkt == b, rb_ref[b, head], acc)
    acc = lax.fori_loop(0, REL_BUCKETS, body, jnp.zeros(bkt.shape, F32))
    return jnp.where(bkt < 0, -jnp.inf, acc)


def _swa_prompt_kernel(rb_ref, sk_ref, bkt_ref, q_ref, kvp_ref, kvc_ref, o_ref, bias_scr):
    first = jnp.logical_and(pl.program_id(0) == 0, pl.program_id(1) == 0)
    n = pl.program_id(1)

    @pl.when(first)
    def _():
        bkt = bkt_ref[...]
        for h in range(SWA_Q_HEADS):
            kv, g = divmod(h, SWA_GROUP)
            bias_scr[kv, g * WINDOW:(g + 1) * WINDOW, :] = _bias_from_buckets(bkt, rb_ref, h)

    q = q_ref[...]
    scale = SWA_HD ** -0.5
    heads = range(SWA_KV_HEADS)
    s_prev, s_cur = [], []
    for kv in heads:
        qs = jnp.concatenate(
            [q[:, (kv * SWA_GROUP + g) * SWA_HD:(kv * SWA_GROUP + g + 1) * SWA_HD]
             for g in range(SWA_GROUP)], axis=0).astype(BF16)
        s_prev.append(_bdot_nt(qs, kvp_ref[:, kv * SWA_HD:(kv + 1) * SWA_HD]))
        s_cur.append(_bdot_nt(qs, kvc_ref[:, kv * SWA_HD:(kv + 1) * SWA_HD]))
    p_prev, p_cur, den = [], [], []
    for kv in heads:
        sp = s_prev[kv] * scale + bias_scr[kv, :, 0:WINDOW]
        sp = jnp.where(n > 0, sp, -jnp.inf)
        sc = s_cur[kv] * scale + bias_scr[kv, :, WINDOW:2 * WINDOW]
        sink = jnp.concatenate(
            [jnp.full((WINDOW, 1), sk_ref[kv * SWA_GROUP + g], F32) for g in range(SWA_GROUP)], axis=0)
        m = jnp.maximum(jnp.maximum(jnp.max(sp, axis=-1, keepdims=True),
                                    jnp.max(sc, axis=-1, keepdims=True)), sink)
        pp = jnp.exp(sp - m)
        pc = jnp.exp(sc - m)
        den.append(jnp.sum(pp, axis=-1, keepdims=True) + jnp.sum(pc, axis=-1, keepdims=True)
                   + jnp.exp(sink - m))
        p_prev.append(pp.astype(BF16))
        p_cur.append(pc.astype(BF16))
    outs = []
    for kv in heads:
        v_prev = kvp_ref[:, SWA_KV_DIM + kv * SWA_HD:SWA_KV_DIM + (kv + 1) * SWA_HD]
        v_cur = kvc_ref[:, SWA_KV_DIM + kv * SWA_HD:SWA_KV_DIM + (kv + 1) * SWA_HD]
        o = (_bdot(p_prev[kv], v_prev) + _bdot(p_cur[kv], v_cur)) / den[kv]
        outs.extend(o[g * WINDOW:(g + 1) * WINDOW] for g in range(SWA_GROUP))
    o_ref[...] = jnp.concatenate(outs, axis=1)


def _swa_prompt_call(qkv, rel_bias, sinks):
    nb = SEQ // WINDOW
    qpos = WINDOW + np.arange(WINDOW)
    kpos = np.arange(2 * WINDOW)
    bkt = jnp.asarray(_bucket_table(qpos, kpos, np.ones(2 * WINDOW, bool)))
    kv_col = SWA_Q_DIM // (2 * SWA_KV_DIM)
    smem = pl.BlockSpec(memory_space=pltpu.SMEM)
    return pl.pallas_call(
        _swa_prompt_kernel,
        grid=(BATCH, nb),
        in_specs=[smem, smem,
                  pl.BlockSpec((WINDOW, 2 * WINDOW), lambda b, n: (0, 0)),
                  pl.BlockSpec((WINDOW, SWA_Q_DIM), lambda b, n: (b * nb + n, 0)),
                  pl.BlockSpec((WINDOW, 2 * SWA_KV_DIM),
                               lambda b, n: (jnp.maximum(b * nb + n - 1, 0), kv_col)),
                  pl.BlockSpec((WINDOW, 2 * SWA_KV_DIM), lambda b, n: (b * nb + n, kv_col))],
        out_specs=pl.BlockSpec((WINDOW, SWA_Q_DIM), lambda b, n: (b * nb + n, 0)),
        out_shape=jax.ShapeDtypeStruct((NP_TOK, SWA_Q_DIM), F32),
        scratch_shapes=[pltpu.VMEM((SWA_KV_HEADS, SWA_GROUP * WINDOW, 2 * WINDOW), F32)],
        compiler_params=_cparams(("arbitrary", "arbitrary")),
        name="swa_prompt",
    )(rel_bias, sinks, bkt, qkv, qkv, qkv)


def _swa_sample_kernel(rb_ref, sk_ref, bkt_ref, q_ref, ck_ref, cv_ref, kn_ref, vn_ref,
                       o_ref, ck_out_ref, cv_out_ref, k_scr, v_scr, bias_scr, sink_scr):
    rows = SWA_GROUP * DEC_SEQ
    k_scr[:, 0:WINDOW, :] = ck_ref[...]
    k_scr[:, WINDOW:SWA_KPAD, :] = kn_ref[...]
    v_scr[:, 0:WINDOW, :] = cv_ref[...]
    v_scr[:, WINDOW:SWA_KPAD, :] = vn_ref[...]
    ck_out_ref[...] = k_scr[:, DEC_SEQ:DEC_SEQ + WINDOW, :]
    cv_out_ref[...] = v_scr[:, DEC_SEQ:DEC_SEQ + WINDOW, :]
    k_ref, v_ref = k_scr, v_scr

    def group_of_row(shape):
        r = lax.broadcasted_iota(I32, shape, 0)
        return sum((r >= g * DEC_SEQ).astype(I32) for g in range(1, SWA_GROUP))

    @pl.when(pl.program_id(0) == 0)
    def _():
        bkt = bkt_ref[...]
        grp = group_of_row((rows, 1))
        grp_full = group_of_row(bkt.shape)
        for kv in range(SWA_KV_HEADS):
            acc = jnp.zeros(bkt.shape, F32)
            snk = jnp.zeros((rows, 1), F32)
            for g in range(SWA_GROUP):
                head = kv * SWA_GROUP + g
                acc = jnp.where(grp_full == g, _bias_from_buckets(bkt, rb_ref, head), acc)
                snk = jnp.where(grp == g, sk_ref[head], snk)
            bias_scr[kv] = acc
            sink_scr[kv] = snk

    scale = SWA_HD ** -0.5
    units = [(s, kv) for s in range(SWA_SB) for kv in range(SWA_KV_HEADS)]
    scores = {(s, kv): _bdot_nt(q_ref[s, kv], k_ref[s, :, kv * SWA_HD:(kv + 1) * SWA_HD])
              for (s, kv) in units}
    probs, den = {}, {}
    for (s, kv) in units:
        sc = scores[s, kv] * scale + bias_scr[kv]
        sink = sink_scr[kv]
        m = jnp.maximum(jnp.max(sc, axis=-1, keepdims=True), sink)
        p = jnp.exp(sc - m)
        den[s, kv] = jnp.sum(p, axis=-1, keepdims=True) + jnp.exp(sink - m)
        probs[s, kv] = p.astype(BF16)
    for (s, kv) in units:
        o_ref[s, kv] = _bdot(probs[s, kv], v_ref[s, :, kv * SWA_HD:(kv + 1) * SWA_HD]) / den[s, kv]


def _swa_sample_call(q_st, cache_k, cache_v, k_new, v_new, layer_pair, rel_bias, sinks):
    rows = SWA_GROUP * DEC_SEQ
    n_keys = WINDOW + DEC_SEQ
    qpos = PAST_LEN + np.arange(DEC_SEQ)
    kpos = PAST_LEN - WINDOW + np.arange(SWA_KPAD)
    bkt4 = _bucket_table(qpos, kpos, np.arange(SWA_KPAD) < n_keys)
    bkt = jnp.asarray(np.tile(bkt4, (SWA_GROUP, 1)))
    steps = DEC_BATCH // SWA_SB
    smem = pl.BlockSpec(memory_space=pltpu.SMEM)
    q_spec = pl.BlockSpec((SWA_SB, SWA_KV_HEADS, rows, SWA_HD), lambda i: (i, 0, 0, 0))
    win_in = pl.BlockSpec((SWA_SB, WINDOW, SWA_KV_DIM), lambda i: (layer_pair * steps + i, 0, 0))
    win_out = pl.BlockSpec((SWA_SB, WINDOW, SWA_KV_DIM), lambda i: (i, 0, 0))
    new_spec = pl.BlockSpec((SWA_SB, SWA_KPAD - WINDOW, SWA_KV_DIM), lambda i: (i, 0, 0))
    win_shape = jax.ShapeDtypeStruct((DEC_BATCH, WINDOW, SWA_KV_DIM), F32)
    return pl.pallas_call(
        _swa_sample_kernel,
        grid=(steps,),
        in_specs=[smem, smem, pl.BlockSpec((rows, SWA_KPAD), lambda i: (0, 0)),
                  q_spec, win_in, win_in, new_spec, new_spec],
        out_specs=[q_spec, win_out, win_out],
        out_shape=[jax.ShapeDtypeStruct((DEC_BATCH, SWA_KV_HEADS, rows, SWA_HD), F32),
                   win_shape, win_shape],
        scratch_shapes=[pltpu.VMEM((SWA_SB, SWA_KPAD, SWA_KV_DIM), F32),
                        pltpu.VMEM((SWA_SB, SWA_KPAD, SWA_KV_DIM), F32),
                        pltpu.VMEM((SWA_KV_HEADS, rows, SWA_KPAD), F32),
                        pltpu.VMEM((SWA_KV_HEADS, rows, 1), F32)],
        compiler_params=_cparams(("arbitrary",)),
        name="swa_sample",
    )(rel_bias, sinks, bkt, q_st, cache_k, cache_v, k_new, v_new)


def _ffn_scratch(rows):
    return [pltpu.VMEM((rows, D), F32),
            pltpu.VMEM((NF, D, TF), BF16),
            pltpu.VMEM((NF, D, TF), BF16),
            pltpu.VMEM((NF, TF, D), BF16),
            pltpu.VMEM((2, D, TF), F32),
            pltpu.VMEM((2, D, TF), F32),
            pltpu.VMEM((2, TF, D), F32),
            pltpu.SemaphoreType.DMA((3, 2))]


def _swiglu_into(acc_ref, x, expert, load, w_hbm, res, stg, sem):
    wg_hbm, wu_hbm, wd_hbm = w_hbm
    res_g, res_u, res_d = res
    stg_g, stg_u, stg_d = stg

    def copies(f, slot):
        col = pl.multiple_of(f * TF, TF)
        return (pltpu.make_async_copy(wg_hbm.at[expert, :, pl.ds(col, TF)], stg_g.at[slot], sem.at[0, slot]),
                pltpu.make_async_copy(wu_hbm.at[expert, :, pl.ds(col, TF)], stg_u.at[slot], sem.at[1, slot]),
                pltpu.make_async_copy(wd_hbm.at[expert, pl.ds(col, TF), :], stg_d.at[slot], sem.at[2, slot]))

    def block(f):
        gate = jnp.dot(x, res_g[f], preferred_element_type=F32)
        up = jnp.dot(x, res_u[f], preferred_element_type=F32)
        act = (_silu(gate) * up).astype(BF16)
        acc_ref[...] += jnp.dot(act, res_d[f], preferred_element_type=F32)

    acc_ref[...] = jnp.zeros(acc_ref.shape, acc_ref.dtype)

    @pl.when(load)
    def _():
        for c in copies(0, 0):
            c.start()

        def body(f, carry):
            slot = lax.rem(f, 2)

            @pl.when(f + 1 < NF)
            def _():
                for c in copies(f + 1, 1 - slot):
                    c.start()

            for c in copies(f, slot):
                c.wait()
            res_g[f] = stg_g[slot].astype(BF16)
            res_u[f] = stg_u[slot].astype(BF16)
            res_d[f] = stg_d[slot].astype(BF16)
            block(f)
            return carry

        lax.fori_loop(0, NF, body, 0)

    @pl.when(jnp.logical_not(load))
    def _():
        def body(f, carry):
            block(f)
            return carry

        lax.fori_loop(0, NF, body, 0)


def _ffn_kernel(te_ref, ld_ref, hf_ref, nt_ref, x_ref, wg_hbm, wu_hbm, wd_hbm, o_ref, acc_ref, *scratch):
    t = pl.program_id(0)
    used = t < nt_ref[0]
    half = hf_ref[t] != 0
    half_rows = TME // 2

    def run(rows):
        _swiglu_into(acc_ref.at[pl.ds(0, rows)], x_ref[0:rows, :].astype(BF16), te_ref[t],
                     ld_ref[t] != 0, (wg_hbm, wu_hbm, wd_hbm), scratch[0:3], scratch[3:6], scratch[6])
        o_ref[0:rows, :] = acc_ref[0:rows, :]

    @pl.when(jnp.logical_and(used, jnp.logical_not(half)))
    def _():
        run(TME)

    @pl.when(jnp.logical_and(used, half))
    def _():
        run(half_rows)
        o_ref[half_rows:, :] = jnp.zeros((TME - half_rows, D), F32)

    @pl.when(jnp.logical_not(used))
    def _():
        o_ref[...] = jnp.zeros_like(o_ref)


def _ffn_call(x, tile_expert, tile_half, n_tiles_used, w_gate, w_up, w_down):
    n_rows = x.shape[0]
    n_tiles = n_rows // TME
    changed = jnp.concatenate([jnp.ones((1,), I32),
                               (tile_expert[1:] != tile_expert[:-1]).astype(I32)])
    hbm = pl.BlockSpec(memory_space=pl.ANY)
    grid_spec = pltpu.PrefetchScalarGridSpec(
        num_scalar_prefetch=4,
        grid=(n_tiles,),
        in_specs=[pl.BlockSpec((TME, D), lambda t, te, ld, hf, nt: (jnp.minimum(t, nt[0] - 1), 0)),
                  hbm, hbm, hbm],
        out_specs=pl.BlockSpec((TME, D), lambda t, *_: (t, 0)),
        scratch_shapes=_ffn_scratch(TME),
    )
    return pl.pallas_call(
        _ffn_kernel,
        grid_spec=grid_spec,
        out_shape=jax.ShapeDtypeStruct((n_rows, D), F32),
        compiler_params=_cparams(("arbitrary",)),
        name="ffn",
    )(tile_expert, changed, tile_half, n_tiles_used, x, w_gate, w_up, w_down)


def _dense_ffn_kernel(x_ref, gpre_ref, shp_ref, shs_ref, scp_ref, scs_ref, wg_hbm, wu_hbm, wd_hbm,
                      gpost_ref, gtp_ref, gts_ref, out_ref, acc_ref, *scratch, layer_pair):
    i = pl.program_id(0)
    x = x_ref[...]
    h = _prenorm(i, x, gpre_ref, shp_ref, shs_ref, scp_ref, scs_ref).astype(BF16)
    _swiglu_into(acc_ref, h, layer_pair, i == 0, (wg_hbm, wu_hbm, wd_hbm),
                 scratch[0:3], scratch[3:6], scratch[6])
    out_ref[...] = x + _pick(i, gtp_ref, gts_ref) * (_rms(acc_ref[...]) * gpost_ref[...])


def _dense_ffn_call(x, g_pre, g_post, mod_p, mod_s, w_gate, w_up, w_down, layer_pair):
    shp, shs = _mod_specs(3)
    scp, scs = _mod_specs(4)
    gtp, gts = _mod_specs(5)
    tile = pl.BlockSpec((TM, D), lambda i: (i, 0))
    row = pl.BlockSpec((1, D), lambda i: (0, 0))
    hbm = pl.BlockSpec(memory_space=pl.ANY)
    return pl.pallas_call(
        functools.partial(_dense_ffn_kernel, layer_pair=layer_pair),
        grid=(N_TILES,),
        in_specs=[tile, row, shp, shs, scp, scs, hbm, hbm, hbm, row, gtp, gts],
        out_specs=tile,
        out_shape=jax.ShapeDtypeStruct((TT, D), F32),
        scratch_shapes=_ffn_scratch(TM),
        compiler_params=_cparams(("arbitrary",)),
        name="dense_ffn",
    )(x, g_pre.reshape(1, D), mod_p, mod_s, mod_p, mod_s, w_gate, w_up, w_down,
      g_post.reshape(1, D), mod_p, mod_s)


DMA_UNROLL = 8


def _row_copy(src_ref, dst_ref, src_row, dst_row, sem):
    return pltpu.make_async_copy(src_ref.at[pl.ds(src_row, 1)], dst_ref.at[pl.ds(dst_row, 1)], sem)


def _dispatch_kernel(tail_ref, nt_ref, d0_ref, d1_ref, h_ref, out_ref, zero_ref, sem, zsem):
    @pl.when(pl.program_id(0) == 0)
    def _():
        zero_ref[...] = jnp.zeros_like(zero_ref)

        def fill_tile(row0):
            fill = pltpu.make_async_copy(zero_ref, out_ref.at[pl.ds(pl.multiple_of(row0, TME), TME)], zsem)
            fill.start()
            fill.wait()

        for e in range(N_EXPERTS):
            @pl.when(tail_ref[e] >= 0)
            def _():
                fill_tile(tail_ref[e])

            @pl.when(nt_ref[0] + e < MOE_TILES)
            def _():
                fill_tile((nt_ref[0] + e) * TME)

    def start(r, carry):
        _row_copy(h_ref, out_ref, r, d0_ref[0, 0, r], sem).start(priority=0)
        _row_copy(h_ref, out_ref, r, d1_ref[0, 0, r], sem).start(priority=1)
        return carry

    def wait(r, carry):
        _row_copy(h_ref, out_ref, 0, 0, sem).wait()
        _row_copy(h_ref, out_ref, 0, 0, sem).wait()
        return carry

    lax.fori_loop(0, TM, start, 0, unroll=DMA_UNROLL)
    lax.fori_loop(0, TM, wait, 0, unroll=DMA_UNROLL)


def _dispatch_call(h, dest, tail_rows, n_tiles_used):
    idx_spec = pl.BlockSpec((1, 1, TM), lambda i, *_: (i, 0, 0), memory_space=pltpu.SMEM)
    grid_spec = pltpu.PrefetchScalarGridSpec(
        num_scalar_prefetch=2,
        grid=(N_TILES,),
        in_specs=[idx_spec, idx_spec, pl.BlockSpec((TM, D), lambda i, *_: (i, 0))],
        out_specs=pl.BlockSpec(memory_space=pl.ANY),
        scratch_shapes=[pltpu.VMEM((TME, D), h.dtype), pltpu.SemaphoreType.DMA(()),
                        pltpu.SemaphoreType.DMA(())],
    )
    return pl.pallas_call(
        _dispatch_kernel,
        grid_spec=grid_spec,
        out_shape=jax.ShapeDtypeStruct((MOE_ROWS, D), h.dtype),
        compiler_params=_cparams(("arbitrary",)),
        name="moe_dispatch",
    )(tail_rows, n_tiles_used, dest[0].reshape(N_TILES, 1, TM), dest[1].reshape(N_TILES, 1, TM), h)


def _combine_kernel(d0_ref, d1_ref, x_ref, wt_ref, ys_ref, g_ref, gtp_ref, gts_ref, *rest, split):
    i = pl.program_id(0)
    buf, sem = rest[-2:]

    def start(r, carry):
        _row_copy(ys_ref, buf.at[0], d0_ref[0, 0, r], r, sem).start(priority=0)
        _row_copy(ys_ref, buf.at[1], d1_ref[0, 0, r], r, sem).start(priority=1)
        return carry

    def wait(r, carry):
        _row_copy(ys_ref, buf.at[0], 0, 0, sem).wait()
        _row_copy(ys_ref, buf.at[1], 0, 0, sem).wait()
        return carry

    lax.fori_loop(0, TM, start, 0, unroll=DMA_UNROLL)
    lax.fori_loop(0, TM, wait, 0, unroll=DMA_UNROLL)
    wt = wt_ref[...]
    y = wt[:, 0:1] * buf[0] + wt[:, 1:2] * buf[1]
    out = x_ref[...] + _pick(i, gtp_ref, gts_ref) * (_rms(y) * g_ref[...])
    if split:
        prompt_ref, sample_ref = rest[:2]

        @pl.when(i < NP_TILES)
        def _():
            prompt_ref[...] = out

        @pl.when(i >= NP_TILES)
        def _():
            sample_ref[...] = out
    else:
        rest[0][...] = out


def _combine_call(x, ys, dest, wts, g, mod_p, mod_s, gate_chunk, split):
    gtp, gts = _mod_specs(gate_chunk)
    tile = pl.BlockSpec((TM, D), lambda i: (i, 0))
    idx_spec = pl.BlockSpec((1, 1, TM), lambda i: (i, 0, 0), memory_space=pltpu.SMEM)
    if split:
        out_specs = [pl.BlockSpec((TM, D), lambda i: (jnp.minimum(i, NP_TILES - 1), 0)),
                     pl.BlockSpec((TM, D), lambda i: (jnp.maximum(i - NP_TILES, 0), 0))]
        out_shape = [jax.ShapeDtypeStruct((NP_TOK, D), F32), jax.ShapeDtypeStruct((NS_TOK, D), F32)]
    else:
        out_specs = tile
        out_shape = jax.ShapeDtypeStruct((TT, D), F32)
    return pl.pallas_call(
        functools.partial(_combine_kernel, split=split),
        grid=(N_TILES,),
        in_specs=[idx_spec, idx_spec, tile, pl.BlockSpec((TM, LANES), lambda i: (i, 0)),
                  pl.BlockSpec(memory_space=pl.ANY),
                  pl.BlockSpec((1, D), lambda i: (0, 0)), gtp, gts],
        out_specs=out_specs,
        out_shape=out_shape,
        scratch_shapes=[pltpu.VMEM((2, TM, D), F32), pltpu.SemaphoreType.DMA(())],
        compiler_params=_cparams(("arbitrary",)),
        name="moe_combine",
    )(dest[0].reshape(N_TILES, 1, TM), dest[1].reshape(N_TILES, 1, TM), x, wts, ys,
      g.reshape(1, D), mod_p, mod_s)


def _route(top_idx):
    flat_e = top_idx.T.reshape(-1)
    onehot = (flat_e[:, None] == jnp.arange(N_EXPERTS, dtype=I32)[None, :]).astype(I32)
    csum = jnp.cumsum(onehot, axis=0)
    rank = jnp.sum((csum - onehot) * onehot, axis=1)
    counts = csum[-1]
    padded = ((counts + TME - 1) // TME) * TME
    ends = jnp.cumsum(padded)
    starts = ends - padded
    dest = (jnp.sum(onehot * starts[None, :], axis=1) + rank).astype(I32)
    tile_start = jnp.arange(MOE_TILES, dtype=I32) * TME
    tile_expert = jnp.minimum(jnp.sum(tile_start[:, None] >= ends[None, :], axis=1),
                              N_EXPERTS - 1).astype(I32)
    n_used = (ends[-1] // TME).astype(I32).reshape(1)
    tail_rows = jnp.where(padded > 0, ends - TME, -1).astype(I32)
    in_tile = jnp.sum((tile_expert[:, None] == jnp.arange(N_EXPERTS, dtype=I32)[None, :])
                      * (starts + counts)[None, :], axis=1) - tile_start
    tile_half = (in_tile <= TME // 2).astype(I32)
    return dest.reshape(2, TT), tile_expert, tile_half, n_used, tail_rows


def _gdn_layer(x, g_pre, w_in, conv_w, a_log, dt_bias, norm_w, w_out, state_conv, state_rec_all,
               layer_pair, g_post, mod_p, mod_s):
    w_pad = jnp.pad(w_in, ((0, 0), (0, GDN_PROJ_PAD - w_in.shape[1])))
    qkvz = _proj_call(x, g_pre, mod_p, mod_s, w_pad, jnp.zeros((GDN_PROJ_PAD,), F32), GDN_PROJ_PAD)

    tb = GDN_CHUNK * GDN_NB
    n_steps = SEQ // tb
    o_p, rec_p = _gdn_call(
        qkvz.reshape(TT // tb, tb, GDN_PROJ_PAD), conv_w, a_log, dt_bias, norm_w,
        jnp.zeros((BATCH, SUBLANES, GDN_CONV_DIM), F32),
        jnp.zeros((BATCH, GDN_V_HEADS, GDN_HD, GDN_HD), F32),
        n_seq=BATCH, n_steps=n_steps, chunk=GDN_CHUNK, n_chunks=GDN_NB, t_real=tb,
        n_par=GDN_PROMPT_PAR)
    o_p = o_p.reshape(NP_TOK, GDN_VAL_DIM)

    row_pad = ((0, 0), (0, GDN_SAMPLE_ROWS - DEC_SEQ), (0, 0))
    qkvz_s = qkvz[NP_TOK:].reshape(DEC_BATCH, DEC_SEQ, GDN_PROJ_PAD)
    cbuf_s = jnp.pad(state_conv, ((0, 0), (SUBLANES - (CONV_WIDTH - 1), 0), (0, 0)))
    o_s, rec_s = _gdn_call(
        jnp.pad(qkvz_s, row_pad), conv_w, a_log, dt_bias, norm_w, cbuf_s, state_rec_all,
        n_seq=DEC_BATCH, n_steps=1, chunk=GDN_SAMPLE_ROWS, n_chunks=1, t_real=DEC_SEQ,
        n_par=GDN_SAMPLE_PAR, s0_first_seq=layer_pair * DEC_BATCH)
    o_s = o_s[:, 0, :DEC_SEQ].reshape(NS_TOK, GDN_VAL_DIM)

    keep = CONV_WIDTH - 1
    pre_p = jnp.stack([qkvz[(b + 1) * SEQ - keep:(b + 1) * SEQ, :GDN_CONV_DIM] for b in range(BATCH)])
    pre_s = jnp.concatenate([state_conv, qkvz_s[:, :, :GDN_CONV_DIM]], axis=1)[:, -keep:]
    x = _post_mm_call(x, o_p, o_s, w_out, jnp.zeros((D,), F32), g_post, mod_p, mod_s, 2)
    return x, pre_p, rec_p, pre_s, rec_s


def _swa_layer(x, g_pre, w_in, b_in, sinks, w_out, b_out, rel_bias, cache_k_all, cache_v_all,
               layer_pair, g_post, mod_p, mod_s):
    qkv = _proj_call(x, g_pre, mod_p, mod_s, w_in, b_in, SWA_PROJ_DIM)
    o_p = _swa_prompt_call(qkv, rel_bias, sinks)
    last = jnp.stack([qkv[(b + 1) * SEQ - WINDOW:(b + 1) * SEQ, SWA_Q_DIM:] for b in range(BATCH)])
    new_k_p = last[:, :, :SWA_KV_DIM]
    new_v_p = last[:, :, SWA_KV_DIM:]

    qkv_s = qkv[NP_TOK:].reshape(DEC_BATCH, DEC_SEQ, SWA_PROJ_DIM)
    q_st = qkv_s[:, :, :SWA_Q_DIM].reshape(DEC_BATCH, DEC_SEQ, SWA_KV_HEADS, SWA_GROUP, SWA_HD)
    q_st = q_st.transpose(0, 2, 3, 1, 4).reshape(DEC_BATCH, SWA_KV_HEADS, SWA_GROUP * DEC_SEQ, SWA_HD)
    row_pad = ((0, 0), (0, SWA_KPAD - WINDOW - DEC_SEQ), (0, 0))
    k_new = jnp.pad(qkv_s[:, :, SWA_Q_DIM:SWA_Q_DIM + SWA_KV_DIM], row_pad)
    v_new = jnp.pad(qkv_s[:, :, SWA_Q_DIM + SWA_KV_DIM:], row_pad)
    o_st, new_k_s, new_v_s = _swa_sample_call(q_st, cache_k_all, cache_v_all, k_new, v_new,
                                              layer_pair, rel_bias, sinks)
    o_s = o_st.reshape(DEC_BATCH, SWA_KV_HEADS, SWA_GROUP, DEC_SEQ, SWA_HD)
    o_s = o_s.transpose(0, 3, 1, 2, 4).reshape(NS_TOK, SWA_Q_DIM)

    x = _post_mm_call(x, o_p, o_s, w_out, b_out, g_post, mod_p, mod_s, 2)
    shape_p = (BATCH, WINDOW, SWA_KV_HEADS, SWA_HD)
    shape_s = (DEC_BATCH, WINDOW, SWA_KV_HEADS, SWA_HD)
    return (x, new_k_p.reshape(shape_p), new_v_p.reshape(shape_p),
            new_k_s.reshape(shape_s), new_v_s.reshape(shape_s))


def kernel(x_prompt, x_sample, c_prompt, c_sample, state_conv, state_rec, cache_win_k, cache_win_v, w_mod, b_mod, g_pre_mix, g_post_mix, g_pre_ffn, g_post_ffn, gdn_w_in, gdn_conv_w, gdn_a_log, gdn_dt_bias, gdn_norm_w, gdn_w_out, swa_w_in, swa_b_in, swa_sinks, swa_w_out, swa_b_out, rel_bias, ffn_w_gate, ffn_w_up, ffn_w_down, moe_w_router, moe_b_router, moe_w_gate, moe_w_up, moe_w_down):
    x = jnp.concatenate([x_prompt.reshape(NP_TOK, D), x_sample.reshape(NS_TOK, D)], axis=0)
    n_c = BATCH + DEC_BATCH
    c_all = jnp.concatenate([c_prompt, c_sample, jnp.zeros((-n_c % SUBLANES, D), F32)], axis=0)
    m_all = _mod_call(c_all, w_mod, b_mod)
    mod_s_all = jnp.repeat(m_all[:, BATCH:n_c], DEC_SEQ, axis=1)
    ffn_w = (ffn_w_gate, ffn_w_up, ffn_w_down)
    moe_w = (moe_w_gate.reshape(-1, D, D_FF), moe_w_up.reshape(-1, D, D_FF),
             moe_w_down.reshape(-1, D_FF, D))
    state_rec_all = state_rec.reshape((-1,) + state_rec.shape[2:])
    assert cache_win_k.shape[2] == WINDOW
    cache_k_all = cache_win_k.reshape(-1, WINDOW, SWA_KV_DIM)
    cache_v_all = cache_win_v.reshape(-1, WINDOW, SWA_KV_DIM)

    conv_p, rec_p, conv_s, rec_s = [], [], [], []
    k_p, v_p, k_s, v_s = [], [], [], []
    for layer in range(DEPTH):
        j = layer // 2
        mod_p = m_all[layer, :BATCH].reshape(BATCH, 1, 6 * D)
        mod_s = mod_s_all[layer]
        if layer % 2 == 0:
            x, cp, rp, cs, rs = _gdn_layer(
                x, g_pre_mix[layer], gdn_w_in[j], gdn_conv_w[j], gdn_a_log[j], gdn_dt_bias[j],
                gdn_norm_w[j], gdn_w_out[j], state_conv[j], state_rec_all, j, g_post_mix[layer],
                mod_p, mod_s)
            conv_p.append(cp); rec_p.append(rp); conv_s.append(cs); rec_s.append(rs)
        else:
            x, kp, vp, ks, vs = _swa_layer(
                x, g_pre_mix[layer], swa_w_in[j], swa_b_in[j], swa_sinks[j], swa_w_out[j],
                swa_b_out[j], rel_bias, cache_k_all, cache_v_all, j, g_post_mix[layer],
                mod_p, mod_s)
            k_p.append(kp); v_p.append(vp); k_s.append(ks); v_s.append(vs)

        if layer % 2 == 0:
            x = _dense_ffn_call(x, g_pre_ffn[layer], g_post_ffn[layer], mod_p, mod_s, *ffn_w, j)
        else:
            h, idx, wts = _prenorm_router_call(x, g_pre_ffn[layer], mod_p, mod_s, 3, 4,
                                               moe_w_router[j], moe_b_router[j])
            dest, tile_expert, tile_half, n_used, tail_rows = _route(idx[:, :2])
            xs = _dispatch_call(h, dest, tail_rows, n_used)
            ys = _ffn_call(xs, tile_expert + j * N_EXPERTS, tile_half, n_used, *moe_w)
            x = _combine_call(x, ys, dest, wts, g_post_ffn[layer], mod_p, mod_s, 5,
                              split=layer == DEPTH - 1)

    x_p, x_s = x if isinstance(x, (list, tuple)) else (x[:NP_TOK], x[NP_TOK:])
    y_prompt = x_p.reshape(BATCH, SEQ, D)
    y_sample = x_s.reshape(DEC_BATCH, DEC_SEQ, D)
    return (y_prompt, y_sample, jnp.stack(conv_p), jnp.stack(rec_p), jnp.stack(k_p), jnp.stack(v_p),
            jnp.stack(conv_s), jnp.stack(rec_s), jnp.stack(k_s), jnp.stack(v_s))
```
